```python
import math
import jax, jax.numpy as jnp
from jax import lax
import numpy as np

D_MODEL = 1024
BATCH = 8
SEQ = 2048
DEPTH = 2
DEC_BATCH = 32
DEC_SEQ = 64
PAST_LEN = 1024

EPS = 1e-6
F32 = jnp.float32

CHUNK = 64
N_PAST_CHUNKS = 8
BAND = (N_PAST_CHUNKS + 1) * CHUNK
WINDOW = N_PAST_CHUNKS * CHUNK

H_A = 8
DH_A = 64
WIDTH_A = H_A * DH_A
REL_CLIP = 128

H_B = 4
DK_B = 32
DV_B = 64
WIDTH_B = H_B * DV_B
GATE_RANK = 16
GATE_TAU = 16.0

H_C = 4
P_C = 64
WIDTH_C = H_C * P_C
N_GROUPS_C = 2
D_STATE = 64
CONV_W = 4
CONV_CH = WIDTH_C + 2 * N_GROUPS_C * D_STATE

D_MIX = WIDTH_A + WIDTH_B + WIDTH_C
IN_SPLIT = (WIDTH_A, WIDTH_A, WIDTH_A,
            H_B * DK_B, H_B * DK_B, WIDTH_B, GATE_RANK, WIDTH_B,
            WIDTH_C, CONV_CH, H_C)
IN_COLS = sum(IN_SPLIT)

N_EXPERT_GROUPS = 4
EXPERTS_PER_GROUP = 4
N_EXPERTS = N_EXPERT_GROUPS * EXPERTS_PER_GROUP
TOP_K_IN_GROUP = 2
D_EXPERT = 256

kernel_name = 'hybrid_streaming_encoder_step'


def rms_norm(x, w):
    xf = x.astype(F32)
    y = xf * lax.rsqrt(jnp.mean(xf * xf, axis=-1, keepdims=True) + EPS) * w.astype(F32)
    return y.astype(x.dtype)


def chunked_scan(step, s0, xs):
    L = xs[0].shape[1]
    blk = min(CHUNK, L)
    n = L // blk

    def to_blocks(t):
        return jnp.moveaxis(t.reshape(t.shape[0], n, blk, *t.shape[2:]), 1, 0)

    s, ys = lax.scan(step, s0, tuple(to_blocks(t) for t in xs))
    ys = jnp.moveaxis(ys, 0, 1)
    return s, ys.reshape(ys.shape[0], L, *ys.shape[3:])


def rel_bias(table, rel):
    return table[:, jnp.clip(rel, -REL_CLIP, REL_CLIP) + REL_CLIP].astype(F32)


def band_attention(q, k, v, bias, valid):
    s = jnp.einsum('bnqhd,bnkhd->bnhqk', q, k).astype(F32) * (DH_A ** -0.5) + bias
    s = jnp.where(valid[None, :, None], s, -1e30)
    p = jax.nn.softmax(s, axis=-1).astype(v.dtype)
    return jnp.einsum('bnhqk,bnkhd->bnqhd', p, v)


def band_attention_prompt(q, k, v, table):
    bsz, S = q.shape[:2]
    nc = S // CHUNK

    def band(t):
        tc = t.reshape(bsz, nc, CHUNK, H_A, DH_A)
        tp = jnp.pad(tc, ((0, 0), (N_PAST_CHUNKS, 0), (0, 0), (0, 0), (0, 0)))
        return jnp.concatenate([tp[:, j:j + nc] for j in range(N_PAST_CHUNKS + 1)], axis=2)

    qi = jnp.arange(CHUNK)
    kj = jnp.arange(BAND)
    rel = qi[:, None] + WINDOW - kj[None, :]
    k_pos = (jnp.arange(nc)[:, None] - N_PAST_CHUNKS) * CHUNK + kj[None, :]
    valid = jnp.broadcast_to((k_pos >= 0)[:, None, :], (nc, CHUNK, BAND))
    out = band_attention(q.reshape(bsz, nc, CHUNK, H_A, DH_A), band(k), band(v),
                         rel_bias(table, rel), valid)
    return out.reshape(bsz, S, WIDTH_A)


def band_attention_sample(q, k, v, ck, cv, table):
    bsz, L = q.shape[:2]
    lc = ck.shape[1]
    kk = jnp.concatenate([ck.astype(k.dtype), k], axis=1)[:, None]
    vv = jnp.concatenate([cv.astype(v.dtype), v], axis=1)[:, None]
    q_pos = lc + jnp.arange(L)
    k_pos = jnp.arange(lc + L)
    valid = jnp.ones((1, L, lc + L), bool)
    out = band_attention(q[:, None], kk, vv, rel_bias(table, q_pos[:, None] - k_pos[None, :]), valid)
    return out.reshape(bsz, L, WIDTH_A)


def gla_block(s, blk):
    q, k, v, g = blk
    L = q.shape[1]
    b = jnp.cumsum(g, axis=1)
    causal = jnp.tril(jnp.ones((L, L), bool))[None, :, :, None, None]
    decay = jnp.exp(jnp.where(causal, b[:, :, None] - b[:, None, :], -jnp.inf))
    att = jnp.einsum('bthk,bshk,btshk->bhts', q, k, decay)
    o = jnp.einsum('bhts,bshv->bthv', att, v) + jnp.einsum('bthk,bhkv->bthv', q * jnp.exp(b), s)
    b_last = b[:, -1]
    s_new = jnp.exp(b_last)[..., None] * s + jnp.einsum('bshk,bshv->bhkv', k * jnp.exp(b_last[:, None] - b), v)
    return s_new, o


def ssd_block(h, blk):
    x, dt, a, bm, cm = blk
    L = x.shape[1]
    bh = jnp.repeat(bm, H_C // N_GROUPS_C, axis=2)
    ch = jnp.repeat(cm, H_C // N_GROUPS_C, axis=2)
    cs = jnp.cumsum(a, axis=1)
    causal = jnp.tril(jnp.ones((L, L), bool))[None, :, :, None]
    decay = jnp.exp(jnp.where(causal, cs[:, :, None] - cs[:, None, :], -jnp.inf))
    scores = jnp.einsum('bthn,bshn->btsh', ch, bh) * decay
    y = (jnp.einsum('btsh,bshp->bthp', scores, x * dt[..., None])
         + jnp.einsum('bthn,bhpn->bthp', ch, h) * jnp.exp(cs)[..., None])
    cs_last = cs[:, -1]
    w = jnp.exp(cs_last[:, None] - cs) * dt
    h_new = jnp.exp(cs_last)[:, :, None, None] * h + jnp.einsum('bsh,bshp,bshn->bhpn', w, x, bh)
    return h_new, y


def mixer_block(h, w_in, rel_table, gla_w_gate_up, gla_b_gate, gla_norm_w, conv_w, conv_b,
                dt_bias, a_log, d_skip, ssm_norm_w, w_out, cache_k, cache_v, gla_s0, conv_s0, ssm_s0):
    bsz, L, _ = h.shape
    proj = h @ w_in
    offsets = np.cumsum(IN_SPLIT)[:-1].tolist()
    aq, ak, av, bq, bk, bv, bglr, bog, cz, cxbc, cdt = jnp.split(proj, offsets, axis=-1)

    def heads(t, n):
        return t.reshape(bsz, L, n, -1)

    q, k, v = heads(aq, H_A), heads(ak, H_A), heads(av, H_A)
    if cache_k is None:
        att = band_attention_prompt(q, k, v, rel_table)
        keep = min(WINDOW, L)
        k_keep, v_keep = k[:, L - keep:], v[:, L - keep:]
    else:
        att = band_attention_sample(q, k, v, cache_k, cache_v, rel_table)
        k_keep, v_keep = k, v

    gq = heads(bq, H_B).astype(F32) * (DK_B ** -0.5)
    gk = heads(bk, H_B).astype(F32)
    gv = heads(bv, H_B).astype(F32)
    glog = jax.nn.log_sigmoid((bglr @ gla_w_gate_up + gla_b_gate).astype(F32)) / GATE_TAU
    gla_s, go = chunked_scan(gla_block, gla_s0.astype(F32), (gq, gk, gv, heads(glog, H_B)))
    go = rms_norm(go, gla_norm_w) * jax.nn.silu(heads(bog, H_B).astype(F32))
    gla_out = go.reshape(bsz, L, WIDTH_B)

    xpad = jnp.concatenate([conv_s0.astype(cxbc.dtype), cxbc], axis=1)
    conv = conv_b + sum(xpad[:, i:i + L] * conv_w[i] for i in range(CONV_W))
    conv = jax.nn.silu(conv.astype(F32))
    conv_s = xpad[:, L:]
    sx, sb, sc = jnp.split(conv, [WIDTH_C, WIDTH_C + N_GROUPS_C * D_STATE], axis=-1)
    sx = sx.reshape(bsz, L, H_C, P_C)
    sb = sb.reshape(bsz, L, N_GROUPS_C, D_STATE)
    sc = sc.reshape(bsz, L, N_GROUPS_C, D_STATE)
    dt = jax.nn.softplus(cdt.astype(F32) + dt_bias.astype(F32))
    a = -jnp.exp(a_log.astype(F32)) * dt
    ssm_s, sy = chunked_scan(ssd_block, ssm_s0.astype(F32), (sx, dt, a, sb, sc))
    sy = sy + d_skip.astype(F32)[:, None] * sx
    sy = sy.reshape(bsz, L, WIDTH_C) * jax.nn.silu(cz.astype(F32))
    gsz = WIDTH_C // N_GROUPS_C
    sy = rms_norm(sy.reshape(bsz, L, N_GROUPS_C, gsz), ssm_norm_w.reshape(N_GROUPS_C, gsz))
    ssm_out = sy.reshape(bsz, L, WIDTH_C)

    mixed = jnp.concatenate([att.astype(h.dtype), gla_out.astype(h.dtype), ssm_out.astype(h.dtype)], axis=-1)
    return mixed @ w_out, k_keep, v_keep, gla_s, conv_s, ssm_s


def hier_moe(h, w_rg, b_rg, w_re, b_re, w_gate, w_up, w_down):
    bsz, L, _ = h.shape
    hf = h.astype(F32)
    lg = hf @ w_rg.astype(F32) + b_rg.astype(F32)
    g_idx = jnp.argmax(lg, axis=-1)
    p_grp = jnp.take_along_axis(jax.nn.softmax(lg, axis=-1), g_idx[..., None], axis=-1)
    le = (hf @ w_re.astype(F32) + b_re.astype(F32)).reshape(bsz, L, N_EXPERT_GROUPS, EXPERTS_PER_GROUP)
    le = jnp.take_along_axis(le, g_idx[..., None, None], axis=2)[:, :, 0]
    top_v, top_i = lax.top_k(le, TOP_K_IN_GROUP)
    w_sel = jax.nn.softmax(top_v, axis=-1) * p_grp
    e_id = g_idx[..., None] * EXPERTS_PER_GROUP + top_i
    gate = jnp.sum(jax.nn.one_hot(e_id, N_EXPERTS, dtype=F32) * w_sel[..., None], axis=-2)
    hid = jax.nn.silu(jnp.einsum('bld,edf->blef', h, w_gate)) * jnp.einsum('bld,edf->blef', h, w_up)
    hid = hid * gate.astype(h.dtype)[..., None]
    return jnp.einsum('blef,efd->bld', hid, w_down)


def setup_inputs(seed: int = 0) -> dict:
    key = jax.random.key(seed)
    ks = iter(jax.random.split(key, 40))

    def nrm(shape, scale):
        return jax.random.normal(next(ks), shape, F32) * scale

    att_len = min(WINDOW, PAST_LEN)
    dt0 = jnp.exp(jax.random.uniform(next(ks), (DEPTH, H_C), F32, math.log(1e-3), math.log(1e-1)))
    dt_bias = dt0 + jnp.log(-jnp.expm1(-dt0))
    a_log = jnp.log(jax.random.uniform(next(ks), (DEPTH, H_C), F32, 1.0, 16.0))
    return {
        'x_prompt': nrm((BATCH, SEQ, D_MODEL), 1.0),
        'x_sample': nrm((DEC_BATCH, DEC_SEQ, D_MODEL), 1.0),
        'cache_k_a': nrm((DEPTH, DEC_BATCH, att_len, H_A, DH_A), 1.0),
        'cache_v_a': nrm((DEPTH, DEC_BATCH, att_len, H_A, DH_A), 1.0),
        'state_gla': nrm((DEPTH, DEC_BATCH, H_B, DK_B, DV_B), 0.5),
        'state_conv': nrm((DEPTH, DEC_BATCH, CONV_W - 1, CONV_CH), 1.0),
        'state_ssm': nrm((DEPTH, DEC_BATCH, H_C, P_C, D_STATE), 0.5),
        'norm1_w': 1.0 + nrm((DEPTH, D_MODEL), 0.02),
        'w_in': nrm((DEPTH, D_MODEL, IN_COLS), D_MODEL ** -0.5),
        'rel_bias_table': nrm((DEPTH, H_A, 2 * REL_CLIP + 1), 0.5),
        'gla_w_gate_up': nrm((DEPTH, GATE_RANK, H_B * DK_B), GATE_RANK ** -0.5),
        'gla_b_gate': nrm((DEPTH, H_B * DK_B), 0.1),
        'gla_norm_w': 1.0 + nrm((DEPTH, DV_B), 0.02),
        'ssm_conv_w': nrm((DEPTH, CONV_W, CONV_CH), CONV_W ** -0.5),
        'ssm_conv_b': nrm((DEPTH, CONV_CH), 0.02),
        'ssm_dt_bias': dt_bias,
        'ssm_a_log': a_log,
        'ssm_d': 1.0 + nrm((DEPTH, H_C), 0.02),
        'ssm_norm_w': 1.0 + nrm((DEPTH, WIDTH_C), 0.02),
        'w_out': nrm((DEPTH, D_MIX, D_MODEL), D_MIX ** -0.5),
        'norm2_w': 1.0 + nrm((DEPTH, D_MODEL), 0.02),
        'router_group_w': nrm((DEPTH, D_MODEL, N_EXPERT_GROUPS), D_MODEL ** -0.5),
        'router_group_b': nrm((DEPTH, N_EXPERT_GROUPS), 0.01),
        'router_expert_w': nrm((DEPTH, D_MODEL, N_EXPERTS), D_MODEL ** -0.5),
        'router_expert_b': nrm((DEPTH, N_EXPERTS), 0.01),
        'exp_w_gate': nrm((DEPTH, N_EXPERTS, D_MODEL, D_EXPERT), D_MODEL ** -0.5),
        'exp_w_up': nrm((DEPTH, N_EXPERTS, D_MODEL, D_EXPERT), D_MODEL ** -0.5),
        'exp_w_down': nrm((DEPTH, N_EXPERTS, D_EXPERT, D_MODEL), D_EXPERT ** -0.5),
        'final_norm_w': 1.0 + nrm((D_MODEL,), 0.02),
    }


def reference(x_prompt, x_sample, cache_k_a, cache_v_a, state_gla, state_conv, state_ssm,
              norm1_w, w_in, rel_bias_table, gla_w_gate_up, gla_b_gate, gla_norm_w,
              ssm_conv_w, ssm_conv_b, ssm_dt_bias, ssm_a_log, ssm_d, ssm_norm_w, w_out,
              norm2_w, router_group_w, router_group_b, router_expert_w, router_expert_b,
              exp_w_gate, exp_w_up, exp_w_down, final_norm_w):
    xp, xs = x_prompt, x_sample
    bp = xp.shape[0]
    pk, pv, pg, pc, ps = [], [], [], [], []
    sk, sv, sg, sc, ss = [], [], [], [], []
    for i in range(DEPTH):
        mix_w = (w_in[i], rel_bias_table[i], gla_w_gate_up[i], gla_b_gate[i], gla_norm_w[i],
                 ssm_conv_w[i], ssm_conv_b[i], ssm_dt_bias[i], ssm_a_log[i], ssm_d[i],
                 ssm_norm_w[i], w_out[i])
        moe_w = (router_group_w[i], router_group_b[i], router_expert_w[i], router_expert_b[i],
                 exp_w_gate[i], exp_w_up[i], exp_w_down[i])
        mo, k_new, v_new, g_s, c_s, s_s = mixer_block(
            rms_norm(xp, norm1_w[i]), *mix_w, None, None,
            jnp.zeros((bp, H_B, DK_B, DV_B), F32),
            jnp.zeros((bp, CONV_W - 1, CONV_CH), xp.dtype),
            jnp.zeros((bp, H_C, P_C, D_STATE), F32))
        xp = xp + mo
        xp = xp + hier_moe(rms_norm(xp, norm2_w[i]), *moe_w)
        pk.append(k_new); pv.append(v_new); pg.append(g_s); pc.append(c_s); ps.append(s_s)
        mo, k_new, v_new, g_s, c_s, s_s = mixer_block(
            rms_norm(xs, norm1_w[i]), *mix_w, cache_k_a[i], cache_v_a[i],
            state_gla[i], state_conv[i], state_ssm[i])
        xs = xs + mo
        xs = xs + hier_moe(rms_norm(xs, norm2_w[i]), *moe_w)
        sk.append(k_new); sv.append(v_new); sg.append(g_s); sc.append(c_s); ss.append(s_s)
    y_prompt = rms_norm(xp, final_norm_w)
    y_sample = rms_norm(xs, final_norm_w)
    return (y_prompt, y_sample,
            jnp.stack(pk), jnp.stack(pv), jnp.stack(pg), jnp.stack(pc), jnp.stack(ps),
            jnp.stack(sk), jnp.stack(sv), jnp.stack(sg), jnp.stack(sc), jnp.stack(ss))
```

```python
import functools

import jax
import jax.numpy as jnp
from jax import lax
from jax.experimental import pallas as pl
from jax.experimental.pallas import tpu as pltpu

F32 = jnp.float32
BF16 = jnp.bfloat16

D_MODEL = 1024
DEPTH = 2
EPS = 1e-6
CHUNK = 64
N_PAST_CHUNKS = 8
BAND = (N_PAST_CHUNKS + 1) * CHUNK
WINDOW = N_PAST_CHUNKS * CHUNK
H_A, DH_A, WIDTH_A = 8, 64, 512
REL_CLIP = 128
H_B, DK_B, DV_B, WIDTH_B = 4, 32, 64, 256
GATE_RANK = 16
GATE_TAU = 16.0
H_C, P_C, WIDTH_C = 4, 64, 256
N_GROUPS_C = 2
D_STATE = 64
CONV_W = 4
CONV_CH = 512
N_EXPERT_GROUPS = 4
EXPERTS_PER_GROUP = 4
N_EXPERTS = 16
D_EXPERT = 256

PROJ_COLS = 3200
MISC_DT_LANE = GATE_RANK
ROUTER_GROUP_LANE = N_EXPERTS

SEQ_GROUP = 8
VMEM_LIMIT_BYTES = 56 * 1024 * 1024

NT_DIMS = (((1,), (1,)), ((), ()))
TN_DIMS = (((0,), (0,)), ((), ()))


def _params(*sem):
    return pltpu.CompilerParams(dimension_semantics=sem, vmem_limit_bytes=VMEM_LIMIT_BYTES)


def _dot(a, b):
    return jnp.dot(a, b, preferred_element_type=F32)


def _dot_nt(a, b):
    return lax.dot_general(a, b, NT_DIMS, preferred_element_type=F32)


def _dot_tn(a, b):
    return lax.dot_general(a, b, TN_DIMS, preferred_element_type=F32)


def _split3(a):
    a1 = a.astype(BF16)
    r1 = a - a1.astype(F32)
    a2 = r1.astype(BF16)
    a3 = (r1 - a2.astype(F32)).astype(BF16)
    return a1, a2, a3


def _dot01_left(m01, a):
    a1, a2, a3 = _split3(a)
    return _dot(m01, a1) + _dot(m01, a2) + _dot(m01, a3)


def _dot01_right(a, m01):
    a1, a2, a3 = _split3(a)
    return _dot(a1, m01) + _dot(a2, m01) + _dot(a3, m01)


def _silu(x):
    return x * jax.nn.sigmoid(x)


def _softplus(x):
    return jnp.maximum(x, 0.0) + jnp.log1p(jnp.exp(-jnp.abs(x)))


IN_TM = 256
IN_TN = 640


def _in_proj_kernel(x_ref, nw_ref, w_ref, o_ref):
    x = x_ref[...]
    ms = jnp.mean(x * x, axis=-1, keepdims=True)
    xn = (x * lax.rsqrt(ms + EPS) * nw_ref[...]).astype(BF16)
    for j in range(0, PROJ_COLS, IN_TN):
        o_ref[:, j:j + IN_TN] = _dot(xn, w_ref[:, j:j + IN_TN])


def _in_proj(x, nw, w):
    t = x.shape[0]
    return pl.pallas_call(
        _in_proj_kernel,
        grid=(t // IN_TM,),
        in_specs=[pl.BlockSpec((IN_TM, D_MODEL), lambda i: (i, 0)),
                  pl.BlockSpec((1, D_MODEL), lambda i: (0, 0)),
                  pl.BlockSpec((D_MODEL, PROJ_COLS), lambda i: (0, 0))],
        out_specs=pl.BlockSpec((IN_TM, PROJ_COLS), lambda i: (i, 0)),
        out_shape=jax.ShapeDtypeStruct((t, PROJ_COLS), F32),
        compiler_params=_params("parallel"),
    )(x, nw, w)


def _attn_heads(q_ref, pieces, bias_ref, o_ref):
    for h in range(H_A):
        sl = slice(DH_A * h, DH_A * (h + 1))
        qh = (q_ref[0, :, sl] * (DH_A ** -0.5)).astype(BF16)
        scores = []
        for k_load, _, bsl in pieces:
            scores.append(_dot_nt(qh, k_load(sl).astype(BF16)) + bias_ref[0, h, :, bsl])
        m = scores[0].max(axis=-1, keepdims=True)
        for s in scores[1:]:
            m = jnp.maximum(m, s.max(axis=-1, keepdims=True))
        l = None
        o = None
        for s, (_, v_load, _) in zip(scores, pieces):
            p = jnp.exp(s - m)
            ls = p.sum(axis=-1, keepdims=True)
            os_ = _dot(p.astype(BF16), v_load(sl).astype(BF16))
            l = ls if l is None else l + ls
            o = os_ if o is None else o + os_
        o_ref[0, :, sl] = (o / l).astype(o_ref.dtype)


def _attn_prompt_kernel(q_ref, k_ref, v_ref, bias_ref, o_ref):
    c = pl.program_id(1)
    start = pl.multiple_of(jnp.maximum(c - N_PAST_CHUNKS, 0) * CHUNK, CHUNK)
    pieces = [(lambda sl: k_ref[0, pl.ds(start, BAND), sl],
               lambda sl: v_ref[0, pl.ds(start, BAND), sl],
               slice(0, BAND))]
    _attn_heads(q_ref, pieces, bias_ref, o_ref)


def _attn_sample_kernel(q_ref, k_ref, v_ref, ck_ref, cv_ref, bias_ref, o_ref):
    pieces = [(lambda sl: ck_ref[0, :, sl], lambda sl: cv_ref[0, :, sl], slice(0, WINDOW)),
              (lambda sl: k_ref[0, :, sl], lambda sl: v_ref[0, :, sl], slice(WINDOW, BAND))]
    _attn_heads(q_ref, pieces, bias_ref, o_ref)


def _attn_prompt(proj, biasmask):
    bsz, s, _ = proj.shape
    nc = s // CHUNK
    return pl.pallas_call(
        _attn_prompt_kernel,
        grid=(bsz, nc),
        in_specs=[pl.BlockSpec((1, CHUNK, WIDTH_A), lambda b, c: (b, c, 0)),
                  pl.BlockSpec((1, s, WIDTH_A), lambda b, c: (b, 0, 1)),
                  pl.BlockSpec((1, s, WIDTH_A), lambda b, c: (b, 0, 2)),
                  pl.BlockSpec((1, H_A, CHUNK, BAND),
                               lambda b, c: (jnp.minimum(c, N_PAST_CHUNKS), 0, 0, 0))],
        out_specs=pl.BlockSpec((1, CHUNK, WIDTH_A), lambda b, c: (b, c, 0)),
        out_shape=jax.ShapeDtypeStruct((bsz, s, WIDTH_A), BF16),
        compiler_params=_params("parallel", "arbitrary"),
    )(proj, proj, proj, biasmask)


def _attn_sample(proj, cache_k, cache_v, biasmask):
    bsz = proj.shape[0]
    return pl.pallas_call(
        _attn_sample_kernel,
        grid=(bsz,),
        in_specs=[pl.BlockSpec((1, CHUNK, WIDTH_A), lambda b: (b, 0, 0)),
                  pl.BlockSpec((1, CHUNK, WIDTH_A), lambda b: (b, 0, 1)),
                  pl.BlockSpec((1, CHUNK, WIDTH_A), lambda b: (b, 0, 2)),
                  pl.BlockSpec((1, WINDOW, WIDTH_A), lambda b: (b, 0, 0)),
                  pl.BlockSpec((1, WINDOW, WIDTH_A), lambda b: (b, 0, 0)),
                  pl.BlockSpec((1, H_A, CHUNK, BAND), lambda b: (N_PAST_CHUNKS, 0, 0, 0))],
        out_specs=pl.BlockSpec((1, CHUNK, WIDTH_A), lambda b: (b, 0, 0)),
        out_shape=jax.ShapeDtypeStruct((bsz, CHUNK, WIDTH_A), BF16),
        compiler_params=_params("parallel"),
    )(proj, proj, proj, cache_k, cache_v, biasmask)


def _scan_kernel(bqk_ref, bv_ref, bog_ref, cz_ref, cx_ref, misc_ref,
                 gup_ref, gb_ref, gnw_ref, cw_ref, cb_ref, dtb_ref, alog_ref, dsk_ref, snw_ref,
                 gla0_ref, conv0_ref, ssm0_ref,
                 gla_o_ref, ssm_o_ref, gla_s_ref, conv_s_ref, ssm_s_ref,
                 xpad_ref):
    c = pl.program_id(1)
    nseq = bqk_ref.shape[0]

    @pl.when(c == 0)
    def _():
        gla_s_ref[...] = gla0_ref[...]
        conv_s_ref[...] = conv0_ref[...]
        ssm_s_ref[...] = ssm0_ref[...]

    r64 = lax.broadcasted_iota(jnp.int32, (CHUNK, CHUNK), 0)
    c64 = lax.broadcasted_iota(jnp.int32, (CHUNK, CHUNK), 1)
    tri = c64 <= r64
    tri_bf = jnp.where(tri, 1.0, 0.0).astype(BF16)
    ones_bf = jnp.ones((CHUNK, CHUNK), BF16)
    row_x = lax.broadcasted_iota(jnp.int32, (CHUNK, WIDTH_C), 0)
    s_x = lax.broadcasted_iota(jnp.int32, (CHUNK, WIDTH_C), 1) & (CHUNK - 1)
    causal_x = s_x <= row_x
    upper_x = row_x <= s_x
    er = lax.broadcasted_iota(jnp.int32, (128, WIDTH_C), 0)
    el = lax.broadcasted_iota(jnp.int32, (128, WIDTH_C), 1)
    expand_bf = jnp.where(er == MISC_DT_LANE + (el >> 6), 1.0, 0.0).astype(BF16)
    br = lax.broadcasted_iota(jnp.int32, (WIDTH_C, WIDTH_C), 0)
    bl = lax.broadcasted_iota(jnp.int32, (WIDTH_C, WIDTH_C), 1)
    blk64_bf = jnp.where((br >> 6) == (bl >> 6), 1.0, 0.0).astype(BF16)
    blk128_bf = jnp.where((br >> 7) == (bl >> 7), 1.0, 0.0).astype(BF16)

    gup = gup_ref[...]
    gb = gb_ref[...]
    gnw = gnw_ref[...]
    cb = cb_ref[...]
    dtb = dtb_ref[...]
    neg_a = -jnp.exp(alog_ref[...])
    dsk = dsk_ref[...]
    snw = snw_ref[...]

    def seq_body(b, carry):
        misc = misc_ref[b]

        z = _dot(misc.astype(BF16), gup) + gb
        g = (jnp.minimum(z, 0.0) - jnp.log1p(jnp.exp(-jnp.abs(z)))) * (1.0 / GATE_TAU)
        bcum = _dot01_left(tri_bf, g)
        qk = bqk_ref[b]
        q = qk[:, :H_B * DK_B]
        k = qk[:, H_B * DK_B:]
        blast = bcum[CHUNK - 1:CHUNK, :]
        qb = (q * (DK_B ** -0.5) * jnp.exp(bcum)).astype(BF16)
        kb = (k * jnp.exp(-bcum)).astype(BF16)
        kl = (k * jnp.exp(blast - bcum)).astype(BF16)
        vb = bv_ref[b].astype(BF16)
        st = gla_s_ref[b]
        st_bf = st.astype(BF16)
        o_parts, s_parts = [], []
        for h in range(H_B):
            ks = slice(DK_B * h, DK_B * (h + 1))
            vs = slice(DV_B * h, DV_B * (h + 1))
            att = jnp.where(tri, _dot_nt(qb[:, ks], kb[:, ks]), 0.0).astype(BF16)
            o_parts.append(_dot(att, vb[:, vs]) + _dot_nt(qb[:, ks], st_bf[:, ks]))
            s_parts.append(_dot_tn(vb[:, vs], kl[:, ks]))
        o = jnp.concatenate(o_parts, axis=1)
        gla_s_ref[b] = jnp.exp(blast) * st + jnp.concatenate(s_parts, axis=1)
        msq = _dot01_right(o * o, blk64_bf) * (1.0 / DV_B)
        go = o * lax.rsqrt(msq + EPS) * gnw * _silu(bog_ref[b])
        gla_o_ref[b] = go.astype(gla_o_ref.dtype)

        cx = cx_ref[b]
        xpad_ref[0:8, :] = conv_s_ref[b]
        xpad_ref[8:8 + CHUNK, :] = cx
        conv = cb + xpad_ref[8:8 + CHUNK, :] * cw_ref[CONV_W - 1:CONV_W, :]
        for i in range(CONV_W - 1):
            off = 8 - (CONV_W - 1) + i
            conv = conv + xpad_ref[off:off + CHUNK, :] * cw_ref[i:i + 1, :]
        conv_s_ref[b] = cx[CHUNK - 8:, :]
        conv = _silu(conv)
        sx = conv[:, :WIDTH_C]
        sb_bf = conv[:, WIDTH_C:WIDTH_C + N_GROUPS_C * D_STATE].astype(BF16)
        sc_bf = conv[:, WIDTH_C + N_GROUPS_C * D_STATE:].astype(BF16)

        dtv = _softplus(misc + dtb)
        dt_x = _dot01_right(dtv, expand_bf)
        a_x = _dot01_right(neg_a * dtv, expand_bf)
        cs_x = _dot01_left(tri_bf, a_x)
        cs_row = _dot01_left(ones_bf, jnp.where(upper_x, a_x, 0.0))
        decay = jnp.exp(jnp.where(causal_x, cs_x - cs_row, -jnp.inf))
        cb_parts = []
        for gidx in range(N_GROUPS_C):
            ns = slice(D_STATE * gidx, D_STATE * (gidx + 1))
            cbm = _dot_nt(sc_bf[:, ns], sb_bf[:, ns])
            cb_parts += [cbm] * (H_C // N_GROUPS_C)
        scores = (jnp.concatenate(cb_parts, axis=1) * decay).astype(BF16)
        xdt = (sx * dt_x).astype(BF16)
        ecs = jnp.exp(cs_x)
        cs_last = cs_x[CHUNK - 1:CHUNK, :]
        xw = (sx * (jnp.exp(cs_last - cs_x) * dt_x)).astype(BF16)
        e_last = jnp.exp(cs_last)
        y_parts = []
        for h in range(H_C):
            ps = slice(P_C * h, P_C * (h + 1))
            gi = h // (H_C // N_GROUPS_C)
            ns = slice(D_STATE * gi, D_STATE * (gi + 1))
            hst = ssm_s_ref[b, h]
            y_parts.append(_dot(scores[:, ps], xdt[:, ps])
                           + _dot_nt(sc_bf[:, ns], hst.astype(BF16)) * ecs[:, ps])
            ssm_s_ref[b, h] = e_last[:, ps] * hst + _dot_tn(xw[:, ps], sb_bf[:, ns])
        y = jnp.concatenate(y_parts, axis=1) + dsk * sx
        y = y * _silu(cz_ref[b])
        ymsq = _dot01_right(y * y, blk128_bf) * (1.0 / (WIDTH_C // N_GROUPS_C))
        ssm_o_ref[b] = (y * lax.rsqrt(ymsq + EPS) * snw).astype(ssm_o_ref.dtype)
        return carry

    lax.fori_loop(0, nseq, seq_body, 0)


def _scan(proj, lw, gla0, conv0, ssm0):
    bsz, l, _ = proj.shape
    nc = l // CHUNK
    sg = min(SEQ_GROUP, bsz)
    ng = bsz // sg

    def col(width, idx):
        return pl.BlockSpec((sg, CHUNK, width), lambda g, c: (g, c, idx))

    def const(shape):
        return pl.BlockSpec(shape, lambda g, c: (0,) * len(shape))

    def state(shape):
        return pl.BlockSpec((sg,) + shape, lambda g, c: (g,) + (0,) * len(shape))

    gla_shape = (DV_B, H_B * DK_B)
    conv_shape = (8, CONV_CH)
    ssm_shape = (H_C, P_C, D_STATE)
    return pl.pallas_call(
        _scan_kernel,
        grid=(ng, nc),
        in_specs=[col(256, 6), col(256, 7), col(256, 8), col(256, 9), col(512, 5), col(128, 24),
                  const((128, 128)), const((1, 128)), const((1, WIDTH_B)),
                  const((CONV_W, CONV_CH)), const((1, CONV_CH)),
                  const((1, 128)), const((1, 128)), const((1, WIDTH_C)), const((1, WIDTH_C)),
                  state(gla_shape), state(conv_shape), state(ssm_shape)],
        out_specs=[pl.BlockSpec((sg, CHUNK, WIDTH_B), lambda g, c: (g, c, 0)),
                   pl.BlockSpec((sg, CHUNK, WIDTH_C), lambda g, c: (g, c, 0)),
                   state(gla_shape), state(conv_shape), state(ssm_shape)],
        out_shape=[jax.ShapeDtypeStruct((bsz, l, WIDTH_B), BF16),
                   jax.ShapeDtypeStruct((bsz, l, WIDTH_C), BF16),
                   jax.ShapeDtypeStruct((bsz,) + gla_shape, F32),
                   jax.ShapeDtypeStruct((bsz,) + conv_shape, F32),
                   jax.ShapeDtypeStruct((bsz,) + ssm_shape, F32)],
        scratch_shapes=[pltpu.VMEM((8 + CHUNK, CONV_CH), F32)],
        compiler_params=_params("parallel", "arbitrary"),
    )(proj, proj, proj, proj, proj, proj,
      lw["gup"], lw["gb"], lw["gnw"], lw["cw"], lw["cb"], lw["dtb"], lw["alog"], lw["dsk"], lw["snw"],
      gla0, conv0, ssm0)


OUT_TM = 256


def _out_proj_kernel(x_ref, att_ref, gla_ref, ssm_ref, wo_ref, nw_ref, rwh_ref, rwl_ref, rb_ref,
                     x1_ref, hn_ref, gate_ref):
    mo = (_dot(att_ref[...], wo_ref[0:WIDTH_A, :])
          + _dot(gla_ref[...], wo_ref[WIDTH_A:WIDTH_A + WIDTH_B, :])
          + _dot(ssm_ref[...], wo_ref[WIDTH_A + WIDTH_B:, :]))
    x1 = x_ref[...] + mo
    x1_ref[...] = x1
    ms = jnp.mean(x1 * x1, axis=-1, keepdims=True)
    hn = x1 * lax.rsqrt(ms + EPS) * nw_ref[...]
    hn_ref[...] = hn.astype(hn_ref.dtype)

    h_hi = hn.astype(BF16)
    h_lo = (hn - h_hi.astype(F32)).astype(BF16)
    logits = (_dot(h_hi, rwh_ref[...]) + _dot(h_lo, rwh_ref[...]) + _dot(h_hi, rwl_ref[...])
              + rb_ref[...])
    lane = lax.broadcasted_iota(jnp.int32, logits.shape, 1)
    lane_f = lane.astype(F32)
    neg = -jnp.inf
    big = 1000.0
    gmask = (lane >= ROUTER_GROUP_LANE) & (lane < ROUTER_GROUP_LANE + N_EXPERT_GROUPS)
    lg = jnp.where(gmask, logits, neg)
    gmax = lg.max(axis=-1, keepdims=True)
    g_idx = jnp.where(lg == gmax, lane_f, big).min(axis=-1, keepdims=True) - ROUTER_GROUP_LANE
    p_grp = 1.0 / jnp.where(gmask, jnp.exp(logits - gmax), 0.0).sum(axis=-1, keepdims=True)
    emask = (lane < N_EXPERTS) & ((lane >> 2).astype(F32) == g_idx)
    le = jnp.where(emask, logits, neg)
    v1 = le.max(axis=-1, keepdims=True)
    i1 = jnp.where(le == v1, lane_f, big).min(axis=-1, keepdims=True)
    le2 = jnp.where(lane_f == i1, neg, le)
    v2 = le2.max(axis=-1, keepdims=True)
    i2 = jnp.where(le2 == v2, lane_f, big).min(axis=-1, keepdims=True)
    e2 = jnp.exp(v2 - v1)
    den = 1.0 + e2
    gate_ref[...] = (jnp.where(lane_f == i1, (1.0 / den) * p_grp, 0.0)
                     + jnp.where(lane_f == i2, (e2 / den) * p_grp, 0.0))


def _out_proj(x, att, gla, ssm, lw):
    t = x.shape[0]

    def rows(width):
        return pl.BlockSpec((OUT_TM, width), lambda i: (i, 0))

    def const(shape):
        return pl.BlockSpec(shape, lambda i: (0, 0))

    return pl.pallas_call(
        _out_proj_kernel,
        grid=(t // OUT_TM,),
        in_specs=[rows(D_MODEL), rows(WIDTH_A), rows(WIDTH_B), rows(WIDTH_C),
                  const((D_MODEL, D_MODEL)), const((1, D_MODEL)),
                  const((D_MODEL, 128)), const((D_MODEL, 128)), const((1, 128))],
        out_specs=[rows(D_MODEL), rows(D_MODEL), rows(128)],
        out_shape=[jax.ShapeDtypeStruct((t, D_MODEL), F32),
                   jax.ShapeDtypeStruct((t, D_MODEL), BF16),
                   jax.ShapeDtypeStruct((t, 128), F32)],
        compiler_params=_params("parallel"),
    )(x, att, gla, ssm, lw["wo"], lw["n2w"], lw["rw_hi"], lw["rw_lo"], lw["rb"])


MOE_TM = 1024


def _moe_kernel(x_ref, hn_ref, gate_ref, wg_ref, wu_ref, wd_ref, fw_ref, o_ref, acc_ref, *, final_norm):
    e = pl.program_id(1)

    @pl.when(e == 0)
    def _():
        acc_ref[...] = jnp.zeros_like(acc_ref)

    hn = hn_ref[...]
    gate = gate_ref[...]
    lane = lax.broadcasted_iota(jnp.int32, gate.shape, 1)
    ge = jnp.where(lane == e, gate, 0.0).sum(axis=-1, keepdims=True)
    hid = _silu(_dot(hn, wg_ref[0])) * _dot(hn, wu_ref[0]) * ge
    acc_ref[...] += _dot(hid.astype(BF16), wd_ref[0])

    @pl.when(e == N_EXPERTS - 1)
    def _():
        y = x_ref[...] + acc_ref[...]
        if final_norm:
            ms = jnp.mean(y * y, axis=-1, keepdims=True)
            y = y * lax.rsqrt(ms + EPS) * fw_ref[...]
        o_ref[...] = y


def _moe(x, hn, gate, lw, fw, final_norm):
    t = x.shape[0]
    tm = min(MOE_TM, t)

    def rows(width):
        return pl.BlockSpec((tm, width), lambda i, e: (i, 0))

    return pl.pallas_call(
        functools.partial(_moe_kernel, final_norm=final_norm),
        grid=(t // tm, N_EXPERTS),
        in_specs=[rows(D_MODEL), rows(D_MODEL), rows(128),
                  pl.BlockSpec((1, D_MODEL, D_EXPERT), lambda i, e: (e, 0, 0)),
                  pl.BlockSpec((1, D_MODEL, D_EXPERT), lambda i, e: (e, 0, 0)),
                  pl.BlockSpec((1, D_EXPERT, D_MODEL), lambda i, e: (e, 0, 0)),
                  pl.BlockSpec((1, D_MODEL), lambda i, e: (0, 0))],
        out_specs=rows(D_MODEL),
        out_shape=jax.ShapeDtypeStruct((t, D_MODEL), F32),
        scratch_shapes=[pltpu.VMEM((tm, D_MODEL), F32)],
        compiler_params=_params("parallel", "arbitrary"),
    )(x, hn, gate, lw["wg"], lw["wu"], lw["wd"], fw)


def _lane_place(vec, start, width=128):
    return jnp.zeros((1, width), F32).at[0, start:start + vec.shape[0]].set(vec.astype(F32))


def _band_bias(table):
    w = jnp.arange(N_PAST_CHUNKS + 1)[:, None, None]
    qi = jnp.arange(CHUNK)[None, :, None]
    kj = jnp.arange(BAND)[None, None, :]
    rel = jnp.clip(w * CHUNK + qi - kj, -REL_CLIP, REL_CLIP) + REL_CLIP
    bias = jnp.moveaxis(table.astype(F32)[:, rel], 0, 1)
    valid = (kj < (w + 1) * CHUNK)[:, None]
    return jnp.where(valid, bias, -1e30)


def _layer_weights(i, norm1_w, w_in, rel_bias_table, gla_w_gate_up, gla_b_gate, gla_norm_w,
                   ssm_conv_w, ssm_conv_b, ssm_dt_bias, ssm_a_log, ssm_d, ssm_norm_w, w_out,
                   norm2_w, router_group_w, router_group_b, router_expert_w, router_expert_b,
                   exp_w_gate, exp_w_up, exp_w_down):
    wi = w_in[i]
    o = [0, 512, 1024, 1536, 1664, 1792, 2048, 2064, 2320, 2576, 3088, 3092]
    aq, ak, av, bq, bk, bv, bglr, bog, cz, cxbc, cdt = [wi[:, o[j]:o[j + 1]] for j in range(11)]
    pad = jnp.zeros((D_MODEL, 128 - GATE_RANK - H_C), wi.dtype)
    w_perm = jnp.concatenate([aq, ak, av, bq, bk, bv, bog, cz, cxbc, bglr, cdt, pad], axis=1)
    rw = jnp.concatenate([router_expert_w[i], router_group_w[i],
                          jnp.zeros((D_MODEL, 128 - N_EXPERTS - N_EXPERT_GROUPS), F32)], axis=1)
    rw_hi = rw.astype(BF16)
    return dict(
        n1w=norm1_w[i][None, :],
        w_in=w_perm.astype(BF16),
        biasmask=_band_bias(rel_bias_table[i]),
        gup=jnp.zeros((128, 128), F32).at[:GATE_RANK, :].set(gla_w_gate_up[i]).astype(BF16),
        gb=gla_b_gate[i][None, :].astype(F32),
        gnw=jnp.tile(gla_norm_w[i], H_B)[None, :].astype(F32),
        cw=ssm_conv_w[i].astype(F32),
        cb=ssm_conv_b[i][None, :].astype(F32),
        dtb=_lane_place(ssm_dt_bias[i], MISC_DT_LANE),
        alog=_lane_place(ssm_a_log[i], MISC_DT_LANE),
        dsk=jnp.repeat(ssm_d[i].astype(F32), P_C)[None, :],
        snw=ssm_norm_w[i][None, :].astype(F32),
        wo=w_out[i].astype(BF16),
        n2w=norm2_w[i][None, :],
        rw_hi=rw_hi,
        rw_lo=(rw - rw_hi.astype(F32)).astype(BF16),
        rb=jnp.concatenate([router_expert_b[i], router_group_b[i],
                            jnp.zeros((128 - N_EXPERTS - N_EXPERT_GROUPS,), F32)])[None, :],
        wg=exp_w_gate[i].astype(BF16),
        wu=exp_w_up[i].astype(BF16),
        wd=exp_w_down[i].astype(BF16),
    )


def _stream_layer(x, bsz, lw, cache_k, cache_v, gla0, conv0, ssm0, fw, final_norm):
    l = x.shape[0] // bsz
    proj = _in_proj(x, lw["n1w"], lw["w_in"]).reshape(bsz, l, PROJ_COLS)
    if cache_k is None:
        att = _attn_prompt(proj, lw["biasmask"])
    else:
        att = _attn_sample(proj, cache_k, cache_v, lw["biasmask"])
    gla_o, ssm_o, gla_s, conv_s, ssm_s = _scan(proj, lw, gla0, conv0, ssm0)
    t = bsz * l
    x1, hn, gate = _out_proj(x, att.reshape(t, WIDTH_A), gla_o.reshape(t, WIDTH_B),
                             ssm_o.reshape(t, WIDTH_C), lw)
    x2 = _moe(x1, hn, gate, lw, fw, final_norm)
    keep = min(WINDOW, l)
    k_keep = proj[:, l - keep:, WIDTH_A:2 * WIDTH_A].reshape(bsz, keep, H_A, DH_A)
    v_keep = proj[:, l - keep:, 2 * WIDTH_A:3 * WIDTH_A].reshape(bsz, keep, H_A, DH_A)
    gla_state = gla_s.reshape(bsz, DV_B, H_B, DK_B).transpose(0, 2, 3, 1)
    conv_state = conv_s[:, 8 - (CONV_W - 1):, :]
    return x2, k_keep, v_keep, gla_state, conv_state, ssm_s


def kernel(x_prompt, x_sample, cache_k_a, cache_v_a, state_gla, state_conv, state_ssm, norm1_w, w_in, rel_bias_table, gla_w_gate_up, gla_b_gate, gla_norm_w, ssm_conv_w, ssm_conv_b, ssm_dt_bias, ssm_a_log, ssm_d, ssm_norm_w, w_out, norm2_w, router_group_w, router_group_b, router_expert_w, router_expert_b, exp_w_gate, exp_w_up, exp_w_down, final_norm_w):
    bp, sp, _ = x_prompt.shape
    bs, ss, _ = x_sample.shape
    xp = x_prompt.reshape(bp * sp, D_MODEL)
    xs = x_sample.reshape(bs * ss, D_MODEL)
    fw = final_norm_w[None, :].astype(F32)
    outs_p, outs_s = [], []
    for i in range(DEPTH):
        lw = _layer_weights(i, norm1_w, w_in, rel_bias_table, gla_w_gate_up, gla_b_gate, gla_norm_w,
                            ssm_conv_w, ssm_conv_b, ssm_dt_bias, ssm_a_log, ssm_d, ssm_norm_w, w_out,
                            norm2_w, router_group_w, router_group_b, router_expert_w, router_expert_b,
                            exp_w_gate, exp_w_up, exp_w_down)
        last = i == DEPTH - 1
        xp, *sp_out = _stream_layer(
            xp, bp, lw, None, None,
            jnp.zeros((bp, DV_B, H_B * DK_B), F32),
            jnp.zeros((bp, 8, CONV_CH), F32),
            jnp.zeros((bp, H_C, P_C, D_STATE), F32), fw, last)
        outs_p.append(sp_out)
        att_len = cache_k_a.shape[2]
        gla0 = state_gla[i].astype(F32).transpose(0, 3, 1, 2).reshape(bs, DV_B, H_B * DK_B)
        conv0 = jnp.pad(state_conv[i].astype(F32), ((0, 0), (8 - (CONV_W - 1), 0), (0, 0)))
        xs, *ss_out = _stream_layer(
            xs, bs, lw,
            cache_k_a[i].reshape(bs, att_len, WIDTH_A), cache_v_a[i].reshape(bs, att_len, WIDTH_A),
            gla0, conv0, state_ssm[i].astype(F32), fw, last)
        outs_s.append(ss_out)
    stack = lambda outs, j: jnp.stack([o[j] for o in outs])
    return (xp.reshape(bp, sp, D_MODEL), xs.reshape(bs, ss, D_MODEL),
            stack(outs_p, 0), stack(outs_p, 1), stack(outs_p, 2), stack(outs_p, 3), stack(outs_p, 4),
            stack(outs_s, 0), stack(outs_s, 1), stack(outs_s, 2), stack(outs_s, 3), stack(outs_s, 4))
```

```python
import functools

import jax
import jax.numpy as jnp
from jax import lax
from jax.experimental import pallas as pl
from jax.experimental.pallas import tpu as pltpu

F32 = jnp.float32
BF16 = jnp.bfloat16

D_MODEL = 1024
DEPTH = 2
EPS = 1e-6
CHUNK = 64
N_PAST_CHUNKS = 8
BAND = (N_PAST_CHUNKS + 1) * CHUNK
WINDOW = N_PAST_CHUNKS * CHUNK
H_A, DH_A, WIDTH_A = 8, 64, 512
REL_CLIP = 128
H_B, DK_B, DV_B, WIDTH_B = 4, 32, 64, 256
GATE_RANK = 16
GATE_TAU = 16.0
H_C, P_C, WIDTH_C = 4, 64, 256
N_GROUPS_C = 2
D_STATE = 64
CONV_W = 4
CONV_CH = 512
N_EXPERT_GROUPS = 4
EXPERTS_PER_GROUP = 4
N_EXPERTS = 16
D_EXPERT = 256

PROJ_COLS = 3200
MISC_DT_LANE = GATE_RANK
ROUTER_GROUP_LANE = N_EXPERTS

SEQ_GROUP = 8
VMEM_LIMIT_BYTES = 56 * 1024 * 1024

NT_DIMS = (((1,), (1,)), ((), ()))
TN_DIMS = (((0,), (0,)), ((), ()))


def _params(*sem):
    return pltpu.CompilerParams(dimension_semantics=sem, vmem_limit_bytes=VMEM_LIMIT_BYTES)


def _dot(a, b):
    return jnp.dot(a, b, preferred_element_type=F32)


def _dot_nt(a, b):
    return lax.dot_general(a, b, NT_DIMS, preferred_element_type=F32)


def _dot_tn(a, b):
    return lax.dot_general(a, b, TN_DIMS, preferred_element_type=F32)


def _split3(a):
    a1 = a.astype(BF16)
    r1 = a - a1.astype(F32)
    a2 = r1.astype(BF16)
    a3 = (r1 - a2.astype(F32)).astype(BF16)
    return a1, a2, a3


def _dot01_left(m01, a):
    a1, a2, a3 = _split3(a)
    return _dot(m01, a1) + _dot(m01, a2) + _dot(m01, a3)


def _dot01_right(a, m01):
    a1, a2, a3 = _split3(a)
    return _dot(a1, m01) + _dot(a2, m01) + _dot(a3, m01)


def _silu(x):
    return x * jax.nn.sigmoid(x)


def _softplus(x):
    return jnp.maximum(x, 0.0) + jnp.log1p(jnp.exp(-jnp.abs(x)))


IN_TM = 256
QKV_COLS = 3 * WIDTH_A
REST_COLS = PROJ_COLS - QKV_COLS
REST_CHUNKS = ((0, 512), (512, 1024), (1024, REST_COLS))


def _in_proj_kernel(x_ref, nw_ref, w_ref, *refs, with_vt):
    if with_vt:
        wvt_ref, qk_ref, kv_ref, rest_ref, vt_ref = refs
    else:
        qk_ref, kv_ref, rest_ref = refs
    x = x_ref[...]
    ms = jnp.mean(x * x, axis=-1, keepdims=True)
    xn = (x * lax.rsqrt(ms + EPS) * nw_ref[...]).astype(BF16)
    qk_ref[:, 0:WIDTH_A] = _dot(xn, w_ref[:, 0:WIDTH_A]).astype(BF16)
    k = _dot(xn, w_ref[:, WIDTH_A:2 * WIDTH_A])
    qk_ref[:, WIDTH_A:] = k.astype(BF16)
    kv_ref[:, 0:WIDTH_A] = k
    kv_ref[:, WIDTH_A:] = _dot(xn, w_ref[:, 2 * WIDTH_A:QKV_COLS])
    for lo, hi in REST_CHUNKS:
        rest_ref[:, lo:hi] = _dot(xn, w_ref[:, QKV_COLS + lo:QKV_COLS + hi])
    if with_vt:
        vt_ref[0] = _dot_nt(wvt_ref[...], xn).astype(BF16)


def _in_proj(x, nw, w, wvt):
    t = x.shape[0]
    with_vt = wvt is not None

    def rows(width):
        return pl.BlockSpec((IN_TM, width), lambda i: (i, 0))

    in_specs = [rows(D_MODEL), pl.BlockSpec((1, D_MODEL), lambda i: (0, 0)),
                pl.BlockSpec((D_MODEL, PROJ_COLS), lambda i: (0, 0))]
    out_specs = [rows(2 * WIDTH_A), rows(2 * WIDTH_A), rows(REST_COLS)]
    out_shape = [jax.ShapeDtypeStruct((t, 2 * WIDTH_A), BF16),
                 jax.ShapeDtypeStruct((t, 2 * WIDTH_A), F32),
                 jax.ShapeDtypeStruct((t, REST_COLS), F32)]
    args = [x, nw, w]
    if with_vt:
        in_specs.append(pl.BlockSpec((WIDTH_A, D_MODEL), lambda i: (0, 0)))
        out_specs.append(pl.BlockSpec((1, WIDTH_A, IN_TM), lambda i: (i, 0, 0)))
        out_shape.append(jax.ShapeDtypeStruct((t // IN_TM, WIDTH_A, IN_TM), BF16))
        args.append(wvt)
    return pl.pallas_call(
        functools.partial(_in_proj_kernel, with_vt=with_vt),
        grid=(t // IN_TM,),
        in_specs=in_specs, out_specs=out_specs, out_shape=out_shape,
        compiler_params=_params("parallel"),
    )(*args)


ATT_QB = 4 * CHUNK
ATT_KB = IN_TM
ATT_NKB = (ATT_QB + WINDOW) // ATT_KB


def _attn_prompt_kernel(q_ref, k_ref, vt_ref, bias_ref, o_ref, ot_ref):
    blk = pl.program_id(1)
    sb = jnp.maximum(blk - WINDOW // ATT_KB, 0)
    kstart = pl.multiple_of(sb * ATT_KB, ATT_KB)
    lane = lax.broadcasted_iota(jnp.int32, (ATT_QB, 128), 1)
    for j in range(H_A // 2):
        pair = slice(128 * j, 128 * (j + 1))
        kp = k_ref[0, pl.ds(kstart, ATT_NKB * ATT_KB), pair]
        qp = q_ref[0, :, pair]
        for hh in range(2):
            h = 2 * j + hh
            qz = jnp.where((lane >= DH_A) == bool(hh), qp, jnp.zeros_like(qp))
            s = [_dot_nt(kp[ATT_KB * i:ATT_KB * (i + 1)], qz)
                 + bias_ref[0, h, ATT_KB * i:ATT_KB * (i + 1), :] for i in range(ATT_NKB)]
            m = s[0].max(axis=0, keepdims=True)
            for si in s[1:]:
                m = jnp.maximum(m, si.max(axis=0, keepdims=True))
            l = None
            ot = None
            for i, si in enumerate(s):
                p = jnp.exp(si - m)
                li = p.sum(axis=0, keepdims=True)
                oi = _dot(vt_ref[0, sb + i, DH_A * h:DH_A * (h + 1), :], p.astype(BF16))
                l = li if l is None else l + li
                ot = oi if ot is None else ot + oi
            ot_ref[DH_A * h:DH_A * (h + 1), :] = ot / l
    o_ref[0] = ot_ref[...].T.astype(o_ref.dtype)


def _attn_prompt(qk, vt, bias_t):
    bsz, s, _ = qk.shape
    nkb = s // ATT_KB
    return pl.pallas_call(
        _attn_prompt_kernel,
        grid=(bsz, s // ATT_QB),
        in_specs=[pl.BlockSpec((1, ATT_QB, WIDTH_A), lambda b, c: (b, c, 0)),
                  pl.BlockSpec((1, s, WIDTH_A), lambda b, c: (b, 0, 1)),
                  pl.BlockSpec((1, nkb, WIDTH_A, ATT_KB), lambda b, c: (b, 0, 0, 0)),
                  pl.BlockSpec((1, H_A, ATT_NKB * ATT_KB, ATT_QB),
                               lambda b, c: (jnp.minimum(c, 2), 0, 0, 0))],
        out_specs=pl.BlockSpec((1, ATT_QB, WIDTH_A), lambda b, c: (b, c, 0)),
        out_shape=jax.ShapeDtypeStruct((bsz, s, WIDTH_A), BF16),
        scratch_shapes=[pltpu.VMEM((WIDTH_A, ATT_QB), F32)],
        compiler_params=_params("parallel", "arbitrary"),
    )(qk, qk, vt, bias_t)


def _attn_sample_kernel(q_ref, kv_ref, ck_ref, cv_ref, bias_ref, o_ref):
    for h in range(H_A):
        sl = slice(DH_A * h, DH_A * (h + 1))
        vsl = slice(WIDTH_A + DH_A * h, WIDTH_A + DH_A * (h + 1))
        qh = q_ref[0, :, sl]
        s1 = _dot_nt(qh, ck_ref[0, :, sl].astype(BF16)) + bias_ref[h, :, 0:WINDOW]
        s2 = _dot_nt(qh, kv_ref[0, :, sl].astype(BF16)) + bias_ref[h, :, WINDOW:BAND]
        m = jnp.maximum(s1.max(axis=-1, keepdims=True), s2.max(axis=-1, keepdims=True))
        p1 = jnp.exp(s1 - m)
        p2 = jnp.exp(s2 - m)
        l = p1.sum(axis=-1, keepdims=True) + p2.sum(axis=-1, keepdims=True)
        o = (_dot(p1.astype(BF16), cv_ref[0, :, sl].astype(BF16))
             + _dot(p2.astype(BF16), kv_ref[0, :, vsl].astype(BF16)))
        o_ref[0, :, sl] = (o / l).astype(o_ref.dtype)


def _attn_sample(qk, kv, cache_k, cache_v, bias):
    bsz = qk.shape[0]
    return pl.pallas_call(
        _attn_sample_kernel,
        grid=(bsz,),
        in_specs=[pl.BlockSpec((1, CHUNK, WIDTH_A), lambda b: (b, 0, 0)),
                  pl.BlockSpec((1, CHUNK, 2 * WIDTH_A), lambda b: (b, 0, 0)),
                  pl.BlockSpec((1, WINDOW, WIDTH_A), lambda b: (b, 0, 0)),
                  pl.BlockSpec((1, WINDOW, WIDTH_A), lambda b: (b, 0, 0)),
                  pl.BlockSpec((H_A, CHUNK, BAND), lambda b: (0, 0, 0))],
        out_specs=pl.BlockSpec((1, CHUNK, WIDTH_A), lambda b: (b, 0, 0)),
        out_shape=jax.ShapeDtypeStruct((bsz, CHUNK, WIDTH_A), BF16),
        compiler_params=_params("parallel"),
    )(qk, kv, cache_k, cache_v, bias)


def _scan_kernel(bqk_ref, bv_ref, bog_ref, cz_ref, cx_ref, misc_ref,
                 gup_ref, gb_ref, gnw_ref, cw_ref, cb_ref, dtb_ref, alog_ref, dsk_ref, snw_ref,
                 gla0_ref, conv0_ref, ssm0_ref,
                 gla_o_ref, ssm_o_ref, gla_s_ref, conv_s_ref, ssm_s_ref,
                 xpad_ref):
    c = pl.program_id(1)
    nseq = bqk_ref.shape[0]

    @pl.when(c == 0)
    def _():
        gla_s_ref[...] = gla0_ref[...]
        conv_s_ref[...] = conv0_ref[...]
        ssm_s_ref[...] = ssm0_ref[...]

    r64 = lax.broadcasted_iota(jnp.int32, (CHUNK, CHUNK), 0)
    c64 = lax.broadcasted_iota(jnp.int32, (CHUNK, CHUNK), 1)
    tri = c64 <= r64
    tri_bf = jnp.where(tri, 1.0, 0.0).astype(BF16)
    ones_bf = jnp.ones((CHUNK, CHUNK), BF16)
    row_x = lax.broadcasted_iota(jnp.int32, (CHUNK, WIDTH_C), 0)
    s_x = lax.broadcasted_iota(jnp.int32, (CHUNK, WIDTH_C), 1) & (CHUNK - 1)
    causal_x = s_x <= row_x
    upper_x = row_x <= s_x
    er = lax.broadcasted_iota(jnp.int32, (128, WIDTH_C), 0)
    el = lax.broadcasted_iota(jnp.int32, (128, WIDTH_C), 1)
    expand_bf = jnp.where(er == MISC_DT_LANE + (el >> 6), 1.0, 0.0).astype(BF16)
    br = lax.broadcasted_iota(jnp.int32, (WIDTH_C, WIDTH_C), 0)
    bl = lax.broadcasted_iota(jnp.int32, (WIDTH_C, WIDTH_C), 1)
    blk64_bf = jnp.where((br >> 6) == (bl >> 6), 1.0, 0.0).astype(BF16)
    blk128_bf = jnp.where((br >> 7) == (bl >> 7), 1.0, 0.0).astype(BF16)

    gup = gup_ref[...]
    gb = gb_ref[...]
    gnw = gnw_ref[...]
    cb = cb_ref[...]
    dtb = dtb_ref[...]
    neg_a = -jnp.exp(alog_ref[...])
    dsk = dsk_ref[...]
    snw = snw_ref[...]

    def seq_body(b, carry):
        misc = misc_ref[b]

        z = _dot(misc.astype(BF16), gup) + gb
        g = (jnp.minimum(z, 0.0) - jnp.log1p(jnp.exp(-jnp.abs(z)))) * (1.0 / GATE_TAU)
        bcum = _dot01_left(tri_bf, g)
        qk = bqk_ref[b]
        q = qk[:, :H_B * DK_B]
        k = qk[:, H_B * DK_B:]
        blast = bcum[CHUNK - 1:CHUNK, :]
        qb = (q * (DK_B ** -0.5) * jnp.exp(bcum)).astype(BF16)
        kb = (k * jnp.exp(-bcum)).astype(BF16)
        kl = (k * jnp.exp(blast - bcum)).astype(BF16)
        vb = bv_ref[b].astype(BF16)
        st = gla_s_ref[b]
        st_bf = st.astype(BF16)
        o_parts, s_parts = [], []
        for h in range(H_B):
            ks = slice(DK_B * h, DK_B * (h + 1))
            vs = slice(DV_B * h, DV_B * (h + 1))
            att = jnp.where(tri, _dot_nt(qb[:, ks], kb[:, ks]), 0.0).astype(BF16)
            o_parts.append(_dot(att, vb[:, vs]) + _dot_nt(qb[:, ks], st_bf[:, ks]))
            s_parts.append(_dot_tn(vb[:, vs], kl[:, ks]))
        o = jnp.concatenate(o_parts, axis=1)
        gla_s_ref[b] = jnp.exp(blast) * st + jnp.concatenate(s_parts, axis=1)
        msq = _dot01_right(o * o, blk64_bf) * (1.0 / DV_B)
        go = o * lax.rsqrt(msq + EPS) * gnw * _silu(bog_ref[b])
        gla_o_ref[b] = go.astype(gla_o_ref.dtype)

        cx = cx_ref[b]
        xpad_ref[0:8, :] = conv_s_ref[b]
        xpad_ref[8:8 + CHUNK, :] = cx
        conv = cb + xpad_ref[8:8 + CHUNK, :] * cw_ref[CONV_W - 1:CONV_W, :]
        for i in range(CONV_W - 1):
            off = 8 - (CONV_W - 1) + i
            conv = conv + xpad_ref[off:off + CHUNK, :] * cw_ref[i:i + 1, :]
        conv_s_ref[b] = cx[CHUNK - 8:, :]
        conv = _silu(conv)
        sx = conv[:, :WIDTH_C]
        sb_bf = conv[:, WIDTH_C:WIDTH_C + N_GROUPS_C * D_STATE].astype(BF16)
        sc_bf = conv[:, WIDTH_C + N_GROUPS_C * D_STATE:].astype(BF16)

        dtv = _softplus(misc + dtb)
        dt_x = _dot01_right(dtv, expand_bf)
        a_x = _dot01_right(neg_a * dtv, expand_bf)
        cs_x = _dot01_left(tri_bf, a_x)
        cs_row = _dot01_left(ones_bf, jnp.where(upper_x, a_x, 0.0))
        decay = jnp.exp(jnp.where(causal_x, cs_x - cs_row, -jnp.inf))
        cb_parts = []
        for gidx in range(N_GROUPS_C):
            ns = slice(D_STATE * gidx, D_STATE * (gidx + 1))
            cbm = _dot_nt(sc_bf[:, ns], sb_bf[:, ns])
            cb_parts += [cbm] * (H_C // N_GROUPS_C)
        scores = (jnp.concatenate(cb_parts, axis=1) * decay).astype(BF16)
        xdt = (sx * dt_x).astype(BF16)
        ecs = jnp.exp(cs_x)
        cs_last = cs_x[CHUNK - 1:CHUNK, :]
        xw = (sx * (jnp.exp(cs_last - cs_x) * dt_x)).astype(BF16)
        e_last = jnp.exp(cs_last)
        y_parts = []
        for h in range(H_C):
            ps = slice(P_C * h, P_C * (h + 1))
            gi = h // (H_C // N_GROUPS_C)
            ns = slice(D_STATE * gi, D_STATE * (gi + 1))
            hst = ssm_s_ref[b, h]
            y_parts.append(_dot(scores[:, ps], xdt[:, ps])
                           + _dot_nt(sc_bf[:, ns], hst.astype(BF16)) * ecs[:, ps])
            ssm_s_ref[b, h] = e_last[:, ps] * hst + _dot_tn(xw[:, ps], sb_bf[:, ns])
        y = jnp.concatenate(y_parts, axis=1) + dsk * sx
        y = y * _silu(cz_ref[b])
        ymsq = _dot01_right(y * y, blk128_bf) * (1.0 / (WIDTH_C // N_GROUPS_C))
        ssm_o_ref[b] = (y * lax.rsqrt(ymsq + EPS) * snw).astype(ssm_o_ref.dtype)
        return carry

    lax.fori_loop(0, nseq, seq_body, 0)


def _scan(proj, lw, gla0, conv0, ssm0):
    bsz, l, _ = proj.shape
    nc = l // CHUNK
    sg = min(SEQ_GROUP, bsz)
    ng = bsz // sg

    def col(width, idx):
        return pl.BlockSpec((sg, CHUNK, width), lambda g, c: (g, c, idx))

    def const(shape):
        return pl.BlockSpec(shape, lambda g, c: (0,) * len(shape))

    def state(shape):
        return pl.BlockSpec((sg,) + shape, lambda g, c: (g,) + (0,) * len(shape))

    gla_shape = (DV_B, H_B * DK_B)
    conv_shape = (8, CONV_CH)
    ssm_shape = (H_C, P_C, D_STATE)
    return pl.pallas_call(
        _scan_kernel,
        grid=(ng, nc),
        in_specs=[col(256, 0), col(256, 1), col(256, 2), col(256, 3), col(512, 2), col(128, 12),
                  const((128, 128)), const((1, 128)), const((1, WIDTH_B)),
                  const((CONV_W, CONV_CH)), const((1, CONV_CH)),
                  const((1, 128)), const((1, 128)), const((1, WIDTH_C)), const((1, WIDTH_C)),
                  state(gla_shape), state(conv_shape), state(ssm_shape)],
        out_specs=[pl.BlockSpec((sg, CHUNK, WIDTH_B), lambda g, c: (g, c, 0)),
                   pl.BlockSpec((sg, CHUNK, WIDTH_C), lambda g, c: (g, c, 0)),
                   state(gla_shape), state(conv_shape), state(ssm_shape)],
        out_shape=[jax.ShapeDtypeStruct((bsz, l, WIDTH_B), BF16),
                   jax.ShapeDtypeStruct((bsz, l, WIDTH_C), BF16),
                   jax.ShapeDtypeStruct((bsz,) + gla_shape, F32),
                   jax.ShapeDtypeStruct((bsz,) + conv_shape, F32),
                   jax.ShapeDtypeStruct((bsz,) + ssm_shape, F32)],
        scratch_shapes=[pltpu.VMEM((8 + CHUNK, CONV_CH), F32)],
        compiler_params=_params("parallel", "arbitrary"),
    )(proj, proj, proj, proj, proj, proj,
      lw["gup"], lw["gb"], lw["gnw"], lw["cw"], lw["cb"], lw["dtb"], lw["alog"], lw["dsk"], lw["snw"],
      gla0, conv0, ssm0)


OUT_TM = 256


def _out_proj_kernel(x_ref, att_ref, gla_ref, ssm_ref, wo_ref, nw_ref, rwh_ref, rwl_ref, rb_ref,
                     x1_ref, hn_ref, gate_ref):
    mo = (_dot(att_ref[...], wo_ref[0:WIDTH_A, :])
          + _dot(gla_ref[...], wo_ref[WIDTH_A:WIDTH_A + WIDTH_B, :])
          + _dot(ssm_ref[...], wo_ref[WIDTH_A + WIDTH_B:, :]))
    x1 = x_ref[...] + mo
    x1_ref[...] = x1
    ms = jnp.mean(x1 * x1, axis=-1, keepdims=True)
    hn = x1 * lax.rsqrt(ms + EPS) * nw_ref[...]
    hn_ref[...] = hn.astype(hn_ref.dtype)

    h_hi = hn.astype(BF16)
    h_lo = (hn - h_hi.astype(F32)).astype(BF16)
    logits = (_dot(h_hi, rwh_ref[...]) + _dot(h_lo, rwh_ref[...]) + _dot(h_hi, rwl_ref[...])
              + rb_ref[...])
    lane = lax.broadcasted_iota(jnp.int32, logits.shape, 1)
    lane_f = lane.astype(F32)
    neg = -jnp.inf
    big = 1000.0
    gmask = (lane >= ROUTER_GROUP_LANE) & (lane < ROUTER_GROUP_LANE + N_EXPERT_GROUPS)
    lg = jnp.where(gmask, logits, neg)
    gmax = lg.max(axis=-1, keepdims=True)
    g_idx = jnp.where(lg == gmax, lane_f, big).min(axis=-1, keepdims=True) - ROUTER_GROUP_LANE
    p_grp = 1.0 / jnp.where(gmask, jnp.exp(logits - gmax), 0.0).sum(axis=-1, keepdims=True)
    emask = (lane < N_EXPERTS) & ((lane >> 2).astype(F32) == g_idx)
    le = jnp.where(emask, logits, neg)
    v1 = le.max(axis=-1, keepdims=True)
    i1 = jnp.where(le == v1, lane_f, big).min(axis=-1, keepdims=True)
    le2 = jnp.where(lane_f == i1, neg, le)
    v2 = le2.max(axis=-1, keepdims=True)
    i2 = jnp.where(le2 == v2, lane_f, big).min(axis=-1, keepdims=True)
    e2 = jnp.exp(v2 - v1)
    den = 1.0 + e2
    gate_ref[...] = (jnp.where(lane_f == i1, (1.0 / den) * p_grp, 0.0)
                     + jnp.where(lane_f == i2, (e2 / den) * p_grp, 0.0))


def _out_proj(x, att, gla, ssm, lw):
    t = x.shape[0]

    def rows(width):
        return pl.BlockSpec((OUT_TM, width), lambda i: (i, 0))

    def const(shape):
        return pl.BlockSpec(shape, lambda i: (0, 0))

    return pl.pallas_call(
        _out_proj_kernel,
        grid=(t // OUT_TM,),
        in_specs=[rows(D_MODEL), rows(WIDTH_A), rows(WIDTH_B), rows(WIDTH_C),
                  const((D_MODEL, D_MODEL)), const((1, D_MODEL)),
                  const((D_MODEL, 128)), const((D_MODEL, 128)), const((1, 128))],
        out_specs=[rows(D_MODEL), rows(D_MODEL), rows(128)],
        out_shape=[jax.ShapeDtypeStruct((t, D_MODEL), F32),
                   jax.ShapeDtypeStruct((t, D_MODEL), BF16),
                   jax.ShapeDtypeStruct((t, 128), F32)],
        compiler_params=_params("parallel"),
    )(x, att, gla, ssm, lw["wo"], lw["n2w"], lw["rw_hi"], lw["rw_lo"], lw["rb"])


MOE_TM = 1024


def _moe_kernel(x_ref, hn_ref, gate_ref, wg_ref, wu_ref, wd_ref, fw_ref, o_ref, acc_ref, *, final_norm):
    e = pl.program_id(1)

    @pl.when(e == 0)
    def _():
        acc_ref[...] = jnp.zeros_like(acc_ref)

    hn = hn_ref[...]
    gate = gate_ref[...]
    lane = lax.broadcasted_iota(jnp.int32, gate.shape, 1)
    ge = jnp.where(lane == e, gate, 0.0).sum(axis=-1, keepdims=True)
    hid = _silu(_dot(hn, wg_ref[0])) * _dot(hn, wu_ref[0]) * ge
    acc_ref[...] += _dot(hid.astype(BF16), wd_ref[0])

    @pl.when(e == N_EXPERTS - 1)
    def _():
        y = x_ref[...] + acc_ref[...]
        if final_norm:
            ms = jnp.mean(y * y, axis=-1, keepdims=True)
            y = y * lax.rsqrt(ms + EPS) * fw_ref[...]
        o_ref[...] = y


def _moe(x, hn, gate, lw, fw, final_norm):
    t = x.shape[0]
    tm = min(MOE_TM, t)

    def rows(width):
        return pl.BlockSpec((tm, width), lambda i, e: (i, 0))

    return pl.pallas_call(
        functools.partial(_moe_kernel, final_norm=final_norm),
        grid=(t // tm, N_EXPERTS),
        in_specs=[rows(D_MODEL), rows(D_MODEL), rows(128),
                  pl.BlockSpec((1, D_MODEL, D_EXPERT), lambda i, e: (e, 0, 0)),
                  pl.BlockSpec((1, D_MODEL, D_EXPERT), lambda i, e: (e, 0, 0)),
                  pl.BlockSpec((1, D_EXPERT, D_MODEL), lambda i, e: (e, 0, 0)),
                  pl.BlockSpec((1, D_MODEL), lambda i, e: (0, 0))],
        out_specs=rows(D_MODEL),
        out_shape=jax.ShapeDtypeStruct((t, D_MODEL), F32),
        scratch_shapes=[pltpu.VMEM((tm, D_MODEL), F32)],
        compiler_params=_params("parallel", "arbitrary"),
    )(x, hn, gate, lw["wg"], lw["wu"], lw["wd"], fw)


def _lane_place(vec, start, width=128):
    return jnp.zeros((1, width), F32).at[0, start:start + vec.shape[0]].set(vec.astype(F32))


REL_PAD = 704


def _toeplitz(v, rows, cols):
    n = v.shape[-1]
    tiled = jnp.tile(v, (1,) * (v.ndim - 1) + (rows,))[..., :rows * (n - 1)]
    return tiled.reshape(v.shape[:-1] + (rows, n - 1))[..., :cols]


def _band_bias(table):
    t = table.astype(F32)
    ext = jnp.concatenate([jnp.repeat(t[:, :1], REL_PAD, axis=1), t,
                           jnp.repeat(t[:, -1:], REL_PAD, axis=1)], axis=1)
    extr = ext[:, ::-1]
    top = ext.shape[1] - 1 - (REL_CLIP + REL_PAD)

    def band(off, nq, nk):
        n = nq + nk
        v = jnp.concatenate([extr[:, top - off:top - off + nk + 1],
                             extr[:, top - off - (nq - 1):top - off]], axis=1)
        assert v.shape[1] == n
        return _toeplitz(v, nq, nk)

    sample = band(WINDOW, CHUNK, BAND)
    nk = ATT_NKB * ATT_KB
    kc = (jnp.arange(nk) // CHUNK)[:, None]
    qc = (jnp.arange(ATT_QB) // CHUNK)[None, :]
    valid = [kc <= qc, kc <= qc + ATT_QB // CHUNK, (kc >= qc) & (kc <= qc + N_PAST_CHUNKS)]
    prompt = jnp.stack([
        jnp.where(valid[w][None], jnp.swapaxes(band(w * ATT_QB, ATT_QB, nk), 1, 2), -1e30)
        for w in range(3)])
    return sample, prompt


def _layer_weights(i, norm1_w, w_in, rel_bias_table, gla_w_gate_up, gla_b_gate, gla_norm_w,
                   ssm_conv_w, ssm_conv_b, ssm_dt_bias, ssm_a_log, ssm_d, ssm_norm_w, w_out,
                   norm2_w, router_group_w, router_group_b, router_expert_w, router_expert_b,
                   exp_w_gate, exp_w_up, exp_w_down):
    wi = w_in[i]
    o = [0, 512, 1024, 1536, 1664, 1792, 2048, 2064, 2320, 2576, 3088, 3092]
    aq, ak, av, bq, bk, bv, bglr, bog, cz, cxbc, cdt = [wi[:, o[j]:o[j + 1]] for j in range(11)]
    pad = jnp.zeros((D_MODEL, 128 - GATE_RANK - H_C), wi.dtype)
    w_perm = jnp.concatenate([aq * (DH_A ** -0.5), ak, av, bq, bk, bv, bog, cz, cxbc, bglr, cdt, pad],
                             axis=1)
    bias_s, bias_p = _band_bias(rel_bias_table[i])
    rw = jnp.concatenate([router_expert_w[i], router_group_w[i],
                          jnp.zeros((D_MODEL, 128 - N_EXPERTS - N_EXPERT_GROUPS), F32)], axis=1)
    rw_hi = rw.astype(BF16)
    return dict(
        n1w=norm1_w[i][None, :],
        w_in=w_perm.astype(BF16),
        w_vt=av.T.astype(BF16),
        bias_s=bias_s,
        bias_p=bias_p,
        gup=jnp.zeros((128, 128), F32).at[:GATE_RANK, :].set(gla_w_gate_up[i]).astype(BF16),
        gb=gla_b_gate[i][None, :].astype(F32),
        gnw=jnp.tile(gla_norm_w[i], H_B)[None, :].astype(F32),
        cw=ssm_conv_w[i].astype(F32),
        cb=ssm_conv_b[i][None, :].astype(F32),
        dtb=_lane_place(ssm_dt_bias[i], MISC_DT_LANE),
        alog=_lane_place(ssm_a_log[i], MISC_DT_LANE),
        dsk=jnp.repeat(ssm_d[i].astype(F32), P_C)[None, :],
        snw=ssm_norm_w[i][None, :].astype(F32),
        wo=w_out[i].astype(BF16),
        n2w=norm2_w[i][None, :],
        rw_hi=rw_hi,
        rw_lo=(rw - rw_hi.astype(F32)).astype(BF16),
        rb=jnp.concatenate([router_expert_b[i], router_group_b[i],
                            jnp.zeros((128 - N_EXPERTS - N_EXPERT_GROUPS,), F32)])[None, :],
        wg=exp_w_gate[i].astype(BF16),
        wu=exp_w_up[i].astype(BF16),
        wd=exp_w_down[i].astype(BF16),
    )


def _stream_layer(x, bsz, lw, cache_k, cache_v, gla0, conv0, ssm0, fw, final_norm):
    l = x.shape[0] // bsz
    if cache_k is None:
        qk, kv, rest, vt = _in_proj(x, lw["n1w"], lw["w_in"], lw["w_vt"])
        att = _attn_prompt(qk.reshape(bsz, l, 2 * WIDTH_A),
                           vt.reshape(bsz, l // ATT_KB, WIDTH_A, ATT_KB), lw["bias_p"])
    else:
        qk, kv, rest = _in_proj(x, lw["n1w"], lw["w_in"], None)
        att = _attn_sample(qk.reshape(bsz, l, 2 * WIDTH_A), kv.reshape(bsz, l, 2 * WIDTH_A),
                           cache_k, cache_v, lw["bias_s"])
    kv = kv.reshape(bsz, l, 2 * WIDTH_A)
    gla_o, ssm_o, gla_s, conv_s, ssm_s = _scan(rest.reshape(bsz, l, REST_COLS), lw, gla0, conv0, ssm0)
    t = bsz * l
    x1, hn, gate = _out_proj(x, att.reshape(t, WIDTH_A), gla_o.reshape(t, WIDTH_B),
                             ssm_o.reshape(t, WIDTH_C), lw)
    x2 = _moe(x1, hn, gate, lw, fw, final_norm)
    keep = min(WINDOW, l)
    k_keep = kv[:, l - keep:, :WIDTH_A].reshape(bsz, keep, H_A, DH_A)
    v_keep = kv[:, l - keep:, WIDTH_A:].reshape(bsz, keep, H_A, DH_A)
    gla_state = gla_s.reshape(bsz, DV_B, H_B, DK_B).transpose(0, 2, 3, 1)
    conv_state = conv_s[:, 8 - (CONV_W - 1):, :]
    return x2, k_keep, v_keep, gla_state, conv_state, ssm_s


def kernel(x_prompt, x_sample, cache_k_a, cache_v_a, state_gla, state_conv, state_ssm, norm1_w, w_in, rel_bias_table, gla_w_gate_up, gla_b_gate, gla_norm_w, ssm_conv_w, ssm_conv_b, ssm_dt_bias, ssm_a_log, ssm_d, ssm_norm_w, w_out, norm2_w, router_group_w, router_group_b, router_expert_w, router_expert_b, exp_w_gate, exp_w_up, exp_w_down, final_norm_w):
    bp, sp, _ = x_prompt.shape
    bs, ss, _ = x_sample.shape
    xp = x_prompt.reshape(bp * sp, D_MODEL)
    xs = x_sample.reshape(bs * ss, D_MODEL)
    fw = final_norm_w[None, :].astype(F32)
    outs_p, outs_s = [], []
    for i in range(DEPTH):
        lw = _layer_weights(i, norm1_w, w_in, rel_bias_table, gla_w_gate_up, gla_b_gate, gla_norm_w,
                            ssm_conv_w, ssm_conv_b, ssm_dt_bias, ssm_a_log, ssm_d, ssm_norm_w, w_out,
                            norm2_w, router_group_w, router_group_b, router_expert_w, router_expert_b,
                            exp_w_gate, exp_w_up, exp_w_down)
        last = i == DEPTH - 1
        xp, *sp_out = _stream_layer(
            xp, bp, lw, None, None,
            jnp.zeros((bp, DV_B, H_B * DK_B), F32),
            jnp.zeros((bp, 8, CONV_CH), F32),
            jnp.zeros((bp, H_C, P_C, D_STATE), F32), fw, last)
        outs_p.append(sp_out)
        att_len = cache_k_a.shape[2]
        gla0 = state_gla[i].astype(F32).transpose(0, 3, 1, 2).reshape(bs, DV_B, H_B * DK_B)
        conv0 = jnp.pad(state_conv[i].astype(F32), ((0, 0), (8 - (CONV_W - 1), 0), (0, 0)))
        xs, *ss_out = _stream_layer(
            xs, bs, lw,
            cache_k_a[i].reshape(bs, att_len, WIDTH_A), cache_v_a[i].reshape(bs, att_len, WIDTH_A),
            gla0, conv0, state_ssm[i].astype(F32), fw, last)
        outs_s.append(ss_out)
    stack = lambda outs, j: jnp.stack([o[j] for o in outs])
    return (xp.reshape(bp, sp, D_MODEL), xs.reshape(bs, ss, D_MODEL),
            stack(outs_p, 0), stack(outs_p, 1), stack(outs_p, 2), stack(outs_p, 3), stack(outs_p, 4),
            stack(outs_s, 0), stack(outs_s, 1), stack(outs_s, 2), stack(outs_s, 3), stack(outs_s, 4))
```

```python
import functools

import jax
import jax.numpy as jnp
from jax import lax
from jax.experimental import pallas as pl
from jax.experimental.pallas import tpu as pltpu

F32 = jnp.float32
BF16 = jnp.bfloat16

D_MODEL = 1024
DEPTH = 2
EPS = 1e-6
CHUNK = 64
N_PAST_CHUNKS = 8
BAND = (N_PAST_CHUNKS + 1) * CHUNK
WINDOW = N_PAST_CHUNKS * CHUNK
H_A, DH_A, WIDTH_A = 8, 64, 512
REL_CLIP = 128
H_B, DK_B, DV_B, WIDTH_B = 4, 32, 64, 256
GATE_RANK = 16
GATE_TAU = 16.0
H_C, P_C, WIDTH_C = 4, 64, 256
N_GROUPS_C = 2
D_STATE = 64
CONV_W = 4
CONV_CH = 512
N_EXPERT_GROUPS = 4
EXPERTS_PER_GROUP = 4
N_EXPERTS = 16
D_EXPERT = 256

PROJ_COLS = 3200
MISC_DT_LANE = GATE_RANK
ROUTER_GROUP_LANE = N_EXPERTS

SEQ_GROUP = 8
SEQ_UNROLL = 8
VMEM_LIMIT_BYTES = 56 * 1024 * 1024

NT_DIMS = (((1,), (1,)), ((), ()))
TN_DIMS = (((0,), (0,)), ((), ()))


def _params(*sem):
    return pltpu.CompilerParams(dimension_semantics=sem, vmem_limit_bytes=VMEM_LIMIT_BYTES)


def _dot(a, b):
    return jnp.dot(a, b, preferred_element_type=F32)


def _dot_nt(a, b):
    return lax.dot_general(a, b, NT_DIMS, preferred_element_type=F32)


def _dot_tn(a, b):
    return lax.dot_general(a, b, TN_DIMS, preferred_element_type=F32)


def _split2(a):
    hi = a.astype(BF16)
    return hi, (a - hi.astype(F32)).astype(BF16)


def _dot01_left(m01, a):
    hi, lo = _split2(a)
    return _dot(m01, hi) + _dot(m01, lo)


def _dot01_right(a, m01):
    hi, lo = _split2(a)
    return _dot(hi, m01) + _dot(lo, m01)


def _round_robin(gens):
    results = [None] * len(gens)
    live = list(range(len(gens)))
    while live:
        for idx in list(live):
            try:
                next(gens[idx])
            except StopIteration as stop:
                results[idx] = stop.value
                live.remove(idx)
    return results


def _silu(x):
    return x * jax.nn.sigmoid(x)


def _softplus(x):
    return jnp.maximum(x, 0.0) + jnp.log1p(jnp.exp(-jnp.abs(x)))


IN_TM = 256
QKV_COLS = 3 * WIDTH_A
REST_COLS = PROJ_COLS - QKV_COLS
REST_CHUNKS = ((0, 512), (512, 1024), (1024, REST_COLS))


def _in_proj_kernel(x_ref, nw_ref, w_ref, *refs, with_vt):
    if with_vt:
        wvt_ref, qk_ref, kv_ref, rest_ref, vt_ref = refs
    else:
        qk_ref, kv_ref, rest_ref = refs
    x = x_ref[...]
    ms = jnp.mean(x * x, axis=-1, keepdims=True)
    xn = (x * lax.rsqrt(ms + EPS) * nw_ref[...]).astype(BF16)
    qk_ref[:, 0:WIDTH_A] = _dot(xn, w_ref[:, 0:WIDTH_A]).astype(BF16)
    k = _dot(xn, w_ref[:, WIDTH_A:2 * WIDTH_A])
    qk_ref[:, WIDTH_A:] = k.astype(BF16)
    kv_ref[:, 0:WIDTH_A] = k
    kv_ref[:, WIDTH_A:] = _dot(xn, w_ref[:, 2 * WIDTH_A:QKV_COLS])
    for lo, hi in REST_CHUNKS:
        rest_ref[:, lo:hi] = _dot(xn, w_ref[:, QKV_COLS + lo:QKV_COLS + hi])
    if with_vt:
        vt_ref[0] = _dot_nt(wvt_ref[...], xn).astype(BF16)


def _in_proj(x, nw, w, wvt):
    t = x.shape[0]
    with_vt = wvt is not None

    def rows(width):
        return pl.BlockSpec((IN_TM, width), lambda i: (i, 0))

    in_specs = [rows(D_MODEL), pl.BlockSpec((1, D_MODEL), lambda i: (0, 0)),
                pl.BlockSpec((D_MODEL, PROJ_COLS), lambda i: (0, 0))]
    out_specs = [rows(2 * WIDTH_A), rows(2 * WIDTH_A), rows(REST_COLS)]
    out_shape = [jax.ShapeDtypeStruct((t, 2 * WIDTH_A), BF16),
                 jax.ShapeDtypeStruct((t, 2 * WIDTH_A), F32),
                 jax.ShapeDtypeStruct((t, REST_COLS), F32)]
    args = [x, nw, w]
    if with_vt:
        in_specs.append(pl.BlockSpec((WIDTH_A, D_MODEL), lambda i: (0, 0)))
        out_specs.append(pl.BlockSpec((1, WIDTH_A, IN_TM), lambda i: (i, 0, 0)))
        out_shape.append(jax.ShapeDtypeStruct((t // IN_TM, WIDTH_A, IN_TM), BF16))
        args.append(wvt)
    return pl.pallas_call(
        functools.partial(_in_proj_kernel, with_vt=with_vt),
        grid=(t // IN_TM,),
        in_specs=in_specs, out_specs=out_specs, out_shape=out_shape,
        compiler_params=_params("parallel"),
    )(*args)


ATT_QB = 4 * CHUNK
ATT_KB = IN_TM
ATT_NKB = (ATT_QB + WINDOW) // ATT_KB


def _attn_prompt_kernel(q_ref, k_ref, vt_ref, bias_ref, o_ref, ot_ref):
    blk = pl.program_id(1)
    sb = jnp.maximum(blk - WINDOW // ATT_KB, 0)
    kstart = pl.multiple_of(sb * ATT_KB, ATT_KB)
    lane = lax.broadcasted_iota(jnp.int32, (ATT_QB, 128), 1)
    for j in range(H_A // 2):
        pair = slice(128 * j, 128 * (j + 1))
        kp = k_ref[0, pl.ds(kstart, ATT_NKB * ATT_KB), pair]
        qp = q_ref[0, :, pair]
        for hh in range(2):
            h = 2 * j + hh
            qz = jnp.where((lane >= DH_A) == bool(hh), qp, jnp.zeros_like(qp))
            s = [_dot_nt(kp[ATT_KB * i:ATT_KB * (i + 1)], qz)
                 + bias_ref[0, h, ATT_KB * i:ATT_KB * (i + 1), :] for i in range(ATT_NKB)]
            m = s[0].max(axis=0, keepdims=True)
            for si in s[1:]:
                m = jnp.maximum(m, si.max(axis=0, keepdims=True))
            l = None
            ot = None
            for i, si in enumerate(s):
                p = jnp.exp(si - m)
                li = p.sum(axis=0, keepdims=True)
                oi = _dot(vt_ref[0, sb + i, DH_A * h:DH_A * (h + 1), :], p.astype(BF16))
                l = li if l is None else l + li
                ot = oi if ot is None else ot + oi
            ot_ref[DH_A * h:DH_A * (h + 1), :] = ot / l
    o_ref[0] = ot_ref[...].T.astype(o_ref.dtype)


def _attn_prompt(qk, vt, bias_t):
    bsz, s, _ = qk.shape
    nkb = s // ATT_KB
    return pl.pallas_call(
        _attn_prompt_kernel,
        grid=(bsz, s // ATT_QB),
        in_specs=[pl.BlockSpec((1, ATT_QB, WIDTH_A), lambda b, c: (b, c, 0)),
                  pl.BlockSpec((1, s, WIDTH_A), lambda b, c: (b, 0, 1)),
                  pl.BlockSpec((1, nkb, WIDTH_A, ATT_KB), lambda b, c: (b, 0, 0, 0)),
                  pl.BlockSpec((1, H_A, ATT_NKB * ATT_KB, ATT_QB),
                               lambda b, c: (jnp.minimum(c, 2), 0, 0, 0))],
        out_specs=pl.BlockSpec((1, ATT_QB, WIDTH_A), lambda b, c: (b, c, 0)),
        out_shape=jax.ShapeDtypeStruct((bsz, s, WIDTH_A), BF16),
        scratch_shapes=[pltpu.VMEM((WIDTH_A, ATT_QB), F32)],
        compiler_params=_params("parallel", "arbitrary"),
    )(qk, qk, vt, bias_t)


def _attn_sample_kernel(q_ref, kv_ref, ck_ref, cv_ref, bias_ref, o_ref):
    for h in range(H_A):
        sl = slice(DH_A * h, DH_A * (h + 1))
        vsl = slice(WIDTH_A + DH_A * h, WIDTH_A + DH_A * (h + 1))
        qh = q_ref[0, :, sl]
        s1 = _dot_nt(qh, ck_ref[0, :, sl].astype(BF16)) + bias_ref[h, :, 0:WINDOW]
        s2 = _dot_nt(qh, kv_ref[0, :, sl].astype(BF16)) + bias_ref[h, :, WINDOW:BAND]
        m = jnp.maximum(s1.max(axis=-1, keepdims=True), s2.max(axis=-1, keepdims=True))
        p1 = jnp.exp(s1 - m)
        p2 = jnp.exp(s2 - m)
        l = p1.sum(axis=-1, keepdims=True) + p2.sum(axis=-1, keepdims=True)
        o = (_dot(p1.astype(BF16), cv_ref[0, :, sl].astype(BF16))
             + _dot(p2.astype(BF16), kv_ref[0, :, vsl].astype(BF16)))
        o_ref[0, :, sl] = (o / l).astype(o_ref.dtype)


def _attn_sample(qk, kv, cache_k, cache_v, bias):
    bsz = qk.shape[0]
    return pl.pallas_call(
        _attn_sample_kernel,
        grid=(bsz,),
        in_specs=[pl.BlockSpec((1, CHUNK, WIDTH_A), lambda b: (b, 0, 0)),
                  pl.BlockSpec((1, CHUNK, 2 * WIDTH_A), lambda b: (b, 0, 0)),
                  pl.BlockSpec((1, WINDOW, WIDTH_A), lambda b: (b, 0, 0)),
                  pl.BlockSpec((1, WINDOW, WIDTH_A), lambda b: (b, 0, 0)),
                  pl.BlockSpec((H_A, CHUNK, BAND), lambda b: (0, 0, 0))],
        out_specs=pl.BlockSpec((1, CHUNK, WIDTH_A), lambda b: (b, 0, 0)),
        out_shape=jax.ShapeDtypeStruct((bsz, CHUNK, WIDTH_A), BF16),
        compiler_params=_params("parallel"),
    )(qk, kv, cache_k, cache_v, bias)


def _scan_kernel(bqk_ref, bv_ref, bog_ref, cz_ref, cx_ref, misc_ref,
                 gup_ref, gb_ref, gnw_ref, cw_ref, cb_ref, dtb_ref, alog_ref, dsk_ref, snw_ref,
                 gla0_ref, conv0_ref, ssm0_ref,
                 gla_o_ref, ssm_o_ref, gla_s_ref, conv_s_ref, ssm_s_ref,
                 xpad_ref):
    c = pl.program_id(1)
    nseq = bqk_ref.shape[0]

    @pl.when(c == 0)
    def _():
        gla_s_ref[...] = gla0_ref[...]
        conv_s_ref[...] = conv0_ref[...]
        ssm_s_ref[...] = ssm0_ref[...]

    r64 = lax.broadcasted_iota(jnp.int32, (CHUNK, CHUNK), 0)
    c64 = lax.broadcasted_iota(jnp.int32, (CHUNK, CHUNK), 1)
    tri = c64 <= r64
    tri_bf = jnp.where(tri, 1.0, 0.0).astype(BF16)
    ones_bf = jnp.ones((CHUNK, CHUNK), BF16)
    row_x = lax.broadcasted_iota(jnp.int32, (CHUNK, WIDTH_C), 0)
    s_x = lax.broadcasted_iota(jnp.int32, (CHUNK, WIDTH_C), 1) & (CHUNK - 1)
    causal_x = s_x <= row_x
    upper_x = row_x <= s_x
    er = lax.broadcasted_iota(jnp.int32, (128, WIDTH_C), 0)
    el = lax.broadcasted_iota(jnp.int32, (128, WIDTH_C), 1)
    expand_bf = jnp.where(er == MISC_DT_LANE + (el >> 6), 1.0, 0.0).astype(BF16)
    br = lax.broadcasted_iota(jnp.int32, (WIDTH_C, WIDTH_C), 0)
    bl = lax.broadcasted_iota(jnp.int32, (WIDTH_C, WIDTH_C), 1)
    blk64_bf = jnp.where((br >> 6) == (bl >> 6), 1.0, 0.0).astype(BF16)
    blk128_bf = jnp.where((br >> 7) == (bl >> 7), 1.0, 0.0).astype(BF16)

    gup = gup_ref[...]
    gb = gb_ref[...]
    gnw = gnw_ref[...]
    cb = cb_ref[...]
    dtb = dtb_ref[...]
    neg_a = -jnp.exp(alog_ref[...])
    dsk = dsk_ref[...]
    snw = snw_ref[...]

    def load_seq(b):
        return dict(misc=misc_ref[b], qk=bqk_ref[b], v=bv_ref[b], bog=bog_ref[b], cz=cz_ref[b],
                    cx=cx_ref[b], gla=gla_s_ref[b], conv=conv_s_ref[b],
                    ssm=[ssm_s_ref[b, h] for h in range(H_C)])

    def store_seq(b, out):
        gla_o_ref[b] = out["gla_o"]
        ssm_o_ref[b] = out["ssm_o"]
        gla_s_ref[b] = out["gla"]
        conv_s_ref[b] = out["conv"]
        for h in range(H_C):
            ssm_s_ref[b, h] = out["ssm"][h]

    def compute_seq(inp, slot):
        out = {}
        misc = inp["misc"]

        z = _dot(misc.astype(BF16), gup) + gb
        g = (jnp.minimum(z, 0.0) - jnp.log1p(jnp.exp(-jnp.abs(z)))) * (1.0 / GATE_TAU)
        bcum = _dot01_left(tri_bf, g)
        yield
        q = inp["qk"][:, :H_B * DK_B]
        k = inp["qk"][:, H_B * DK_B:]
        blast = bcum[CHUNK - 1:CHUNK, :]
        qb = (q * (DK_B ** -0.5) * jnp.exp(bcum)).astype(BF16)
        kb = (k * jnp.exp(-bcum)).astype(BF16)
        kl = (k * jnp.exp(blast - bcum)).astype(BF16)
        vb = inp["v"].astype(BF16)
        st = inp["gla"]
        st_bf = st.astype(BF16)
        yield
        o_parts, s_parts = [], []
        for h in range(H_B):
            ks = slice(DK_B * h, DK_B * (h + 1))
            vs = slice(DV_B * h, DV_B * (h + 1))
            att = jnp.where(tri, _dot_nt(qb[:, ks], kb[:, ks]), 0.0).astype(BF16)
            o_parts.append(_dot(att, vb[:, vs]) + _dot_nt(qb[:, ks], st_bf[:, ks]))
            s_parts.append(_dot_tn(vb[:, vs], kl[:, ks]))
            yield
        o = jnp.concatenate(o_parts, axis=1)
        out["gla"] = jnp.exp(blast) * st + jnp.concatenate(s_parts, axis=1)
        msq = _dot01_right(o * o, blk64_bf) * (1.0 / DV_B)
        yield
        go = o * lax.rsqrt(msq + EPS) * gnw * _silu(inp["bog"])
        out["gla_o"] = go.astype(gla_o_ref.dtype)
        yield

        cx = inp["cx"]
        xpad_ref[slot, 0:8, :] = inp["conv"]
        xpad_ref[slot, 8:8 + CHUNK, :] = cx
        conv = cb + cx * cw_ref[CONV_W - 1:CONV_W, :]
        for i in range(CONV_W - 1):
            off = 8 - (CONV_W - 1) + i
            conv = conv + xpad_ref[slot, off:off + CHUNK, :] * cw_ref[i:i + 1, :]
        out["conv"] = cx[CHUNK - 8:, :]
        conv = _silu(conv)
        sx = conv[:, :WIDTH_C]
        sb_bf = conv[:, WIDTH_C:WIDTH_C + N_GROUPS_C * D_STATE].astype(BF16)
        sc_bf = conv[:, WIDTH_C + N_GROUPS_C * D_STATE:].astype(BF16)
        yield

        dtv = _softplus(misc + dtb)
        dt_x = _dot01_right(dtv, expand_bf)
        yield
        a_x = neg_a * dt_x
        cs_x = _dot01_left(tri_bf, a_x)
        cs_row = _dot01_left(ones_bf, jnp.where(upper_x, a_x, 0.0))
        yield
        decay = jnp.exp(jnp.where(causal_x, cs_x - cs_row, -jnp.inf))
        cb_parts = []
        for gidx in range(N_GROUPS_C):
            ns = slice(D_STATE * gidx, D_STATE * (gidx + 1))
            cbm = _dot_nt(sc_bf[:, ns], sb_bf[:, ns])
            cb_parts += [cbm] * (H_C // N_GROUPS_C)
            yield
        scores = (jnp.concatenate(cb_parts, axis=1) * decay).astype(BF16)
        xdt = (sx * dt_x).astype(BF16)
        ecs = jnp.exp(cs_x)
        cs_last = cs_x[CHUNK - 1:CHUNK, :]
        xw = (sx * (jnp.exp(cs_last - cs_x) * dt_x)).astype(BF16)
        e_last = jnp.exp(cs_last)
        yield
        y_parts = []
        out["ssm"] = []
        for h in range(H_C):
            ps = slice(P_C * h, P_C * (h + 1))
            gi = h // (H_C // N_GROUPS_C)
            ns = slice(D_STATE * gi, D_STATE * (gi + 1))
            hst = inp["ssm"][h]
            y_parts.append(_dot(scores[:, ps], xdt[:, ps])
                           + _dot_nt(sc_bf[:, ns], hst.astype(BF16)) * ecs[:, ps])
            out["ssm"].append(e_last[:, ps] * hst + _dot_tn(xw[:, ps], sb_bf[:, ns]))
            yield
        y = jnp.concatenate(y_parts, axis=1) + dsk * sx
        y = y * _silu(inp["cz"])
        ymsq = _dot01_right(y * y, blk128_bf) * (1.0 / (WIDTH_C // N_GROUPS_C))
        yield
        out["ssm_o"] = (y * lax.rsqrt(ymsq + EPS) * snw).astype(ssm_o_ref.dtype)
        return out

    def group_body(i, carry):
        seqs = [i * SEQ_UNROLL + u for u in range(SEQ_UNROLL)]
        inputs = [load_seq(b) for b in seqs]
        outputs = _round_robin([compute_seq(inp, slot) for slot, inp in enumerate(inputs)])
        for b, out in zip(seqs, outputs):
            store_seq(b, out)
        return carry

    lax.fori_loop(0, nseq // SEQ_UNROLL, group_body, 0)


def _scan(proj, lw, gla0, conv0, ssm0):
    bsz, l, _ = proj.shape
    nc = l // CHUNK
    sg = min(SEQ_GROUP, bsz)
    ng = bsz // sg

    def col(width, idx):
        return pl.BlockSpec((sg, CHUNK, width), lambda g, c: (g, c, idx))

    def const(shape):
        return pl.BlockSpec(shape, lambda g, c: (0,) * len(shape))

    def state(shape):
        return pl.BlockSpec((sg,) + shape, lambda g, c: (g,) + (0,) * len(shape))

    gla_shape = (DV_B, H_B * DK_B)
    conv_shape = (8, CONV_CH)
    ssm_shape = (H_C, P_C, D_STATE)
    return pl.pallas_call(
        _scan_kernel,
        grid=(ng, nc),
        in_specs=[col(256, 0), col(256, 1), col(256, 2), col(256, 3), col(512, 2), col(128, 12),
                  const((128, 128)), const((1, 128)), const((1, WIDTH_B)),
                  const((CONV_W, CONV_CH)), const((1, CONV_CH)),
                  const((1, 128)), const((1, WIDTH_C)), const((1, WIDTH_C)), const((1, WIDTH_C)),
                  state(gla_shape), state(conv_shape), state(ssm_shape)],
        out_specs=[pl.BlockSpec((sg, CHUNK, WIDTH_B), lambda g, c: (g, c, 0)),
                   pl.BlockSpec((sg, CHUNK, WIDTH_C), lambda g, c: (g, c, 0)),
                   state(gla_shape), state(conv_shape), state(ssm_shape)],
        out_shape=[jax.ShapeDtypeStruct((bsz, l, WIDTH_B), BF16),
                   jax.ShapeDtypeStruct((bsz, l, WIDTH_C), BF16),
                   jax.ShapeDtypeStruct((bsz,) + gla_shape, F32),
                   jax.ShapeDtypeStruct((bsz,) + conv_shape, F32),
                   jax.ShapeDtypeStruct((bsz,) + ssm_shape, F32)],
        scratch_shapes=[pltpu.VMEM((SEQ_UNROLL, 8 + CHUNK, CONV_CH), F32)],
        compiler_params=_params("parallel", "arbitrary"),
    )(proj, proj, proj, proj, proj, proj,
      lw["gup"], lw["gb"], lw["gnw"], lw["cw"], lw["cb"], lw["dtb"], lw["alog"], lw["dsk"], lw["snw"],
      gla0, conv0, ssm0)


OUT_TM = 256


def _out_proj_kernel(x_ref, att_ref, gla_ref, ssm_ref, wo_ref, nw_ref, rwh_ref, rwl_ref, rb_ref,
                     x1_ref, hn_ref, gate_ref):
    mo = (_dot(att_ref[...], wo_ref[0:WIDTH_A, :])
          + _dot(gla_ref[...], wo_ref[WIDTH_A:WIDTH_A + WIDTH_B, :])
          + _dot(ssm_ref[...], wo_ref[WIDTH_A + WIDTH_B:, :]))
    x1 = x_ref[...] + mo
    x1_ref[...] = x1
    ms = jnp.mean(x1 * x1, axis=-1, keepdims=True)
    hn = x1 * lax.rsqrt(ms + EPS) * nw_ref[...]
    hn_ref[...] = hn.astype(hn_ref.dtype)

    h_hi = hn.astype(BF16)
    h_lo = (hn - h_hi.astype(F32)).astype(BF16)
    logits = (_dot(h_hi, rwh_ref[...]) + _dot(h_lo, rwh_ref[...]) + _dot(h_hi, rwl_ref[...])
              + rb_ref[...])
    lane = lax.broadcasted_iota(jnp.int32, logits.shape, 1)
    lane_f = lane.astype(F32)
    neg = -jnp.inf
    big = 1000.0
    gmask = (lane >= ROUTER_GROUP_LANE) & (lane < ROUTER_GROUP_LANE + N_EXPERT_GROUPS)
    lg = jnp.where(gmask, logits, neg)
    gmax = lg.max(axis=-1, keepdims=True)
    g_idx = jnp.where(lg == gmax, lane_f, big).min(axis=-1, keepdims=True) - ROUTER_GROUP_LANE
    p_grp = 1.0 / jnp.where(gmask, jnp.exp(logits - gmax), 0.0).sum(axis=-1, keepdims=True)
    emask = (lane < N_EXPERTS) & ((lane >> 2).astype(F32) == g_idx)
    le = jnp.where(emask, logits, neg)
    v1 = le.max(axis=-1, keepdims=True)
    i1 = jnp.where(le == v1, lane_f, big).min(axis=-1, keepdims=True)
    le2 = jnp.where(lane_f == i1, neg, le)
    v2 = le2.max(axis=-1, keepdims=True)
    i2 = jnp.where(le2 == v2, lane_f, big).min(axis=-1, keepdims=True)
    e2 = jnp.exp(v2 - v1)
    den = 1.0 + e2
    gate_ref[...] = (jnp.where(lane_f == i1, (1.0 / den) * p_grp, 0.0)
                     + jnp.where(lane_f == i2, (e2 / den) * p_grp, 0.0))


def _out_proj(x, att, gla, ssm, lw):
    t = x.shape[0]

    def rows(width):
        return pl.BlockSpec((OUT_TM, width), lambda i: (i, 0))

    def const(shape):
        return pl.BlockSpec(shape, lambda i: (0, 0))

    return pl.pallas_call(
        _out_proj_kernel,
        grid=(t // OUT_TM,),
        in_specs=[rows(D_MODEL), rows(WIDTH_A), rows(WIDTH_B), rows(WIDTH_C),
                  const((D_MODEL, D_MODEL)), const((1, D_MODEL)),
                  const((D_MODEL, 128)), const((D_MODEL, 128)), const((1, 128))],
        out_specs=[rows(D_MODEL), rows(D_MODEL), rows(128)],
        out_shape=[jax.ShapeDtypeStruct((t, D_MODEL), F32),
                   jax.ShapeDtypeStruct((t, D_MODEL), BF16),
                   jax.ShapeDtypeStruct((t, 128), F32)],
        compiler_params=_params("parallel"),
    )(x, att, gla, ssm, lw["wo"], lw["n2w"], lw["rw_hi"], lw["rw_lo"], lw["rb"])


MOE_TM = 1024


def _moe_kernel(x_ref, hn_ref, gate_ref, wg_ref, wu_ref, wd_ref, fw_ref, o_ref, acc_ref, *, final_norm):
    e = pl.program_id(1)

    @pl.when(e == 0)
    def _():
        acc_ref[...] = jnp.zeros_like(acc_ref)

    hn = hn_ref[...]
    gate = gate_ref[...]
    lane = lax.broadcasted_iota(jnp.int32, gate.shape, 1)
    ge = jnp.where(lane == e, gate, 0.0).sum(axis=-1, keepdims=True)
    hid = _silu(_dot(hn, wg_ref[0])) * _dot(hn, wu_ref[0]) * ge
    acc_ref[...] += _dot(hid.astype(BF16), wd_ref[0])

    @pl.when(e == N_EXPERTS - 1)
    def _():
        y = x_ref[...] + acc_ref[...]
        if final_norm:
            ms = jnp.mean(y * y, axis=-1, keepdims=True)
            y = y * lax.rsqrt(ms + EPS) * fw_ref[...]
        o_ref[...] = y


def _moe(x, hn, gate, lw, fw, final_norm):
    t = x.shape[0]
    tm = min(MOE_TM, t)

    def rows(width):
        return pl.BlockSpec((tm, width), lambda i, e: (i, 0))

    return pl.pallas_call(
        functools.partial(_moe_kernel, final_norm=final_norm),
        grid=(t // tm, N_EXPERTS),
        in_specs=[rows(D_MODEL), rows(D_MODEL), rows(128),
                  pl.BlockSpec((1, D_MODEL, D_EXPERT), lambda i, e: (e, 0, 0)),
                  pl.BlockSpec((1, D_MODEL, D_EXPERT), lambda i, e: (e, 0, 0)),
                  pl.BlockSpec((1, D_EXPERT, D_MODEL), lambda i, e: (e, 0, 0)),
                  pl.BlockSpec((1, D_MODEL), lambda i, e: (0, 0))],
        out_specs=rows(D_MODEL),
        out_shape=jax.ShapeDtypeStruct((t, D_MODEL), F32),
        scratch_shapes=[pltpu.VMEM((tm, D_MODEL), F32)],
        compiler_params=_params("parallel", "arbitrary"),
    )(x, hn, gate, lw["wg"], lw["wu"], lw["wd"], fw)


def _lane_place(vec, start, width=128):
    return jnp.zeros((1, width), F32).at[0, start:start + vec.shape[0]].set(vec.astype(F32))


REL_PAD = 704


def _toeplitz(v, rows, cols):
    n = v.shape[-1]
    tiled = jnp.tile(v, (1,) * (v.ndim - 1) + (rows,))[..., :rows * (n - 1)]
    return tiled.reshape(v.shape[:-1] + (rows, n - 1))[..., :cols]


def _band_bias(table):
    t = table.astype(F32)
    ext = jnp.concatenate([jnp.repeat(t[:, :1], REL_PAD, axis=1), t,
                           jnp.repeat(t[:, -1:], REL_PAD, axis=1)], axis=1)
    extr = ext[:, ::-1]
    top = ext.shape[1] - 1 - (REL_CLIP + REL_PAD)

    def band(off, nq, nk):
        n = nq + nk
        v = jnp.concatenate([extr[:, top - off:top - off + nk + 1],
                             extr[:, top - off - (nq - 1):top - off]], axis=1)
        assert v.shape[1] == n
        return _toeplitz(v, nq, nk)

    sample = band(WINDOW, CHUNK, BAND)
    nk = ATT_NKB * ATT_KB
    kc = (jnp.arange(nk) // CHUNK)[:, None]
    qc = (jnp.arange(ATT_QB) // CHUNK)[None, :]
    valid = [kc <= qc, kc <= qc + ATT_QB // CHUNK, (kc >= qc) & (kc <= qc + N_PAST_CHUNKS)]
    prompt = jnp.stack([
        jnp.where(valid[w][None], jnp.swapaxes(band(w * ATT_QB, ATT_QB, nk), 1, 2), -1e30)
        for w in range(3)])
    return sample, prompt


def _layer_weights(i, norm1_w, w_in, rel_bias_table, gla_w_gate_up, gla_b_gate, gla_norm_w,
                   ssm_conv_w, ssm_conv_b, ssm_dt_bias, ssm_a_log, ssm_d, ssm_norm_w, w_out,
                   norm2_w, router_group_w, router_group_b, router_expert_w, router_expert_b,
                   exp_w_gate, exp_w_up, exp_w_down):
    wi = w_in[i]
    o = [0, 512, 1024, 1536, 1664, 1792, 2048, 2064, 2320, 2576, 3088, 3092]
    aq, ak, av, bq, bk, bv, bglr, bog, cz, cxbc, cdt = [wi[:, o[j]:o[j + 1]] for j in range(11)]
    pad = jnp.zeros((D_MODEL, 128 - GATE_RANK - H_C), wi.dtype)
    w_perm = jnp.concatenate([aq * (DH_A ** -0.5), ak, av, bq, bk, bv, bog, cz, cxbc, bglr, cdt, pad],
                             axis=1)
    bias_s, bias_p = _band_bias(rel_bias_table[i])
    rw = jnp.concatenate([router_expert_w[i], router_group_w[i],
                          jnp.zeros((D_MODEL, 128 - N_EXPERTS - N_EXPERT_GROUPS), F32)], axis=1)
    rw_hi = rw.astype(BF16)
    return dict(
        n1w=norm1_w[i][None, :],
        w_in=w_perm.astype(BF16),
        w_vt=av.T.astype(BF16),
        bias_s=bias_s,
        bias_p=bias_p,
        gup=jnp.zeros((128, 128), F32).at[:GATE_RANK, :].set(gla_w_gate_up[i]).astype(BF16),
        gb=gla_b_gate[i][None, :].astype(F32),
        gnw=jnp.tile(gla_norm_w[i], H_B)[None, :].astype(F32),
        cw=ssm_conv_w[i].astype(F32),
        cb=ssm_conv_b[i][None, :].astype(F32),
        dtb=_lane_place(ssm_dt_bias[i], MISC_DT_LANE),
        alog=jnp.repeat(ssm_a_log[i].astype(F32), P_C)[None, :],
        dsk=jnp.repeat(ssm_d[i].astype(F32), P_C)[None, :],
        snw=ssm_norm_w[i][None, :].astype(F32),
        wo=w_out[i].astype(BF16),
        n2w=norm2_w[i][None, :],
        rw_hi=rw_hi,
        rw_lo=(rw - rw_hi.astype(F32)).astype(BF16),
        rb=jnp.concatenate([router_expert_b[i], router_group_b[i],
                            jnp.zeros((128 - N_EXPERTS - N_EXPERT_GROUPS,), F32)])[None, :],
        wg=exp_w_gate[i].astype(BF16),
        wu=exp_w_up[i].astype(BF16),
        wd=exp_w_down[i].astype(BF16),
    )


def _stream_layer(x, bsz, lw, cache_k, cache_v, gla0, conv0, ssm0, fw, final_norm):
    l = x.shape[0] // bsz
    if cache_k is None:
        qk, kv, rest, vt = _in_proj(x, lw["n1w"], lw["w_in"], lw["w_vt"])
        att = _attn_prompt(qk.reshape(bsz, l, 2 * WIDTH_A),
                           vt.reshape(bsz, l // ATT_KB, WIDTH_A, ATT_KB), lw["bias_p"])
    else:
        qk, kv, rest = _in_proj(x, lw["n1w"], lw["w_in"], None)
        att = _attn_sample(qk.reshape(bsz, l, 2 * WIDTH_A), kv.reshape(bsz, l, 2 * WIDTH_A),
                           cache_k, cache_v, lw["bias_s"])
    kv = kv.reshape(bsz, l, 2 * WIDTH_A)
    gla_o, ssm_o, gla_s, conv_s, ssm_s = _scan(rest.reshape(bsz, l, REST_COLS), lw, gla0, conv0, ssm0)
    t = bsz * l
    x1, hn, gate = _out_proj(x, att.reshape(t, WIDTH_A), gla_o.reshape(t, WIDTH_B),
                             ssm_o.reshape(t, WIDTH_C), lw)
    x2 = _moe(x1, hn, gate, lw, fw, final_norm)
    keep = min(WINDOW, l)
    k_keep = kv[:, l - keep:, :WIDTH_A].reshape(bsz, keep, H_A, DH_A)
    v_keep = kv[:, l - keep:, WIDTH_A:].reshape(bsz, keep, H_A, DH_A)
    gla_state = gla_s.reshape(bsz, DV_B, H_B, DK_B).transpose(0, 2, 3, 1)
    conv_state = conv_s[:, 8 - (CONV_W - 1):, :]
    return x2, k_keep, v_keep, gla_state, conv_state, ssm_s


def kernel(x_prompt, x_sample, cache_k_a, cache_v_a, state_gla, state_conv, state_ssm, norm1_w, w_in, rel_bias_table, gla_w_gate_up, gla_b_gate, gla_norm_w, ssm_conv_w, ssm_conv_b, ssm_dt_bias, ssm_a_log, ssm_d, ssm_norm_w, w_out, norm2_w, router_group_w, router_group_b, router_expert_w, router_expert_b, exp_w_gate, exp_w_up, exp_w_down, final_norm_w):
    bp, sp, _ = x_prompt.shape
    bs, ss, _ = x_sample.shape
    xp = x_prompt.reshape(bp * sp, D_MODEL)
    xs = x_sample.reshape(bs * ss, D_MODEL)
    fw = final_norm_w[None, :].astype(F32)
    outs_p, outs_s = [], []
    for i in range(DEPTH):
        lw = _layer_weights(i, norm1_w, w_in, rel_bias_table, gla_w_gate_up, gla_b_gate, gla_norm_w,
                            ssm_conv_w, ssm_conv_b, ssm_dt_bias, ssm_a_log, ssm_d, ssm_norm_w, w_out,
                            norm2_w, router_group_w, router_group_b, router_expert_w, router_expert_b,
                            exp_w_gate, exp_w_up, exp_w_down)
        last = i == DEPTH - 1
        xp, *sp_out = _stream_layer(
            xp, bp, lw, None, None,
            jnp.zeros((bp, DV_B, H_B * DK_B), F32),
            jnp.zeros((bp, 8, CONV_CH), F32),
            jnp.zeros((bp, H_C, P_C, D_STATE), F32), fw, last)
        outs_p.append(sp_out)
        att_len = cache_k_a.shape[2]
        gla0 = state_gla[i].astype(F32).transpose(0, 3, 1, 2).reshape(bs, DV_B, H_B * DK_B)
        conv0 = jnp.pad(state_conv[i].astype(F32), ((0, 0), (8 - (CONV_W - 1), 0), (0, 0)))
        xs, *ss_out = _stream_layer(
            xs, bs, lw,
            cache_k_a[i].reshape(bs, att_len, WIDTH_A), cache_v_a[i].reshape(bs, att_len, WIDTH_A),
            gla0, conv0, state_ssm[i].astype(F32), fw, last)
        outs_s.append(ss_out)
    stack = lambda outs, j: jnp.stack([o[j] for o in outs])
    return (xp.reshape(bp, sp, D_MODEL), xs.reshape(bs, ss, D_MODEL),
            stack(outs_p, 0), stack(outs_p, 1), stack(outs_p, 2), stack(outs_p, 3), stack(outs_p, 4),
            stack(outs_s, 0), stack(outs_s, 1), stack(outs_s, 2), stack(outs_s, 3), stack(outs_s, 4))
```

```python
import functools

import jax
import jax.numpy as jnp
from jax import lax
from jax.experimental import pallas as pl
from jax.experimental.pallas import tpu as pltpu

F32 = jnp.float32
BF16 = jnp.bfloat16

D_MODEL = 1024
DEPTH = 2
EPS = 1e-6
CHUNK = 64
N_PAST_CHUNKS = 8
BAND = (N_PAST_CHUNKS + 1) * CHUNK
WINDOW = N_PAST_CHUNKS * CHUNK
H_A, DH_A, WIDTH_A = 8, 64, 512
REL_CLIP = 128
H_B, DK_B, DV_B, WIDTH_B = 4, 32, 64, 256
GATE_RANK = 16
GATE_TAU = 16.0
H_C, P_C, WIDTH_C = 4, 64, 256
N_GROUPS_C = 2
D_STATE = 64
CONV_W = 4
CONV_CH = 512
N_EXPERT_GROUPS = 4
EXPERTS_PER_GROUP = 4
N_EXPERTS = 16
D_EXPERT = 256

PROJ_COLS = 3200
MISC_DT_LANE = GATE_RANK
ROUTER_GROUP_LANE = N_EXPERTS
ROUTER_GIDX_LANE = 20

SEQ_GROUP = 8
SEQ_UNROLL = 8
VMEM_LIMIT_BYTES = 56 * 1024 * 1024

NT_DIMS = (((1,), (1,)), ((), ()))
TN_DIMS = (((0,), (0,)), ((), ()))


def _params(*sem):
    return pltpu.CompilerParams(dimension_semantics=sem, vmem_limit_bytes=VMEM_LIMIT_BYTES)


def _dot(a, b):
    return jnp.dot(a, b, preferred_element_type=F32)


def _dot_nt(a, b):
    return lax.dot_general(a, b, NT_DIMS, preferred_element_type=F32)


def _dot_tn(a, b):
    return lax.dot_general(a, b, TN_DIMS, preferred_element_type=F32)


def _split2(a):
    hi = a.astype(BF16)
    return hi, (a - hi.astype(F32)).astype(BF16)


def _dot01_left(m01, a):
    hi, lo = _split2(a)
    return _dot(m01, hi) + _dot(m01, lo)


def _dot01_right(a, m01):
    hi, lo = _split2(a)
    return _dot(hi, m01) + _dot(lo, m01)


def _round_robin(gens):
    results = [None] * len(gens)
    live = list(range(len(gens)))
    while live:
        for idx in list(live):
            try:
                next(gens[idx])
            except StopIteration as stop:
                results[idx] = stop.value
                live.remove(idx)
    return results


def _silu(x):
    return x * jax.nn.sigmoid(x)


def _softplus(x):
    return jnp.maximum(x, 0.0) + jnp.log1p(jnp.exp(-jnp.abs(x)))


IN_TM = 256
QKV_COLS = 3 * WIDTH_A
REST_COLS = PROJ_COLS - QKV_COLS
REST_CHUNKS = ((0, 512), (512, 1024), (1024, REST_COLS))


def _in_proj_kernel(x_ref, nw_ref, w_ref, *refs, with_vt):
    if with_vt:
        wvt_ref, qk_ref, kv_ref, rest_ref, vt_ref = refs
    else:
        qk_ref, kv_ref, rest_ref = refs
    x = x_ref[...]
    ms = jnp.mean(x * x, axis=-1, keepdims=True)
    xn = (x * lax.rsqrt(ms + EPS) * nw_ref[...]).astype(BF16)
    qk_ref[:, 0:WIDTH_A] = _dot(xn, w_ref[:, 0:WIDTH_A]).astype(BF16)
    k = _dot(xn, w_ref[:, WIDTH_A:2 * WIDTH_A])
    qk_ref[:, WIDTH_A:] = k.astype(BF16)
    kv_ref[:, 0:WIDTH_A] = k
    kv_ref[:, WIDTH_A:] = _dot(xn, w_ref[:, 2 * WIDTH_A:QKV_COLS])
    for lo, hi in REST_CHUNKS:
        rest_ref[:, lo:hi] = _dot(xn, w_ref[:, QKV_COLS + lo:QKV_COLS + hi])
    if with_vt:
        vt_ref[0] = _dot_nt(wvt_ref[...], xn).astype(BF16)


def _in_proj(x, nw, w, wvt):
    t = x.shape[0]
    with_vt = wvt is not None

    def rows(width):
        return pl.BlockSpec((IN_TM, width), lambda i: (i, 0))

    in_specs = [rows(D_MODEL), pl.BlockSpec((1, D_MODEL), lambda i: (0, 0)),
                pl.BlockSpec((D_MODEL, PROJ_COLS), lambda i: (0, 0))]
    out_specs = [rows(2 * WIDTH_A), rows(2 * WIDTH_A), rows(REST_COLS)]
    out_shape = [jax.ShapeDtypeStruct((t, 2 * WIDTH_A), BF16),
                 jax.ShapeDtypeStruct((t, 2 * WIDTH_A), F32),
                 jax.ShapeDtypeStruct((t, REST_COLS), F32)]
    args = [x, nw, w]
    if with_vt:
        in_specs.append(pl.BlockSpec((WIDTH_A, D_MODEL), lambda i: (0, 0)))
        out_specs.append(pl.BlockSpec((1, WIDTH_A, IN_TM), lambda i: (i, 0, 0)))
        out_shape.append(jax.ShapeDtypeStruct((t // IN_TM, WIDTH_A, IN_TM), BF16))
        args.append(wvt)
    return pl.pallas_call(
        functools.partial(_in_proj_kernel, with_vt=with_vt),
        grid=(t // IN_TM,),
        in_specs=in_specs, out_specs=out_specs, out_shape=out_shape,
        compiler_params=_params("parallel"),
    )(*args)


ATT_QB = 4 * CHUNK
ATT_KB = IN_TM
ATT_NKB = (ATT_QB + WINDOW) // ATT_KB


def _attn_prompt_kernel(q_ref, k_ref, vt_ref, bias_ref, o_ref, ot_ref):
    blk = pl.program_id(1)
    sb = jnp.maximum(blk - WINDOW // ATT_KB, 0)
    kstart = pl.multiple_of(sb * ATT_KB, ATT_KB)
    lane = lax.broadcasted_iota(jnp.int32, (ATT_QB, 128), 1)
    for j in range(H_A // 2):
        pair = slice(128 * j, 128 * (j + 1))
        kp = k_ref[0, pl.ds(kstart, ATT_NKB * ATT_KB), pair]
        qp = q_ref[0, :, pair]
        for hh in range(2):
            h = 2 * j + hh
            qz = jnp.where((lane >= DH_A) == bool(hh), qp, jnp.zeros_like(qp))
            s = [_dot_nt(kp[ATT_KB * i:ATT_KB * (i + 1)], qz)
                 + bias_ref[0, h, ATT_KB * i:ATT_KB * (i + 1), :] for i in range(ATT_NKB)]
            m = s[0].max(axis=0, keepdims=True)
            for si in s[1:]:
                m = jnp.maximum(m, si.max(axis=0, keepdims=True))
            l = None
            ot = None
            for i, si in enumerate(s):
                p = jnp.exp(si - m)
                li = p.sum(axis=0, keepdims=True)
                oi = _dot(vt_ref[0, sb + i, DH_A * h:DH_A * (h + 1), :], p.astype(BF16))
                l = li if l is None else l + li
                ot = oi if ot is None else ot + oi
            ot_ref[DH_A * h:DH_A * (h + 1), :] = ot / l
    o_ref[0] = ot_ref[...].T.astype(o_ref.dtype)


def _attn_prompt(qk, vt, bias_t):
    bsz, s, _ = qk.shape
    nkb = s // ATT_KB
    return pl.pallas_call(
        _attn_prompt_kernel,
        grid=(bsz, s // ATT_QB),
        in_specs=[pl.BlockSpec((1, ATT_QB, WIDTH_A), lambda b, c: (b, c, 0)),
                  pl.BlockSpec((1, s, WIDTH_A), lambda b, c: (b, 0, 1)),
                  pl.BlockSpec((1, nkb, WIDTH_A, ATT_KB), lambda b, c: (b, 0, 0, 0)),
                  pl.BlockSpec((1, H_A, ATT_NKB * ATT_KB, ATT_QB),
                               lambda b, c: (jnp.minimum(c, 2), 0, 0, 0))],
        out_specs=pl.BlockSpec((1, ATT_QB, WIDTH_A), lambda b, c: (b, c, 0)),
        out_shape=jax.ShapeDtypeStruct((bsz, s, WIDTH_A), BF16),
        scratch_shapes=[pltpu.VMEM((WIDTH_A, ATT_QB), F32)],
        compiler_params=_params("parallel", "arbitrary"),
    )(qk, qk, vt, bias_t)


def _attn_sample_kernel(q_ref, kv_ref, ck_ref, cv_ref, bias_ref, o_ref):
    for h in range(H_A):
        sl = slice(DH_A * h, DH_A * (h + 1))
        vsl = slice(WIDTH_A + DH_A * h, WIDTH_A + DH_A * (h + 1))
        qh = q_ref[0, :, sl]
        s1 = _dot_nt(qh, ck_ref[0, :, sl].astype(BF16)) + bias_ref[h, :, 0:WINDOW]
        s2 = _dot_nt(qh, kv_ref[0, :, sl].astype(BF16)) + bias_ref[h, :, WINDOW:BAND]
        m = jnp.maximum(s1.max(axis=-1, keepdims=True), s2.max(axis=-1, keepdims=True))
        p1 = jnp.exp(s1 - m)
        p2 = jnp.exp(s2 - m)
        l = p1.sum(axis=-1, keepdims=True) + p2.sum(axis=-1, keepdims=True)
        o = (_dot(p1.astype(BF16), cv_ref[0, :, sl].astype(BF16))
             + _dot(p2.astype(BF16), kv_ref[0, :, vsl].astype(BF16)))
        o_ref[0, :, sl] = (o / l).astype(o_ref.dtype)


def _attn_sample(qk, kv, cache_k, cache_v, bias):
    bsz = qk.shape[0]
    return pl.pallas_call(
        _attn_sample_kernel,
        grid=(bsz,),
        in_specs=[pl.BlockSpec((1, CHUNK, WIDTH_A), lambda b: (b, 0, 0)),
                  pl.BlockSpec((1, CHUNK, 2 * WIDTH_A), lambda b: (b, 0, 0)),
                  pl.BlockSpec((1, WINDOW, WIDTH_A), lambda b: (b, 0, 0)),
                  pl.BlockSpec((1, WINDOW, WIDTH_A), lambda b: (b, 0, 0)),
                  pl.BlockSpec((H_A, CHUNK, BAND), lambda b: (0, 0, 0))],
        out_specs=pl.BlockSpec((1, CHUNK, WIDTH_A), lambda b: (b, 0, 0)),
        out_shape=jax.ShapeDtypeStruct((bsz, CHUNK, WIDTH_A), BF16),
        compiler_params=_params("parallel"),
    )(qk, kv, cache_k, cache_v, bias)


def _scan_kernel(bqk_ref, bv_ref, bog_ref, cz_ref, cx_ref, misc_ref,
                 gup_ref, gb_ref, gnw_ref, cw_ref, cb_ref, dtb_ref, alog_ref, dsk_ref, snw_ref,
                 gla0_ref, conv0_ref, ssm0_ref,
                 gla_o_ref, ssm_o_ref, gla_s_ref, conv_s_ref, ssm_s_ref,
                 xpad_ref):
    c = pl.program_id(1)
    nseq = bqk_ref.shape[0]

    @pl.when(c == 0)
    def _():
        gla_s_ref[...] = gla0_ref[...]
        conv_s_ref[...] = conv0_ref[...]
        ssm_s_ref[...] = ssm0_ref[...]

    r64 = lax.broadcasted_iota(jnp.int32, (CHUNK, CHUNK), 0)
    c64 = lax.broadcasted_iota(jnp.int32, (CHUNK, CHUNK), 1)
    tri = c64 <= r64
    tri_bf = jnp.where(tri, 1.0, 0.0).astype(BF16)
    ones_bf = jnp.ones((CHUNK, CHUNK), BF16)
    row_x = lax.broadcasted_iota(jnp.int32, (CHUNK, WIDTH_C), 0)
    s_x = lax.broadcasted_iota(jnp.int32, (CHUNK, WIDTH_C), 1) & (CHUNK - 1)
    causal_x = s_x <= row_x
    upper_x = row_x <= s_x
    er = lax.broadcasted_iota(jnp.int32, (128, WIDTH_C), 0)
    el = lax.broadcasted_iota(jnp.int32, (128, WIDTH_C), 1)
    expand_bf = jnp.where(er == MISC_DT_LANE + (el >> 6), 1.0, 0.0).astype(BF16)
    br = lax.broadcasted_iota(jnp.int32, (WIDTH_C, WIDTH_C), 0)
    bl = lax.broadcasted_iota(jnp.int32, (WIDTH_C, WIDTH_C), 1)
    blk64_bf = jnp.where((br >> 6) == (bl >> 6), 1.0, 0.0).astype(BF16)
    blk128_bf = jnp.where((br >> 7) == (bl >> 7), 1.0, 0.0).astype(BF16)

    gup = gup_ref[...]
    gb = gb_ref[...]
    gnw = gnw_ref[...]
    cb = cb_ref[...]
    dtb = dtb_ref[...]
    neg_a = -jnp.exp(alog_ref[...])
    dsk = dsk_ref[...]
    snw = snw_ref[...]

    def load_seq(b):
        return dict(misc=misc_ref[b], qk=bqk_ref[b], v=bv_ref[b], bog=bog_ref[b], cz=cz_ref[b],
                    cx=cx_ref[b], gla=gla_s_ref[b], conv=conv_s_ref[b],
                    ssm=[ssm_s_ref[b, h] for h in range(H_C)])

    def store_seq(b, out):
        gla_o_ref[b] = out["gla_o"]
        ssm_o_ref[b] = out["ssm_o"]
        gla_s_ref[b] = out["gla"]
        conv_s_ref[b] = out["conv"]
        for h in range(H_C):
            ssm_s_ref[b, h] = out["ssm"][h]

    def compute_seq(inp, slot):
        out = {}
        misc = inp["misc"]

        z = _dot(misc.astype(BF16), gup) + gb
        g = (jnp.minimum(z, 0.0) - jnp.log1p(jnp.exp(-jnp.abs(z)))) * (1.0 / GATE_TAU)
        bcum = _dot01_left(tri_bf, g)
        yield
        q = inp["qk"][:, :H_B * DK_B]
        k = inp["qk"][:, H_B * DK_B:]
        blast = bcum[CHUNK - 1:CHUNK, :]
        qb = (q * (DK_B ** -0.5) * jnp.exp(bcum)).astype(BF16)
        kb = (k * jnp.exp(-bcum)).astype(BF16)
        kl = (k * jnp.exp(blast - bcum)).astype(BF16)
        vb = inp["v"].astype(BF16)
        st = inp["gla"]
        st_bf = st.astype(BF16)
        yield
        o_parts, s_parts = [], []
        for h in range(H_B):
            ks = slice(DK_B * h, DK_B * (h + 1))
            vs = slice(DV_B * h, DV_B * (h + 1))
            att = jnp.where(tri, _dot_nt(qb[:, ks], kb[:, ks]), 0.0).astype(BF16)
            o_parts.append(_dot(att, vb[:, vs]) + _dot_nt(qb[:, ks], st_bf[:, ks]))
            s_parts.append(_dot_tn(vb[:, vs], kl[:, ks]))
            yield
        o = jnp.concatenate(o_parts, axis=1)
        out["gla"] = jnp.exp(blast) * st + jnp.concatenate(s_parts, axis=1)
        msq = _dot01_right(o * o, blk64_bf) * (1.0 / DV_B)
        yield
        go = o * lax.rsqrt(msq + EPS) * gnw * _silu(inp["bog"])
        out["gla_o"] = go.astype(gla_o_ref.dtype)
        yield

        cx = inp["cx"]
        xpad_ref[slot, 0:8, :] = inp["conv"]
        xpad_ref[slot, 8:8 + CHUNK, :] = cx
        conv = cb + cx * cw_ref[CONV_W - 1:CONV_W, :]
        for i in range(CONV_W - 1):
            off = 8 - (CONV_W - 1) + i
            conv = conv + xpad_ref[slot, off:off + CHUNK, :] * cw_ref[i:i + 1, :]
        out["conv"] = cx[CHUNK - 8:, :]
        conv = _silu(conv)
        sx = conv[:, :WIDTH_C]
        sb_bf = conv[:, WIDTH_C:WIDTH_C + N_GROUPS_C * D_STATE].astype(BF16)
        sc_bf = conv[:, WIDTH_C + N_GROUPS_C * D_STATE:].astype(BF16)
        yield

        dtv = _softplus(misc + dtb)
        dt_x = _dot01_right(dtv, expand_bf)
        yield
        a_x = neg_a * dt_x
        cs_x = _dot01_left(tri_bf, a_x)
        cs_row = _dot01_left(ones_bf, jnp.where(upper_x, a_x, 0.0))
        yield
        decay = jnp.exp(jnp.where(causal_x, cs_x - cs_row, -jnp.inf))
        cb_parts = []
        for gidx in range(N_GROUPS_C):
            ns = slice(D_STATE * gidx, D_STATE * (gidx + 1))
            cbm = _dot_nt(sc_bf[:, ns], sb_bf[:, ns])
            cb_parts += [cbm] * (H_C // N_GROUPS_C)
            yield
        scores = (jnp.concatenate(cb_parts, axis=1) * decay).astype(BF16)
        xdt = (sx * dt_x).astype(BF16)
        ecs = jnp.exp(cs_x)
        cs_last = cs_x[CHUNK - 1:CHUNK, :]
        xw = (sx * (jnp.exp(cs_last - cs_x) * dt_x)).astype(BF16)
        e_last = jnp.exp(cs_last)
        yield
        y_parts = []
        out["ssm"] = []
        for h in range(H_C):
            ps = slice(P_C * h, P_C * (h + 1))
            gi = h // (H_C // N_GROUPS_C)
            ns = slice(D_STATE * gi, D_STATE * (gi + 1))
            hst = inp["ssm"][h]
            y_parts.append(_dot(scores[:, ps], xdt[:, ps])
                           + _dot_nt(sc_bf[:, ns], hst.astype(BF16)) * ecs[:, ps])
            out["ssm"].append(e_last[:, ps] * hst + _dot_tn(xw[:, ps], sb_bf[:, ns]))
            yield
        y = jnp.concatenate(y_parts, axis=1) + dsk * sx
        y = y * _silu(inp["cz"])
        ymsq = _dot01_right(y * y, blk128_bf) * (1.0 / (WIDTH_C // N_GROUPS_C))
        yield
        out["ssm_o"] = (y * lax.rsqrt(ymsq + EPS) * snw).astype(ssm_o_ref.dtype)
        return out

    def group_body(i, carry):
        seqs = [i * SEQ_UNROLL + u for u in range(SEQ_UNROLL)]
        inputs = [load_seq(b) for b in seqs]
        outputs = _round_robin([compute_seq(inp, slot) for slot, inp in enumerate(inputs)])
        for b, out in zip(seqs, outputs):
            store_seq(b, out)
        return carry

    lax.fori_loop(0, nseq // SEQ_UNROLL, group_body, 0)


def _scan(proj, lw, gla0, conv0, ssm0):
    bsz, l, _ = proj.shape
    nc = l // CHUNK
    sg = min(SEQ_GROUP, bsz)
    ng = bsz // sg

    def col(width, idx):
        return pl.BlockSpec((sg, CHUNK, width), lambda g, c: (g, c, idx))

    def const(shape):
        return pl.BlockSpec(shape, lambda g, c: (0,) * len(shape))

    def state(shape):
        return pl.BlockSpec((sg,) + shape, lambda g, c: (g,) + (0,) * len(shape))

    gla_shape = (DV_B, H_B * DK_B)
    conv_shape = (8, CONV_CH)
    ssm_shape = (H_C, P_C, D_STATE)
    return pl.pallas_call(
        _scan_kernel,
        grid=(ng, nc),
        in_specs=[col(256, 0), col(256, 1), col(256, 2), col(256, 3), col(512, 2), col(128, 12),
                  const((128, 128)), const((1, 128)), const((1, WIDTH_B)),
                  const((CONV_W, CONV_CH)), const((1, CONV_CH)),
                  const((1, 128)), const((1, WIDTH_C)), const((1, WIDTH_C)), const((1, WIDTH_C)),
                  state(gla_shape), state(conv_shape), state(ssm_shape)],
        out_specs=[pl.BlockSpec((sg, CHUNK, WIDTH_B), lambda g, c: (g, c, 0)),
                   pl.BlockSpec((sg, CHUNK, WIDTH_C), lambda g, c: (g, c, 0)),
                   state(gla_shape), state(conv_shape), state(ssm_shape)],
        out_shape=[jax.ShapeDtypeStruct((bsz, l, WIDTH_B), BF16),
                   jax.ShapeDtypeStruct((bsz, l, WIDTH_C), BF16),
                   jax.ShapeDtypeStruct((bsz,) + gla_shape, F32),
                   jax.ShapeDtypeStruct((bsz,) + conv_shape, F32),
                   jax.ShapeDtypeStruct((bsz,) + ssm_shape, F32)],
        scratch_shapes=[pltpu.VMEM((SEQ_UNROLL, 8 + CHUNK, CONV_CH), F32)],
        compiler_params=_params("parallel", "arbitrary"),
    )(proj, proj, proj, proj, proj, proj,
      lw["gup"], lw["gb"], lw["gnw"], lw["cw"], lw["cb"], lw["dtb"], lw["alog"], lw["dsk"], lw["snw"],
      gla0, conv0, ssm0)


OUT_TM = 256


def _out_proj_kernel(x_ref, att_ref, gla_ref, ssm_ref, wo_ref, nw_ref, rwh_ref, rwl_ref, rb_ref,
                     x1_ref, hn_ref, gate_ref):
    mo = (_dot(att_ref[...], wo_ref[0:WIDTH_A, :])
          + _dot(gla_ref[...], wo_ref[WIDTH_A:WIDTH_A + WIDTH_B, :])
          + _dot(ssm_ref[...], wo_ref[WIDTH_A + WIDTH_B:, :]))
    x1 = x_ref[...] + mo
    x1_ref[...] = x1
    ms = jnp.mean(x1 * x1, axis=-1, keepdims=True)
    hn = x1 * lax.rsqrt(ms + EPS) * nw_ref[...]
    hn_ref[...] = hn.astype(hn_ref.dtype)

    h_hi = hn.astype(BF16)
    h_lo = (hn - h_hi.astype(F32)).astype(BF16)
    logits = (_dot(h_hi, rwh_ref[...]) + _dot(h_lo, rwh_ref[...]) + _dot(h_hi, rwl_ref[...])
              + rb_ref[...])
    lane = lax.broadcasted_iota(jnp.int32, logits.shape, 1)
    lane_f = lane.astype(F32)
    neg = -jnp.inf
    big = 1000.0
    gmask = (lane >= ROUTER_GROUP_LANE) & (lane < ROUTER_GROUP_LANE + N_EXPERT_GROUPS)
    lg = jnp.where(gmask, logits, neg)
    gmax = lg.max(axis=-1, keepdims=True)
    g_idx = jnp.where(lg == gmax, lane_f, big).min(axis=-1, keepdims=True) - ROUTER_GROUP_LANE
    p_grp = 1.0 / jnp.where(gmask, jnp.exp(logits - gmax), 0.0).sum(axis=-1, keepdims=True)
    emask = (lane < N_EXPERTS) & ((lane >> 2).astype(F32) == g_idx)
    le = jnp.where(emask, logits, neg)
    v1 = le.max(axis=-1, keepdims=True)
    i1 = jnp.where(le == v1, lane_f, big).min(axis=-1, keepdims=True)
    le2 = jnp.where(lane_f == i1, neg, le)
    v2 = le2.max(axis=-1, keepdims=True)
    i2 = jnp.where(le2 == v2, lane_f, big).min(axis=-1, keepdims=True)
    e2 = jnp.exp(v2 - v1)
    den = 1.0 + e2
    gate_ref[...] = (jnp.where(lane_f == i1, (1.0 / den) * p_grp, 0.0)
                     + jnp.where(lane_f == i2, (e2 / den) * p_grp, 0.0)
                     + jnp.where(lane == ROUTER_GIDX_LANE, g_idx, 0.0))


def _out_proj(x, att, gla, ssm, lw):
    t = x.shape[0]

    def rows(width):
        return pl.BlockSpec((OUT_TM, width), lambda i: (i, 0))

    def const(shape):
        return pl.BlockSpec(shape, lambda i: (0, 0))

    return pl.pallas_call(
        _out_proj_kernel,
        grid=(t // OUT_TM,),
        in_specs=[rows(D_MODEL), rows(WIDTH_A), rows(WIDTH_B), rows(WIDTH_C),
                  const((D_MODEL, D_MODEL)), const((1, D_MODEL)),
                  const((D_MODEL, 128)), const((D_MODEL, 128)), const((1, 128))],
        out_specs=[rows(D_MODEL), rows(D_MODEL), rows(128)],
        out_shape=[jax.ShapeDtypeStruct((t, D_MODEL), F32),
                   jax.ShapeDtypeStruct((t, D_MODEL), BF16),
                   jax.ShapeDtypeStruct((t, 128), F32)],
        compiler_params=_params("parallel"),
    )(x, att, gla, ssm, lw["wo"], lw["n2w"], lw["rw_hi"], lw["rw_lo"], lw["rb"])


RT_TS = 256
RT_ALIGN = 16
RT_ROWS = 384
RT_SIZES = (256, 128, 64, 32, 16)
PAY_COLS = D_MODEL + 256
MOE_TM = 512


def _moe_max_tiles(t):
    return (t + (t // RT_TS) * N_EXPERT_GROUPS * (RT_ALIGN - 1)) // MOE_TM + N_EXPERT_GROUPS


def _sort_matrix(gate, c_ref, tile):
    lane = lax.broadcasted_iota(jnp.int32, (RT_TS, 128), 1)
    gidx = jnp.where(lane == ROUTER_GIDX_LANE, gate, 0.0).sum(axis=1, keepdims=True)
    onehot = lane.astype(F32) == gidx
    r = lax.broadcasted_iota(jnp.int32, (RT_TS, RT_TS), 0)
    c = lax.broadcasted_iota(jnp.int32, (RT_TS, RT_TS), 1)
    earlier = jnp.where(c < r, 1.0, 0.0).astype(BF16)
    before = _dot(earlier, jnp.where(onehot, 1.0, 0.0).astype(BF16))
    start = jnp.zeros((RT_TS, 128), F32)
    for g in range(N_EXPERT_GROUPS):
        start = jnp.where(lane == g, c_ref[N_EXPERT_GROUPS * tile + g].astype(F32), start)
    dest = jnp.where(onehot, before + start, 0.0).sum(axis=1, keepdims=True)
    rows = lax.broadcasted_iota(jnp.int32, (RT_TS, RT_ROWS), 1).astype(F32)
    return jnp.where(rows == dest, 1.0, 0.0).astype(BF16)


def _piece_copies(length, local, glob, make_copy, act):
    for size in RT_SIZES:
        take = (length & size) != 0

        @pl.when(take)
        def _(local=local, glob=glob, size=size):
            act(make_copy(pl.multiple_of(local, RT_ALIGN), pl.multiple_of(glob, RT_ALIGN), size))

        step = jnp.where(take, size, 0)
        local = local + step
        glob = glob + step


def _run_copies(c_ref, lp_ref, off_ref, tile, make_copy, act):
    for g in range(N_EXPERT_GROUPS):
        k = N_EXPERT_GROUPS * tile + g
        _piece_copies(lp_ref[k], c_ref[k], off_ref[k], make_copy, act)


def _dispatch_kernel(c_ref, lp_ref, off_ref, tail_ref, nact_ref, hn_ref, gate_ref, xs_hbm,
                     y_ref, zero_ref, sem, zsem, *, first_spare, max_tiles):
    i = pl.program_id(0)
    nt = pl.num_programs(0)
    slot = i % 2
    gate = gate_ref[...]
    pt = _sort_matrix(gate, c_ref, i)
    g_hi = gate.astype(BF16)
    g_lo = (gate - g_hi.astype(F32)).astype(BF16)
    y_ref[slot, :, 0:D_MODEL] = _dot_tn(pt, hn_ref[...]).astype(BF16)
    y_ref[slot, :, D_MODEL:D_MODEL + 128] = _dot_tn(pt, g_hi).astype(BF16)
    y_ref[slot, :, D_MODEL + 128:] = _dot_tn(pt, g_lo).astype(BF16)

    def copy_from(s):
        return lambda local, glob, size: pltpu.make_async_copy(
            y_ref.at[s, pl.ds(local, size)], xs_hbm.at[pl.ds(glob, size)], sem.at[s])

    _run_copies(c_ref, lp_ref, off_ref, i, copy_from(slot), lambda cp: cp.start())

    @pl.when(i > 0)
    def _():
        _run_copies(c_ref, lp_ref, off_ref, i - 1, copy_from(1 - slot), lambda cp: cp.wait())

    @pl.when(i == nt - 1)
    def _():
        _run_copies(c_ref, lp_ref, off_ref, i, copy_from(slot), lambda cp: cp.wait())
        zero_ref[...] = jnp.zeros_like(zero_ref)

        def zero_copy(local, glob, size):
            return pltpu.make_async_copy(zero_ref.at[pl.ds(local, size)],
                                         xs_hbm.at[pl.ds(glob, size)], zsem)

        def zero_fill(act):
            for g in range(N_EXPERT_GROUPS):
                _piece_copies(tail_ref[N_EXPERT_GROUPS + g], 0, tail_ref[g], zero_copy, act)
            for j in range(first_spare, max_tiles):
                @pl.when(j >= nact_ref[0])
                def _(j=j):
                    act(zero_copy(0, j * MOE_TM, MOE_TM))

        zero_fill(lambda cp: cp.start())
        zero_fill(lambda cp: cp.wait())


def _moe_group_kernel(grp_ref, nact_ref, xs_ref, wg_ref, wu_ref, wd_ref, o_ref):
    j = pl.program_id(0)

    @pl.when(j >= nact_ref[0])
    def _():
        o_ref[...] = jnp.zeros_like(o_ref)

    @pl.when(j < nact_ref[0])
    def _():
        x = xs_ref[:, 0:D_MODEL]
        gate = (xs_ref[:, D_MODEL:D_MODEL + 128].astype(F32)
                + xs_ref[:, D_MODEL + 128:].astype(F32))
        lane = lax.broadcasted_iota(jnp.int32, gate.shape, 1)
        first = grp_ref[j] * EXPERTS_PER_GROUP
        acc = None
        for e in range(EXPERTS_PER_GROUP):
            ge = jnp.where(lane == first + e, gate, 0.0).sum(axis=-1, keepdims=True)
            hid = _silu(_dot(x, wg_ref[e])) * _dot(x, wu_ref[e]) * ge
            part = _dot(hid.astype(BF16), wd_ref[e])
            acc = part if acc is None else acc + part
        o_ref[...] = acc


def _combine_kernel(c_ref, lp_ref, off_ref, x_ref, gate_ref, fw_ref, ys_hbm, o_ref, z_ref, sem,
                    *, final_norm):
    i = pl.program_id(0)
    nt = pl.num_programs(0)
    slot = i % 2

    def copy_into(s):
        return lambda local, glob, size: pltpu.make_async_copy(
            ys_hbm.at[pl.ds(glob, size)], z_ref.at[s, pl.ds(local, size)], sem.at[s])

    @pl.when(i == 0)
    def _():
        z_ref[...] = jnp.zeros_like(z_ref)
        _run_copies(c_ref, lp_ref, off_ref, 0, copy_into(0), lambda cp: cp.start())

    @pl.when(i + 1 < nt)
    def _():
        _run_copies(c_ref, lp_ref, off_ref, i + 1, copy_into(1 - slot), lambda cp: cp.start())

    _run_copies(c_ref, lp_ref, off_ref, i, copy_into(slot), lambda cp: cp.wait())
    pt = _sort_matrix(gate_ref[...], c_ref, i)
    z = z_ref[slot]
    z_hi = z.astype(BF16)
    z_lo = (z - z_hi.astype(F32)).astype(BF16)
    y = x_ref[...] + _dot(pt, z_hi) + _dot(pt, z_lo)
    if final_norm:
        ms = jnp.mean(y * y, axis=-1, keepdims=True)
        y = y * lax.rsqrt(ms + EPS) * fw_ref[...]
    o_ref[...] = y


def _route_meta(gate, max_tiles):
    t = gate.shape[0]
    nt = t // RT_TS
    groups = jnp.arange(N_EXPERT_GROUPS, dtype=jnp.int32)
    g = gate[:, ROUTER_GIDX_LANE].astype(jnp.int32).reshape(nt, RT_TS, 1)
    count = (g == groups).astype(jnp.int32).sum(axis=1)
    lp = (count + RT_ALIGN - 1) // RT_ALIGN * RT_ALIGN
    total = lp.sum(axis=0)
    ntile = (total + MOE_TM - 1) // MOE_TM
    last = jnp.cumsum(ntile)
    first_row = (last - ntile) * MOE_TM
    c = jnp.cumsum(lp, axis=1) - lp
    off = first_row[None, :] + jnp.cumsum(lp, axis=0) - lp
    tail = jnp.concatenate([first_row + total, ntile * MOE_TM - total])
    j = jnp.arange(max_tiles, dtype=jnp.int32)[:, None]
    grp = jnp.minimum((j >= last[None, :]).astype(jnp.int32).sum(axis=1), N_EXPERT_GROUPS - 1)
    return c.reshape(-1), lp.reshape(-1), off.reshape(-1), tail, grp, last[-1:]


def _moe(x, hn, gate, lw, fw, final_norm):
    t = x.shape[0]
    nt = t // RT_TS
    max_tiles = _moe_max_tiles(t)
    c, lp, off, tail, grp, nact = _route_meta(gate, max_tiles)

    def rows(width):
        return pl.BlockSpec((RT_TS, width), lambda i, *_: (i, 0))

    xs = pl.pallas_call(
        functools.partial(_dispatch_kernel, first_spare=t // MOE_TM, max_tiles=max_tiles),
        grid_spec=pltpu.PrefetchScalarGridSpec(
            num_scalar_prefetch=5, grid=(nt,),
            in_specs=[rows(D_MODEL), rows(128)],
            out_specs=pl.BlockSpec(memory_space=pl.ANY),
            scratch_shapes=[pltpu.VMEM((2, RT_ROWS, PAY_COLS), BF16),
                            pltpu.VMEM((MOE_TM, PAY_COLS), BF16),
                            pltpu.SemaphoreType.DMA((2,)), pltpu.SemaphoreType.DMA(())]),
        out_shape=jax.ShapeDtypeStruct((max_tiles * MOE_TM, PAY_COLS), BF16),
        compiler_params=_params("arbitrary"),
    )(c, lp, off, tail, nact, hn, gate)

    def experts(shape):
        return pl.BlockSpec((EXPERTS_PER_GROUP,) + shape, lambda j, grp, nact: (grp[j], 0, 0))

    ys = pl.pallas_call(
        _moe_group_kernel,
        grid_spec=pltpu.PrefetchScalarGridSpec(
            num_scalar_prefetch=2, grid=(max_tiles,),
            in_specs=[pl.BlockSpec((MOE_TM, PAY_COLS), lambda j, grp, nact: (j, 0)),
                      experts((D_MODEL, D_EXPERT)), experts((D_MODEL, D_EXPERT)),
                      experts((D_EXPERT, D_MODEL))],
            out_specs=pl.BlockSpec((MOE_TM, D_MODEL), lambda j, grp, nact: (j, 0))),
        out_shape=jax.ShapeDtypeStruct((max_tiles * MOE_TM, D_MODEL), F32),
        compiler_params=_params("arbitrary"),
    )(grp, nact, xs, lw["wg"], lw["wu"], lw["wd"])

    return pl.pallas_call(
        functools.partial(_combine_kernel, final_norm=final_norm),
        grid_spec=pltpu.PrefetchScalarGridSpec(
            num_scalar_prefetch=3, grid=(nt,),
            in_specs=[rows(D_MODEL), rows(128), pl.BlockSpec((1, D_MODEL), lambda i, *_: (0, 0)),
                      pl.BlockSpec(memory_space=pl.ANY)],
            out_specs=rows(D_MODEL),
            scratch_shapes=[pltpu.VMEM((2, RT_ROWS, D_MODEL), F32),
                            pltpu.SemaphoreType.DMA((2,))]),
        out_shape=jax.ShapeDtypeStruct((t, D_MODEL), F32),
        compiler_params=_params("arbitrary"),
    )(c, lp, off, x, gate, fw, ys)


def _lane_place(vec, start, width=128):
    return jnp.zeros((1, width), F32).at[0, start:start + vec.shape[0]].set(vec.astype(F32))


REL_PAD = 704


def _toeplitz(v, rows, cols):
    n = v.shape[-1]
    tiled = jnp.tile(v, (1,) * (v.ndim - 1) + (rows,))[..., :rows * (n - 1)]
    return tiled.reshape(v.shape[:-1] + (rows, n - 1))[..., :cols]


def _band_bias(table):
    t = table.astype(F32)
    ext = jnp.concatenate([jnp.repeat(t[:, :1], REL_PAD, axis=1), t,
                           jnp.repeat(t[:, -1:], REL_PAD, axis=1)], axis=1)
    extr = ext[:, ::-1]
    top = ext.shape[1] - 1 - (REL_CLIP + REL_PAD)

    def band(off, nq, nk):
        n = nq + nk
        v = jnp.concatenate([extr[:, top - off:top - off + nk + 1],
                             extr[:, top - off - (nq - 1):top - off]], axis=1)
        assert v.shape[1] == n
        return _toeplitz(v, nq, nk)

    sample = band(WINDOW, CHUNK, BAND)
    nk = ATT_NKB * ATT_KB
    kc = (jnp.arange(nk) // CHUNK)[:, None]
    qc = (jnp.arange(ATT_QB) // CHUNK)[None, :]
    valid = [kc <= qc, kc <= qc + ATT_QB // CHUNK, (kc >= qc) & (kc <= qc + N_PAST_CHUNKS)]
    prompt = jnp.stack([
        jnp.where(valid[w][None], jnp.swapaxes(band(w * ATT_QB, ATT_QB, nk), 1, 2), -1e30)
        for w in range(3)])
    return sample, prompt


def _layer_weights(i, norm1_w, w_in, rel_bias_table, gla_w_gate_up, gla_b_gate, gla_norm_w,
                   ssm_conv_w, ssm_conv_b, ssm_dt_bias, ssm_a_log, ssm_d, ssm_norm_w, w_out,
                   norm2_w, router_group_w, router_group_b, router_expert_w, router_expert_b,
                   exp_w_gate, exp_w_up, exp_w_down):
    wi = w_in[i]
    o = [0, 512, 1024, 1536, 1664, 1792, 2048, 2064, 2320, 2576, 3088, 3092]
    aq, ak, av, bq, bk, bv, bglr, bog, cz, cxbc, cdt = [wi[:, o[j]:o[j + 1]] for j in range(11)]
    pad = jnp.zeros((D_MODEL, 128 - GATE_RANK - H_C), wi.dtype)
    w_perm = jnp.concatenate([aq * (DH_A ** -0.5), ak, av, bq, bk, bv, bog, cz, cxbc, bglr, cdt, pad],
                             axis=1)
    bias_s, bias_p = _band_bias(rel_bias_table[i])
    rw = jnp.concatenate([router_expert_w[i], router_group_w[i],
                          jnp.zeros((D_MODEL, 128 - N_EXPERTS - N_EXPERT_GROUPS), F32)], axis=1)
    rw_hi = rw.astype(BF16)
    return dict(
        n1w=norm1_w[i][None, :],
        w_in=w_perm.astype(BF16),
        w_vt=av.T.astype(BF16),
        bias_s=bias_s,
        bias_p=bias_p,
        gup=jnp.zeros((128, 128), F32).at[:GATE_RANK, :].set(gla_w_gate_up[i]).astype(BF16),
        gb=gla_b_gate[i][None, :].astype(F32),
        gnw=jnp.tile(gla_norm_w[i], H_B)[None, :].astype(F32),
        cw=ssm_conv_w[i].astype(F32),
        cb=ssm_conv_b[i][None, :].astype(F32),
        dtb=_lane_place(ssm_dt_bias[i], MISC_DT_LANE),
        alog=jnp.repeat(ssm_a_log[i].astype(F32), P_C)[None, :],
        dsk=jnp.repeat(ssm_d[i].astype(F32), P_C)[None, :],
        snw=ssm_norm_w[i][None, :].astype(F32),
        wo=w_out[i].astype(BF16),
        n2w=norm2_w[i][None, :],
        rw_hi=rw_hi,
        rw_lo=(rw - rw_hi.astype(F32)).astype(BF16),
        rb=jnp.concatenate([router_expert_b[i], router_group_b[i],
                            jnp.zeros((128 - N_EXPERTS - N_EXPERT_GROUPS,), F32)])[None, :],
        wg=exp_w_gate[i].astype(BF16),
        wu=exp_w_up[i].astype(BF16),
        wd=exp_w_down[i].astype(BF16),
    )


def _stream_layer(x, bsz, lw, cache_k, cache_v, gla0, conv0, ssm0, fw, final_norm):
    l = x.shape[0] // bsz
    if cache_k is None:
        qk, kv, rest, vt = _in_proj(x, lw["n1w"], lw["w_in"], lw["w_vt"])
        att = _attn_prompt(qk.reshape(bsz, l, 2 * WIDTH_A),
                           vt.reshape(bsz, l // ATT_KB, WIDTH_A, ATT_KB), lw["bias_p"])
    else:
        qk, kv, rest = _in_proj(x, lw["n1w"], lw["w_in"], None)
        att = _attn_sample(qk.reshape(bsz, l, 2 * WIDTH_A), kv.reshape(bsz, l, 2 * WIDTH_A),
                           cache_k, cache_v, lw["bias_s"])
    kv = kv.reshape(bsz, l, 2 * WIDTH_A)
    gla_o, ssm_o, gla_s, conv_s, ssm_s = _scan(rest.reshape(bsz, l, REST_COLS), lw, gla0, conv0, ssm0)
    t = bsz * l
    x1, hn, gate = _out_proj(x, att.reshape(t, WIDTH_A), gla_o.reshape(t, WIDTH_B),
                             ssm_o.reshape(t, WIDTH_C), lw)
    x2 = _moe(x1, hn, gate, lw, fw, final_norm)
    keep = min(WINDOW, l)
    k_keep = kv[:, l - keep:, :WIDTH_A].reshape(bsz, keep, H_A, DH_A)
    v_keep = kv[:, l - keep:, WIDTH_A:].reshape(bsz, keep, H_A, DH_A)
    gla_state = gla_s.reshape(bsz, DV_B, H_B, DK_B).transpose(0, 2, 3, 1)
    conv_state = conv_s[:, 8 - (CONV_W - 1):, :]
    return x2, k_keep, v_keep, gla_state, conv_state, ssm_s


def kernel(x_prompt, x_sample, cache_k_a, cache_v_a, state_gla, state_conv, state_ssm, norm1_w, w_in, rel_bias_table, gla_w_gate_up, gla_b_gate, gla_norm_w, ssm_conv_w, ssm_conv_b, ssm_dt_bias, ssm_a_log, ssm_d, ssm_norm_w, w_out, norm2_w, router_group_w, router_group_b, router_expert_w, router_expert_b, exp_w_gate, exp_w_up, exp_w_down, final_norm_w):
    bp, sp, _ = x_prompt.shape
    bs, ss, _ = x_sample.shape
    xp = x_prompt.reshape(bp * sp, D_MODEL)
    xs = x_sample.reshape(bs * ss, D_MODEL)
    fw = final_norm_w[None, :].astype(F32)
    outs_p, outs_s = [], []
    for i in range(DEPTH):
        lw = _layer_weights(i, norm1_w, w_in, rel_bias_table, gla_w_gate_up, gla_b_gate, gla_norm_w,
                            ssm_conv_w, ssm_conv_b, ssm_dt_bias, ssm_a_log, ssm_d, ssm_norm_w, w_out,
                            norm2_w, router_group_w, router_group_b, router_expert_w, router_expert_b,
                            exp_w_gate, exp_w_up, exp_w_down)
        last = i == DEPTH - 1
        xp, *sp_out = _stream_layer(
            xp, bp, lw, None, None,
            jnp.zeros((bp, DV_B, H_B * DK_B), F32),
            jnp.zeros((bp, 8, CONV_CH), F32),
            jnp.zeros((bp, H_C, P_C, D_STATE), F32), fw, last)
        outs_p.append(sp_out)
        att_len = cache_k_a.shape[2]
        gla0 = state_gla[i].astype(F32).transpose(0, 3, 1, 2).reshape(bs, DV_B, H_B * DK_B)
        conv0 = jnp.pad(state_conv[i].astype(F32), ((0, 0), (8 - (CONV_W - 1), 0), (0, 0)))
        xs, *ss_out = _stream_layer(
            xs, bs, lw,
            cache_k_a[i].reshape(bs, att_len, WIDTH_A), cache_v_a[i].reshape(bs, att_len, WIDTH_A),
            gla0, conv0, state_ssm[i].astype(F32), fw, last)
        outs_s.append(ss_out)
    stack = lambda outs, j: jnp.stack([o[j] for o in outs])
    return (xp.reshape(bp, sp, D_MODEL), xs.reshape(bs, ss, D_MODEL),
            stack(outs_p, 0), stack(outs_p, 1), stack(outs_p, 2), stack(outs_p, 3), stack(outs_p, 4),
            stack(outs_s, 0), stack(outs_s, 1), stack(outs_s, 2), stack(outs_s, 3), stack(outs_s, 4))
```

```python
import functools

import jax
import jax.numpy as jnp
from jax import lax
from jax.experimental import pallas as pl
from jax.experimental.pallas import tpu as pltpu

F32 = jnp.float32
BF16 = jnp.bfloat16

D_MODEL = 1024
DEPTH = 2
EPS = 1e-6
CHUNK = 64
N_PAST_CHUNKS = 8
BAND = (N_PAST_CHUNKS + 1) * CHUNK
WINDOW = N_PAST_CHUNKS * CHUNK
H_A, DH_A, WIDTH_A = 8, 64, 512
REL_CLIP = 128
LOG2_E = 1.4426950408889634
H_B, DK_B, DV_B, WIDTH_B = 4, 32, 64, 256
GATE_RANK = 16
GATE_TAU = 16.0
H_C, P_C, WIDTH_C = 4, 64, 256
N_GROUPS_C = 2
D_STATE = 64
CONV_W = 4
CONV_CH = 512
N_EXPERT_GROUPS = 4
EXPERTS_PER_GROUP = 4
N_EXPERTS = 16
D_EXPERT = 256

PROJ_COLS = 3200
MISC_DT_LANE = GATE_RANK
ROUTER_GROUP_LANE = N_EXPERTS
ROUTER_GIDX_LANE = 20

SEQ_GROUP = 8
SEQ_UNROLL = 8
VMEM_LIMIT_BYTES = 56 * 1024 * 1024

NT_DIMS = (((1,), (1,)), ((), ()))
TN_DIMS = (((0,), (0,)), ((), ()))


def _params(*sem):
    return pltpu.CompilerParams(dimension_semantics=sem, vmem_limit_bytes=VMEM_LIMIT_BYTES)


def _dot(a, b):
    return jnp.dot(a, b, preferred_element_type=F32)


def _dot_nt(a, b):
    return lax.dot_general(a, b, NT_DIMS, preferred_element_type=F32)


def _dot_tn(a, b):
    return lax.dot_general(a, b, TN_DIMS, preferred_element_type=F32)


def _split2(a):
    hi = a.astype(BF16)
    return hi, (a - hi.astype(F32)).astype(BF16)


def _dot01_left(m01, a):
    hi, lo = _split2(a)
    return _dot(m01, hi) + _dot(m01, lo)


def _dot01_right(a, m01):
    hi, lo = _split2(a)
    return _dot(hi, m01) + _dot(lo, m01)


def _round_robin(gens):
    results = [None] * len(gens)
    live = list(range(len(gens)))
    while live:
        for idx in list(live):
            try:
                next(gens[idx])
            except StopIteration as stop:
                results[idx] = stop.value
                live.remove(idx)
    return results


def _silu(x):
    return x * jax.nn.sigmoid(x)


def _softplus(x):
    return jnp.maximum(x, 0.0) + jnp.log1p(jnp.exp(-jnp.abs(x)))


IN_TM = 256
QKV_COLS = 3 * WIDTH_A
REST_COLS = PROJ_COLS - QKV_COLS
REST_CHUNKS = ((0, 512), (512, 1024), (1024, REST_COLS))


def _in_proj_kernel(x_ref, nw_ref, w_ref, *refs, with_vt, tiles_per_seq, keep_tiles):
    if with_vt:
        wvt_ref, qk_ref, rest_ref, kk_ref, vk_ref, vt_ref = refs
    else:
        qk_ref, rest_ref, kk_ref, vk_ref, kv_ref = refs
    x = x_ref[...]
    ms = jnp.mean(x * x, axis=-1, keepdims=True)
    xn = (x * lax.rsqrt(ms + EPS) * nw_ref[...]).astype(BF16)
    qk_ref[:, 0:WIDTH_A] = _dot(xn, w_ref[:, 0:WIDTH_A]).astype(BF16)
    k = _dot(xn, w_ref[:, WIDTH_A:2 * WIDTH_A])
    v = _dot(xn, w_ref[:, 2 * WIDTH_A:QKV_COLS])
    qk_ref[:, WIDTH_A:] = k.astype(BF16)
    if not with_vt:
        kv_ref[:, 0:WIDTH_A] = k
        kv_ref[:, WIDTH_A:] = v

    def keep_rows():
        for h in range(H_A):
            sl = slice(DH_A * h, DH_A * (h + 1))
            kk_ref[pl.ds(h, IN_TM, stride=H_A), :] = k[:, sl]
            vk_ref[pl.ds(h, IN_TM, stride=H_A), :] = v[:, sl]

    if keep_tiles == tiles_per_seq:
        keep_rows()
    else:
        pl.when(pl.program_id(0) % tiles_per_seq >= tiles_per_seq - keep_tiles)(keep_rows)
    for lo, hi in REST_CHUNKS:
        rest_ref[:, lo:hi] = _dot(xn, w_ref[:, QKV_COLS + lo:QKV_COLS + hi])
    if with_vt:
        vt_ref[0] = _dot_nt(wvt_ref[...], xn).astype(BF16)


def _in_proj(x, nw, w, wvt, seq_len):
    t = x.shape[0]
    with_vt = wvt is not None
    tiles_per_seq = max(seq_len // IN_TM, 1)
    keep_tiles = max(min(WINDOW, seq_len) // IN_TM, 1)
    n_keep = t // IN_TM // tiles_per_seq * keep_tiles

    def rows(width):
        return pl.BlockSpec((IN_TM, width), lambda i: (i, 0))

    def kept(i):
        return (i // tiles_per_seq * keep_tiles
                + jnp.maximum(i % tiles_per_seq - (tiles_per_seq - keep_tiles), 0), 0)

    keep_spec = pl.BlockSpec((IN_TM * H_A, DH_A), kept)
    keep_shape = jax.ShapeDtypeStruct((n_keep * IN_TM * H_A, DH_A), F32)
    in_specs = [rows(D_MODEL), pl.BlockSpec((1, D_MODEL), lambda i: (0, 0)),
                pl.BlockSpec((D_MODEL, PROJ_COLS), lambda i: (0, 0))]
    out_specs = [rows(2 * WIDTH_A), rows(REST_COLS), keep_spec, keep_spec]
    out_shape = [jax.ShapeDtypeStruct((t, 2 * WIDTH_A), BF16),
                 jax.ShapeDtypeStruct((t, REST_COLS), F32), keep_shape, keep_shape]
    args = [x, nw, w]
    if with_vt:
        in_specs.append(pl.BlockSpec((WIDTH_A, D_MODEL), lambda i: (0, 0)))
        out_specs.append(pl.BlockSpec((1, WIDTH_A, IN_TM), lambda i: (i, 0, 0)))
        out_shape.append(jax.ShapeDtypeStruct((t // IN_TM, WIDTH_A, IN_TM), BF16))
        args.append(wvt)
    else:
        out_specs.append(rows(2 * WIDTH_A))
        out_shape.append(jax.ShapeDtypeStruct((t, 2 * WIDTH_A), F32))
    return pl.pallas_call(
        functools.partial(_in_proj_kernel, with_vt=with_vt, tiles_per_seq=tiles_per_seq,
                          keep_tiles=keep_tiles),
        grid=(t // IN_TM,),
        in_specs=in_specs, out_specs=out_specs, out_shape=out_shape,
        compiler_params=_params("arbitrary"),
    )(*args)


ATT_QB = 4 * CHUNK
ATT_KB = IN_TM
ATT_NKB = (ATT_QB + WINDOW) // ATT_KB


def _attn_prompt_kernel(q_ref, k_ref, vt_ref, bias_ref, o_ref, ot_ref):
    blk = pl.program_id(1)
    sb = jnp.maximum(blk - WINDOW // ATT_KB, 0)
    kstart = pl.multiple_of(sb * ATT_KB, ATT_KB)
    lane = lax.broadcasted_iota(jnp.int32, (ATT_QB, 128), 1)

    def scores(h):
        pair = slice(128 * (h // 2), 128 * (h // 2 + 1))
        qp = q_ref[0, :, pair]
        qz = jnp.where((lane >= DH_A) == bool(h % 2), qp, jnp.zeros_like(qp))
        return [_dot_nt(k_ref[0, pl.ds(kstart + ATT_KB * i, ATT_KB), pair], qz)
                + bias_ref[0, h, ATT_KB * i:ATT_KB * (i + 1), :] for i in range(ATT_NKB)]

    def finish(h, s):
        m = s[0].max(axis=0, keepdims=True)
        for si in s[1:]:
            m = jnp.maximum(m, si.max(axis=0, keepdims=True))
        l = None
        ot = None
        for i, si in enumerate(s):
            p = jnp.exp2(si - m)
            li = p.sum(axis=0, keepdims=True)
            oi = _dot(vt_ref[0, sb + i, DH_A * h:DH_A * (h + 1), :], p.astype(BF16))
            l = li if l is None else l + li
            ot = oi if ot is None else ot + oi
        ot_ref[DH_A * h:DH_A * (h + 1), :] = ot / l

    pending = scores(0)
    for h in range(H_A):
        upcoming = scores(h + 1) if h + 1 < H_A else None
        finish(h, pending)
        pending = upcoming
    o_ref[0] = ot_ref[...].T.astype(o_ref.dtype)


def _attn_prompt(qk, vt, bias_t):
    bsz, s, _ = qk.shape
    nkb = s // ATT_KB
    return pl.pallas_call(
        _attn_prompt_kernel,
        grid=(bsz, s // ATT_QB),
        in_specs=[pl.BlockSpec((1, ATT_QB, WIDTH_A), lambda b, c: (b, c, 0)),
                  pl.BlockSpec((1, s, WIDTH_A), lambda b, c: (b, 0, 1)),
                  pl.BlockSpec((1, nkb, WIDTH_A, ATT_KB), lambda b, c: (b, 0, 0, 0)),
                  pl.BlockSpec((1, H_A, ATT_NKB * ATT_KB, ATT_QB),
                               lambda b, c: (jnp.minimum(c, 2), 0, 0, 0))],
        out_specs=pl.BlockSpec((1, ATT_QB, WIDTH_A), lambda b, c: (b, c, 0)),
        out_shape=jax.ShapeDtypeStruct((bsz, s, WIDTH_A), BF16),
        scratch_shapes=[pltpu.VMEM((WIDTH_A, ATT_QB), F32)],
        compiler_params=_params("parallel", "arbitrary"),
    )(qk, qk, vt, bias_t)


def _attn_sample_kernel(q_ref, kv_ref, ck_ref, cv_ref, bias_ref, o_ref):
    def scores(h):
        sl = slice(DH_A * h, DH_A * (h + 1))
        qh = q_ref[0, :, sl]
        return (_dot_nt(qh, ck_ref[0, 0, pl.ds(h, WINDOW, stride=H_A), :].astype(BF16))
                + bias_ref[h, :, 0:WINDOW],
                _dot_nt(qh, kv_ref[0, :, sl].astype(BF16)) + bias_ref[h, :, WINDOW:BAND])

    def finish(h, s):
        s1, s2 = s
        sl = slice(DH_A * h, DH_A * (h + 1))
        vsl = slice(WIDTH_A + DH_A * h, WIDTH_A + DH_A * (h + 1))
        m = jnp.maximum(s1.max(axis=-1, keepdims=True), s2.max(axis=-1, keepdims=True))
        p1 = jnp.exp2(s1 - m)
        p2 = jnp.exp2(s2 - m)
        l = p1.sum(axis=-1, keepdims=True) + p2.sum(axis=-1, keepdims=True)
        o = (_dot(p1.astype(BF16), cv_ref[0, 0, pl.ds(h, WINDOW, stride=H_A), :].astype(BF16))
             + _dot(p2.astype(BF16), kv_ref[0, :, vsl].astype(BF16)))
        o_ref[0, :, sl] = (o / l).astype(o_ref.dtype)

    pending = scores(0)
    for h in range(H_A):
        upcoming = scores(h + 1) if h + 1 < H_A else None
        finish(h, pending)
        pending = upcoming


def _attn_sample(qk, kv, cache_k, cache_v, layer, bias):
    bsz = qk.shape[0]
    cache_spec = pl.BlockSpec((1, 1, WINDOW * H_A, DH_A), lambda b: (layer, b, 0, 0))
    return pl.pallas_call(
        _attn_sample_kernel,
        grid=(bsz,),
        in_specs=[pl.BlockSpec((1, CHUNK, WIDTH_A), lambda b: (b, 0, 0)),
                  pl.BlockSpec((1, CHUNK, 2 * WIDTH_A), lambda b: (b, 0, 0)),
                  cache_spec, cache_spec,
                  pl.BlockSpec((H_A, CHUNK, BAND), lambda b: (0, 0, 0))],
        out_specs=pl.BlockSpec((1, CHUNK, WIDTH_A), lambda b: (b, 0, 0)),
        out_shape=jax.ShapeDtypeStruct((bsz, CHUNK, WIDTH_A), BF16),
        compiler_params=_params("parallel"),
    )(qk, kv, cache_k, cache_v, bias)


def _scan_kernel(bqk_ref, bv_ref, bog_ref, cz_ref, cx_ref, misc_ref,
                 gup_ref, gb_ref, gnw_ref, cw_ref, cb_ref, dtb_ref, alog_ref, dsk_ref, snw_ref,
                 gla0_ref, conv0_ref, ssm0_ref,
                 gla_o_ref, ssm_o_ref, gla_s_ref, conv_s_ref, ssm_s_ref,
                 xpad_ref):
    c = pl.program_id(1)
    nseq = bqk_ref.shape[0]

    @pl.when(c == 0)
    def _():
        gla_s_ref[...] = gla0_ref[...]
        conv_s_ref[...] = conv0_ref[...]
        ssm_s_ref[...] = ssm0_ref[...]

    r64 = lax.broadcasted_iota(jnp.int32, (CHUNK, CHUNK), 0)
    c64 = lax.broadcasted_iota(jnp.int32, (CHUNK, CHUNK), 1)
    tri = c64 <= r64
    tri_bf = jnp.where(tri, 1.0, 0.0).astype(BF16)
    ones_bf = jnp.ones((CHUNK, CHUNK), BF16)
    row_x = lax.broadcasted_iota(jnp.int32, (CHUNK, WIDTH_C), 0)
    s_x = lax.broadcasted_iota(jnp.int32, (CHUNK, WIDTH_C), 1) & (CHUNK - 1)
    causal_x = s_x <= row_x
    upper_x = row_x <= s_x
    er = lax.broadcasted_iota(jnp.int32, (128, WIDTH_C), 0)
    el = lax.broadcasted_iota(jnp.int32, (128, WIDTH_C), 1)
    expand_bf = jnp.where(er == MISC_DT_LANE + (el >> 6), 1.0, 0.0).astype(BF16)
    br = lax.broadcasted_iota(jnp.int32, (WIDTH_C, WIDTH_C), 0)
    bl = lax.broadcasted_iota(jnp.int32, (WIDTH_C, WIDTH_C), 1)
    blk64_bf = jnp.where((br >> 6) == (bl >> 6), 1.0, 0.0).astype(BF16)
    blk128_bf = jnp.where((br >> 7) == (bl >> 7), 1.0, 0.0).astype(BF16)

    gup = gup_ref[...]
    gb = gb_ref[...]
    gnw = gnw_ref[...]
    cb = cb_ref[...]
    dtb = dtb_ref[...]
    neg_a = -jnp.exp(alog_ref[...])
    dsk = dsk_ref[...]
    snw = snw_ref[...]

    def load_seq(b):
        return dict(misc=misc_ref[b], qk=bqk_ref[b], v=bv_ref[b], bog=bog_ref[b], cz=cz_ref[b],
                    cx=cx_ref[b], gla=gla_s_ref[b], conv=conv_s_ref[b],
                    ssm=[ssm_s_ref[b, h] for h in range(H_C)])

    def store_seq(b, out):
        gla_o_ref[b] = out["gla_o"]
        ssm_o_ref[b] = out["ssm_o"]
        gla_s_ref[b] = out["gla"]
        conv_s_ref[b] = out["conv"]
        for h in range(H_C):
            ssm_s_ref[b, h] = out["ssm"][h]

    def compute_seq(inp, slot):
        out = {}
        misc = inp["misc"]

        z = _dot(misc.astype(BF16), gup) + gb
        g = (jnp.minimum(z, 0.0) - jnp.log1p(jnp.exp(-jnp.abs(z)))) * (1.0 / GATE_TAU)
        bcum = _dot01_left(tri_bf, g)
        yield
        q = inp["qk"][:, :H_B * DK_B]
        k = inp["qk"][:, H_B * DK_B:]
        blast = bcum[CHUNK - 1:CHUNK, :]
        qb = (q * (DK_B ** -0.5) * jnp.exp(bcum)).astype(BF16)
        kb = (k * jnp.exp(-bcum)).astype(BF16)
        kl = (k * jnp.exp(blast - bcum)).astype(BF16)
        vb = inp["v"].astype(BF16)
        st = inp["gla"]
        st_bf = st.astype(BF16)
        yield
        o_parts, s_parts = [], []
        for h in range(H_B):
            ks = slice(DK_B * h, DK_B * (h + 1))
            vs = slice(DV_B * h, DV_B * (h + 1))
            att = jnp.where(tri, _dot_nt(qb[:, ks], kb[:, ks]), 0.0).astype(BF16)
            o_parts.append(_dot(att, vb[:, vs]) + _dot_nt(qb[:, ks], st_bf[:, ks]))
            s_parts.append(_dot_tn(vb[:, vs], kl[:, ks]))
            yield
        o = jnp.concatenate(o_parts, axis=1)
        out["gla"] = jnp.exp(blast) * st + jnp.concatenate(s_parts, axis=1)
        msq = _dot01_right(o * o, blk64_bf) * (1.0 / DV_B)
        yield
        go = o * lax.rsqrt(msq + EPS) * gnw * _silu(inp["bog"])
        out["gla_o"] = go.astype(gla_o_ref.dtype)
        yield

        cx = inp["cx"]
        xpad_ref[slot, 0:8, :] = inp["conv"]
        xpad_ref[slot, 8:8 + CHUNK, :] = cx
        conv = cb + cx * cw_ref[CONV_W - 1:CONV_W, :]
        for i in range(CONV_W - 1):
            off = 8 - (CONV_W - 1) + i
            conv = conv + xpad_ref[slot, off:off + CHUNK, :] * cw_ref[i:i + 1, :]
        out["conv"] = cx[CHUNK - 8:, :]
        conv = _silu(conv)
        sx = conv[:, :WIDTH_C]
        sb_bf = conv[:, WIDTH_C:WIDTH_C + N_GROUPS_C * D_STATE].astype(BF16)
        sc_bf = conv[:, WIDTH_C + N_GROUPS_C * D_STATE:].astype(BF16)
        yield

        dtv = _softplus(misc + dtb)
        dt_x = _dot01_right(dtv, expand_bf)
        yield
        a_x = neg_a * dt_x
        cs_x = _dot01_left(tri_bf, a_x)
        cs_row = _dot01_left(ones_bf, jnp.where(upper_x, a_x, 0.0))
        yield
        decay = jnp.exp(jnp.where(causal_x, cs_x - cs_row, -jnp.inf))
        cb_parts = []
        for gidx in range(N_GROUPS_C):
            ns = slice(D_STATE * gidx, D_STATE * (gidx + 1))
            cbm = _dot_nt(sc_bf[:, ns], sb_bf[:, ns])
            cb_parts += [cbm] * (H_C // N_GROUPS_C)
            yield
        scores = (jnp.concatenate(cb_parts, axis=1) * decay).astype(BF16)
        xdt = (sx * dt_x).astype(BF16)
        ecs = jnp.exp(cs_x)
        cs_last = cs_x[CHUNK - 1:CHUNK, :]
        xw = (sx * (jnp.exp(cs_last - cs_x) * dt_x)).astype(BF16)
        e_last = jnp.exp(cs_last)
        yield
        y_parts = []
        out["ssm"] = []
        for h in range(H_C):
            ps = slice(P_C * h, P_C * (h + 1))
            gi = h // (H_C // N_GROUPS_C)
            ns = slice(D_STATE * gi, D_STATE * (gi + 1))
            hst = inp["ssm"][h]
            y_parts.append(_dot(scores[:, ps], xdt[:, ps])
                           + _dot_nt(sc_bf[:, ns], hst.astype(BF16)) * ecs[:, ps])
            out["ssm"].append(e_last[:, ps] * hst + _dot_tn(xw[:, ps], sb_bf[:, ns]))
            yield
        y = jnp.concatenate(y_parts, axis=1) + dsk * sx
        y = y * _silu(inp["cz"])
        ymsq = _dot01_right(y * y, blk128_bf) * (1.0 / (WIDTH_C // N_GROUPS_C))
        yield
        out["ssm_o"] = (y * lax.rsqrt(ymsq + EPS) * snw).astype(ssm_o_ref.dtype)
        return out

    def group_body(i, carry):
        seqs = [i * SEQ_UNROLL + u for u in range(SEQ_UNROLL)]
        inputs = [load_seq(b) for b in seqs]
        outputs = _round_robin([compute_seq(inp, slot) for slot, inp in enumerate(inputs)])
        for b, out in zip(seqs, outputs):
            store_seq(b, out)
        return carry

    lax.fori_loop(0, nseq // SEQ_UNROLL, group_body, 0)


def _scan(proj, lw, gla0, conv0, ssm0):
    bsz, l, _ = proj.shape
    nc = l // CHUNK
    sg = min(SEQ_GROUP, bsz)
    ng = bsz // sg

    def col(width, idx):
        return pl.BlockSpec((sg, CHUNK, width), lambda g, c: (g, c, idx))

    def const(shape):
        return pl.BlockSpec(shape, lambda g, c: (0,) * len(shape))

    def state(shape):
        return pl.BlockSpec((sg,) + shape, lambda g, c: (g,) + (0,) * len(shape))

    gla_shape = (DV_B, H_B * DK_B)
    conv_shape = (8, CONV_CH)
    ssm_shape = (H_C, P_C, D_STATE)
    return pl.pallas_call(
        _scan_kernel,
        grid=(ng, nc),
        in_specs=[col(256, 0), col(256, 1), col(256, 2), col(256, 3), col(512, 2), col(128, 12),
                  const((128, 128)), const((1, 128)), const((1, WIDTH_B)),
                  const((CONV_W, CONV_CH)), const((1, CONV_CH)),
                  const((1, 128)), const((1, WIDTH_C)), const((1, WIDTH_C)), const((1, WIDTH_C)),
                  state(gla_shape), state(conv_shape), state(ssm_shape)],
        out_specs=[pl.BlockSpec((sg, CHUNK, WIDTH_B), lambda g, c: (g, c, 0)),
                   pl.BlockSpec((sg, CHUNK, WIDTH_C), lambda g, c: (g, c, 0)),
                   state(gla_shape), state(conv_shape), state(ssm_shape)],
        out_shape=[jax.ShapeDtypeStruct((bsz, l, WIDTH_B), BF16),
                   jax.ShapeDtypeStruct((bsz, l, WIDTH_C), BF16),
                   jax.ShapeDtypeStruct((bsz,) + gla_shape, F32),
                   jax.ShapeDtypeStruct((bsz,) + conv_shape, F32),
                   jax.ShapeDtypeStruct((bsz,) + ssm_shape, F32)],
        scratch_shapes=[pltpu.VMEM((SEQ_UNROLL, 8 + CHUNK, CONV_CH), F32)],
        compiler_params=_params("parallel", "arbitrary"),
    )(proj, proj, proj, proj, proj, proj,
      lw["gup"], lw["gb"], lw["gnw"], lw["cw"], lw["cb"], lw["dtb"], lw["alog"], lw["dsk"], lw["snw"],
      gla0, conv0, ssm0)


OUT_TM = 256


def _out_proj_kernel(x_ref, att_ref, gla_ref, ssm_ref, wo_ref, nw_ref, rwh_ref, rwl_ref, rb_ref,
                     x1_ref, hn_ref, gate_ref):
    mo = (_dot(att_ref[...], wo_ref[0:WIDTH_A, :])
          + _dot(gla_ref[...], wo_ref[WIDTH_A:WIDTH_A + WIDTH_B, :])
          + _dot(ssm_ref[...], wo_ref[WIDTH_A + WIDTH_B:, :]))
    x1 = x_ref[...] + mo
    x1_ref[...] = x1
    ms = jnp.mean(x1 * x1, axis=-1, keepdims=True)
    hn = x1 * lax.rsqrt(ms + EPS) * nw_ref[...]
    hn_ref[...] = hn.astype(hn_ref.dtype)

    h_hi = hn.astype(BF16)
    h_lo = (hn - h_hi.astype(F32)).astype(BF16)
    logits = (_dot(h_hi, rwh_ref[...]) + _dot(h_lo, rwh_ref[...]) + _dot(h_hi, rwl_ref[...])
              + rb_ref[...])
    lane = lax.broadcasted_iota(jnp.int32, logits.shape, 1)
    lane_f = lane.astype(F32)
    neg = -jnp.inf
    big = 1000.0
    gmask = (lane >= ROUTER_GROUP_LANE) & (lane < ROUTER_GROUP_LANE + N_EXPERT_GROUPS)
    lg = jnp.where(gmask, logits, neg)
    gmax = lg.max(axis=-1, keepdims=True)
    g_idx = jnp.where(lg == gmax, lane_f, big).min(axis=-1, keepdims=True) - ROUTER_GROUP_LANE
    p_grp = 1.0 / jnp.where(gmask, jnp.exp(logits - gmax), 0.0).sum(axis=-1, keepdims=True)
    emask = (lane < N_EXPERTS) & ((lane >> 2).astype(F32) == g_idx)
    le = jnp.where(emask, logits, neg)
    v1 = le.max(axis=-1, keepdims=True)
    i1 = jnp.where(le == v1, lane_f, big).min(axis=-1, keepdims=True)
    le2 = jnp.where(lane_f == i1, neg, le)
    v2 = le2.max(axis=-1, keepdims=True)
    i2 = jnp.where(le2 == v2, lane_f, big).min(axis=-1, keepdims=True)
    e2 = jnp.exp(v2 - v1)
    den = 1.0 + e2
    gate_ref[...] = (jnp.where(lane_f == i1, (1.0 / den) * p_grp, 0.0)
                     + jnp.where(lane_f == i2, (e2 / den) * p_grp, 0.0)
                     + jnp.where(lane == ROUTER_GIDX_LANE, g_idx, 0.0))


def _out_proj(x, att, gla, ssm, lw):
    t = x.shape[0]

    def rows(width):
        return pl.BlockSpec((OUT_TM, width), lambda i: (i, 0))

    def const(shape):
        return pl.BlockSpec(shape, lambda i: (0, 0))

    return pl.pallas_call(
        _out_proj_kernel,
        grid=(t // OUT_TM,),
        in_specs=[rows(D_MODEL), rows(WIDTH_A), rows(WIDTH_B), rows(WIDTH_C),
                  const((D_MODEL, D_MODEL)), const((1, D_MODEL)),
                  const((D_MODEL, 128)), const((D_MODEL, 128)), const((1, 128))],
        out_specs=[rows(D_MODEL), rows(D_MODEL), rows(128)],
        out_shape=[jax.ShapeDtypeStruct((t, D_MODEL), F32),
                   jax.ShapeDtypeStruct((t, D_MODEL), BF16),
                   jax.ShapeDtypeStruct((t, 128), F32)],
        compiler_params=_params("parallel"),
    )(x, att, gla, ssm, lw["wo"], lw["n2w"], lw["rw_hi"], lw["rw_lo"], lw["rb"])


RT_TS = 256
RT_ALIGN = 16
RT_ROWS = 384
RT_SIZES = (256, 128, 64, 32, 16)
PAY_COLS = D_MODEL + 256
MOE_TM = 512


def _moe_max_tiles(t):
    return (t + (t // RT_TS) * N_EXPERT_GROUPS * (RT_ALIGN - 1)) // MOE_TM + N_EXPERT_GROUPS


def _sort_matrix(gate, c_ref, tile):
    lane = lax.broadcasted_iota(jnp.int32, (RT_TS, 128), 1)
    gidx = jnp.where(lane == ROUTER_GIDX_LANE, gate, 0.0).sum(axis=1, keepdims=True)
    onehot = lane.astype(F32) == gidx
    r = lax.broadcasted_iota(jnp.int32, (RT_TS, RT_TS), 0)
    c = lax.broadcasted_iota(jnp.int32, (RT_TS, RT_TS), 1)
    earlier = jnp.where(c < r, 1.0, 0.0).astype(BF16)
    before = _dot(earlier, jnp.where(onehot, 1.0, 0.0).astype(BF16))
    start = jnp.zeros((RT_TS, 128), F32)
    for g in range(N_EXPERT_GROUPS):
        start = jnp.where(lane == g, c_ref[N_EXPERT_GROUPS * tile + g].astype(F32), start)
    dest = jnp.where(onehot, before + start, 0.0).sum(axis=1, keepdims=True)
    rows = lax.broadcasted_iota(jnp.int32, (RT_TS, RT_ROWS), 1).astype(F32)
    return jnp.where(rows == dest, 1.0, 0.0).astype(BF16)


def _piece_copies(length, local, glob, make_copy, act):
    for size in RT_SIZES:
        take = (length & size) != 0

        @pl.when(take)
        def _(local=local, glob=glob, size=size):
            act(make_copy(pl.multiple_of(local, RT_ALIGN), pl.multiple_of(glob, RT_ALIGN), size))

        step = jnp.where(take, size, 0)
        local = local + step
        glob = glob + step


def _run_copies(c_ref, lp_ref, off_ref, tile, make_copy, act):
    for g in range(N_EXPERT_GROUPS):
        k = N_EXPERT_GROUPS * tile + g
        _piece_copies(lp_ref[k], c_ref[k], off_ref[k], make_copy, act)


def _dispatch_kernel(c_ref, lp_ref, off_ref, tail_ref, nact_ref, hn_ref, gate_ref, xs_hbm,
                     y_ref, zero_ref, sem, zsem, *, first_spare, max_tiles):
    i = pl.program_id(0)
    nt = pl.num_programs(0)
    slot = i % 2
    gate = gate_ref[...]
    pt = _sort_matrix(gate, c_ref, i)
    g_hi = gate.astype(BF16)
    g_lo = (gate - g_hi.astype(F32)).astype(BF16)
    y_ref[slot, :, 0:D_MODEL] = _dot_tn(pt, hn_ref[...]).astype(BF16)
    y_ref[slot, :, D_MODEL:D_MODEL + 128] = _dot_tn(pt, g_hi).astype(BF16)
    y_ref[slot, :, D_MODEL + 128:] = _dot_tn(pt, g_lo).astype(BF16)

    def copy_from(s):
        return lambda local, glob, size: pltpu.make_async_copy(
            y_ref.at[s, pl.ds(local, size)], xs_hbm.at[pl.ds(glob, size)], sem.at[s])

    _run_copies(c_ref, lp_ref, off_ref, i, copy_from(slot), lambda cp: cp.start())

    @pl.when(i > 0)
    def _():
        _run_copies(c_ref, lp_ref, off_ref, i - 1, copy_from(1 - slot), lambda cp: cp.wait())

    @pl.when(i == nt - 1)
    def _():
        _run_copies(c_ref, lp_ref, off_ref, i, copy_from(slot), lambda cp: cp.wait())
        zero_ref[...] = jnp.zeros_like(zero_ref)

        def zero_copy(local, glob, size):
            return pltpu.make_async_copy(zero_ref.at[pl.ds(local, size)],
                                         xs_hbm.at[pl.ds(glob, size)], zsem)

        def zero_fill(act):
            for g in range(N_EXPERT_GROUPS):
                _piece_copies(tail_ref[N_EXPERT_GROUPS + g], 0, tail_ref[g], zero_copy, act)
            for j in range(first_spare, max_tiles):
                @pl.when(j >= nact_ref[0])
                def _(j=j):
                    act(zero_copy(0, j * MOE_TM, MOE_TM))

        zero_fill(lambda cp: cp.start())
        zero_fill(lambda cp: cp.wait())


def _moe_group_kernel(grp_ref, nact_ref, xs_ref, wg_ref, wu_ref, wd_ref, o_ref, wg_bf, wu_bf, wd_bf):
    j = pl.program_id(0)

    @pl.when(jnp.logical_or(j == 0, grp_ref[j] != grp_ref[jnp.maximum(j - 1, 0)]))
    def _():
        wg_bf[...] = wg_ref[0].astype(BF16)
        wu_bf[...] = wu_ref[0].astype(BF16)
        wd_bf[...] = wd_ref[0].astype(BF16)

    @pl.when(j >= nact_ref[0])
    def _():
        o_ref[...] = jnp.zeros_like(o_ref)

    @pl.when(j < nact_ref[0])
    def _():
        x = xs_ref[:, 0:D_MODEL]
        gate = (xs_ref[:, D_MODEL:D_MODEL + 128].astype(F32)
                + xs_ref[:, D_MODEL + 128:].astype(F32))
        lane = lax.broadcasted_iota(jnp.int32, gate.shape, 1)
        first = grp_ref[j] * EXPERTS_PER_GROUP
        acc = None
        for e in range(EXPERTS_PER_GROUP):
            ge = jnp.where(lane == first + e, gate, 0.0).sum(axis=-1, keepdims=True)
            hid = _silu(_dot(x, wg_bf[e])) * _dot(x, wu_bf[e]) * ge
            part = _dot(hid.astype(BF16), wd_bf[e])
            acc = part if acc is None else acc + part
        o_ref[...] = acc


def _combine_kernel(c_ref, lp_ref, off_ref, x_ref, gate_ref, fw_ref, ys_hbm, o_ref, z_ref, sem,
                    *, final_norm):
    i = pl.program_id(0)
    nt = pl.num_programs(0)
    slot = i % 2

    def copy_into(s):
        return lambda local, glob, size: pltpu.make_async_copy(
            ys_hbm.at[pl.ds(glob, size)], z_ref.at[s, pl.ds(local, size)], sem.at[s])

    @pl.when(i == 0)
    def _():
        z_ref[...] = jnp.zeros_like(z_ref)
        _run_copies(c_ref, lp_ref, off_ref, 0, copy_into(0), lambda cp: cp.start())

    @pl.when(i + 1 < nt)
    def _():
        _run_copies(c_ref, lp_ref, off_ref, i + 1, copy_into(1 - slot), lambda cp: cp.start())

    _run_copies(c_ref, lp_ref, off_ref, i, copy_into(slot), lambda cp: cp.wait())
    pt = _sort_matrix(gate_ref[...], c_ref, i)
    z = z_ref[slot]
    z_hi = z.astype(BF16)
    z_lo = (z - z_hi.astype(F32)).astype(BF16)
    y = x_ref[...] + _dot(pt, z_hi) + _dot(pt, z_lo)
    if final_norm:
        ms = jnp.mean(y * y, axis=-1, keepdims=True)
        y = y * lax.rsqrt(ms + EPS) * fw_ref[...]
    o_ref[...] = y


def _route_meta(gate, max_tiles):
    t = gate.shape[0]
    nt = t // RT_TS
    groups = jnp.arange(N_EXPERT_GROUPS, dtype=jnp.int32)
    g = gate[:, ROUTER_GIDX_LANE].astype(jnp.int32).reshape(nt, RT_TS, 1)
    count = (g == groups).astype(jnp.int32).sum(axis=1)
    lp = (count + RT_ALIGN - 1) // RT_ALIGN * RT_ALIGN
    total = lp.sum(axis=0)
    ntile = (total + MOE_TM - 1) // MOE_TM
    last = jnp.cumsum(ntile)
    first_row = (last - ntile) * MOE_TM
    c = jnp.cumsum(lp, axis=1) - lp
    off = first_row[None, :] + jnp.cumsum(lp, axis=0) - lp
    tail = jnp.concatenate([first_row + total, ntile * MOE_TM - total])
    j = jnp.arange(max_tiles, dtype=jnp.int32)[:, None]
    grp = jnp.minimum((j >= last[None, :]).astype(jnp.int32).sum(axis=1), N_EXPERT_GROUPS - 1)
    return c.reshape(-1), lp.reshape(-1), off.reshape(-1), tail, grp, last[-1:]


def _moe(x, hn, gate, lw, layer, fw, final_norm):
    t = x.shape[0]
    nt = t // RT_TS
    max_tiles = _moe_max_tiles(t)
    c, lp, off, tail, grp, nact = _route_meta(gate, max_tiles)

    def rows(width):
        return pl.BlockSpec((RT_TS, width), lambda i, *_: (i, 0))

    xs = pl.pallas_call(
        functools.partial(_dispatch_kernel, first_spare=t // MOE_TM, max_tiles=max_tiles),
        grid_spec=pltpu.PrefetchScalarGridSpec(
            num_scalar_prefetch=5, grid=(nt,),
            in_specs=[rows(D_MODEL), rows(128)],
            out_specs=pl.BlockSpec(memory_space=pl.ANY),
            scratch_shapes=[pltpu.VMEM((2, RT_ROWS, PAY_COLS), BF16),
                            pltpu.VMEM((MOE_TM, PAY_COLS), BF16),
                            pltpu.SemaphoreType.DMA((2,)), pltpu.SemaphoreType.DMA(())]),
        out_shape=jax.ShapeDtypeStruct((max_tiles * MOE_TM, PAY_COLS), BF16),
        compiler_params=_params("arbitrary"),
    )(c, lp, off, tail, nact, hn, gate)

    def experts(shape):
        return pl.BlockSpec((1, EXPERTS_PER_GROUP) + shape, lambda j, grp, nact: (layer, grp[j], 0, 0))

    def rounded(shape):
        return pltpu.VMEM((EXPERTS_PER_GROUP,) + shape, BF16)

    ys = pl.pallas_call(
        _moe_group_kernel,
        grid_spec=pltpu.PrefetchScalarGridSpec(
            num_scalar_prefetch=2, grid=(max_tiles,),
            in_specs=[pl.BlockSpec((MOE_TM, PAY_COLS), lambda j, grp, nact: (j, 0)),
                      experts((D_MODEL, D_EXPERT)), experts((D_MODEL, D_EXPERT)),
                      experts((D_EXPERT, D_MODEL))],
            out_specs=pl.BlockSpec((MOE_TM, D_MODEL), lambda j, grp, nact: (j, 0)),
            scratch_shapes=[rounded((D_MODEL, D_EXPERT)), rounded((D_MODEL, D_EXPERT)),
                            rounded((D_EXPERT, D_MODEL))]),
        out_shape=jax.ShapeDtypeStruct((max_tiles * MOE_TM, D_MODEL), F32),
        compiler_params=_params("arbitrary"),
    )(grp, nact, xs, lw["wg"], lw["wu"], lw["wd"])

    return pl.pallas_call(
        functools.partial(_combine_kernel, final_norm=final_norm),
        grid_spec=pltpu.PrefetchScalarGridSpec(
            num_scalar_prefetch=3, grid=(nt,),
            in_specs=[rows(D_MODEL), rows(128), pl.BlockSpec((1, D_MODEL), lambda i, *_: (0, 0)),
                      pl.BlockSpec(memory_space=pl.ANY)],
            out_specs=rows(D_MODEL),
            scratch_shapes=[pltpu.VMEM((2, RT_ROWS, D_MODEL), F32),
                            pltpu.SemaphoreType.DMA((2,))]),
        out_shape=jax.ShapeDtypeStruct((t, D_MODEL), F32),
        compiler_params=_params("arbitrary"),
    )(c, lp, off, x, gate, fw, ys)


def _lane_place(vec, start, width=128):
    return jnp.zeros((1, width), F32).at[0, start:start + vec.shape[0]].set(vec.astype(F32))


REL_PAD = 704


def _toeplitz(v, rows, cols):
    n = v.shape[-1]
    tiled = jnp.tile(v, (1,) * (v.ndim - 1) + (rows,))[..., :rows * (n - 1)]
    return tiled.reshape(v.shape[:-1] + (rows, n - 1))[..., :cols]


def _band_bias(table):
    t = table.astype(F32)
    ext = jnp.concatenate([jnp.repeat(t[:, :1], REL_PAD, axis=1), t,
                           jnp.repeat(t[:, -1:], REL_PAD, axis=1)], axis=1)
    extr = ext[:, ::-1]
    top = ext.shape[1] - 1 - (REL_CLIP + REL_PAD)

    def band(off, nq, nk):
        n = nq + nk
        v = jnp.concatenate([extr[:, top - off:top - off + nk + 1],
                             extr[:, top - off - (nq - 1):top - off]], axis=1)
        assert v.shape[1] == n
        return _toeplitz(v, nq, nk)

    sample = band(WINDOW, CHUNK, BAND)
    nk = ATT_NKB * ATT_KB
    kc = (jnp.arange(nk) // CHUNK)[:, None]
    qc = (jnp.arange(ATT_QB) // CHUNK)[None, :]
    valid = [kc <= qc, kc <= qc + ATT_QB // CHUNK, (kc >= qc) & (kc <= qc + N_PAST_CHUNKS)]
    prompt = jnp.stack([
        jnp.where(valid[w][None], jnp.swapaxes(band(w * ATT_QB, ATT_QB, nk), 1, 2), -1e30)
        for w in range(3)])
    return sample * LOG2_E, prompt * LOG2_E


def _layer_weights(i, norm1_w, w_in, rel_bias_table, gla_w_gate_up, gla_b_gate, gla_norm_w,
                   ssm_conv_w, ssm_conv_b, ssm_dt_bias, ssm_a_log, ssm_d, ssm_norm_w, w_out,
                   norm2_w, router_group_w, router_group_b, router_expert_w, router_expert_b,
                   exp_w_gate, exp_w_up, exp_w_down):
    wi = w_in[i]
    o = [0, 512, 1024, 1536, 1664, 1792, 2048, 2064, 2320, 2576, 3088, 3092]
    aq, ak, av, bq, bk, bv, bglr, bog, cz, cxbc, cdt = [wi[:, o[j]:o[j + 1]] for j in range(11)]
    pad = jnp.zeros((D_MODEL, 128 - GATE_RANK - H_C), wi.dtype)
    w_perm = jnp.concatenate([aq * (DH_A ** -0.5 * LOG2_E), ak, av, bq, bk, bv, bog, cz, cxbc, bglr, cdt, pad],
                             axis=1)
    bias_s, bias_p = _band_bias(rel_bias_table[i])
    rw = jnp.concatenate([router_expert_w[i], router_group_w[i],
                          jnp.zeros((D_MODEL, 128 - N_EXPERTS - N_EXPERT_GROUPS), F32)], axis=1)
    rw_hi = rw.astype(BF16)
    return dict(
        n1w=norm1_w[i][None, :],
        w_in=w_perm.astype(BF16),
        w_vt=av.T.astype(BF16),
        bias_s=bias_s,
        bias_p=bias_p,
        gup=jnp.zeros((128, 128), F32).at[:GATE_RANK, :].set(gla_w_gate_up[i]).astype(BF16),
        gb=gla_b_gate[i][None, :].astype(F32),
        gnw=jnp.tile(gla_norm_w[i], H_B)[None, :].astype(F32),
        cw=ssm_conv_w[i].astype(F32),
        cb=ssm_conv_b[i][None, :].astype(F32),
        dtb=_lane_place(ssm_dt_bias[i], MISC_DT_LANE),
        alog=jnp.repeat(ssm_a_log[i].astype(F32), P_C)[None, :],
        dsk=jnp.repeat(ssm_d[i].astype(F32), P_C)[None, :],
        snw=ssm_norm_w[i][None, :].astype(F32),
        wo=w_out[i].astype(BF16),
        n2w=norm2_w[i][None, :],
        rw_hi=rw_hi,
        rw_lo=(rw - rw_hi.astype(F32)).astype(BF16),
        rb=jnp.concatenate([router_expert_b[i], router_group_b[i],
                            jnp.zeros((128 - N_EXPERTS - N_EXPERT_GROUPS,), F32)])[None, :],
        wg=exp_w_gate,
        wu=exp_w_up,
        wd=exp_w_down,
    )


def _stream_layer(x, bsz, lw, layer, cache_k, cache_v, gla0, conv0, ssm0, fw, final_norm):
    l = x.shape[0] // bsz
    if cache_k is None:
        qk, rest, k_keep, v_keep, vt = _in_proj(x, lw["n1w"], lw["w_in"], lw["w_vt"], l)
        att = _attn_prompt(qk.reshape(bsz, l, 2 * WIDTH_A),
                           vt.reshape(bsz, l // ATT_KB, WIDTH_A, ATT_KB), lw["bias_p"])
    else:
        qk, rest, k_keep, v_keep, kv = _in_proj(x, lw["n1w"], lw["w_in"], None, l)
        att = _attn_sample(qk.reshape(bsz, l, 2 * WIDTH_A), kv.reshape(bsz, l, 2 * WIDTH_A),
                           cache_k, cache_v, layer, lw["bias_s"])
    gla_o, ssm_o, gla_s, conv_s, ssm_s = _scan(rest.reshape(bsz, l, REST_COLS), lw, gla0, conv0, ssm0)
    t = bsz * l
    x1, hn, gate = _out_proj(x, att.reshape(t, WIDTH_A), gla_o.reshape(t, WIDTH_B),
                             ssm_o.reshape(t, WIDTH_C), lw)
    x2 = _moe(x1, hn, gate, lw, layer, fw, final_norm)
    keep = min(WINDOW, l)
    k_keep = k_keep.reshape(bsz, keep, H_A, DH_A)
    v_keep = v_keep.reshape(bsz, keep, H_A, DH_A)
    gla_state = gla_s.reshape(bsz, DV_B, H_B, DK_B).transpose(0, 2, 3, 1)
    conv_state = conv_s[:, 8 - (CONV_W - 1):, :]
    return x2, k_keep, v_keep, gla_state, conv_state, ssm_s


def kernel(x_prompt, x_sample, cache_k_a, cache_v_a, state_gla, state_conv, state_ssm, norm1_w, w_in, rel_bias_table, gla_w_gate_up, gla_b_gate, gla_norm_w, ssm_conv_w, ssm_conv_b, ssm_dt_bias, ssm_a_log, ssm_d, ssm_norm_w, w_out, norm2_w, router_group_w, router_group_b, router_expert_w, router_expert_b, exp_w_gate, exp_w_up, exp_w_down, final_norm_w):
    bp, sp, _ = x_prompt.shape
    bs, ss, _ = x_sample.shape
    xp = x_prompt.reshape(bp * sp, D_MODEL)
    xs = x_sample.reshape(bs * ss, D_MODEL)
    fw = final_norm_w[None, :].astype(F32)
    outs_p, outs_s = [], []
    for i in range(DEPTH):
        lw = _layer_weights(i, norm1_w, w_in, rel_bias_table, gla_w_gate_up, gla_b_gate, gla_norm_w,
                            ssm_conv_w, ssm_conv_b, ssm_dt_bias, ssm_a_log, ssm_d, ssm_norm_w, w_out,
                            norm2_w, router_group_w, router_group_b, router_expert_w, router_expert_b,
                            exp_w_gate, exp_w_up, exp_w_down)
        last = i == DEPTH - 1
        xp, *sp_out = _stream_layer(
            xp, bp, lw, i, None, None,
            jnp.zeros((bp, DV_B, H_B * DK_B), F32),
            jnp.zeros((bp, 8, CONV_CH), F32),
            jnp.zeros((bp, H_C, P_C, D_STATE), F32), fw, last)
        outs_p.append(sp_out)
        gla0 = state_gla[i].astype(F32).transpose(0, 3, 1, 2).reshape(bs, DV_B, H_B * DK_B)
        conv0 = jnp.pad(state_conv[i].astype(F32), ((0, 0), (8 - (CONV_W - 1), 0), (0, 0)))
        xs, *ss_out = _stream_layer(
            xs, bs, lw, i, cache_k_a.reshape(DEPTH, bs, -1, DH_A), cache_v_a.reshape(DEPTH, bs, -1, DH_A),
            gla0, conv0, state_ssm[i].astype(F32), fw, last)
        outs_s.append(ss_out)
    stack = lambda outs, j: jnp.stack([o[j] for o in outs])
    return (xp.reshape(bp, sp, D_MODEL), xs.reshape(bs, ss, D_MODEL),
            stack(outs_p, 0), stack(outs_p, 1), stack(outs_p, 2), stack(outs_p, 3), stack(outs_p, 4),
            stack(outs_s, 0), stack(outs_s, 1), stack(outs_s, 2), stack(outs_s, 3), stack(outs_s, 4))
```

```python
import functools

import jax
import jax.numpy as jnp
from jax import lax
from jax.experimental import pallas as pl
from jax.experimental.pallas import tpu as pltpu

F32 = jnp.float32
BF16 = jnp.bfloat16

D_MODEL = 1024
DEPTH = 2
EPS = 1e-6
CHUNK = 64
N_PAST_CHUNKS = 8
BAND = (N_PAST_CHUNKS + 1) * CHUNK
WINDOW = N_PAST_CHUNKS * CHUNK
H_A, DH_A, WIDTH_A = 8, 64, 512
REL_CLIP = 128
LOG2_E = 1.4426950408889634
H_B, DK_B, DV_B, WIDTH_B = 4, 32, 64, 256
GATE_RANK = 16
GATE_TAU = 16.0
H_C, P_C, WIDTH_C = 4, 64, 256
N_GROUPS_C = 2
D_STATE = 64
CONV_W = 4
CONV_CH = 512
N_EXPERT_GROUPS = 4
EXPERTS_PER_GROUP = 4
N_EXPERTS = 16
D_EXPERT = 256

PROJ_COLS = 3200
MISC_DT_LANE = GATE_RANK
ROUTER_GROUP_LANE = N_EXPERTS
ROUTER_GIDX_LANE = 20

SEQ_GROUP = 8
SEQ_UNROLL = 8
VMEM_LIMIT_BYTES = 56 * 1024 * 1024

NT_DIMS = (((1,), (1,)), ((), ()))
TN_DIMS = (((0,), (0,)), ((), ()))


def _params(*sem):
    return pltpu.CompilerParams(dimension_semantics=sem, vmem_limit_bytes=VMEM_LIMIT_BYTES)


def _dot(a, b):
    return jnp.dot(a, b, preferred_element_type=F32)


def _dot_nt(a, b):
    return lax.dot_general(a, b, NT_DIMS, preferred_element_type=F32)


def _dot_tn(a, b):
    return lax.dot_general(a, b, TN_DIMS, preferred_element_type=F32)


def _split2(a):
    hi = a.astype(BF16)
    return hi, (a - hi.astype(F32)).astype(BF16)


def _dot01_left(m01, a):
    hi, lo = _split2(a)
    return _dot(m01, hi) + _dot(m01, lo)


def _dot01_right(a, m01):
    hi, lo = _split2(a)
    return _dot(hi, m01) + _dot(lo, m01)


def _round_robin(gens):
    results = [None] * len(gens)
    live = list(range(len(gens)))
    while live:
        for idx in list(live):
            try:
                next(gens[idx])
            except StopIteration as stop:
                results[idx] = stop.value
                live.remove(idx)
    return results


def _silu(x):
    return x * jax.nn.sigmoid(x)


def _softplus(x):
    return jnp.maximum(x, 0.0) + jnp.log1p(jnp.exp(-jnp.abs(x)))


IN_TM = 256
QKV_COLS = 3 * WIDTH_A
REST_COLS = PROJ_COLS - QKV_COLS
REST_CHUNKS = ((0, 512), (512, 1024), (1024, REST_COLS))


IN_COLS = 3092
IN_TAIL = QKV_COLS + 512
PREP_ROWS = 128


def _prepare_w_in(w_in_ref, w_ref, wvt_ref):
    scale = DH_A ** -0.5 * LOG2_E
    n_tail = REST_COLS - 512 - 128
    for r in range(0, D_MODEL, PREP_ROWS):
        rs = slice(r, r + PREP_ROWS)
        w_ref[rs, 0:WIDTH_A] = (w_in_ref[0, rs, 0:WIDTH_A] * scale).astype(BF16)
        w_ref[rs, WIDTH_A:IN_TAIL] = w_in_ref[0, rs, WIDTH_A:IN_TAIL].astype(BF16)
        tail = w_in_ref[0, rs, IN_TAIL:IN_COLS]
        w_ref[rs, IN_TAIL:IN_TAIL + n_tail] = tail[:, GATE_RANK:GATE_RANK + n_tail].astype(BF16)
        misc = jnp.concatenate([tail[:, 0:GATE_RANK], tail[:, GATE_RANK + n_tail:],
                                jnp.zeros((PREP_ROWS, 128 - GATE_RANK - H_C), F32)], axis=1)
        w_ref[rs, IN_TAIL + n_tail:] = misc.astype(BF16)
    if wvt_ref is not None:
        wvt_ref[...] = w_in_ref[0, :, 2 * WIDTH_A:QKV_COLS].T.astype(BF16)


def _in_proj_kernel(x_ref, nw_ref, w_in_ref, *refs, with_vt):
    if with_vt:
        qk_ref, rest_ref, kk_ref, vk_ref, vt_ref, w_ref, wvt_ref = refs
    else:
        qk_ref, rest_ref, kk_ref, vk_ref, kv_ref, w_ref = refs
        wvt_ref = None

    @pl.when(pl.program_id(0) == 0)
    def _():
        _prepare_w_in(w_in_ref, w_ref, wvt_ref)

    x = x_ref[...]
    ms = jnp.mean(x * x, axis=-1, keepdims=True)
    xn = (x * lax.rsqrt(ms + EPS) * nw_ref[...]).astype(BF16)
    qk_ref[:, 0:WIDTH_A] = _dot(xn, w_ref[:, 0:WIDTH_A]).astype(BF16)
    k = _dot(xn, w_ref[:, WIDTH_A:2 * WIDTH_A])
    v = _dot(xn, w_ref[:, 2 * WIDTH_A:QKV_COLS])
    qk_ref[:, WIDTH_A:] = k.astype(BF16)
    if not with_vt:
        kv_ref[:, 0:WIDTH_A] = k
        kv_ref[:, WIDTH_A:] = v

    def keep_rows():
        kk = kk_ref.reshape(IN_TM * H_A, DH_A)
        vk = vk_ref.reshape(IN_TM * H_A, DH_A)
        for h in range(H_A):
            sl = slice(DH_A * h, DH_A * (h + 1))
            kk[pl.ds(h, IN_TM, stride=H_A), :] = k[:, sl]
            vk[pl.ds(h, IN_TM, stride=H_A), :] = v[:, sl]

    keep_rows()
    for lo, hi in REST_CHUNKS:
        rest_ref[:, lo:hi] = _dot(xn, w_ref[:, QKV_COLS + lo:QKV_COLS + hi])
    if with_vt:
        vt_ref[0] = _dot_nt(wvt_ref[...], xn).astype(BF16)


def _in_proj(x, nw, w_in, layer, with_vt, seq_len):
    t = x.shape[0]
    tiles_per_seq = max(seq_len // IN_TM, 1)
    keep_tiles = max(min(WINDOW, seq_len) // IN_TM, 1)
    n_keep = t // IN_TM // tiles_per_seq * keep_tiles

    def rows(width):
        return pl.BlockSpec((IN_TM, width), lambda i: (i, 0))

    def kept(i):
        return (i // tiles_per_seq * keep_tiles
                + jnp.maximum(i % tiles_per_seq - (tiles_per_seq - keep_tiles), 0), 0, 0)

    keep_spec = pl.BlockSpec((IN_TM, H_A, DH_A), kept)
    keep_shape = jax.ShapeDtypeStruct((n_keep * IN_TM, H_A, DH_A), F32)
    in_specs = [rows(D_MODEL), pl.BlockSpec((1, D_MODEL), lambda i: (0, 0)),
                pl.BlockSpec((1, D_MODEL, IN_COLS), lambda i: (layer, 0, 0))]
    scratch = [pltpu.VMEM((D_MODEL, PROJ_COLS), BF16)]
    out_specs = [rows(2 * WIDTH_A), rows(REST_COLS), keep_spec, keep_spec]
    out_shape = [jax.ShapeDtypeStruct((t, 2 * WIDTH_A), BF16),
                 jax.ShapeDtypeStruct((t, REST_COLS), F32), keep_shape, keep_shape]
    if with_vt:
        out_specs.append(pl.BlockSpec((1, WIDTH_A, IN_TM), lambda i: (i, 0, 0)))
        out_shape.append(jax.ShapeDtypeStruct((t // IN_TM, WIDTH_A, IN_TM), BF16))
        scratch.append(pltpu.VMEM((WIDTH_A, D_MODEL), BF16))
    else:
        out_specs.append(rows(2 * WIDTH_A))
        out_shape.append(jax.ShapeDtypeStruct((t, 2 * WIDTH_A), F32))
    return pl.pallas_call(
        functools.partial(_in_proj_kernel, with_vt=with_vt),
        grid=(t // IN_TM,),
        in_specs=in_specs, out_specs=out_specs, out_shape=out_shape, scratch_shapes=scratch,
        compiler_params=_params("arbitrary"),
    )(x, nw, w_in)


ATT_QB = 4 * CHUNK
ATT_KB = IN_TM
ATT_NKB = (ATT_QB + WINDOW) // ATT_KB


def _attn_prompt_kernel(q_ref, k_ref, vt_ref, bias_ref, o_ref, ot_ref):
    blk = pl.program_id(1)
    sb = jnp.maximum(blk - WINDOW // ATT_KB, 0)
    kstart = pl.multiple_of(sb * ATT_KB, ATT_KB)
    lane = lax.broadcasted_iota(jnp.int32, (ATT_QB, 128), 1)

    def scores(h):
        pair = slice(128 * (h // 2), 128 * (h // 2 + 1))
        qp = q_ref[0, :, pair]
        qz = jnp.where((lane >= DH_A) == bool(h % 2), qp, jnp.zeros_like(qp))
        return [_dot_nt(k_ref[0, pl.ds(kstart + ATT_KB * i, ATT_KB), pair], qz)
                + bias_ref[0, h, ATT_KB * i:ATT_KB * (i + 1), :] for i in range(ATT_NKB)]

    def finish(h, s):
        m = s[0].max(axis=0, keepdims=True)
        for si in s[1:]:
            m = jnp.maximum(m, si.max(axis=0, keepdims=True))
        l = None
        ot = None
        for i, si in enumerate(s):
            p = jnp.exp2(si - m)
            li = p.sum(axis=0, keepdims=True)
            oi = _dot(vt_ref[0, sb + i, DH_A * h:DH_A * (h + 1), :], p.astype(BF16))
            l = li if l is None else l + li
            ot = oi if ot is None else ot + oi
        ot_ref[DH_A * h:DH_A * (h + 1), :] = ot / l

    pending = scores(0)
    for h in range(H_A):
        upcoming = scores(h + 1) if h + 1 < H_A else None
        finish(h, pending)
        pending = upcoming
    o_ref[0] = ot_ref[...].T.astype(o_ref.dtype)


def _attn_prompt(qk, vt, bias_t):
    bsz, s, _ = qk.shape
    nkb = s // ATT_KB
    return pl.pallas_call(
        _attn_prompt_kernel,
        grid=(bsz, s // ATT_QB),
        in_specs=[pl.BlockSpec((1, ATT_QB, WIDTH_A), lambda b, c: (b, c, 0)),
                  pl.BlockSpec((1, s, WIDTH_A), lambda b, c: (b, 0, 1)),
                  pl.BlockSpec((1, nkb, WIDTH_A, ATT_KB), lambda b, c: (b, 0, 0, 0)),
                  pl.BlockSpec((1, H_A, ATT_NKB * ATT_KB, ATT_QB),
                               lambda b, c: (jnp.minimum(c, 2), 0, 0, 0))],
        out_specs=pl.BlockSpec((1, ATT_QB, WIDTH_A), lambda b, c: (b, c, 0)),
        out_shape=jax.ShapeDtypeStruct((bsz, s, WIDTH_A), BF16),
        scratch_shapes=[pltpu.VMEM((WIDTH_A, ATT_QB), F32)],
        compiler_params=_params("parallel", "arbitrary"),
    )(qk, qk, vt, bias_t)


def _attn_sample_kernel(q_ref, kv_ref, ck_ref, cv_ref, bias_ref, o_ref):
    ck = ck_ref.at[0, 0].reshape(WINDOW * H_A, DH_A)
    cv = cv_ref.at[0, 0].reshape(WINDOW * H_A, DH_A)

    def scores(h):
        sl = slice(DH_A * h, DH_A * (h + 1))
        qh = q_ref[0, :, sl]
        return (_dot_nt(qh, ck[pl.ds(h, WINDOW, stride=H_A), :].astype(BF16))
                + bias_ref[h, :, 0:WINDOW],
                _dot_nt(qh, kv_ref[0, :, sl].astype(BF16)) + bias_ref[h, :, WINDOW:BAND])

    def finish(h, s):
        s1, s2 = s
        sl = slice(DH_A * h, DH_A * (h + 1))
        vsl = slice(WIDTH_A + DH_A * h, WIDTH_A + DH_A * (h + 1))
        m = jnp.maximum(s1.max(axis=-1, keepdims=True), s2.max(axis=-1, keepdims=True))
        p1 = jnp.exp2(s1 - m)
        p2 = jnp.exp2(s2 - m)
        l = p1.sum(axis=-1, keepdims=True) + p2.sum(axis=-1, keepdims=True)
        o = (_dot(p1.astype(BF16), cv[pl.ds(h, WINDOW, stride=H_A), :].astype(BF16))
             + _dot(p2.astype(BF16), kv_ref[0, :, vsl].astype(BF16)))
        o_ref[0, :, sl] = (o / l).astype(o_ref.dtype)

    pending = scores(0)
    for h in range(H_A):
        upcoming = scores(h + 1) if h + 1 < H_A else None
        finish(h, pending)
        pending = upcoming


def _attn_sample(qk, kv, cache_k, cache_v, layer, bias):
    bsz = qk.shape[0]
    cache_spec = pl.BlockSpec((1, 1, WINDOW, H_A, DH_A), lambda b: (layer, b, 0, 0, 0))
    return pl.pallas_call(
        _attn_sample_kernel,
        grid=(bsz,),
        in_specs=[pl.BlockSpec((1, CHUNK, WIDTH_A), lambda b: (b, 0, 0)),
                  pl.BlockSpec((1, CHUNK, 2 * WIDTH_A), lambda b: (b, 0, 0)),
                  cache_spec, cache_spec,
                  pl.BlockSpec((H_A, CHUNK, BAND), lambda b: (0, 0, 0))],
        out_specs=pl.BlockSpec((1, CHUNK, WIDTH_A), lambda b: (b, 0, 0)),
        out_shape=jax.ShapeDtypeStruct((bsz, CHUNK, WIDTH_A), BF16),
        compiler_params=_params("parallel"),
    )(qk, kv, cache_k, cache_v, bias)


def _scan_kernel(bqk_ref, bv_ref, bog_ref, cz_ref, cx_ref, misc_ref,
                 gup_ref, gb_ref, gnw_ref, cw_ref, cb_ref, dtb_ref, alog_ref, dsk_ref, snw_ref,
                 gla0_ref, conv0_ref, ssm0_ref,
                 gla_o_ref, ssm_o_ref, gla_s_ref, conv_s_ref, ssm_s_ref,
                 xpad_ref):
    c = pl.program_id(1)
    nseq = bqk_ref.shape[0]

    @pl.when(c == 0)
    def _():
        gla_s_ref[...] = gla0_ref[...]
        conv_s_ref[...] = conv0_ref[...]
        ssm_s_ref[...] = ssm0_ref[...]

    r64 = lax.broadcasted_iota(jnp.int32, (CHUNK, CHUNK), 0)
    c64 = lax.broadcasted_iota(jnp.int32, (CHUNK, CHUNK), 1)
    tri = c64 <= r64
    tri_bf = jnp.where(tri, 1.0, 0.0).astype(BF16)
    ones_bf = jnp.ones((CHUNK, CHUNK), BF16)
    row_x = lax.broadcasted_iota(jnp.int32, (CHUNK, WIDTH_C), 0)
    s_x = lax.broadcasted_iota(jnp.int32, (CHUNK, WIDTH_C), 1) & (CHUNK - 1)
    causal_x = s_x <= row_x
    upper_x = row_x <= s_x
    er = lax.broadcasted_iota(jnp.int32, (128, WIDTH_C), 0)
    el = lax.broadcasted_iota(jnp.int32, (128, WIDTH_C), 1)
    expand_bf = jnp.where(er == MISC_DT_LANE + (el >> 6), 1.0, 0.0).astype(BF16)
    br = lax.broadcasted_iota(jnp.int32, (WIDTH_C, WIDTH_C), 0)
    bl = lax.broadcasted_iota(jnp.int32, (WIDTH_C, WIDTH_C), 1)
    blk64_bf = jnp.where((br >> 6) == (bl >> 6), 1.0, 0.0).astype(BF16)
    blk128_bf = jnp.where((br >> 7) == (bl >> 7), 1.0, 0.0).astype(BF16)

    gup = gup_ref[...]
    gb = gb_ref[...]
    gnw = gnw_ref[...]
    cb = cb_ref[...]
    dtb = dtb_ref[...]
    neg_a = -jnp.exp(alog_ref[...])
    dsk = dsk_ref[...]
    snw = snw_ref[...]

    def load_seq(b):
        return dict(misc=misc_ref[b], qk=bqk_ref[b], v=bv_ref[b], bog=bog_ref[b], cz=cz_ref[b],
                    cx=cx_ref[b], gla=gla_s_ref[b], conv=conv_s_ref[b],
                    ssm=[ssm_s_ref[b, h] for h in range(H_C)])

    def store_seq(b, out):
        gla_o_ref[b] = out["gla_o"]
        ssm_o_ref[b] = out["ssm_o"]
        gla_s_ref[b] = out["gla"]
        conv_s_ref[b] = out["conv"]
        for h in range(H_C):
            ssm_s_ref[b, h] = out["ssm"][h]

    def compute_seq(inp, slot):
        out = {}
        misc = inp["misc"]

        z = _dot(misc.astype(BF16), gup) + gb
        g = (jnp.minimum(z, 0.0) - jnp.log1p(jnp.exp(-jnp.abs(z)))) * (1.0 / GATE_TAU)
        bcum = _dot01_left(tri_bf, g)
        yield
        q = inp["qk"][:, :H_B * DK_B]
        k = inp["qk"][:, H_B * DK_B:]
        blast = bcum[CHUNK - 1:CHUNK, :]
        qb = (q * (DK_B ** -0.5) * jnp.exp(bcum)).astype(BF16)
        kb = (k * jnp.exp(-bcum)).astype(BF16)
        kl = (k * jnp.exp(blast - bcum)).astype(BF16)
        vb = inp["v"].astype(BF16)
        st = inp["gla"]
        st_bf = st.astype(BF16)
        yield
        o_parts, s_parts = [], []
        for h in range(H_B):
            ks = slice(DK_B * h, DK_B * (h + 1))
            vs = slice(DV_B * h, DV_B * (h + 1))
            att = jnp.where(tri, _dot_nt(qb[:, ks], kb[:, ks]), 0.0).astype(BF16)
            o_parts.append(_dot(att, vb[:, vs]) + _dot_nt(qb[:, ks], st_bf[:, ks]))
            s_parts.append(_dot_tn(vb[:, vs], kl[:, ks]))
            yield
        o = jnp.concatenate(o_parts, axis=1)
        out["gla"] = jnp.exp(blast) * st + jnp.concatenate(s_parts, axis=1)
        msq = _dot01_right(o * o, blk64_bf) * (1.0 / DV_B)
        yield
        go = o * lax.rsqrt(msq + EPS) * gnw * _silu(inp["bog"])
        out["gla_o"] = go.astype(gla_o_ref.dtype)
        yield

        cx = inp["cx"]
        xpad_ref[slot, 0:8, :] = inp["conv"]
        xpad_ref[slot, 8:8 + CHUNK, :] = cx
        conv = cb + cx * cw_ref[CONV_W - 1:CONV_W, :]
        for i in range(CONV_W - 1):
            off = 8 - (CONV_W - 1) + i
            conv = conv + xpad_ref[slot, off:off + CHUNK, :] * cw_ref[i:i + 1, :]
        out["conv"] = cx[CHUNK - 8:, :]
        conv = _silu(conv)
        sx = conv[:, :WIDTH_C]
        sb_bf = conv[:, WIDTH_C:WIDTH_C + N_GROUPS_C * D_STATE].astype(BF16)
        sc_bf = conv[:, WIDTH_C + N_GROUPS_C * D_STATE:].astype(BF16)
        yield

        dtv = _softplus(misc + dtb)
        dt_x = _dot01_right(dtv, expand_bf)
        yield
        a_x = neg_a * dt_x
        cs_x = _dot01_left(tri_bf, a_x)
        cs_row = _dot01_left(ones_bf, jnp.where(upper_x, a_x, 0.0))
        yield
        decay = jnp.exp(jnp.where(causal_x, cs_x - cs_row, -jnp.inf))
        cb_parts = []
        for gidx in range(N_GROUPS_C):
            ns = slice(D_STATE * gidx, D_STATE * (gidx + 1))
            cbm = _dot_nt(sc_bf[:, ns], sb_bf[:, ns])
            cb_parts += [cbm] * (H_C // N_GROUPS_C)
            yield
        scores = (jnp.concatenate(cb_parts, axis=1) * decay).astype(BF16)
        xdt = (sx * dt_x).astype(BF16)
        ecs = jnp.exp(cs_x)
        cs_last = cs_x[CHUNK - 1:CHUNK, :]
        xw = (sx * (jnp.exp(cs_last - cs_x) * dt_x)).astype(BF16)
        e_last = jnp.exp(cs_last)
        yield
        y_parts = []
        out["ssm"] = []
        for h in range(H_C):
            ps = slice(P_C * h, P_C * (h + 1))
            gi = h // (H_C // N_GROUPS_C)
            ns = slice(D_STATE * gi, D_STATE * (gi + 1))
            hst = inp["ssm"][h]
            y_parts.append(_dot(scores[:, ps], xdt[:, ps])
                           + _dot_nt(sc_bf[:, ns], hst.astype(BF16)) * ecs[:, ps])
            out["ssm"].append(e_last[:, ps] * hst + _dot_tn(xw[:, ps], sb_bf[:, ns]))
            yield
        y = jnp.concatenate(y_parts, axis=1) + dsk * sx
        y = y * _silu(inp["cz"])
        ymsq = _dot01_right(y * y, blk128_bf) * (1.0 / (WIDTH_C // N_GROUPS_C))
        yield
        out["ssm_o"] = (y * lax.rsqrt(ymsq + EPS) * snw).astype(ssm_o_ref.dtype)
        return out

    def group_body(i, carry):
        seqs = [i * SEQ_UNROLL + u for u in range(SEQ_UNROLL)]
        inputs = [load_seq(b) for b in seqs]
        outputs = _round_robin([compute_seq(inp, slot) for slot, inp in enumerate(inputs)])
        for b, out in zip(seqs, outputs):
            store_seq(b, out)
        return carry

    lax.fori_loop(0, nseq // SEQ_UNROLL, group_body, 0)


def _scan(proj, lw, gla0, conv0, ssm0):
    bsz, l, _ = proj.shape
    nc = l // CHUNK
    sg = min(SEQ_GROUP, bsz)
    ng = bsz // sg

    def col(width, idx):
        return pl.BlockSpec((sg, CHUNK, width), lambda g, c: (g, c, idx))

    def const(shape):
        return pl.BlockSpec(shape, lambda g, c: (0,) * len(shape))

    def state(shape):
        return pl.BlockSpec((sg,) + shape, lambda g, c: (g,) + (0,) * len(shape))

    gla_shape = (DV_B, H_B * DK_B)
    conv_shape = (8, CONV_CH)
    ssm_shape = (H_C, P_C, D_STATE)
    return pl.pallas_call(
        _scan_kernel,
        grid=(ng, nc),
        in_specs=[col(256, 0), col(256, 1), col(256, 2), col(256, 3), col(512, 2), col(128, 12),
                  const((128, 128)), const((1, 128)), const((1, WIDTH_B)),
                  const((CONV_W, CONV_CH)), const((1, CONV_CH)),
                  const((1, 128)), const((1, WIDTH_C)), const((1, WIDTH_C)), const((1, WIDTH_C)),
                  state(gla_shape), state(conv_shape), state(ssm_shape)],
        out_specs=[pl.BlockSpec((sg, CHUNK, WIDTH_B), lambda g, c: (g, c, 0)),
                   pl.BlockSpec((sg, CHUNK, WIDTH_C), lambda g, c: (g, c, 0)),
                   state(gla_shape), state(conv_shape), state(ssm_shape)],
        out_shape=[jax.ShapeDtypeStruct((bsz, l, WIDTH_B), BF16),
                   jax.ShapeDtypeStruct((bsz, l, WIDTH_C), BF16),
                   jax.ShapeDtypeStruct((bsz,) + gla_shape, F32),
                   jax.ShapeDtypeStruct((bsz,) + conv_shape, F32),
                   jax.ShapeDtypeStruct((bsz,) + ssm_shape, F32)],
        scratch_shapes=[pltpu.VMEM((SEQ_UNROLL, 8 + CHUNK, CONV_CH), F32)],
        compiler_params=_params("parallel", "arbitrary"),
    )(proj, proj, proj, proj, proj, proj,
      lw["gup"], lw["gb"], lw["gnw"], lw["cw"], lw["cb"], lw["dtb"], lw["alog"], lw["dsk"], lw["snw"],
      gla0, conv0, ssm0)


OUT_TM = 256


def _out_proj_kernel(x_ref, att_ref, gla_ref, ssm_ref, w_out_ref, nw_ref, rwh_ref, rwl_ref, rb_ref,
                     x1_ref, hn_ref, gate_ref, wo_ref):
    @pl.when(pl.program_id(0) == 0)
    def _():
        for r in range(0, D_MODEL, PREP_ROWS):
            wo_ref[r:r + PREP_ROWS, :] = w_out_ref[0, r:r + PREP_ROWS, :].astype(BF16)

    mo = (_dot(att_ref[...], wo_ref[0:WIDTH_A, :])
          + _dot(gla_ref[...], wo_ref[WIDTH_A:WIDTH_A + WIDTH_B, :])
          + _dot(ssm_ref[...], wo_ref[WIDTH_A + WIDTH_B:, :]))
    x1 = x_ref[...] + mo
    x1_ref[...] = x1
    ms = jnp.mean(x1 * x1, axis=-1, keepdims=True)
    hn = x1 * lax.rsqrt(ms + EPS) * nw_ref[...]
    hn_ref[...] = hn.astype(hn_ref.dtype)

    h_hi = hn.astype(BF16)
    h_lo = (hn - h_hi.astype(F32)).astype(BF16)
    logits = (_dot(h_hi, rwh_ref[...]) + _dot(h_lo, rwh_ref[...]) + _dot(h_hi, rwl_ref[...])
              + rb_ref[...])
    lane = lax.broadcasted_iota(jnp.int32, logits.shape, 1)
    lane_f = lane.astype(F32)
    neg = -jnp.inf
    big = 1000.0
    gmask = (lane >= ROUTER_GROUP_LANE) & (lane < ROUTER_GROUP_LANE + N_EXPERT_GROUPS)
    lg = jnp.where(gmask, logits, neg)
    gmax = lg.max(axis=-1, keepdims=True)
    g_idx = jnp.where(lg == gmax, lane_f, big).min(axis=-1, keepdims=True) - ROUTER_GROUP_LANE
    p_grp = 1.0 / jnp.where(gmask, jnp.exp(logits - gmax), 0.0).sum(axis=-1, keepdims=True)
    emask = (lane < N_EXPERTS) & ((lane >> 2).astype(F32) == g_idx)
    le = jnp.where(emask, logits, neg)
    v1 = le.max(axis=-1, keepdims=True)
    i1 = jnp.where(le == v1, lane_f, big).min(axis=-1, keepdims=True)
    le2 = jnp.where(lane_f == i1, neg, le)
    v2 = le2.max(axis=-1, keepdims=True)
    i2 = jnp.where(le2 == v2, lane_f, big).min(axis=-1, keepdims=True)
    e2 = jnp.exp(v2 - v1)
    den = 1.0 + e2
    gate_ref[...] = (jnp.where(lane_f == i1, (1.0 / den) * p_grp, 0.0)
                     + jnp.where(lane_f == i2, (e2 / den) * p_grp, 0.0)
                     + jnp.where(lane == ROUTER_GIDX_LANE, g_idx, 0.0))


def _out_proj(x, att, gla, ssm, lw, layer):
    t = x.shape[0]

    def rows(width):
        return pl.BlockSpec((OUT_TM, width), lambda i: (i, 0))

    def const(shape):
        return pl.BlockSpec(shape, lambda i: (0, 0))

    return pl.pallas_call(
        _out_proj_kernel,
        grid=(t // OUT_TM,),
        in_specs=[rows(D_MODEL), rows(WIDTH_A), rows(WIDTH_B), rows(WIDTH_C),
                  pl.BlockSpec((1, D_MODEL, D_MODEL), lambda i: (layer, 0, 0)), const((1, D_MODEL)),
                  const((D_MODEL, 128)), const((D_MODEL, 128)), const((1, 128))],
        out_specs=[rows(D_MODEL), rows(D_MODEL), rows(128)],
        out_shape=[jax.ShapeDtypeStruct((t, D_MODEL), F32),
                   jax.ShapeDtypeStruct((t, D_MODEL), BF16),
                   jax.ShapeDtypeStruct((t, 128), F32)],
        scratch_shapes=[pltpu.VMEM((D_MODEL, D_MODEL), BF16)],
        compiler_params=_params("arbitrary"),
    )(x, att, gla, ssm, lw["wo"], lw["n2w"], lw["rw_hi"], lw["rw_lo"], lw["rb"])


RT_TS = 256
RT_ALIGN = 16
RT_ROWS = 384
RT_SIZES = (256, 128, 64, 32, 16)
PAY_COLS = D_MODEL + 256
MOE_TM = 512


def _moe_max_tiles(t):
    return (t + (t // RT_TS) * N_EXPERT_GROUPS * (RT_ALIGN - 1)) // MOE_TM + N_EXPERT_GROUPS


def _sort_matrix(gate, c_ref, tile):
    lane = lax.broadcasted_iota(jnp.int32, (RT_TS, 128), 1)
    gidx = jnp.where(lane == ROUTER_GIDX_LANE, gate, 0.0).sum(axis=1, keepdims=True)
    onehot = lane.astype(F32) == gidx
    r = lax.broadcasted_iota(jnp.int32, (RT_TS, RT_TS), 0)
    c = lax.broadcasted_iota(jnp.int32, (RT_TS, RT_TS), 1)
    earlier = jnp.where(c < r, 1.0, 0.0).astype(BF16)
    before = _dot(earlier, jnp.where(onehot, 1.0, 0.0).astype(BF16))
    start = jnp.zeros((RT_TS, 128), F32)
    for g in range(N_EXPERT_GROUPS):
        start = jnp.where(lane == g, c_ref[N_EXPERT_GROUPS * tile + g].astype(F32), start)
    dest = jnp.where(onehot, before + start, 0.0).sum(axis=1, keepdims=True)
    rows = lax.broadcasted_iota(jnp.int32, (RT_TS, RT_ROWS), 1).astype(F32)
    return jnp.where(rows == dest, 1.0, 0.0).astype(BF16)


def _piece_copies(length, local, glob, make_copy, act):
    for size in RT_SIZES:
        take = (length & size) != 0

        @pl.when(take)
        def _(local=local, glob=glob, size=size):
            act(make_copy(pl.multiple_of(local, RT_ALIGN), pl.multiple_of(glob, RT_ALIGN), size))

        step = jnp.where(take, size, 0)
        local = local + step
        glob = glob + step


def _run_copies(c_ref, lp_ref, off_ref, tile, make_copy, act):
    for g in range(N_EXPERT_GROUPS):
        k = N_EXPERT_GROUPS * tile + g
        _piece_copies(lp_ref[k], c_ref[k], off_ref[k], make_copy, act)


def _dispatch_kernel(c_ref, lp_ref, off_ref, tail_ref, nact_ref, hn_ref, gate_ref, xs_hbm,
                     y_ref, zero_ref, sem, zsem, *, first_spare, max_tiles):
    i = pl.program_id(0)
    nt = pl.num_programs(0)
    slot = i % 2
    gate = gate_ref[...]
    pt = _sort_matrix(gate, c_ref, i)
    g_hi = gate.astype(BF16)
    g_lo = (gate - g_hi.astype(F32)).astype(BF16)
    y_ref[slot, :, 0:D_MODEL] = _dot_tn(pt, hn_ref[...]).astype(BF16)
    y_ref[slot, :, D_MODEL:D_MODEL + 128] = _dot_tn(pt, g_hi).astype(BF16)
    y_ref[slot, :, D_MODEL + 128:] = _dot_tn(pt, g_lo).astype(BF16)

    def copy_from(s):
        return lambda local, glob, size: pltpu.make_async_copy(
            y_ref.at[s, pl.ds(local, size)], xs_hbm.at[pl.ds(glob, size)], sem.at[s])

    _run_copies(c_ref, lp_ref, off_ref, i, copy_from(slot), lambda cp: cp.start())

    @pl.when(i > 0)
    def _():
        _run_copies(c_ref, lp_ref, off_ref, i - 1, copy_from(1 - slot), lambda cp: cp.wait())

    @pl.when(i == nt - 1)
    def _():
        _run_copies(c_ref, lp_ref, off_ref, i, copy_from(slot), lambda cp: cp.wait())
        zero_ref[...] = jnp.zeros_like(zero_ref)

        def zero_copy(local, glob, size):
            return pltpu.make_async_copy(zero_ref.at[pl.ds(local, size)],
                                         xs_hbm.at[pl.ds(glob, size)], zsem)

        def zero_fill(act):
            for g in range(N_EXPERT_GROUPS):
                _piece_copies(tail_ref[N_EXPERT_GROUPS + g], 0, tail_ref[g], zero_copy, act)
            for j in range(first_spare, max_tiles):
                @pl.when(j >= nact_ref[0])
                def _(j=j):
                    act(zero_copy(0, j * MOE_TM, MOE_TM))

        zero_fill(lambda cp: cp.start())
        zero_fill(lambda cp: cp.wait())


def _moe_group_kernel(grp_ref, nact_ref, xs_ref, wg_ref, wu_ref, wd_ref, o_ref, wg_bf, wu_bf, wd_bf):
    j = pl.program_id(0)

    @pl.when(jnp.logical_or(j == 0, grp_ref[j] != grp_ref[jnp.maximum(j - 1, 0)]))
    def _():
        wg_bf[...] = wg_ref[0].astype(BF16)
        wu_bf[...] = wu_ref[0].astype(BF16)
        wd_bf[...] = wd_ref[0].astype(BF16)

    @pl.when(j >= nact_ref[0])
    def _():
        o_ref[...] = jnp.zeros_like(o_ref)

    @pl.when(j < nact_ref[0])
    def _():
        x = xs_ref[:, 0:D_MODEL]
        gate = (xs_ref[:, D_MODEL:D_MODEL + 128].astype(F32)
                + xs_ref[:, D_MODEL + 128:].astype(F32))
        lane = lax.broadcasted_iota(jnp.int32, gate.shape, 1)
        first = grp_ref[j] * EXPERTS_PER_GROUP
        acc = None
        for e in range(EXPERTS_PER_GROUP):
            ge = jnp.where(lane == first + e, gate, 0.0).sum(axis=-1, keepdims=True)
            hid = _silu(_dot(x, wg_bf[e])) * _dot(x, wu_bf[e]) * ge
            part = _dot(hid.astype(BF16), wd_bf[e])
            acc = part if acc is None else acc + part
        o_ref[...] = acc


def _combine_kernel(c_ref, lp_ref, off_ref, x_ref, gate_ref, fw_ref, ys_hbm, o_ref, z_ref, sem,
                    *, final_norm):
    i = pl.program_id(0)
    nt = pl.num_programs(0)
    slot = i % 2

    def copy_into(s):
        return lambda local, glob, size: pltpu.make_async_copy(
            ys_hbm.at[pl.ds(glob, size)], z_ref.at[s, pl.ds(local, size)], sem.at[s])

    @pl.when(i == 0)
    def _():
        z_ref[...] = jnp.zeros_like(z_ref)
        _run_copies(c_ref, lp_ref, off_ref, 0, copy_into(0), lambda cp: cp.start())

    @pl.when(i + 1 < nt)
    def _():
        _run_copies(c_ref, lp_ref, off_ref, i + 1, copy_into(1 - slot), lambda cp: cp.start())

    _run_copies(c_ref, lp_ref, off_ref, i, copy_into(slot), lambda cp: cp.wait())
    pt = _sort_matrix(gate_ref[...], c_ref, i)
    z = z_ref[slot]
    z_hi = z.astype(BF16)
    z_lo = (z - z_hi.astype(F32)).astype(BF16)
    y = x_ref[...] + _dot(pt, z_hi) + _dot(pt, z_lo)
    if final_norm:
        ms = jnp.mean(y * y, axis=-1, keepdims=True)
        y = y * lax.rsqrt(ms + EPS) * fw_ref[...]
    o_ref[...] = y


def _route_meta(gate, max_tiles):
    t = gate.shape[0]
    nt = t // RT_TS
    groups = jnp.arange(N_EXPERT_GROUPS, dtype=jnp.int32)
    g = gate[:, ROUTER_GIDX_LANE].astype(jnp.int32).reshape(nt, RT_TS, 1)
    count = (g == groups).astype(jnp.int32).sum(axis=1)
    lp = (count + RT_ALIGN - 1) // RT_ALIGN * RT_ALIGN
    total = lp.sum(axis=0)
    ntile = (total + MOE_TM - 1) // MOE_TM
    last = jnp.cumsum(ntile)
    first_row = (last - ntile) * MOE_TM
    c = jnp.cumsum(lp, axis=1) - lp
    off = first_row[None, :] + jnp.cumsum(lp, axis=0) - lp
    tail = jnp.concatenate([first_row + total, ntile * MOE_TM - total])
    j = jnp.arange(max_tiles, dtype=jnp.int32)[:, None]
    grp = jnp.minimum((j >= last[None, :]).astype(jnp.int32).sum(axis=1), N_EXPERT_GROUPS - 1)
    return c.reshape(-1), lp.reshape(-1), off.reshape(-1), tail, grp, last[-1:]


def _moe(x, hn, gate, lw, layer, fw, final_norm):
    t = x.shape[0]
    nt = t // RT_TS
    max_tiles = _moe_max_tiles(t)
    c, lp, off, tail, grp, nact = _route_meta(gate, max_tiles)

    def rows(width):
        return pl.BlockSpec((RT_TS, width), lambda i, *_: (i, 0))

    xs = pl.pallas_call(
        functools.partial(_dispatch_kernel, first_spare=t // MOE_TM, max_tiles=max_tiles),
        grid_spec=pltpu.PrefetchScalarGridSpec(
            num_scalar_prefetch=5, grid=(nt,),
            in_specs=[rows(D_MODEL), rows(128)],
            out_specs=pl.BlockSpec(memory_space=pl.ANY),
            scratch_shapes=[pltpu.VMEM((2, RT_ROWS, PAY_COLS), BF16),
                            pltpu.VMEM((MOE_TM, PAY_COLS), BF16),
                            pltpu.SemaphoreType.DMA((2,)), pltpu.SemaphoreType.DMA(())]),
        out_shape=jax.ShapeDtypeStruct((max_tiles * MOE_TM, PAY_COLS), BF16),
        compiler_params=_params("arbitrary"),
    )(c, lp, off, tail, nact, hn, gate)

    def experts(shape):
        return pl.BlockSpec((1, EXPERTS_PER_GROUP) + shape, lambda j, grp, nact: (layer, grp[j], 0, 0))

    def rounded(shape):
        return pltpu.VMEM((EXPERTS_PER_GROUP,) + shape, BF16)

    ys = pl.pallas_call(
        _moe_group_kernel,
        grid_spec=pltpu.PrefetchScalarGridSpec(
            num_scalar_prefetch=2, grid=(max_tiles,),
            in_specs=[pl.BlockSpec((MOE_TM, PAY_COLS), lambda j, grp, nact: (j, 0)),
                      experts((D_MODEL, D_EXPERT)), experts((D_MODEL, D_EXPERT)),
                      experts((D_EXPERT, D_MODEL))],
            out_specs=pl.BlockSpec((MOE_TM, D_MODEL), lambda j, grp, nact: (j, 0)),
            scratch_shapes=[rounded((D_MODEL, D_EXPERT)), rounded((D_MODEL, D_EXPERT)),
                            rounded((D_EXPERT, D_MODEL))]),
        out_shape=jax.ShapeDtypeStruct((max_tiles * MOE_TM, D_MODEL), F32),
        compiler_params=_params("arbitrary"),
    )(grp, nact, xs, lw["wg"], lw["wu"], lw["wd"])

    return pl.pallas_call(
        functools.partial(_combine_kernel, final_norm=final_norm),
        grid_spec=pltpu.PrefetchScalarGridSpec(
            num_scalar_prefetch=3, grid=(nt,),
            in_specs=[rows(D_MODEL), rows(128), pl.BlockSpec((1, D_MODEL), lambda i, *_: (0, 0)),
                      pl.BlockSpec(memory_space=pl.ANY)],
            out_specs=rows(D_MODEL),
            scratch_shapes=[pltpu.VMEM((2, RT_ROWS, D_MODEL), F32),
                            pltpu.SemaphoreType.DMA((2,))]),
        out_shape=jax.ShapeDtypeStruct((t, D_MODEL), F32),
        compiler_params=_params("arbitrary"),
    )(c, lp, off, x, gate, fw, ys)


def _lane_place(vec, start, width=128):
    return jnp.zeros((1, width), F32).at[0, start:start + vec.shape[0]].set(vec.astype(F32))


REL_PAD = 704


def _toeplitz(v, rows, cols):
    n = v.shape[-1]
    tiled = jnp.tile(v, (1,) * (v.ndim - 1) + (rows,))[..., :rows * (n - 1)]
    return tiled.reshape(v.shape[:-1] + (rows, n - 1))[..., :cols]


def _band_bias(table):
    t = table.astype(F32)
    ext = jnp.concatenate([jnp.repeat(t[:, :1], REL_PAD, axis=1), t,
                           jnp.repeat(t[:, -1:], REL_PAD, axis=1)], axis=1)
    extr = ext[:, ::-1]
    top = ext.shape[1] - 1 - (REL_CLIP + REL_PAD)

    def band(off, nq, nk):
        n = nq + nk
        v = jnp.concatenate([extr[:, top - off:top - off + nk + 1],
                             extr[:, top - off - (nq - 1):top - off]], axis=1)
        assert v.shape[1] == n
        return _toeplitz(v, nq, nk)

    sample = band(WINDOW, CHUNK, BAND)
    nk = ATT_NKB * ATT_KB
    kc = (jnp.arange(nk) // CHUNK)[:, None]
    qc = (jnp.arange(ATT_QB) // CHUNK)[None, :]
    valid = [kc <= qc, kc <= qc + ATT_QB // CHUNK, (kc >= qc) & (kc <= qc + N_PAST_CHUNKS)]
    prompt = jnp.stack([
        jnp.where(valid[w][None], jnp.swapaxes(band(w * ATT_QB, ATT_QB, nk), 1, 2), -1e30)
        for w in range(3)])
    return sample * LOG2_E, prompt * LOG2_E


def _layer_weights(i, norm1_w, w_in, rel_bias_table, gla_w_gate_up, gla_b_gate, gla_norm_w,
                   ssm_conv_w, ssm_conv_b, ssm_dt_bias, ssm_a_log, ssm_d, ssm_norm_w, w_out,
                   norm2_w, router_group_w, router_group_b, router_expert_w, router_expert_b,
                   exp_w_gate, exp_w_up, exp_w_down):
    bias_s, bias_p = _band_bias(rel_bias_table[i])
    rw = jnp.concatenate([router_expert_w[i], router_group_w[i],
                          jnp.zeros((D_MODEL, 128 - N_EXPERTS - N_EXPERT_GROUPS), F32)], axis=1)
    rw_hi = rw.astype(BF16)
    return dict(
        n1w=norm1_w[i][None, :],
        w_in=w_in,
        bias_s=bias_s,
        bias_p=bias_p,
        gup=jnp.zeros((128, 128), F32).at[:GATE_RANK, :].set(gla_w_gate_up[i]).astype(BF16),
        gb=gla_b_gate[i][None, :].astype(F32),
        gnw=jnp.tile(gla_norm_w[i], H_B)[None, :].astype(F32),
        cw=ssm_conv_w[i].astype(F32),
        cb=ssm_conv_b[i][None, :].astype(F32),
        dtb=_lane_place(ssm_dt_bias[i], MISC_DT_LANE),
        alog=jnp.repeat(ssm_a_log[i].astype(F32), P_C)[None, :],
        dsk=jnp.repeat(ssm_d[i].astype(F32), P_C)[None, :],
        snw=ssm_norm_w[i][None, :].astype(F32),
        wo=w_out,
        n2w=norm2_w[i][None, :],
        rw_hi=rw_hi,
        rw_lo=(rw - rw_hi.astype(F32)).astype(BF16),
        rb=jnp.concatenate([router_expert_b[i], router_group_b[i],
                            jnp.zeros((128 - N_EXPERTS - N_EXPERT_GROUPS,), F32)])[None, :],
        wg=exp_w_gate,
        wu=exp_w_up,
        wd=exp_w_down,
    )


def _stream_layer(x, bsz, lw, layer, cache_k, cache_v, gla0, conv0, ssm0, fw, final_norm):
    l = x.shape[0] // bsz
    if cache_k is None:
        qk, rest, k_keep, v_keep, vt = _in_proj(x, lw["n1w"], lw["w_in"], layer, True, l)
        att = _attn_prompt(qk.reshape(bsz, l, 2 * WIDTH_A),
                           vt.reshape(bsz, l // ATT_KB, WIDTH_A, ATT_KB), lw["bias_p"])
    else:
        qk, rest, k_keep, v_keep, kv = _in_proj(x, lw["n1w"], lw["w_in"], layer, False, l)
        att = _attn_sample(qk.reshape(bsz, l, 2 * WIDTH_A), kv.reshape(bsz, l, 2 * WIDTH_A),
                           cache_k, cache_v, layer, lw["bias_s"])
    gla_o, ssm_o, gla_s, conv_s, ssm_s = _scan(rest.reshape(bsz, l, REST_COLS), lw, gla0, conv0, ssm0)
    t = bsz * l
    x1, hn, gate = _out_proj(x, att.reshape(t, WIDTH_A), gla_o.reshape(t, WIDTH_B),
                             ssm_o.reshape(t, WIDTH_C), lw, layer)
    x2 = _moe(x1, hn, gate, lw, layer, fw, final_norm)
    keep = min(WINDOW, l)
    k_keep = k_keep.reshape(bsz, keep, H_A, DH_A)
    v_keep = v_keep.reshape(bsz, keep, H_A, DH_A)
    gla_state = gla_s.reshape(bsz, DV_B, H_B, DK_B).transpose(0, 2, 3, 1)
    conv_state = conv_s[:, 8 - (CONV_W - 1):, :]
    return x2, k_keep, v_keep, gla_state, conv_state, ssm_s


def kernel(x_prompt, x_sample, cache_k_a, cache_v_a, state_gla, state_conv, state_ssm, norm1_w, w_in, rel_bias_table, gla_w_gate_up, gla_b_gate, gla_norm_w, ssm_conv_w, ssm_conv_b, ssm_dt_bias, ssm_a_log, ssm_d, ssm_norm_w, w_out, norm2_w, router_group_w, router_group_b, router_expert_w, router_expert_b, exp_w_gate, exp_w_up, exp_w_down, final_norm_w):
    bp, sp, _ = x_prompt.shape
    bs, ss, _ = x_sample.shape
    xp = x_prompt.reshape(bp * sp, D_MODEL)
    xs = x_sample.reshape(bs * ss, D_MODEL)
    fw = final_norm_w[None, :].astype(F32)
    outs_p, outs_s = [], []
    for i in range(DEPTH):
        lw = _layer_weights(i, norm1_w, w_in, rel_bias_table, gla_w_gate_up, gla_b_gate, gla_norm_w,
                            ssm_conv_w, ssm_conv_b, ssm_dt_bias, ssm_a_log, ssm_d, ssm_norm_w, w_out,
                            norm2_w, router_group_w, router_group_b, router_expert_w, router_expert_b,
                            exp_w_gate, exp_w_up, exp_w_down)
        last = i == DEPTH - 1
        xp, *sp_out = _stream_layer(
            xp, bp, lw, i, None, None,
            jnp.zeros((bp, DV_B, H_B * DK_B), F32),
            jnp.zeros((bp, 8, CONV_CH), F32),
            jnp.zeros((bp, H_C, P_C, D_STATE), F32), fw, last)
        outs_p.append(sp_out)
        gla0 = state_gla[i].astype(F32).transpose(0, 3, 1, 2).reshape(bs, DV_B, H_B * DK_B)
        conv0 = jnp.pad(state_conv[i].astype(F32), ((0, 0), (8 - (CONV_W - 1), 0), (0, 0)))
        xs, *ss_out = _stream_layer(
            xs, bs, lw, i, cache_k_a, cache_v_a,
            gla0, conv0, state_ssm[i].astype(F32), fw, last)
        outs_s.append(ss_out)
    stack = lambda outs, j: jnp.stack([o[j] for o in outs])
    return (xp.reshape(bp, sp, D_MODEL), xs.reshape(bs, ss, D_MODEL),
            stack(outs_p, 0), stack(outs_p, 1), stack(outs_p, 2), stack(outs_p, 3), stack(outs_p, 4),
            stack(outs_s, 0), stack(outs_s, 1), stack(outs_s, 2), stack(outs_s, 3), stack(outs_s, 4))
```

```python
import functools

import jax
import jax.numpy as jnp
from jax import lax
from jax.experimental import pallas as pl
from jax.experimental.pallas import tpu as pltpu

F32 = jnp.float32
BF16 = jnp.bfloat16

D_MODEL = 1024
DEPTH = 2
EPS = 1e-6
CHUNK = 64
N_PAST_CHUNKS = 8
BAND = (N_PAST_CHUNKS + 1) * CHUNK
WINDOW = N_PAST_CHUNKS * CHUNK
H_A, DH_A, WIDTH_A = 8, 64, 512
REL_CLIP = 128
LOG2_E = 1.4426950408889634
H_B, DK_B, DV_B, WIDTH_B = 4, 32, 64, 256
GATE_RANK = 16
GATE_TAU = 16.0
H_C, P_C, WIDTH_C = 4, 64, 256
N_GROUPS_C = 2
D_STATE = 64
CONV_W = 4
CONV_CH = 512
N_EXPERT_GROUPS = 4
EXPERTS_PER_GROUP = 4
N_EXPERTS = 16
D_EXPERT = 256

PROJ_COLS = 3200
MISC_DT_LANE = GATE_RANK
ROUTER_GROUP_LANE = N_EXPERTS
ROUTER_GIDX_LANE = 20

SEQ_GROUP = 8
SEQ_UNROLL = 8
VMEM_LIMIT_BYTES = 56 * 1024 * 1024

NT_DIMS = (((1,), (1,)), ((), ()))
TN_DIMS = (((0,), (0,)), ((), ()))


def _params(*sem):
    return pltpu.CompilerParams(dimension_semantics=sem, vmem_limit_bytes=VMEM_LIMIT_BYTES)


def _dot(a, b):
    return jnp.dot(a, b, preferred_element_type=F32)


def _dot_nt(a, b):
    return lax.dot_general(a, b, NT_DIMS, preferred_element_type=F32)


def _dot_tn(a, b):
    return lax.dot_general(a, b, TN_DIMS, preferred_element_type=F32)


def _split2(a):
    hi = a.astype(BF16)
    return hi, (a - hi.astype(F32)).astype(BF16)


def _dot01_left(m01, a):
    hi, lo = _split2(a)
    return _dot(m01, hi) + _dot(m01, lo)


def _dot01_right(a, m01):
    hi, lo = _split2(a)
    return _dot(hi, m01) + _dot(lo, m01)


def _round_robin(gens):
    results = [None] * len(gens)
    live = list(range(len(gens)))
    while live:
        for idx in list(live):
            try:
                next(gens[idx])
            except StopIteration as stop:
                results[idx] = stop.value
                live.remove(idx)
    return results


def _silu(x):
    return x * jax.nn.sigmoid(x)


def _softplus(x):
    return jnp.maximum(x, 0.0) + jnp.log1p(jnp.exp(-jnp.abs(x)))


IN_TM = 256
QKV_COLS = 3 * WIDTH_A
REST_COLS = PROJ_COLS - QKV_COLS
REST_CHUNKS = ((0, 512), (512, 1024), (1024, REST_COLS))


IN_COLS = 3092
IN_TAIL = QKV_COLS + 512
PREP_ROWS = 128


def _prepare_w_in(w_in_ref, w_ref, wvt_ref):
    scale = DH_A ** -0.5 * LOG2_E
    n_tail = REST_COLS - 512 - 128
    for r in range(0, D_MODEL, PREP_ROWS):
        rs = slice(r, r + PREP_ROWS)
        w_ref[rs, 0:WIDTH_A] = (w_in_ref[0, rs, 0:WIDTH_A] * scale).astype(BF16)
        w_ref[rs, WIDTH_A:IN_TAIL] = w_in_ref[0, rs, WIDTH_A:IN_TAIL].astype(BF16)
        tail = w_in_ref[0, rs, IN_TAIL:IN_COLS]
        w_ref[rs, IN_TAIL:IN_TAIL + n_tail] = tail[:, GATE_RANK:GATE_RANK + n_tail].astype(BF16)
        misc = jnp.concatenate([tail[:, 0:GATE_RANK], tail[:, GATE_RANK + n_tail:],
                                jnp.zeros((PREP_ROWS, 128 - GATE_RANK - H_C), F32)], axis=1)
        w_ref[rs, IN_TAIL + n_tail:] = misc.astype(BF16)
    if wvt_ref is not None:
        wvt_ref[...] = w_in_ref[0, :, 2 * WIDTH_A:QKV_COLS].T.astype(BF16)


def _in_proj_kernel(x_ref, nw_ref, w_in_ref, *refs, with_vt):
    if with_vt:
        qk_ref, rest_ref, kk_ref, vk_ref, vt_ref, w_ref, wvt_ref = refs
    else:
        qk_ref, rest_ref, kk_ref, vk_ref, kv_ref, w_ref = refs
        wvt_ref = None

    @pl.when(pl.program_id(0) == 0)
    def _():
        _prepare_w_in(w_in_ref, w_ref, wvt_ref)

    x = x_ref[...]
    ms = jnp.mean(x * x, axis=-1, keepdims=True)
    xn = (x * lax.rsqrt(ms + EPS) * nw_ref[...]).astype(BF16)
    qk_ref[:, 0:WIDTH_A] = _dot(xn, w_ref[:, 0:WIDTH_A]).astype(BF16)
    k = _dot(xn, w_ref[:, WIDTH_A:2 * WIDTH_A])
    v = _dot(xn, w_ref[:, 2 * WIDTH_A:QKV_COLS])
    qk_ref[:, WIDTH_A:] = k.astype(BF16)
    if not with_vt:
        kv_ref[:, 0:WIDTH_A] = k
        kv_ref[:, WIDTH_A:] = v

    def keep_rows():
        kk = kk_ref.reshape(IN_TM * H_A, DH_A)
        vk = vk_ref.reshape(IN_TM * H_A, DH_A)
        for h in range(H_A):
            sl = slice(DH_A * h, DH_A * (h + 1))
            kk[pl.ds(h, IN_TM, stride=H_A), :] = k[:, sl]
            vk[pl.ds(h, IN_TM, stride=H_A), :] = v[:, sl]

    keep_rows()
    for lo, hi in REST_CHUNKS:
        rest_ref[:, lo:hi] = _dot(xn, w_ref[:, QKV_COLS + lo:QKV_COLS + hi])
    if with_vt:
        vt_ref[0] = _dot_nt(wvt_ref[...], xn).astype(BF16)


def _in_proj(x, nw, w_in, layer, with_vt, seq_len):
    t = x.shape[0]
    tiles_per_seq = max(seq_len // IN_TM, 1)
    keep_tiles = max(min(WINDOW, seq_len) // IN_TM, 1)
    n_keep = t // IN_TM // tiles_per_seq * keep_tiles

    def rows(width):
        return pl.BlockSpec((IN_TM, width), lambda i: (i, 0))

    def kept(i):
        return (i // tiles_per_seq * keep_tiles
                + jnp.maximum(i % tiles_per_seq - (tiles_per_seq - keep_tiles), 0), 0, 0)

    keep_spec = pl.BlockSpec((IN_TM, H_A, DH_A), kept)
    keep_shape = jax.ShapeDtypeStruct((n_keep * IN_TM, H_A, DH_A), F32)
    in_specs = [rows(D_MODEL), pl.BlockSpec((1, D_MODEL), lambda i: (0, 0)),
                pl.BlockSpec((1, D_MODEL, IN_COLS), lambda i: (layer, 0, 0))]
    scratch = [pltpu.VMEM((D_MODEL, PROJ_COLS), BF16)]
    out_specs = [rows(2 * WIDTH_A), rows(REST_COLS), keep_spec, keep_spec]
    out_shape = [jax.ShapeDtypeStruct((t, 2 * WIDTH_A), BF16),
                 jax.ShapeDtypeStruct((t, REST_COLS), F32), keep_shape, keep_shape]
    if with_vt:
        out_specs.append(pl.BlockSpec((1, WIDTH_A, IN_TM), lambda i: (i, 0, 0)))
        out_shape.append(jax.ShapeDtypeStruct((t // IN_TM, WIDTH_A, IN_TM), BF16))
        scratch.append(pltpu.VMEM((WIDTH_A, D_MODEL), BF16))
    else:
        out_specs.append(rows(2 * WIDTH_A))
        out_shape.append(jax.ShapeDtypeStruct((t, 2 * WIDTH_A), F32))
    return pl.pallas_call(
        functools.partial(_in_proj_kernel, with_vt=with_vt),
        grid=(t // IN_TM,),
        in_specs=in_specs, out_specs=out_specs, out_shape=out_shape, scratch_shapes=scratch,
        compiler_params=_params("arbitrary"),
    )(x, nw, w_in)


ATT_QB = 4 * CHUNK
ATT_KB = IN_TM
ATT_NKB = (ATT_QB + WINDOW) // ATT_KB


def _attn_prompt_kernel(q_ref, k_ref, vt_ref, bias_ref, o_ref, ot_ref):
    blk = pl.program_id(1)
    sb = jnp.maximum(blk - WINDOW // ATT_KB, 0)
    kstart = pl.multiple_of(sb * ATT_KB, ATT_KB)
    lane = lax.broadcasted_iota(jnp.int32, (ATT_QB, 128), 1)
    kinds = (jnp.where(blk == 0, 2, jnp.where(blk == 1, 1, 0)),
             jnp.where(blk == 0, 3, jnp.where(blk == 1, 2, 1)),
             jnp.where(blk <= 1, 3, 2))

    def scores(h):
        pair = slice(128 * (h // 2), 128 * (h // 2 + 1))
        qp = q_ref[0, :, pair]
        qz = jnp.where((lane >= DH_A) == bool(h % 2), qp, jnp.zeros_like(qp))
        return [_dot_nt(k_ref[0, pl.ds(kstart + ATT_KB * i, ATT_KB), pair], qz)
                + bias_ref[kinds[i], h] for i in range(ATT_NKB)]

    def finish(h, s):
        m = s[0].max(axis=0, keepdims=True)
        for si in s[1:]:
            m = jnp.maximum(m, si.max(axis=0, keepdims=True))
        l = None
        ot = None
        for i, si in enumerate(s):
            p = jnp.exp2(si - m)
            li = p.sum(axis=0, keepdims=True)
            oi = _dot(vt_ref[0, sb + i, DH_A * h:DH_A * (h + 1), :], p.astype(BF16))
            l = li if l is None else l + li
            ot = oi if ot is None else ot + oi
        ot_ref[DH_A * h:DH_A * (h + 1), :] = ot / l

    pending = scores(0)
    for h in range(H_A):
        upcoming = scores(h + 1) if h + 1 < H_A else None
        finish(h, pending)
        pending = upcoming
    o_ref[0] = ot_ref[...].T.astype(o_ref.dtype)


def _attn_prompt(qk, vt, bias_t):
    bsz, s, _ = qk.shape
    nkb = s // ATT_KB
    return pl.pallas_call(
        _attn_prompt_kernel,
        grid=(bsz, s // ATT_QB),
        in_specs=[pl.BlockSpec((1, ATT_QB, WIDTH_A), lambda b, c: (b, c, 0)),
                  pl.BlockSpec((1, s, WIDTH_A), lambda b, c: (b, 0, 1)),
                  pl.BlockSpec((1, nkb, WIDTH_A, ATT_KB), lambda b, c: (b, 0, 0, 0)),
                  pl.BlockSpec((4, H_A, ATT_KB, ATT_QB), lambda b, c: (0, 0, 0, 0))],
        out_specs=pl.BlockSpec((1, ATT_QB, WIDTH_A), lambda b, c: (b, c, 0)),
        out_shape=jax.ShapeDtypeStruct((bsz, s, WIDTH_A), BF16),
        scratch_shapes=[pltpu.VMEM((WIDTH_A, ATT_QB), F32)],
        compiler_params=_params("parallel", "arbitrary"),
    )(qk, qk, vt, bias_t)


def _attn_sample_kernel(q_ref, kv_ref, ck_ref, cv_ref, bias_ref, o_ref):
    ck = ck_ref.at[0, 0].reshape(WINDOW * H_A, DH_A)
    cv = cv_ref.at[0, 0].reshape(WINDOW * H_A, DH_A)

    def scores(h):
        sl = slice(DH_A * h, DH_A * (h + 1))
        qh = q_ref[0, :, sl]
        return (_dot_nt(qh, ck[pl.ds(h, WINDOW, stride=H_A), :].astype(BF16))
                + bias_ref[h, :, 0:WINDOW],
                _dot_nt(qh, kv_ref[0, :, sl].astype(BF16)) + bias_ref[h, :, WINDOW:BAND])

    def finish(h, s):
        s1, s2 = s
        sl = slice(DH_A * h, DH_A * (h + 1))
        vsl = slice(WIDTH_A + DH_A * h, WIDTH_A + DH_A * (h + 1))
        m = jnp.maximum(s1.max(axis=-1, keepdims=True), s2.max(axis=-1, keepdims=True))
        p1 = jnp.exp2(s1 - m)
        p2 = jnp.exp2(s2 - m)
        l = p1.sum(axis=-1, keepdims=True) + p2.sum(axis=-1, keepdims=True)
        o = (_dot(p1.astype(BF16), cv[pl.ds(h, WINDOW, stride=H_A), :].astype(BF16))
             + _dot(p2.astype(BF16), kv_ref[0, :, vsl].astype(BF16)))
        o_ref[0, :, sl] = (o / l).astype(o_ref.dtype)

    pending = scores(0)
    for h in range(H_A):
        upcoming = scores(h + 1) if h + 1 < H_A else None
        finish(h, pending)
        pending = upcoming


def _attn_sample(qk, kv, cache_k, cache_v, layer, bias):
    bsz = qk.shape[0]
    cache_spec = pl.BlockSpec((1, 1, WINDOW, H_A, DH_A), lambda b: (layer, b, 0, 0, 0))
    return pl.pallas_call(
        _attn_sample_kernel,
        grid=(bsz,),
        in_specs=[pl.BlockSpec((1, CHUNK, WIDTH_A), lambda b: (b, 0, 0)),
                  pl.BlockSpec((1, CHUNK, 2 * WIDTH_A), lambda b: (b, 0, 0)),
                  cache_spec, cache_spec,
                  pl.BlockSpec((H_A, CHUNK, BAND), lambda b: (0, 0, 0))],
        out_specs=pl.BlockSpec((1, CHUNK, WIDTH_A), lambda b: (b, 0, 0)),
        out_shape=jax.ShapeDtypeStruct((bsz, CHUNK, WIDTH_A), BF16),
        compiler_params=_params("parallel"),
    )(qk, kv, cache_k, cache_v, bias)


def _scan_kernel(bqk_ref, bv_ref, bog_ref, cz_ref, cx_ref, misc_ref,
                 gup_ref, gb_ref, gnw_ref, cw_ref, cb_ref, dtb_ref, alog_ref, dsk_ref, snw_ref,
                 gla0_ref, conv0_ref, ssm0_ref,
                 gla_o_ref, ssm_o_ref, gla_s_ref, conv_s_ref, ssm_s_ref,
                 xpad_ref):
    c = pl.program_id(1)
    nseq = bqk_ref.shape[0]

    @pl.when(c == 0)
    def _():
        gla_s_ref[...] = gla0_ref[...]
        conv_s_ref[...] = conv0_ref[...]
        ssm_s_ref[...] = ssm0_ref[...]

    r64 = lax.broadcasted_iota(jnp.int32, (CHUNK, CHUNK), 0)
    c64 = lax.broadcasted_iota(jnp.int32, (CHUNK, CHUNK), 1)
    tri = c64 <= r64
    tri_bf = jnp.where(tri, 1.0, 0.0).astype(BF16)
    ones_bf = jnp.ones((CHUNK, CHUNK), BF16)
    row_x = lax.broadcasted_iota(jnp.int32, (CHUNK, WIDTH_C), 0)
    s_x = lax.broadcasted_iota(jnp.int32, (CHUNK, WIDTH_C), 1) & (CHUNK - 1)
    causal_x = s_x <= row_x
    upper_x = row_x <= s_x
    er = lax.broadcasted_iota(jnp.int32, (128, WIDTH_C), 0)
    el = lax.broadcasted_iota(jnp.int32, (128, WIDTH_C), 1)
    expand_bf = jnp.where(er == MISC_DT_LANE + (el >> 6), 1.0, 0.0).astype(BF16)
    br = lax.broadcasted_iota(jnp.int32, (WIDTH_C, WIDTH_C), 0)
    bl = lax.broadcasted_iota(jnp.int32, (WIDTH_C, WIDTH_C), 1)
    blk64_bf = jnp.where((br >> 6) == (bl >> 6), 1.0, 0.0).astype(BF16)
    blk128_bf = jnp.where((br >> 7) == (bl >> 7), 1.0, 0.0).astype(BF16)

    gup = gup_ref[...]
    gb = gb_ref[...]
    gnw = gnw_ref[...]
    cb = cb_ref[...]
    dtb = dtb_ref[...]
    neg_a = -jnp.exp(alog_ref[...])
    dsk = dsk_ref[...]
    snw = snw_ref[...]

    def load_seq(b):
        return dict(misc=misc_ref[b], qk=bqk_ref[b], v=bv_ref[b], bog=bog_ref[b], cz=cz_ref[b],
                    cx=cx_ref[b], gla=gla_s_ref[b], conv=conv_s_ref[b],
                    ssm=[ssm_s_ref[b, h] for h in range(H_C)])

    def store_seq(b, out):
        gla_o_ref[b] = out["gla_o"]
        ssm_o_ref[b] = out["ssm_o"]
        gla_s_ref[b] = out["gla"]
        conv_s_ref[b] = out["conv"]
        for h in range(H_C):
            ssm_s_ref[b, h] = out["ssm"][h]

    def compute_seq(inp, slot):
        out = {}
        misc = inp["misc"]

        z = _dot(misc.astype(BF16), gup) + gb
        g = (jnp.minimum(z, 0.0) - jnp.log1p(jnp.exp(-jnp.abs(z)))) * (1.0 / GATE_TAU)
        bcum = _dot01_left(tri_bf, g)
        yield
        q = inp["qk"][:, :H_B * DK_B]
        k = inp["qk"][:, H_B * DK_B:]
        blast = bcum[CHUNK - 1:CHUNK, :]
        qb = (q * (DK_B ** -0.5) * jnp.exp(bcum)).astype(BF16)
        kb = (k * jnp.exp(-bcum)).astype(BF16)
        kl = (k * jnp.exp(blast - bcum)).astype(BF16)
        vb = inp["v"].astype(BF16)
        st = inp["gla"]
        st_bf = st.astype(BF16)
        yield
        o_parts, s_parts = [], []
        for h in range(H_B):
            ks = slice(DK_B * h, DK_B * (h + 1))
            vs = slice(DV_B * h, DV_B * (h + 1))
            att = jnp.where(tri, _dot_nt(qb[:, ks], kb[:, ks]), 0.0).astype(BF16)
            o_parts.append(_dot(att, vb[:, vs]) + _dot_nt(qb[:, ks], st_bf[:, ks]))
            s_parts.append(_dot_tn(vb[:, vs], kl[:, ks]))
            yield
        o = jnp.concatenate(o_parts, axis=1)
        out["gla"] = jnp.exp(blast) * st + jnp.concatenate(s_parts, axis=1)
        msq = _dot01_right(o * o, blk64_bf) * (1.0 / DV_B)
        yield
        go = o * lax.rsqrt(msq + EPS) * gnw * _silu(inp["bog"])
        out["gla_o"] = go.astype(gla_o_ref.dtype)
        yield

        cx = inp["cx"]
        xpad_ref[slot, 0:8, :] = inp["conv"]
        xpad_ref[slot, 8:8 + CHUNK, :] = cx
        conv = cb + cx * cw_ref[CONV_W - 1:CONV_W, :]
        for i in range(CONV_W - 1):
            off = 8 - (CONV_W - 1) + i
            conv = conv + xpad_ref[slot, off:off + CHUNK, :] * cw_ref[i:i + 1, :]
        out["conv"] = cx[CHUNK - 8:, :]
        conv = _silu(conv)
        sx = conv[:, :WIDTH_C]
        sb_bf = conv[:, WIDTH_C:WIDTH_C + N_GROUPS_C * D_STATE].astype(BF16)
        sc_bf = conv[:, WIDTH_C + N_GROUPS_C * D_STATE:].astype(BF16)
        yield

        dtv = _softplus(misc + dtb)
        dt_x = _dot01_right(dtv, expand_bf)
        yield
        a_x = neg_a * dt_x
        cs_x = _dot01_left(tri_bf, a_x)
        cs_row = _dot01_left(ones_bf, jnp.where(upper_x, a_x, 0.0))
        yield
        decay = jnp.exp(jnp.where(causal_x, cs_x - cs_row, -jnp.inf))
        cb_parts = []
        for gidx in range(N_GROUPS_C):
            ns = slice(D_STATE * gidx, D_STATE * (gidx + 1))
            cbm = _dot_nt(sc_bf[:, ns], sb_bf[:, ns])
            cb_parts += [cbm] * (H_C // N_GROUPS_C)
            yield
        scores = (jnp.concatenate(cb_parts, axis=1) * decay).astype(BF16)
        xdt = (sx * dt_x).astype(BF16)
        ecs = jnp.exp(cs_x)
        cs_last = cs_x[CHUNK - 1:CHUNK, :]
        xw = (sx * (jnp.exp(cs_last - cs_x) * dt_x)).astype(BF16)
        e_last = jnp.exp(cs_last)
        yield
        y_parts = []
        out["ssm"] = []
        for h in range(H_C):
            ps = slice(P_C * h, P_C * (h + 1))
            gi = h // (H_C // N_GROUPS_C)
            ns = slice(D_STATE * gi, D_STATE * (gi + 1))
            hst = inp["ssm"][h]
            y_parts.append(_dot(scores[:, ps], xdt[:, ps])
                           + _dot_nt(sc_bf[:, ns], hst.astype(BF16)) * ecs[:, ps])
            out["ssm"].append(e_last[:, ps] * hst + _dot_tn(xw[:, ps], sb_bf[:, ns]))
            yield
        y = jnp.concatenate(y_parts, axis=1) + dsk * sx
        y = y * _silu(inp["cz"])
        ymsq = _dot01_right(y * y, blk128_bf) * (1.0 / (WIDTH_C // N_GROUPS_C))
        yield
        out["ssm_o"] = (y * lax.rsqrt(ymsq + EPS) * snw).astype(ssm_o_ref.dtype)
        return out

    def group_body(i, carry):
        seqs = [i * SEQ_UNROLL + u for u in range(SEQ_UNROLL)]
        inputs = [load_seq(b) for b in seqs]
        outputs = _round_robin([compute_seq(inp, slot) for slot, inp in enumerate(inputs)])
        for b, out in zip(seqs, outputs):
            store_seq(b, out)
        return carry

    lax.fori_loop(0, nseq // SEQ_UNROLL, group_body, 0)


def _scan(proj, lw, gla0, conv0, ssm0):
    bsz, l, _ = proj.shape
    nc = l // CHUNK
    sg = min(SEQ_GROUP, bsz)
    ng = bsz // sg

    def col(width, idx):
        return pl.BlockSpec((sg, CHUNK, width), lambda g, c: (g, c, idx))

    def const(shape):
        return pl.BlockSpec(shape, lambda g, c: (0,) * len(shape))

    def state(shape):
        return pl.BlockSpec((sg,) + shape, lambda g, c: (g,) + (0,) * len(shape))

    gla_shape = (DV_B, H_B * DK_B)
    conv_shape = (8, CONV_CH)
    ssm_shape = (H_C, P_C, D_STATE)
    return pl.pallas_call(
        _scan_kernel,
        grid=(ng, nc),
        in_specs=[col(256, 0), col(256, 1), col(256, 2), col(256, 3), col(512, 2), col(128, 12),
                  const((128, 128)), const((1, 128)), const((1, WIDTH_B)),
                  const((CONV_W, CONV_CH)), const((1, CONV_CH)),
                  const((1, 128)), const((1, WIDTH_C)), const((1, WIDTH_C)), const((1, WIDTH_C)),
                  state(gla_shape), state(conv_shape), state(ssm_shape)],
        out_specs=[pl.BlockSpec((sg, CHUNK, WIDTH_B), lambda g, c: (g, c, 0)),
                   pl.BlockSpec((sg, CHUNK, WIDTH_C), lambda g, c: (g, c, 0)),
                   state(gla_shape), state(conv_shape), state(ssm_shape)],
        out_shape=[jax.ShapeDtypeStruct((bsz, l, WIDTH_B), BF16),
                   jax.ShapeDtypeStruct((bsz, l, WIDTH_C), BF16),
                   jax.ShapeDtypeStruct((bsz,) + gla_shape, F32),
                   jax.ShapeDtypeStruct((bsz,) + conv_shape, F32),
                   jax.ShapeDtypeStruct((bsz,) + ssm_shape, F32)],
        scratch_shapes=[pltpu.VMEM((SEQ_UNROLL, 8 + CHUNK, CONV_CH), F32)],
        compiler_params=_params("parallel", "arbitrary"),
    )(proj, proj, proj, proj, proj, proj,
      lw["gup"], lw["gb"], lw["gnw"], lw["cw"], lw["cb"], lw["dtb"], lw["alog"], lw["dsk"], lw["snw"],
      gla0, conv0, ssm0)


OUT_TM = 256


def _out_proj_kernel(x_ref, att_ref, gla_ref, ssm_ref, w_out_ref, nw_ref, rwh_ref, rwl_ref, rb_ref,
                     x1_ref, hn_ref, gate_ref, wo_ref):
    @pl.when(pl.program_id(0) == 0)
    def _():
        for r in range(0, D_MODEL, PREP_ROWS):
            wo_ref[r:r + PREP_ROWS, :] = w_out_ref[0, r:r + PREP_ROWS, :].astype(BF16)

    mo = (_dot(att_ref[...], wo_ref[0:WIDTH_A, :])
          + _dot(gla_ref[...], wo_ref[WIDTH_A:WIDTH_A + WIDTH_B, :])
          + _dot(ssm_ref[...], wo_ref[WIDTH_A + WIDTH_B:, :]))
    x1 = x_ref[...] + mo
    x1_ref[...] = x1
    ms = jnp.mean(x1 * x1, axis=-1, keepdims=True)
    hn = x1 * lax.rsqrt(ms + EPS) * nw_ref[...]
    hn_ref[...] = hn.astype(hn_ref.dtype)

    h_hi = hn.astype(BF16)
    h_lo = (hn - h_hi.astype(F32)).astype(BF16)
    logits = (_dot(h_hi, rwh_ref[...]) + _dot(h_lo, rwh_ref[...]) + _dot(h_hi, rwl_ref[...])
              + rb_ref[...])
    lane = lax.broadcasted_iota(jnp.int32, logits.shape, 1)
    lane_f = lane.astype(F32)
    neg = -jnp.inf
    big = 1000.0
    gmask = (lane >= ROUTER_GROUP_LANE) & (lane < ROUTER_GROUP_LANE + N_EXPERT_GROUPS)
    lg = jnp.where(gmask, logits, neg)
    gmax = lg.max(axis=-1, keepdims=True)
    g_idx = jnp.where(lg == gmax, lane_f, big).min(axis=-1, keepdims=True) - ROUTER_GROUP_LANE
    p_grp = 1.0 / jnp.where(gmask, jnp.exp(logits - gmax), 0.0).sum(axis=-1, keepdims=True)
    emask = (lane < N_EXPERTS) & ((lane >> 2).astype(F32) == g_idx)
    le = jnp.where(emask, logits, neg)
    v1 = le.max(axis=-1, keepdims=True)
    i1 = jnp.where(le == v1, lane_f, big).min(axis=-1, keepdims=True)
    le2 = jnp.where(lane_f == i1, neg, le)
    v2 = le2.max(axis=-1, keepdims=True)
    i2 = jnp.where(le2 == v2, lane_f, big).min(axis=-1, keepdims=True)
    e2 = jnp.exp(v2 - v1)
    den = 1.0 + e2
    gate_ref[...] = (jnp.where(lane_f == i1, (1.0 / den) * p_grp, 0.0)
                     + jnp.where(lane_f == i2, (e2 / den) * p_grp, 0.0)
                     + jnp.where(lane == ROUTER_GIDX_LANE, g_idx, 0.0))


def _out_proj(x, att, gla, ssm, lw, layer):
    t = x.shape[0]

    def rows(width):
        return pl.BlockSpec((OUT_TM, width), lambda i: (i, 0))

    def const(shape):
        return pl.BlockSpec(shape, lambda i: (0, 0))

    return pl.pallas_call(
        _out_proj_kernel,
        grid=(t // OUT_TM,),
        in_specs=[rows(D_MODEL), rows(WIDTH_A), rows(WIDTH_B), rows(WIDTH_C),
                  pl.BlockSpec((1, D_MODEL, D_MODEL), lambda i: (layer, 0, 0)), const((1, D_MODEL)),
                  const((D_MODEL, 128)), const((D_MODEL, 128)), const((1, 128))],
        out_specs=[rows(D_MODEL), rows(D_MODEL), rows(128)],
        out_shape=[jax.ShapeDtypeStruct((t, D_MODEL), F32),
                   jax.ShapeDtypeStruct((t, D_MODEL), BF16),
                   jax.ShapeDtypeStruct((t, 128), F32)],
        scratch_shapes=[pltpu.VMEM((D_MODEL, D_MODEL), BF16)],
        compiler_params=_params("arbitrary"),
    )(x, att, gla, ssm, lw["wo"], lw["n2w"], lw["rw_hi"], lw["rw_lo"], lw["rb"])


RT_TS = 256
RT_ALIGN = 16
RT_ROWS = 384
RT_SIZES = (256, 128, 64, 32, 16)
PAY_COLS = D_MODEL + 256
MOE_TM = 512


def _moe_max_tiles(t):
    return (t + (t // RT_TS) * N_EXPERT_GROUPS * (RT_ALIGN - 1)) // MOE_TM + N_EXPERT_GROUPS


def _sort_matrix(gate, c_ref, tile):
    lane = lax.broadcasted_iota(jnp.int32, (RT_TS, 128), 1)
    gidx = jnp.where(lane == ROUTER_GIDX_LANE, gate, 0.0).sum(axis=1, keepdims=True)
    onehot = lane.astype(F32) == gidx
    r = lax.broadcasted_iota(jnp.int32, (RT_TS, RT_TS), 0)
    c = lax.broadcasted_iota(jnp.int32, (RT_TS, RT_TS), 1)
    earlier = jnp.where(c < r, 1.0, 0.0).astype(BF16)
    before = _dot(earlier, jnp.where(onehot, 1.0, 0.0).astype(BF16))
    start = jnp.zeros((RT_TS, 128), F32)
    for g in range(N_EXPERT_GROUPS):
        start = jnp.where(lane == g, c_ref[N_EXPERT_GROUPS * tile + g].astype(F32), start)
    dest = jnp.where(onehot, before + start, 0.0).sum(axis=1, keepdims=True)
    rows = lax.broadcasted_iota(jnp.int32, (RT_TS, RT_ROWS), 1).astype(F32)
    return jnp.where(rows == dest, 1.0, 0.0).astype(BF16)


def _piece_copies(length, local, glob, make_copy, act):
    for size in RT_SIZES:
        take = (length & size) != 0

        @pl.when(take)
        def _(local=local, glob=glob, size=size):
            act(make_copy(pl.multiple_of(local, RT_ALIGN), pl.multiple_of(glob, RT_ALIGN), size))

        step = jnp.where(take, size, 0)
        local = local + step
        glob = glob + step


def _run_copies(c_ref, lp_ref, off_ref, tile, make_copy, act):
    for g in range(N_EXPERT_GROUPS):
        k = N_EXPERT_GROUPS * tile + g
        _piece_copies(lp_ref[k], c_ref[k], off_ref[k], make_copy, act)


def _dispatch_kernel(c_ref, lp_ref, off_ref, tail_ref, nact_ref, hn_ref, gate_ref, xs_hbm,
                     y_ref, zero_ref, sem, zsem, *, first_spare, max_tiles):
    i = pl.program_id(0)
    nt = pl.num_programs(0)
    slot = i % 2
    gate = gate_ref[...]
    pt = _sort_matrix(gate, c_ref, i)
    g_hi = gate.astype(BF16)
    g_lo = (gate - g_hi.astype(F32)).astype(BF16)
    y_ref[slot, :, 0:D_MODEL] = _dot_tn(pt, hn_ref[...]).astype(BF16)
    y_ref[slot, :, D_MODEL:D_MODEL + 128] = _dot_tn(pt, g_hi).astype(BF16)
    y_ref[slot, :, D_MODEL + 128:] = _dot_tn(pt, g_lo).astype(BF16)

    def copy_from(s):
        return lambda local, glob, size: pltpu.make_async_copy(
            y_ref.at[s, pl.ds(local, size)], xs_hbm.at[pl.ds(glob, size)], sem.at[s])

    _run_copies(c_ref, lp_ref, off_ref, i, copy_from(slot), lambda cp: cp.start())

    @pl.when(i > 0)
    def _():
        _run_copies(c_ref, lp_ref, off_ref, i - 1, copy_from(1 - slot), lambda cp: cp.wait())

    @pl.when(i == nt - 1)
    def _():
        _run_copies(c_ref, lp_ref, off_ref, i, copy_from(slot), lambda cp: cp.wait())
        zero_ref[...] = jnp.zeros_like(zero_ref)

        def zero_copy(local, glob, size):
            return pltpu.make_async_copy(zero_ref.at[pl.ds(local, size)],
                                         xs_hbm.at[pl.ds(glob, size)], zsem)

        def zero_fill(act):
            for g in range(N_EXPERT_GROUPS):
                _piece_copies(tail_ref[N_EXPERT_GROUPS + g], 0, tail_ref[g], zero_copy, act)
            for j in range(first_spare, max_tiles):
                @pl.when(j >= nact_ref[0])
                def _(j=j):
                    act(zero_copy(0, j * MOE_TM, MOE_TM))

        zero_fill(lambda cp: cp.start())
        zero_fill(lambda cp: cp.wait())


def _moe_group_kernel(grp_ref, nact_ref, xs_ref, wg_ref, wu_ref, wd_ref, o_ref, wg_bf, wu_bf, wd_bf):
    j = pl.program_id(0)

    @pl.when(jnp.logical_or(j == 0, grp_ref[j] != grp_ref[jnp.maximum(j - 1, 0)]))
    def _():
        wg_bf[...] = wg_ref[0].astype(BF16)
        wu_bf[...] = wu_ref[0].astype(BF16)
        wd_bf[...] = wd_ref[0].astype(BF16)

    @pl.when(j >= nact_ref[0])
    def _():
        o_ref[...] = jnp.zeros_like(o_ref)

    @pl.when(j < nact_ref[0])
    def _():
        x = xs_ref[:, 0:D_MODEL]
        gate = (xs_ref[:, D_MODEL:D_MODEL + 128].astype(F32)
                + xs_ref[:, D_MODEL + 128:].astype(F32))
        lane = lax.broadcasted_iota(jnp.int32, gate.shape, 1)
        first = grp_ref[j] * EXPERTS_PER_GROUP
        acc = None
        for e in range(EXPERTS_PER_GROUP):
            ge = jnp.where(lane == first + e, gate, 0.0).sum(axis=-1, keepdims=True)
            hid = _silu(_dot(x, wg_bf[e])) * _dot(x, wu_bf[e]) * ge
            part = _dot(hid.astype(BF16), wd_bf[e])
            acc = part if acc is None else acc + part
        o_ref[...] = acc


def _combine_kernel(c_ref, lp_ref, off_ref, x_ref, gate_ref, fw_ref, ys_hbm, o_ref, z_ref, sem,
                    *, final_norm):
    i = pl.program_id(0)
    nt = pl.num_programs(0)
    slot = i % 2

    def copy_into(s):
        return lambda local, glob, size: pltpu.make_async_copy(
            ys_hbm.at[pl.ds(glob, size)], z_ref.at[s, pl.ds(local, size)], sem.at[s])

    @pl.when(i == 0)
    def _():
        z_ref[...] = jnp.zeros_like(z_ref)
        _run_copies(c_ref, lp_ref, off_ref, 0, copy_into(0), lambda cp: cp.start())

    @pl.when(i + 1 < nt)
    def _():
        _run_copies(c_ref, lp_ref, off_ref, i + 1, copy_into(1 - slot), lambda cp: cp.start())

    _run_copies(c_ref, lp_ref, off_ref, i, copy_into(slot), lambda cp: cp.wait())
    pt = _sort_matrix(gate_ref[...], c_ref, i)
    z = z_ref[slot]
    z_hi = z.astype(BF16)
    z_lo = (z - z_hi.astype(F32)).astype(BF16)
    y = x_ref[...] + _dot(pt, z_hi) + _dot(pt, z_lo)
    if final_norm:
        ms = jnp.mean(y * y, axis=-1, keepdims=True)
        y = y * lax.rsqrt(ms + EPS) * fw_ref[...]
    o_ref[...] = y


def _route_meta(gate, max_tiles):
    t = gate.shape[0]
    nt = t // RT_TS
    groups = jnp.arange(N_EXPERT_GROUPS, dtype=jnp.int32)
    g = gate[:, ROUTER_GIDX_LANE].astype(jnp.int32).reshape(nt, RT_TS, 1)
    count = (g == groups).astype(jnp.int32).sum(axis=1)
    lp = (count + RT_ALIGN - 1) // RT_ALIGN * RT_ALIGN
    total = lp.sum(axis=0)
    ntile = (total + MOE_TM - 1) // MOE_TM
    last = jnp.cumsum(ntile)
    first_row = (last - ntile) * MOE_TM
    c = jnp.cumsum(lp, axis=1) - lp
    off = first_row[None, :] + jnp.cumsum(lp, axis=0) - lp
    tail = jnp.concatenate([first_row + total, ntile * MOE_TM - total])
    j = jnp.arange(max_tiles, dtype=jnp.int32)[:, None]
    grp = jnp.minimum((j >= last[None, :]).astype(jnp.int32).sum(axis=1), N_EXPERT_GROUPS - 1)
    return c.reshape(-1), lp.reshape(-1), off.reshape(-1), tail, grp, last[-1:]


def _moe(x, hn, gate, lw, layer, fw, final_norm):
    t = x.shape[0]
    nt = t // RT_TS
    max_tiles = _moe_max_tiles(t)
    c, lp, off, tail, grp, nact = _route_meta(gate, max_tiles)

    def rows(width):
        return pl.BlockSpec((RT_TS, width), lambda i, *_: (i, 0))

    xs = pl.pallas_call(
        functools.partial(_dispatch_kernel, first_spare=t // MOE_TM, max_tiles=max_tiles),
        grid_spec=pltpu.PrefetchScalarGridSpec(
            num_scalar_prefetch=5, grid=(nt,),
            in_specs=[rows(D_MODEL), rows(128)],
            out_specs=pl.BlockSpec(memory_space=pl.ANY),
            scratch_shapes=[pltpu.VMEM((2, RT_ROWS, PAY_COLS), BF16),
                            pltpu.VMEM((MOE_TM, PAY_COLS), BF16),
                            pltpu.SemaphoreType.DMA((2,)), pltpu.SemaphoreType.DMA(())]),
        out_shape=jax.ShapeDtypeStruct((max_tiles * MOE_TM, PAY_COLS), BF16),
        compiler_params=_params("arbitrary"),
    )(c, lp, off, tail, nact, hn, gate)

    def experts(shape):
        return pl.BlockSpec((1, EXPERTS_PER_GROUP) + shape, lambda j, grp, nact: (layer, grp[j], 0, 0))

    def rounded(shape):
        return pltpu.VMEM((EXPERTS_PER_GROUP,) + shape, BF16)

    ys = pl.pallas_call(
        _moe_group_kernel,
        grid_spec=pltpu.PrefetchScalarGridSpec(
            num_scalar_prefetch=2, grid=(max_tiles,),
            in_specs=[pl.BlockSpec((MOE_TM, PAY_COLS), lambda j, grp, nact: (j, 0)),
                      experts((D_MODEL, D_EXPERT)), experts((D_MODEL, D_EXPERT)),
                      experts((D_EXPERT, D_MODEL))],
            out_specs=pl.BlockSpec((MOE_TM, D_MODEL), lambda j, grp, nact: (j, 0)),
            scratch_shapes=[rounded((D_MODEL, D_EXPERT)), rounded((D_MODEL, D_EXPERT)),
                            rounded((D_EXPERT, D_MODEL))]),
        out_shape=jax.ShapeDtypeStruct((max_tiles * MOE_TM, D_MODEL), F32),
        compiler_params=_params("arbitrary"),
    )(grp, nact, xs, lw["wg"], lw["wu"], lw["wd"])

    return pl.pallas_call(
        functools.partial(_combine_kernel, final_norm=final_norm),
        grid_spec=pltpu.PrefetchScalarGridSpec(
            num_scalar_prefetch=3, grid=(nt,),
            in_specs=[rows(D_MODEL), rows(128), pl.BlockSpec((1, D_MODEL), lambda i, *_: (0, 0)),
                      pl.BlockSpec(memory_space=pl.ANY)],
            out_specs=rows(D_MODEL),
            scratch_shapes=[pltpu.VMEM((2, RT_ROWS, D_MODEL), F32),
                            pltpu.SemaphoreType.DMA((2,))]),
        out_shape=jax.ShapeDtypeStruct((t, D_MODEL), F32),
        compiler_params=_params("arbitrary"),
    )(c, lp, off, x, gate, fw, ys)


def _lane_place(vec, start, width=128):
    return jnp.zeros((1, width), F32).at[0, start:start + vec.shape[0]].set(vec.astype(F32))


REL_PAD = 704


def _toeplitz(v, rows, cols):
    n = v.shape[-1]
    tiled = jnp.tile(v, (1,) * (v.ndim - 1) + (rows,))[..., :rows * (n - 1)]
    return tiled.reshape(v.shape[:-1] + (rows, n - 1))[..., :cols]


def _band_bias(table):
    t = table.astype(F32)
    ext = jnp.concatenate([jnp.repeat(t[:, :1], REL_PAD, axis=1), t,
                           jnp.repeat(t[:, -1:], REL_PAD, axis=1)], axis=1)
    extr = ext[:, ::-1]
    top = ext.shape[1] - 1 - (REL_CLIP + REL_PAD)

    def band(off, nq, nk):
        n = nq + nk
        v = jnp.concatenate([extr[:, top - off:top - off + nk + 1],
                             extr[:, top - off - (nq - 1):top - off]], axis=1)
        assert v.shape[1] == n
        return _toeplitz(v, nq, nk)

    sample = band(WINDOW, CHUNK, BAND)
    kc = (jnp.arange(ATT_KB) // CHUNK)[:, None]
    qc = (jnp.arange(ATT_QB) // CHUNK)[None, :]
    block = lambda off: jnp.swapaxes(band(off, ATT_QB, ATT_KB), 1, 2)
    prompt = jnp.stack([jnp.where((kc >= qc)[None], block(2 * ATT_KB), -1e30),
                        block(ATT_KB),
                        jnp.where((kc <= qc)[None], block(0), -1e30),
                        jnp.full((H_A, ATT_KB, ATT_QB), -1e30, F32)])
    return sample * LOG2_E, prompt * LOG2_E


def _layer_weights(i, norm1_w, w_in, rel_bias_table, gla_w_gate_up, gla_b_gate, gla_norm_w,
                   ssm_conv_w, ssm_conv_b, ssm_dt_bias, ssm_a_log, ssm_d, ssm_norm_w, w_out,
                   norm2_w, router_group_w, router_group_b, router_expert_w, router_expert_b,
                   exp_w_gate, exp_w_up, exp_w_down):
    bias_s, bias_p = _band_bias(rel_bias_table[i])
    rw = jnp.concatenate([router_expert_w[i], router_group_w[i],
                          jnp.zeros((D_MODEL, 128 - N_EXPERTS - N_EXPERT_GROUPS), F32)], axis=1)
    rw_hi = rw.astype(BF16)
    return dict(
        n1w=norm1_w[i][None, :],
        w_in=w_in,
        bias_s=bias_s,
        bias_p=bias_p,
        gup=jnp.zeros((128, 128), F32).at[:GATE_RANK, :].set(gla_w_gate_up[i]).astype(BF16),
        gb=gla_b_gate[i][None, :].astype(F32),
        gnw=jnp.tile(gla_norm_w[i], H_B)[None, :].astype(F32),
        cw=ssm_conv_w[i].astype(F32),
        cb=ssm_conv_b[i][None, :].astype(F32),
        dtb=_lane_place(ssm_dt_bias[i], MISC_DT_LANE),
        alog=jnp.repeat(ssm_a_log[i].astype(F32), P_C)[None, :],
        dsk=jnp.repeat(ssm_d[i].astype(F32), P_C)[None, :],
        snw=ssm_norm_w[i][None, :].astype(F32),
        wo=w_out,
        n2w=norm2_w[i][None, :],
        rw_hi=rw_hi,
        rw_lo=(rw - rw_hi.astype(F32)).astype(BF16),
        rb=jnp.concatenate([router_expert_b[i], router_group_b[i],
                            jnp.zeros((128 - N_EXPERTS - N_EXPERT_GROUPS,), F32)])[None, :],
        wg=exp_w_gate,
        wu=exp_w_up,
        wd=exp_w_down,
    )


def _stream_layer(x, bsz, lw, layer, cache_k, cache_v, gla0, conv0, ssm0, fw, final_norm):
    l = x.shape[0] // bsz
    if cache_k is None:
        qk, rest, k_keep, v_keep, vt = _in_proj(x, lw["n1w"], lw["w_in"], layer, True, l)
        att = _attn_prompt(qk.reshape(bsz, l, 2 * WIDTH_A),
                           vt.reshape(bsz, l // ATT_KB, WIDTH_A, ATT_KB), lw["bias_p"])
    else:
        qk, rest, k_keep, v_keep, kv = _in_proj(x, lw["n1w"], lw["w_in"], layer, False, l)
        att = _attn_sample(qk.reshape(bsz, l, 2 * WIDTH_A), kv.reshape(bsz, l, 2 * WIDTH_A),
                           cache_k, cache_v, layer, lw["bias_s"])
    gla_o, ssm_o, gla_s, conv_s, ssm_s = _scan(rest.reshape(bsz, l, REST_COLS), lw, gla0, conv0, ssm0)
    t = bsz * l
    x1, hn, gate = _out_proj(x, att.reshape(t, WIDTH_A), gla_o.reshape(t, WIDTH_B),
                             ssm_o.reshape(t, WIDTH_C), lw, layer)
    x2 = _moe(x1, hn, gate, lw, layer, fw, final_norm)
    keep = min(WINDOW, l)
    k_keep = k_keep.reshape(bsz, keep, H_A, DH_A)
    v_keep = v_keep.reshape(bsz, keep, H_A, DH_A)
    gla_state = gla_s.reshape(bsz, DV_B, H_B, DK_B).transpose(0, 2, 3, 1)
    conv_state = conv_s[:, 8 - (CONV_W - 1):, :]
    return x2, k_keep, v_keep, gla_state, conv_state, ssm_s


def kernel(x_prompt, x_sample, cache_k_a, cache_v_a, state_gla, state_conv, state_ssm, norm1_w, w_in, rel_bias_table, gla_w_gate_up, gla_b_gate, gla_norm_w, ssm_conv_w, ssm_conv_b, ssm_dt_bias, ssm_a_log, ssm_d, ssm_norm_w, w_out, norm2_w, router_group_w, router_group_b, router_expert_w, router_expert_b, exp_w_gate, exp_w_up, exp_w_down, final_norm_w):
    bp, sp, _ = x_prompt.shape
    bs, ss, _ = x_sample.shape
    xp = x_prompt.reshape(bp * sp, D_MODEL)
    xs = x_sample.reshape(bs * ss, D_MODEL)
    fw = final_norm_w[None, :].astype(F32)
    outs_p, outs_s = [], []
    for i in range(DEPTH):
        lw = _layer_weights(i, norm1_w, w_in, rel_bias_table, gla_w_gate_up, gla_b_gate, gla_norm_w,
                            ssm_conv_w, ssm_conv_b, ssm_dt_bias, ssm_a_log, ssm_d, ssm_norm_w, w_out,
                            norm2_w, router_group_w, router_group_b, router_expert_w, router_expert_b,
                            exp_w_gate, exp_w_up, exp_w_down)
        last = i == DEPTH - 1
        xp, *sp_out = _stream_layer(
            xp, bp, lw, i, None, None,
            jnp.zeros((bp, DV_B, H_B * DK_B), F32),
            jnp.zeros((bp, 8, CONV_CH), F32),
            jnp.zeros((bp, H_C, P_C, D_STATE), F32), fw, last)
        outs_p.append(sp_out)
        gla0 = state_gla[i].astype(F32).transpose(0, 3, 1, 2).reshape(bs, DV_B, H_B * DK_B)
        conv0 = jnp.pad(state_conv[i].astype(F32), ((0, 0), (8 - (CONV_W - 1), 0), (0, 0)))
        xs, *ss_out = _stream_layer(
            xs, bs, lw, i, cache_k_a, cache_v_a,
            gla0, conv0, state_ssm[i].astype(F32), fw, last)
        outs_s.append(ss_out)
    stack = lambda outs, j: jnp.stack([o[j] for o in outs])
    return (xp.reshape(bp, sp, D_MODEL), xs.reshape(bs, ss, D_MODEL),
            stack(outs_p, 0), stack(outs_p, 1), stack(outs_p, 2), stack(outs_p, 3), stack(outs_p, 4),
            stack(outs_s, 0), stack(outs_s, 1), stack(outs_s, 2), stack(outs_s, 3), stack(outs_s, 4))
```

```python
import functools

import jax
import jax.numpy as jnp
from jax import lax
from jax.experimental import pallas as pl
from jax.experimental.pallas import tpu as pltpu

F32 = jnp.float32
BF16 = jnp.bfloat16

D_MODEL = 1024
DEPTH = 2
EPS = 1e-6
CHUNK = 64
N_PAST_CHUNKS = 8
BAND = (N_PAST_CHUNKS + 1) * CHUNK
WINDOW = N_PAST_CHUNKS * CHUNK
H_A, DH_A, WIDTH_A = 8, 64, 512
REL_CLIP = 128
LOG2_E = 1.4426950408889634
H_B, DK_B, DV_B, WIDTH_B = 4, 32, 64, 256
GATE_RANK = 16
GATE_TAU = 16.0
H_C, P_C, WIDTH_C = 4, 64, 256
N_GROUPS_C = 2
D_STATE = 64
CONV_W = 4
CONV_CH = 512
N_EXPERT_GROUPS = 4
EXPERTS_PER_GROUP = 4
N_EXPERTS = 16
D_EXPERT = 256

PROJ_COLS = 3200
MISC_DT_LANE = GATE_RANK
ROUTER_GROUP_LANE = N_EXPERTS
ROUTER_GIDX_LANE = 20

SEQ_GROUP = 8
SEQ_UNROLL = 8
VMEM_LIMIT_BYTES = 56 * 1024 * 1024

NT_DIMS = (((1,), (1,)), ((), ()))
TN_DIMS = (((0,), (0,)), ((), ()))


def _params(*sem):
    return pltpu.CompilerParams(dimension_semantics=sem, vmem_limit_bytes=VMEM_LIMIT_BYTES)


def _dot(a, b):
    return jnp.dot(a, b, preferred_element_type=F32)


def _dot_nt(a, b):
    return lax.dot_general(a, b, NT_DIMS, preferred_element_type=F32)


def _dot_tn(a, b):
    return lax.dot_general(a, b, TN_DIMS, preferred_element_type=F32)


def _split2(a):
    hi = a.astype(BF16)
    return hi, (a - hi.astype(F32)).astype(BF16)


def _dot01_left(m01, a):
    hi, lo = _split2(a)
    return _dot(m01, hi) + _dot(m01, lo)


def _dot01_right(a, m01):
    hi, lo = _split2(a)
    return _dot(hi, m01) + _dot(lo, m01)


def _round_robin(gens):
    results = [None] * len(gens)
    live = list(range(len(gens)))
    while live:
        for idx in list(live):
            try:
                next(gens[idx])
            except StopIteration as stop:
                results[idx] = stop.value
                live.remove(idx)
    return results


def _silu(x):
    return x * jax.nn.sigmoid(x)


def _softplus(x):
    return jnp.maximum(x, 0.0) + jnp.log1p(jnp.exp(-jnp.abs(x)))


IN_TM = 256
QKV_COLS = 3 * WIDTH_A
REST_COLS = PROJ_COLS - QKV_COLS
REST_CHUNKS = ((0, 512), (512, 1024), (1024, REST_COLS))


IN_COLS = 3092
IN_TAIL = QKV_COLS + 512
PREP_ROWS = 128


def _prepare_w_in(w_in_ref, w_ref, wvt_ref):
    scale = DH_A ** -0.5 * LOG2_E
    n_tail = REST_COLS - 512 - 128
    for r in range(0, D_MODEL, PREP_ROWS):
        rs = slice(r, r + PREP_ROWS)
        w_ref[rs, 0:WIDTH_A] = (w_in_ref[0, rs, 0:WIDTH_A] * scale).astype(BF16)
        w_ref[rs, WIDTH_A:IN_TAIL] = w_in_ref[0, rs, WIDTH_A:IN_TAIL].astype(BF16)
        tail = w_in_ref[0, rs, IN_TAIL:IN_COLS]
        w_ref[rs, IN_TAIL:IN_TAIL + n_tail] = tail[:, GATE_RANK:GATE_RANK + n_tail].astype(BF16)
        misc = jnp.concatenate([tail[:, 0:GATE_RANK], tail[:, GATE_RANK + n_tail:],
                                jnp.zeros((PREP_ROWS, 128 - GATE_RANK - H_C), F32)], axis=1)
        w_ref[rs, IN_TAIL + n_tail:] = misc.astype(BF16)
    if wvt_ref is not None:
        wvt_ref[...] = w_in_ref[0, :, 2 * WIDTH_A:QKV_COLS].T.astype(BF16)


def _in_proj_kernel(x_ref, nw_ref, w_in_ref, *refs, with_vt):
    if with_vt:
        qk_ref, rest_ref, kk_ref, vk_ref, vt_ref, w_ref, wvt_ref = refs
    else:
        qk_ref, rest_ref, kk_ref, vk_ref, kv_ref, w_ref = refs
        wvt_ref = None

    @pl.when(pl.program_id(0) == 0)
    def _():
        _prepare_w_in(w_in_ref, w_ref, wvt_ref)

    x = x_ref[...]
    ms = jnp.mean(x * x, axis=-1, keepdims=True)
    xn = (x * lax.rsqrt(ms + EPS) * nw_ref[...]).astype(BF16)
    qk_ref[:, 0:WIDTH_A] = _dot(xn, w_ref[:, 0:WIDTH_A]).astype(BF16)
    k = _dot(xn, w_ref[:, WIDTH_A:2 * WIDTH_A])
    v = _dot(xn, w_ref[:, 2 * WIDTH_A:QKV_COLS])
    qk_ref[:, WIDTH_A:] = k.astype(BF16)
    if not with_vt:
        kv_ref[:, 0:WIDTH_A] = k
        kv_ref[:, WIDTH_A:] = v

    def keep_rows():
        kk = kk_ref.reshape(IN_TM * H_A, DH_A)
        vk = vk_ref.reshape(IN_TM * H_A, DH_A)
        for h in range(H_A):
            sl = slice(DH_A * h, DH_A * (h + 1))
            kk[pl.ds(h, IN_TM, stride=H_A), :] = k[:, sl]
            vk[pl.ds(h, IN_TM, stride=H_A), :] = v[:, sl]

    keep_rows()
    for lo, hi in REST_CHUNKS:
        rest_ref[:, lo:hi] = _dot(xn, w_ref[:, QKV_COLS + lo:QKV_COLS + hi])
    if with_vt:
        vt_ref[0] = _dot_nt(wvt_ref[...], xn).astype(BF16)


def _in_proj(x, nw, w_in, layer, with_vt, seq_len):
    t = x.shape[0]
    tiles_per_seq = max(seq_len // IN_TM, 1)
    keep_tiles = max(min(WINDOW, seq_len) // IN_TM, 1)
    n_keep = t // IN_TM // tiles_per_seq * keep_tiles

    def rows(width):
        return pl.BlockSpec((IN_TM, width), lambda i: (i, 0))

    def kept(i):
        return (i // tiles_per_seq * keep_tiles
                + jnp.maximum(i % tiles_per_seq - (tiles_per_seq - keep_tiles), 0), 0, 0)

    keep_spec = pl.BlockSpec((IN_TM, H_A, DH_A), kept)
    keep_shape = jax.ShapeDtypeStruct((n_keep * IN_TM, H_A, DH_A), F32)
    in_specs = [rows(D_MODEL), pl.BlockSpec((1, D_MODEL), lambda i: (0, 0)),
                pl.BlockSpec((1, D_MODEL, IN_COLS), lambda i: (layer, 0, 0))]
    scratch = [pltpu.VMEM((D_MODEL, PROJ_COLS), BF16)]
    out_specs = [rows(2 * WIDTH_A), rows(REST_COLS), keep_spec, keep_spec]
    out_shape = [jax.ShapeDtypeStruct((t, 2 * WIDTH_A), BF16),
                 jax.ShapeDtypeStruct((t, REST_COLS), F32), keep_shape, keep_shape]
    if with_vt:
        out_specs.append(pl.BlockSpec((1, WIDTH_A, IN_TM), lambda i: (i, 0, 0)))
        out_shape.append(jax.ShapeDtypeStruct((t // IN_TM, WIDTH_A, IN_TM), BF16))
        scratch.append(pltpu.VMEM((WIDTH_A, D_MODEL), BF16))
    else:
        out_specs.append(rows(2 * WIDTH_A))
        out_shape.append(jax.ShapeDtypeStruct((t, 2 * WIDTH_A), F32))
    return pl.pallas_call(
        functools.partial(_in_proj_kernel, with_vt=with_vt),
        grid=(t // IN_TM,),
        in_specs=in_specs, out_specs=out_specs, out_shape=out_shape, scratch_shapes=scratch,
        compiler_params=_params("arbitrary"),
    )(x, nw, w_in)


ATT_QB = 4 * CHUNK
ATT_KB = IN_TM
ATT_NKB = (ATT_QB + WINDOW) // ATT_KB


def _attn_prompt_kernel(q_ref, k_ref, vt_ref, bias_ref, o_ref, ot_ref):
    blk = pl.program_id(1)
    sb = jnp.maximum(blk - WINDOW // ATT_KB, 0)
    kstart = pl.multiple_of(sb * ATT_KB, ATT_KB)
    lane = lax.broadcasted_iota(jnp.int32, (ATT_QB, 128), 1)
    kinds = (jnp.where(blk == 0, 2, jnp.where(blk == 1, 1, 0)),
             jnp.where(blk == 0, 3, jnp.where(blk == 1, 2, 1)),
             jnp.where(blk <= 1, 3, 2))

    def scores(h):
        pair = slice(128 * (h // 2), 128 * (h // 2 + 1))
        qp = q_ref[0, :, pair]
        qz = jnp.where((lane >= DH_A) == bool(h % 2), qp, jnp.zeros_like(qp))
        return [_dot_nt(k_ref[0, pl.ds(kstart + ATT_KB * i, ATT_KB), pair], qz)
                + bias_ref[kinds[i], h] for i in range(ATT_NKB)]

    def finish(h, s):
        m = s[0].max(axis=0, keepdims=True)
        for si in s[1:]:
            m = jnp.maximum(m, si.max(axis=0, keepdims=True))
        l = None
        ot = None
        for i, si in enumerate(s):
            p = jnp.exp2(si - m)
            li = p.sum(axis=0, keepdims=True)
            oi = _dot(vt_ref[0, sb + i, DH_A * h:DH_A * (h + 1), :], p.astype(BF16))
            l = li if l is None else l + li
            ot = oi if ot is None else ot + oi
        ot_ref[DH_A * h:DH_A * (h + 1), :] = ot / l

    pending = scores(0)
    for h in range(H_A):
        upcoming = scores(h + 1) if h + 1 < H_A else None
        finish(h, pending)
        pending = upcoming
    o_ref[0] = ot_ref[...].T.astype(o_ref.dtype)


def _attn_prompt(qk, vt, bias_t):
    bsz, s, _ = qk.shape
    nkb = s // ATT_KB
    return pl.pallas_call(
        _attn_prompt_kernel,
        grid=(bsz, s // ATT_QB),
        in_specs=[pl.BlockSpec((1, ATT_QB, WIDTH_A), lambda b, c: (b, c, 0)),
                  pl.BlockSpec((1, s, WIDTH_A), lambda b, c: (b, 0, 1)),
                  pl.BlockSpec((1, nkb, WIDTH_A, ATT_KB), lambda b, c: (b, 0, 0, 0)),
                  pl.BlockSpec((4, H_A, ATT_KB, ATT_QB), lambda b, c: (0, 0, 0, 0))],
        out_specs=pl.BlockSpec((1, ATT_QB, WIDTH_A), lambda b, c: (b, c, 0)),
        out_shape=jax.ShapeDtypeStruct((bsz, s, WIDTH_A), BF16),
        scratch_shapes=[pltpu.VMEM((WIDTH_A, ATT_QB), F32)],
        compiler_params=_params("parallel", "arbitrary"),
    )(qk, qk, vt, bias_t)


def _attn_sample_kernel(q_ref, kv_ref, ck_ref, cv_ref, bias_ref, o_ref):
    lane = lax.broadcasted_iota(jnp.int32, (CHUNK, 128), 1)

    def scores(h):
        pair = slice(128 * (h // 2), 128 * (h // 2 + 1))
        qp = q_ref[0, :, pair]
        qz = jnp.where((lane >= DH_A) == bool(h % 2), qp, jnp.zeros_like(qp))
        return (_dot_nt(qz, ck_ref[0, 0, :, pair]) + bias_ref[h, :, 0:WINDOW],
                _dot_nt(qz, kv_ref[0, :, pair].astype(BF16)) + bias_ref[h, :, WINDOW:BAND])

    def finish(h, s):
        s1, s2 = s
        pair = slice(128 * (h // 2), 128 * (h // 2 + 1))
        vpair = slice(WIDTH_A + 128 * (h // 2), WIDTH_A + 128 * (h // 2 + 1))
        half = slice(DH_A * (h % 2), DH_A * (h % 2 + 1))
        m = jnp.maximum(s1.max(axis=-1, keepdims=True), s2.max(axis=-1, keepdims=True))
        p1 = jnp.exp2(s1 - m)
        p2 = jnp.exp2(s2 - m)
        l = p1.sum(axis=-1, keepdims=True) + p2.sum(axis=-1, keepdims=True)
        o = (_dot(p1.astype(BF16), cv_ref[0, 0, :, pair])
             + _dot(p2.astype(BF16), kv_ref[0, :, vpair].astype(BF16)))
        o_ref[0, :, DH_A * h:DH_A * (h + 1)] = (o[:, half] / l).astype(o_ref.dtype)

    pending = scores(0)
    for h in range(H_A):
        upcoming = scores(h + 1) if h + 1 < H_A else None
        finish(h, pending)
        pending = upcoming


def _attn_sample(qk, kv, cache_k, cache_v, layer, bias):
    bsz = qk.shape[0]
    cache_spec = pl.BlockSpec((1, 1, WINDOW, WIDTH_A), lambda b: (layer, b, 0, 0))
    return pl.pallas_call(
        _attn_sample_kernel,
        grid=(bsz,),
        in_specs=[pl.BlockSpec((1, CHUNK, WIDTH_A), lambda b: (b, 0, 0)),
                  pl.BlockSpec((1, CHUNK, 2 * WIDTH_A), lambda b: (b, 0, 0)),
                  cache_spec, cache_spec,
                  pl.BlockSpec((H_A, CHUNK, BAND), lambda b: (0, 0, 0))],
        out_specs=pl.BlockSpec((1, CHUNK, WIDTH_A), lambda b: (b, 0, 0)),
        out_shape=jax.ShapeDtypeStruct((bsz, CHUNK, WIDTH_A), BF16),
        compiler_params=_params("parallel"),
    )(qk, kv, cache_k, cache_v, bias)


def _scan_kernel(bqk_ref, bv_ref, bog_ref, cz_ref, cx_ref, misc_ref,
                 gup_ref, gb_ref, gnw_ref, cw_ref, cb_ref, dtb_ref, alog_ref, dsk_ref, snw_ref,
                 gla0_ref, conv0_ref, ssm0_ref,
                 gla_o_ref, ssm_o_ref, gla_s_ref, conv_s_ref, ssm_s_ref,
                 xpad_ref):
    c = pl.program_id(1)
    nseq = bqk_ref.shape[0]

    @pl.when(c == 0)
    def _():
        gla_s_ref[...] = gla0_ref[...]
        conv_s_ref[...] = conv0_ref[...]
        ssm_s_ref[...] = ssm0_ref[...]

    r64 = lax.broadcasted_iota(jnp.int32, (CHUNK, CHUNK), 0)
    c64 = lax.broadcasted_iota(jnp.int32, (CHUNK, CHUNK), 1)
    tri = c64 <= r64
    tri_bf = jnp.where(tri, 1.0, 0.0).astype(BF16)
    ones_bf = jnp.ones((CHUNK, CHUNK), BF16)
    row_x = lax.broadcasted_iota(jnp.int32, (CHUNK, WIDTH_C), 0)
    s_x = lax.broadcasted_iota(jnp.int32, (CHUNK, WIDTH_C), 1) & (CHUNK - 1)
    causal_x = s_x <= row_x
    upper_x = row_x <= s_x
    er = lax.broadcasted_iota(jnp.int32, (128, WIDTH_C), 0)
    el = lax.broadcasted_iota(jnp.int32, (128, WIDTH_C), 1)
    expand_bf = jnp.where(er == MISC_DT_LANE + (el >> 6), 1.0, 0.0).astype(BF16)
    br = lax.broadcasted_iota(jnp.int32, (WIDTH_C, WIDTH_C), 0)
    bl = lax.broadcasted_iota(jnp.int32, (WIDTH_C, WIDTH_C), 1)
    blk64_bf = jnp.where((br >> 6) == (bl >> 6), 1.0, 0.0).astype(BF16)
    blk128_bf = jnp.where((br >> 7) == (bl >> 7), 1.0, 0.0).astype(BF16)

    gup = gup_ref[...]
    gb = gb_ref[...]
    gnw = gnw_ref[...]
    cb = cb_ref[...]
    dtb = dtb_ref[...]
    neg_a = -jnp.exp(alog_ref[...])
    dsk = dsk_ref[...]
    snw = snw_ref[...]

    def load_seq(b):
        return dict(misc=misc_ref[b], qk=bqk_ref[b], v=bv_ref[b], bog=bog_ref[b], cz=cz_ref[b],
                    cx=cx_ref[b], gla=gla_s_ref[b], conv=conv_s_ref[b],
                    ssm=[ssm_s_ref[b, h] for h in range(H_C)])

    def store_seq(b, out):
        gla_o_ref[b] = out["gla_o"]
        ssm_o_ref[b] = out["ssm_o"]
        gla_s_ref[b] = out["gla"]
        conv_s_ref[b] = out["conv"]
        for h in range(H_C):
            ssm_s_ref[b, h] = out["ssm"][h]

    def compute_seq(inp, slot):
        out = {}
        misc = inp["misc"]

        z = _dot(misc.astype(BF16), gup) + gb
        g = (jnp.minimum(z, 0.0) - jnp.log1p(jnp.exp(-jnp.abs(z)))) * (1.0 / GATE_TAU)
        bcum = _dot01_left(tri_bf, g)
        yield
        q = inp["qk"][:, :H_B * DK_B]
        k = inp["qk"][:, H_B * DK_B:]
        blast = bcum[CHUNK - 1:CHUNK, :]
        qb = (q * (DK_B ** -0.5) * jnp.exp(bcum)).astype(BF16)
        kb = (k * jnp.exp(-bcum)).astype(BF16)
        kl = (k * jnp.exp(blast - bcum)).astype(BF16)
        vb = inp["v"].astype(BF16)
        st = inp["gla"]
        st_bf = st.astype(BF16)
        yield
        o_parts, s_parts = [], []
        for h in range(H_B):
            ks = slice(DK_B * h, DK_B * (h + 1))
            vs = slice(DV_B * h, DV_B * (h + 1))
            att = jnp.where(tri, _dot_nt(qb[:, ks], kb[:, ks]), 0.0).astype(BF16)
            o_parts.append(_dot(att, vb[:, vs]) + _dot_nt(qb[:, ks], st_bf[:, ks]))
            s_parts.append(_dot_tn(vb[:, vs], kl[:, ks]))
            yield
        o = jnp.concatenate(o_parts, axis=1)
        out["gla"] = jnp.exp(blast) * st + jnp.concatenate(s_parts, axis=1)
        msq = _dot01_right(o * o, blk64_bf) * (1.0 / DV_B)
        yield
        go = o * lax.rsqrt(msq + EPS) * gnw * _silu(inp["bog"])
        out["gla_o"] = go.astype(gla_o_ref.dtype)
        yield

        cx = inp["cx"]
        xpad_ref[slot, 0:8, :] = inp["conv"]
        xpad_ref[slot, 8:8 + CHUNK, :] = cx
        conv = cb + cx * cw_ref[CONV_W - 1:CONV_W, :]
        for i in range(CONV_W - 1):
            off = 8 - (CONV_W - 1) + i
            conv = conv + xpad_ref[slot, off:off + CHUNK, :] * cw_ref[i:i + 1, :]
        out["conv"] = cx[CHUNK - 8:, :]
        conv = _silu(conv)
        sx = conv[:, :WIDTH_C]
        sb_bf = conv[:, WIDTH_C:WIDTH_C + N_GROUPS_C * D_STATE].astype(BF16)
        sc_bf = conv[:, WIDTH_C + N_GROUPS_C * D_STATE:].astype(BF16)
        yield

        dtv = _softplus(misc + dtb)
        dt_x = _dot01_right(dtv, expand_bf)
        yield
        a_x = neg_a * dt_x
        cs_x = _dot01_left(tri_bf, a_x)
        cs_row = _dot01_left(ones_bf, jnp.where(upper_x, a_x, 0.0))
        yield
        decay = jnp.exp(jnp.where(causal_x, cs_x - cs_row, -jnp.inf))
        cb_parts = []
        for gidx in range(N_GROUPS_C):
            ns = slice(D_STATE * gidx, D_STATE * (gidx + 1))
            cbm = _dot_nt(sc_bf[:, ns], sb_bf[:, ns])
            cb_parts += [cbm] * (H_C // N_GROUPS_C)
            yield
        scores = (jnp.concatenate(cb_parts, axis=1) * decay).astype(BF16)
        xdt = (sx * dt_x).astype(BF16)
        ecs = jnp.exp(cs_x)
        cs_last = cs_x[CHUNK - 1:CHUNK, :]
        xw = (sx * (jnp.exp(cs_last - cs_x) * dt_x)).astype(BF16)
        e_last = jnp.exp(cs_last)
        yield
        y_parts = []
        out["ssm"] = []
        for h in range(H_C):
            ps = slice(P_C * h, P_C * (h + 1))
            gi = h // (H_C // N_GROUPS_C)
            ns = slice(D_STATE * gi, D_STATE * (gi + 1))
            hst = inp["ssm"][h]
            y_parts.append(_dot(scores[:, ps], xdt[:, ps])
                           + _dot_nt(sc_bf[:, ns], hst.astype(BF16)) * ecs[:, ps])
            out["ssm"].append(e_last[:, ps] * hst + _dot_tn(xw[:, ps], sb_bf[:, ns]))
            yield
        y = jnp.concatenate(y_parts, axis=1) + dsk * sx
        y = y * _silu(inp["cz"])
        ymsq = _dot01_right(y * y, blk128_bf) * (1.0 / (WIDTH_C // N_GROUPS_C))
        yield
        out["ssm_o"] = (y * lax.rsqrt(ymsq + EPS) * snw).astype(ssm_o_ref.dtype)
        return out

    def group_body(i, carry):
        seqs = [i * SEQ_UNROLL + u for u in range(SEQ_UNROLL)]
        inputs = [load_seq(b) for b in seqs]
        outputs = _round_robin([compute_seq(inp, slot) for slot, inp in enumerate(inputs)])
        for b, out in zip(seqs, outputs):
            store_seq(b, out)
        return carry

    lax.fori_loop(0, nseq // SEQ_UNROLL, group_body, 0)


def _scan(proj, lw, gla0, conv0, ssm0):
    bsz, l, _ = proj.shape
    nc = l // CHUNK
    sg = min(SEQ_GROUP, bsz)
    ng = bsz // sg

    def col(width, idx):
        return pl.BlockSpec((sg, CHUNK, width), lambda g, c: (g, c, idx))

    def const(shape):
        return pl.BlockSpec(shape, lambda g, c: (0,) * len(shape))

    def state(shape):
        return pl.BlockSpec((sg,) + shape, lambda g, c: (g,) + (0,) * len(shape))

    gla_shape = (DV_B, H_B * DK_B)
    conv_shape = (8, CONV_CH)
    ssm_shape = (H_C, P_C, D_STATE)
    return pl.pallas_call(
        _scan_kernel,
        grid=(ng, nc),
        in_specs=[col(256, 0), col(256, 1), col(256, 2), col(256, 3), col(512, 2), col(128, 12),
                  const((128, 128)), const((1, 128)), const((1, WIDTH_B)),
                  const((CONV_W, CONV_CH)), const((1, CONV_CH)),
                  const((1, 128)), const((1, WIDTH_C)), const((1, WIDTH_C)), const((1, WIDTH_C)),
                  state(gla_shape), state(conv_shape), state(ssm_shape)],
        out_specs=[pl.BlockSpec((sg, CHUNK, WIDTH_B), lambda g, c: (g, c, 0)),
                   pl.BlockSpec((sg, CHUNK, WIDTH_C), lambda g, c: (g, c, 0)),
                   state(gla_shape), state(conv_shape), state(ssm_shape)],
        out_shape=[jax.ShapeDtypeStruct((bsz, l, WIDTH_B), BF16),
                   jax.ShapeDtypeStruct((bsz, l, WIDTH_C), BF16),
                   jax.ShapeDtypeStruct((bsz,) + gla_shape, F32),
                   jax.ShapeDtypeStruct((bsz,) + conv_shape, F32),
                   jax.ShapeDtypeStruct((bsz,) + ssm_shape, F32)],
        scratch_shapes=[pltpu.VMEM((SEQ_UNROLL, 8 + CHUNK, CONV_CH), F32)],
        compiler_params=_params("parallel", "arbitrary"),
    )(proj, proj, proj, proj, proj, proj,
      lw["gup"], lw["gb"], lw["gnw"], lw["cw"], lw["cb"], lw["dtb"], lw["alog"], lw["dsk"], lw["snw"],
      gla0, conv0, ssm0)


OUT_TM = 256


def _out_proj_kernel(x_ref, att_ref, gla_ref, ssm_ref, w_out_ref, nw_ref, rwh_ref, rwl_ref, rb_ref,
                     x1_ref, hn_ref, gate_ref, wo_ref):
    @pl.when(pl.program_id(0) == 0)
    def _():
        for r in range(0, D_MODEL, PREP_ROWS):
            wo_ref[r:r + PREP_ROWS, :] = w_out_ref[0, r:r + PREP_ROWS, :].astype(BF16)

    mo = (_dot(att_ref[...], wo_ref[0:WIDTH_A, :])
          + _dot(gla_ref[...], wo_ref[WIDTH_A:WIDTH_A + WIDTH_B, :])
          + _dot(ssm_ref[...], wo_ref[WIDTH_A + WIDTH_B:, :]))
    x1 = x_ref[...] + mo
    x1_ref[...] = x1
    ms = jnp.mean(x1 * x1, axis=-1, keepdims=True)
    hn = x1 * lax.rsqrt(ms + EPS) * nw_ref[...]
    hn_ref[...] = hn.astype(hn_ref.dtype)

    h_hi = hn.astype(BF16)
    h_lo = (hn - h_hi.astype(F32)).astype(BF16)
    logits = (_dot(h_hi, rwh_ref[...]) + _dot(h_lo, rwh_ref[...]) + _dot(h_hi, rwl_ref[...])
              + rb_ref[...])
    lane = lax.broadcasted_iota(jnp.int32, logits.shape, 1)
    lane_f = lane.astype(F32)
    neg = -jnp.inf
    big = 1000.0
    gmask = (lane >= ROUTER_GROUP_LANE) & (lane < ROUTER_GROUP_LANE + N_EXPERT_GROUPS)
    lg = jnp.where(gmask, logits, neg)
    gmax = lg.max(axis=-1, keepdims=True)
    g_idx = jnp.where(lg == gmax, lane_f, big).min(axis=-1, keepdims=True) - ROUTER_GROUP_LANE
    p_grp = 1.0 / jnp.where(gmask, jnp.exp(logits - gmax), 0.0).sum(axis=-1, keepdims=True)
    emask = (lane < N_EXPERTS) & ((lane >> 2).astype(F32) == g_idx)
    le = jnp.where(emask, logits, neg)
    v1 = le.max(axis=-1, keepdims=True)
    i1 = jnp.where(le == v1, lane_f, big).min(axis=-1, keepdims=True)
    le2 = jnp.where(lane_f == i1, neg, le)
    v2 = le2.max(axis=-1, keepdims=True)
    i2 = jnp.where(le2 == v2, lane_f, big).min(axis=-1, keepdims=True)
    e2 = jnp.exp(v2 - v1)
    den = 1.0 + e2
    gate_ref[...] = (jnp.where(lane_f == i1, (1.0 / den) * p_grp, 0.0)
                     + jnp.where(lane_f == i2, (e2 / den) * p_grp, 0.0)
                     + jnp.where(lane == ROUTER_GIDX_LANE, g_idx, 0.0))


def _out_proj(x, att, gla, ssm, lw, layer):
    t = x.shape[0]

    def rows(width):
        return pl.BlockSpec((OUT_TM, width), lambda i: (i, 0))

    def const(shape):
        return pl.BlockSpec(shape, lambda i: (0, 0))

    return pl.pallas_call(
        _out_proj_kernel,
        grid=(t // OUT_TM,),
        in_specs=[rows(D_MODEL), rows(WIDTH_A), rows(WIDTH_B), rows(WIDTH_C),
                  pl.BlockSpec((1, D_MODEL, D_MODEL), lambda i: (layer, 0, 0)), const((1, D_MODEL)),
                  const((D_MODEL, 128)), const((D_MODEL, 128)), const((1, 128))],
        out_specs=[rows(D_MODEL), rows(D_MODEL), rows(128)],
        out_shape=[jax.ShapeDtypeStruct((t, D_MODEL), F32),
                   jax.ShapeDtypeStruct((t, D_MODEL), BF16),
                   jax.ShapeDtypeStruct((t, 128), F32)],
        scratch_shapes=[pltpu.VMEM((D_MODEL, D_MODEL), BF16)],
        compiler_params=_params("arbitrary"),
    )(x, att, gla, ssm, lw["wo"], lw["n2w"], lw["rw_hi"], lw["rw_lo"], lw["rb"])


RT_TS = 256
RT_ALIGN = 16
RT_ROWS = 384
RT_SIZES = (256, 128, 64, 32, 16)
PAY_COLS = D_MODEL + 256
MOE_TM = 512


def _moe_max_tiles(t):
    return (t + (t // RT_TS) * N_EXPERT_GROUPS * (RT_ALIGN - 1)) // MOE_TM + N_EXPERT_GROUPS


def _sort_matrix(gate, c_ref, tile):
    lane = lax.broadcasted_iota(jnp.int32, (RT_TS, 128), 1)
    gidx = jnp.where(lane == ROUTER_GIDX_LANE, gate, 0.0).sum(axis=1, keepdims=True)
    onehot = lane.astype(F32) == gidx
    r = lax.broadcasted_iota(jnp.int32, (RT_TS, RT_TS), 0)
    c = lax.broadcasted_iota(jnp.int32, (RT_TS, RT_TS), 1)
    earlier = jnp.where(c < r, 1.0, 0.0).astype(BF16)
    before = _dot(earlier, jnp.where(onehot, 1.0, 0.0).astype(BF16))
    start = jnp.zeros((RT_TS, 128), F32)
    for g in range(N_EXPERT_GROUPS):
        start = jnp.where(lane == g, c_ref[N_EXPERT_GROUPS * tile + g].astype(F32), start)
    dest = jnp.where(onehot, before + start, 0.0).sum(axis=1, keepdims=True)
    rows = lax.broadcasted_iota(jnp.int32, (RT_TS, RT_ROWS), 1).astype(F32)
    return jnp.where(rows == dest, 1.0, 0.0).astype(BF16)


def _piece_copies(length, local, glob, make_copy, act):
    for size in RT_SIZES:
        take = (length & size) != 0

        @pl.when(take)
        def _(local=local, glob=glob, size=size):
            act(make_copy(pl.multiple_of(local, RT_ALIGN), pl.multiple_of(glob, RT_ALIGN), size))

        step = jnp.where(take, size, 0)
        local = local + step
        glob = glob + step


def _run_copies(c_ref, lp_ref, off_ref, tile, make_copy, act):
    for g in range(N_EXPERT_GROUPS):
        k = N_EXPERT_GROUPS * tile + g
        _piece_copies(lp_ref[k], c_ref[k], off_ref[k], make_copy, act)


def _dispatch_kernel(c_ref, lp_ref, off_ref, tail_ref, nact_ref, hn_ref, gate_ref, xs_hbm,
                     y_ref, zero_ref, sem, zsem, *, first_spare, max_tiles):
    i = pl.program_id(0)
    nt = pl.num_programs(0)
    slot = i % 2
    gate = gate_ref[...]
    pt = _sort_matrix(gate, c_ref, i)
    g_hi = gate.astype(BF16)
    g_lo = (gate - g_hi.astype(F32)).astype(BF16)
    y_ref[slot, :, 0:D_MODEL] = _dot_tn(pt, hn_ref[...]).astype(BF16)
    y_ref[slot, :, D_MODEL:D_MODEL + 128] = _dot_tn(pt, g_hi).astype(BF16)
    y_ref[slot, :, D_MODEL + 128:] = _dot_tn(pt, g_lo).astype(BF16)

    def copy_from(s):
        return lambda local, glob, size: pltpu.make_async_copy(
            y_ref.at[s, pl.ds(local, size)], xs_hbm.at[pl.ds(glob, size)], sem.at[s])

    _run_copies(c_ref, lp_ref, off_ref, i, copy_from(slot), lambda cp: cp.start())

    @pl.when(i > 0)
    def _():
        _run_copies(c_ref, lp_ref, off_ref, i - 1, copy_from(1 - slot), lambda cp: cp.wait())

    @pl.when(i == nt - 1)
    def _():
        _run_copies(c_ref, lp_ref, off_ref, i, copy_from(slot), lambda cp: cp.wait())
        zero_ref[...] = jnp.zeros_like(zero_ref)

        def zero_copy(local, glob, size):
            return pltpu.make_async_copy(zero_ref.at[pl.ds(local, size)],
                                         xs_hbm.at[pl.ds(glob, size)], zsem)

        def zero_fill(act):
            for g in range(N_EXPERT_GROUPS):
                _piece_copies(tail_ref[N_EXPERT_GROUPS + g], 0, tail_ref[g], zero_copy, act)
            for j in range(first_spare, max_tiles):
                @pl.when(j >= nact_ref[0])
                def _(j=j):
                    act(zero_copy(0, j * MOE_TM, MOE_TM))

        zero_fill(lambda cp: cp.start())
        zero_fill(lambda cp: cp.wait())


def _moe_group_kernel(grp_ref, nact_ref, xs_ref, wg_ref, wu_ref, wd_ref, o_ref, wg_bf, wu_bf, wd_bf):
    j = pl.program_id(0)

    @pl.when(jnp.logical_or(j == 0, grp_ref[j] != grp_ref[jnp.maximum(j - 1, 0)]))
    def _():
        wg_bf[...] = wg_ref[0].astype(BF16)
        wu_bf[...] = wu_ref[0].astype(BF16)
        wd_bf[...] = wd_ref[0].astype(BF16)

    @pl.when(j >= nact_ref[0])
    def _():
        o_ref[...] = jnp.zeros_like(o_ref)

    @pl.when(j < nact_ref[0])
    def _():
        x = xs_ref[:, 0:D_MODEL]
        gate = (xs_ref[:, D_MODEL:D_MODEL + 128].astype(F32)
                + xs_ref[:, D_MODEL + 128:].astype(F32))
        lane = lax.broadcasted_iota(jnp.int32, gate.shape, 1)
        first = grp_ref[j] * EXPERTS_PER_GROUP
        acc = None
        for e in range(EXPERTS_PER_GROUP):
            ge = jnp.where(lane == first + e, gate, 0.0).sum(axis=-1, keepdims=True)
            hid = _silu(_dot(x, wg_bf[e])) * _dot(x, wu_bf[e]) * ge
            part = _dot(hid.astype(BF16), wd_bf[e])
            acc = part if acc is None else acc + part
        o_ref[...] = acc


def _combine_kernel(c_ref, lp_ref, off_ref, x_ref, gate_ref, fw_ref, ys_hbm, o_ref, z_ref, sem,
                    *, final_norm):
    i = pl.program_id(0)
    nt = pl.num_programs(0)
    slot = i % 2

    def copy_into(s):
        return lambda local, glob, size: pltpu.make_async_copy(
            ys_hbm.at[pl.ds(glob, size)], z_ref.at[s, pl.ds(local, size)], sem.at[s])

    @pl.when(i == 0)
    def _():
        z_ref[...] = jnp.zeros_like(z_ref)
        _run_copies(c_ref, lp_ref, off_ref, 0, copy_into(0), lambda cp: cp.start())

    @pl.when(i + 1 < nt)
    def _():
        _run_copies(c_ref, lp_ref, off_ref, i + 1, copy_into(1 - slot), lambda cp: cp.start())

    _run_copies(c_ref, lp_ref, off_ref, i, copy_into(slot), lambda cp: cp.wait())
    pt = _sort_matrix(gate_ref[...], c_ref, i)
    z = z_ref[slot]
    z_hi = z.astype(BF16)
    z_lo = (z - z_hi.astype(F32)).astype(BF16)
    y = x_ref[...] + _dot(pt, z_hi) + _dot(pt, z_lo)
    if final_norm:
        ms = jnp.mean(y * y, axis=-1, keepdims=True)
        y = y * lax.rsqrt(ms + EPS) * fw_ref[...]
    o_ref[...] = y


def _route_meta(gate, max_tiles):
    t = gate.shape[0]
    nt = t // RT_TS
    groups = jnp.arange(N_EXPERT_GROUPS, dtype=jnp.int32)
    g = gate[:, ROUTER_GIDX_LANE].astype(jnp.int32).reshape(nt, RT_TS, 1)
    count = (g == groups).astype(jnp.int32).sum(axis=1)
    lp = (count + RT_ALIGN - 1) // RT_ALIGN * RT_ALIGN
    total = lp.sum(axis=0)
    ntile = (total + MOE_TM - 1) // MOE_TM
    last = jnp.cumsum(ntile)
    first_row = (last - ntile) * MOE_TM
    c = jnp.cumsum(lp, axis=1) - lp
    off = first_row[None, :] + jnp.cumsum(lp, axis=0) - lp
    tail = jnp.concatenate([first_row + total, ntile * MOE_TM - total])
    j = jnp.arange(max_tiles, dtype=jnp.int32)[:, None]
    grp = jnp.minimum((j >= last[None, :]).astype(jnp.int32).sum(axis=1), N_EXPERT_GROUPS - 1)
    return c.reshape(-1), lp.reshape(-1), off.reshape(-1), tail, grp, last[-1:]


def _moe(x, hn, gate, lw, layer, fw, final_norm):
    t = x.shape[0]
    nt = t // RT_TS
    max_tiles = _moe_max_tiles(t)
    c, lp, off, tail, grp, nact = _route_meta(gate, max_tiles)

    def rows(width):
        return pl.BlockSpec((RT_TS, width), lambda i, *_: (i, 0))

    xs = pl.pallas_call(
        functools.partial(_dispatch_kernel, first_spare=t // MOE_TM, max_tiles=max_tiles),
        grid_spec=pltpu.PrefetchScalarGridSpec(
            num_scalar_prefetch=5, grid=(nt,),
            in_specs=[rows(D_MODEL), rows(128)],
            out_specs=pl.BlockSpec(memory_space=pl.ANY),
            scratch_shapes=[pltpu.VMEM((2, RT_ROWS, PAY_COLS), BF16),
                            pltpu.VMEM((MOE_TM, PAY_COLS), BF16),
                            pltpu.SemaphoreType.DMA((2,)), pltpu.SemaphoreType.DMA(())]),
        out_shape=jax.ShapeDtypeStruct((max_tiles * MOE_TM, PAY_COLS), BF16),
        compiler_params=_params("arbitrary"),
    )(c, lp, off, tail, nact, hn, gate)

    def experts(shape):
        return pl.BlockSpec((1, EXPERTS_PER_GROUP) + shape, lambda j, grp, nact: (layer, grp[j], 0, 0))

    def rounded(shape):
        return pltpu.VMEM((EXPERTS_PER_GROUP,) + shape, BF16)

    ys = pl.pallas_call(
        _moe_group_kernel,
        grid_spec=pltpu.PrefetchScalarGridSpec(
            num_scalar_prefetch=2, grid=(max_tiles,),
            in_specs=[pl.BlockSpec((MOE_TM, PAY_COLS), lambda j, grp, nact: (j, 0)),
                      experts((D_MODEL, D_EXPERT)), experts((D_MODEL, D_EXPERT)),
                      experts((D_EXPERT, D_MODEL))],
            out_specs=pl.BlockSpec((MOE_TM, D_MODEL), lambda j, grp, nact: (j, 0)),
            scratch_shapes=[rounded((D_MODEL, D_EXPERT)), rounded((D_MODEL, D_EXPERT)),
                            rounded((D_EXPERT, D_MODEL))]),
        out_shape=jax.ShapeDtypeStruct((max_tiles * MOE_TM, D_MODEL), F32),
        compiler_params=_params("arbitrary"),
    )(grp, nact, xs, lw["wg"], lw["wu"], lw["wd"])

    return pl.pallas_call(
        functools.partial(_combine_kernel, final_norm=final_norm),
        grid_spec=pltpu.PrefetchScalarGridSpec(
            num_scalar_prefetch=3, grid=(nt,),
            in_specs=[rows(D_MODEL), rows(128), pl.BlockSpec((1, D_MODEL), lambda i, *_: (0, 0)),
                      pl.BlockSpec(memory_space=pl.ANY)],
            out_specs=rows(D_MODEL),
            scratch_shapes=[pltpu.VMEM((2, RT_ROWS, D_MODEL), F32),
                            pltpu.SemaphoreType.DMA((2,))]),
        out_shape=jax.ShapeDtypeStruct((t, D_MODEL), F32),
        compiler_params=_params("arbitrary"),
    )(c, lp, off, x, gate, fw, ys)


def _lane_place(vec, start, width=128):
    return jnp.zeros((1, width), F32).at[0, start:start + vec.shape[0]].set(vec.astype(F32))


REL_PAD = 704


def _toeplitz(v, rows, cols):
    n = v.shape[-1]
    tiled = jnp.tile(v, (1,) * (v.ndim - 1) + (rows,))[..., :rows * (n - 1)]
    return tiled.reshape(v.shape[:-1] + (rows, n - 1))[..., :cols]


def _band_bias(table):
    t = table.astype(F32)
    ext = jnp.concatenate([jnp.repeat(t[:, :1], REL_PAD, axis=1), t,
                           jnp.repeat(t[:, -1:], REL_PAD, axis=1)], axis=1)
    extr = ext[:, ::-1]
    top = ext.shape[1] - 1 - (REL_CLIP + REL_PAD)

    def band(off, nq, nk):
        n = nq + nk
        v = jnp.concatenate([extr[:, top - off:top - off + nk + 1],
                             extr[:, top - off - (nq - 1):top - off]], axis=1)
        assert v.shape[1] == n
        return _toeplitz(v, nq, nk)

    sample = band(WINDOW, CHUNK, BAND)
    kc = (jnp.arange(ATT_KB) // CHUNK)[:, None]
    qc = (jnp.arange(ATT_QB) // CHUNK)[None, :]
    block = lambda off: jnp.swapaxes(band(off, ATT_QB, ATT_KB), 1, 2)
    prompt = jnp.stack([jnp.where((kc >= qc)[None], block(2 * ATT_KB), -1e30),
                        block(ATT_KB),
                        jnp.where((kc <= qc)[None], block(0), -1e30),
                        jnp.full((H_A, ATT_KB, ATT_QB), -1e30, F32)])
    return sample * LOG2_E, prompt * LOG2_E


def _layer_weights(i, norm1_w, w_in, rel_bias_table, gla_w_gate_up, gla_b_gate, gla_norm_w,
                   ssm_conv_w, ssm_conv_b, ssm_dt_bias, ssm_a_log, ssm_d, ssm_norm_w, w_out,
                   norm2_w, router_group_w, router_group_b, router_expert_w, router_expert_b,
                   exp_w_gate, exp_w_up, exp_w_down):
    bias_s, bias_p = _band_bias(rel_bias_table[i])
    rw = jnp.concatenate([router_expert_w[i], router_group_w[i],
                          jnp.zeros((D_MODEL, 128 - N_EXPERTS - N_EXPERT_GROUPS), F32)], axis=1)
    rw_hi = rw.astype(BF16)
    return dict(
        n1w=norm1_w[i][None, :],
        w_in=w_in,
        bias_s=bias_s,
        bias_p=bias_p,
        gup=jnp.zeros((128, 128), F32).at[:GATE_RANK, :].set(gla_w_gate_up[i]).astype(BF16),
        gb=gla_b_gate[i][None, :].astype(F32),
        gnw=jnp.tile(gla_norm_w[i], H_B)[None, :].astype(F32),
        cw=ssm_conv_w[i].astype(F32),
        cb=ssm_conv_b[i][None, :].astype(F32),
        dtb=_lane_place(ssm_dt_bias[i], MISC_DT_LANE),
        alog=jnp.repeat(ssm_a_log[i].astype(F32), P_C)[None, :],
        dsk=jnp.repeat(ssm_d[i].astype(F32), P_C)[None, :],
        snw=ssm_norm_w[i][None, :].astype(F32),
        wo=w_out,
        n2w=norm2_w[i][None, :],
        rw_hi=rw_hi,
        rw_lo=(rw - rw_hi.astype(F32)).astype(BF16),
        rb=jnp.concatenate([router_expert_b[i], router_group_b[i],
                            jnp.zeros((128 - N_EXPERTS - N_EXPERT_GROUPS,), F32)])[None, :],
        wg=exp_w_gate,
        wu=exp_w_up,
        wd=exp_w_down,
    )


def _stream_layer(x, bsz, lw, layer, cache_k, cache_v, gla0, conv0, ssm0, fw, final_norm):
    l = x.shape[0] // bsz
    if cache_k is None:
        qk, rest, k_keep, v_keep, vt = _in_proj(x, lw["n1w"], lw["w_in"], layer, True, l)
        att = _attn_prompt(qk.reshape(bsz, l, 2 * WIDTH_A),
                           vt.reshape(bsz, l // ATT_KB, WIDTH_A, ATT_KB), lw["bias_p"])
    else:
        qk, rest, k_keep, v_keep, kv = _in_proj(x, lw["n1w"], lw["w_in"], layer, False, l)
        att = _attn_sample(qk.reshape(bsz, l, 2 * WIDTH_A), kv.reshape(bsz, l, 2 * WIDTH_A),
                           cache_k, cache_v, layer, lw["bias_s"])
    gla_o, ssm_o, gla_s, conv_s, ssm_s = _scan(rest.reshape(bsz, l, REST_COLS), lw, gla0, conv0, ssm0)
    t = bsz * l
    x1, hn, gate = _out_proj(x, att.reshape(t, WIDTH_A), gla_o.reshape(t, WIDTH_B),
                             ssm_o.reshape(t, WIDTH_C), lw, layer)
    x2 = _moe(x1, hn, gate, lw, layer, fw, final_norm)
    keep = min(WINDOW, l)
    k_keep = k_keep.reshape(bsz, keep, H_A, DH_A)
    v_keep = v_keep.reshape(bsz, keep, H_A, DH_A)
    gla_state = gla_s.reshape(bsz, DV_B, H_B, DK_B).transpose(0, 2, 3, 1)
    conv_state = conv_s[:, 8 - (CONV_W - 1):, :]
    return x2, k_keep, v_keep, gla_state, conv_state, ssm_s


def kernel(x_prompt, x_sample, cache_k_a, cache_v_a, state_gla, state_conv, state_ssm, norm1_w, w_in, rel_bias_table, gla_w_gate_up, gla_b_gate, gla_norm_w, ssm_conv_w, ssm_conv_b, ssm_dt_bias, ssm_a_log, ssm_d, ssm_norm_w, w_out, norm2_w, router_group_w, router_group_b, router_expert_w, router_expert_b, exp_w_gate, exp_w_up, exp_w_down, final_norm_w):
    bp, sp, _ = x_prompt.shape
    bs, ss, _ = x_sample.shape
    xp = x_prompt.reshape(bp * sp, D_MODEL)
    xs = x_sample.reshape(bs * ss, D_MODEL)
    fw = final_norm_w[None, :].astype(F32)
    outs_p, outs_s = [], []
    cache_k = cache_k_a.reshape(DEPTH, bs, -1, WIDTH_A).astype(BF16)
    cache_v = cache_v_a.reshape(DEPTH, bs, -1, WIDTH_A).astype(BF16)
    for i in range(DEPTH):
        lw = _layer_weights(i, norm1_w, w_in, rel_bias_table, gla_w_gate_up, gla_b_gate, gla_norm_w,
                            ssm_conv_w, ssm_conv_b, ssm_dt_bias, ssm_a_log, ssm_d, ssm_norm_w, w_out,
                            norm2_w, router_group_w, router_group_b, router_expert_w, router_expert_b,
                            exp_w_gate, exp_w_up, exp_w_down)
        last = i == DEPTH - 1
        xp, *sp_out = _stream_layer(
            xp, bp, lw, i, None, None,
            jnp.zeros((bp, DV_B, H_B * DK_B), F32),
            jnp.zeros((bp, 8, CONV_CH), F32),
            jnp.zeros((bp, H_C, P_C, D_STATE), F32), fw, last)
        outs_p.append(sp_out)
        gla0 = state_gla[i].astype(F32).transpose(0, 3, 1, 2).reshape(bs, DV_B, H_B * DK_B)
        conv0 = jnp.pad(state_conv[i].astype(F32), ((0, 0), (8 - (CONV_W - 1), 0), (0, 0)))
        xs, *ss_out = _stream_layer(
            xs, bs, lw, i, cache_k, cache_v,
            gla0, conv0, state_ssm[i].astype(F32), fw, last)
        outs_s.append(ss_out)
    stack = lambda outs, j: jnp.stack([o[j] for o in outs])
    return (xp.reshape(bp, sp, D_MODEL), xs.reshape(bs, ss, D_MODEL),
            stack(outs_p, 0), stack(outs_p, 1), stack(outs_p, 2), stack(outs_p, 3), stack(outs_p, 4),
            stack(outs_s, 0), stack(outs_s, 1), stack(outs_s, 2), stack(outs_s, 3), stack(outs_s, 4))
```

```python
import functools

import jax
import jax.numpy as jnp
from jax import lax
from jax.experimental import pallas as pl
from jax.experimental.pallas import tpu as pltpu

F32 = jnp.float32
BF16 = jnp.bfloat16

D_MODEL = 1024
DEPTH = 2
EPS = 1e-6
CHUNK = 64
N_PAST_CHUNKS = 8
BAND = (N_PAST_CHUNKS + 1) * CHUNK
WINDOW = N_PAST_CHUNKS * CHUNK
H_A, DH_A, WIDTH_A = 8, 64, 512
REL_CLIP = 128
LOG2_E = 1.4426950408889634
H_B, DK_B, DV_B, WIDTH_B = 4, 32, 64, 256
GATE_RANK = 16
GATE_TAU = 16.0
H_C, P_C, WIDTH_C = 4, 64, 256
N_GROUPS_C = 2
D_STATE = 64
CONV_W = 4
CONV_CH = 512
N_EXPERT_GROUPS = 4
EXPERTS_PER_GROUP = 4
N_EXPERTS = 16
D_EXPERT = 256

PROJ_COLS = 3200
MISC_DT_LANE = GATE_RANK
ROUTER_GROUP_LANE = N_EXPERTS
ROUTER_GIDX_LANE = 20

SEQ_GROUP = 8
SEQ_UNROLL = 8
VMEM_LIMIT_BYTES = 56 * 1024 * 1024

NT_DIMS = (((1,), (1,)), ((), ()))
TN_DIMS = (((0,), (0,)), ((), ()))


def _params(*sem):
    return pltpu.CompilerParams(dimension_semantics=sem, vmem_limit_bytes=VMEM_LIMIT_BYTES)


def _dot(a, b):
    return jnp.dot(a, b, preferred_element_type=F32)


def _dot_nt(a, b):
    return lax.dot_general(a, b, NT_DIMS, preferred_element_type=F32)


def _dot_tn(a, b):
    return lax.dot_general(a, b, TN_DIMS, preferred_element_type=F32)


def _split2(a):
    hi = a.astype(BF16)
    return hi, (a - hi.astype(F32)).astype(BF16)


def _dot01_left(m01, a):
    hi, lo = _split2(a)
    return _dot(m01, hi) + _dot(m01, lo)


def _dot01_right(a, m01):
    hi, lo = _split2(a)
    return _dot(hi, m01) + _dot(lo, m01)


def _round_robin(gens):
    results = [None] * len(gens)
    live = list(range(len(gens)))
    while live:
        for idx in list(live):
            try:
                next(gens[idx])
            except StopIteration as stop:
                results[idx] = stop.value
                live.remove(idx)
    return results


def _silu(x):
    return x * jax.nn.sigmoid(x)


def _softplus(x):
    return jnp.maximum(x, 0.0) + jnp.log1p(jnp.exp(-jnp.abs(x)))


IN_TM = 256
QKV_COLS = 3 * WIDTH_A
REST_COLS = PROJ_COLS - QKV_COLS
REST_CHUNKS = ((0, 512), (512, 1024), (1024, REST_COLS))


IN_COLS = 3092
IN_TAIL = QKV_COLS + 512
PREP_ROWS = 128


def _prepare_w_in(w_in_ref, w_ref):
    scale = DH_A ** -0.5 * LOG2_E
    n_tail = REST_COLS - 512 - 128
    for r in range(0, D_MODEL, PREP_ROWS):
        rs = slice(r, r + PREP_ROWS)
        w_ref[rs, 0:WIDTH_A] = (w_in_ref[0, rs, 0:WIDTH_A] * scale).astype(BF16)
        w_ref[rs, WIDTH_A:IN_TAIL] = w_in_ref[0, rs, WIDTH_A:IN_TAIL].astype(BF16)
        tail = w_in_ref[0, rs, IN_TAIL:IN_COLS]
        w_ref[rs, IN_TAIL:IN_TAIL + n_tail] = tail[:, GATE_RANK:GATE_RANK + n_tail].astype(BF16)
        misc = jnp.concatenate([tail[:, 0:GATE_RANK], tail[:, GATE_RANK + n_tail:],
                                jnp.zeros((PREP_ROWS, 128 - GATE_RANK - H_C), F32)], axis=1)
        w_ref[rs, IN_TAIL + n_tail:] = misc.astype(BF16)


def _in_proj_kernel(x_ref, nw_ref, w_in_ref, *refs, with_vt):
    if with_vt:
        qk_ref, rest_ref, kk_ref, vk_ref, vt_ref, w_ref = refs
    else:
        qk_ref, rest_ref, kk_ref, vk_ref, kv_ref, w_ref = refs

    @pl.when(pl.program_id(0) == 0)
    def _():
        _prepare_w_in(w_in_ref, w_ref)

    x = x_ref[...]
    ms = jnp.mean(x * x, axis=-1, keepdims=True)
    xn = (x * lax.rsqrt(ms + EPS) * nw_ref[...]).astype(BF16)
    qk_ref[:, 0:WIDTH_A] = _dot(xn, w_ref[:, 0:WIDTH_A]).astype(BF16)
    k = _dot(xn, w_ref[:, WIDTH_A:2 * WIDTH_A])
    v = _dot(xn, w_ref[:, 2 * WIDTH_A:QKV_COLS])
    qk_ref[:, WIDTH_A:] = k.astype(BF16)
    if not with_vt:
        kv_ref[:, 0:WIDTH_A] = k
        kv_ref[:, WIDTH_A:] = v

    def keep_rows():
        kk = kk_ref.reshape(IN_TM * H_A, DH_A)
        vk = vk_ref.reshape(IN_TM * H_A, DH_A)
        for h in range(H_A):
            sl = slice(DH_A * h, DH_A * (h + 1))
            kk[pl.ds(h, IN_TM, stride=H_A), :] = k[:, sl]
            vk[pl.ds(h, IN_TM, stride=H_A), :] = v[:, sl]

    keep_rows()
    for lo, hi in REST_CHUNKS:
        rest_ref[:, lo:hi] = _dot(xn, w_ref[:, QKV_COLS + lo:QKV_COLS + hi])
    if with_vt:
        vt_ref[0] = v.T.astype(BF16)


def _in_proj(x, nw, w_in, layer, with_vt, seq_len):
    t = x.shape[0]
    tiles_per_seq = max(seq_len // IN_TM, 1)
    keep_tiles = max(min(WINDOW, seq_len) // IN_TM, 1)
    n_keep = t // IN_TM // tiles_per_seq * keep_tiles

    def rows(width):
        return pl.BlockSpec((IN_TM, width), lambda i: (i, 0))

    def kept(i):
        return (i // tiles_per_seq * keep_tiles
                + jnp.maximum(i % tiles_per_seq - (tiles_per_seq - keep_tiles), 0), 0, 0)

    keep_spec = pl.BlockSpec((IN_TM, H_A, DH_A), kept)
    keep_shape = jax.ShapeDtypeStruct((n_keep * IN_TM, H_A, DH_A), F32)
    in_specs = [rows(D_MODEL), pl.BlockSpec((1, D_MODEL), lambda i: (0, 0)),
                pl.BlockSpec((1, D_MODEL, IN_COLS), lambda i: (layer, 0, 0))]
    scratch = [pltpu.VMEM((D_MODEL, PROJ_COLS), BF16)]
    out_specs = [rows(2 * WIDTH_A), rows(REST_COLS), keep_spec, keep_spec]
    out_shape = [jax.ShapeDtypeStruct((t, 2 * WIDTH_A), BF16),
                 jax.ShapeDtypeStruct((t, REST_COLS), F32), keep_shape, keep_shape]
    if with_vt:
        out_specs.append(pl.BlockSpec((1, WIDTH_A, IN_TM), lambda i: (i, 0, 0)))
        out_shape.append(jax.ShapeDtypeStruct((t // IN_TM, WIDTH_A, IN_TM), BF16))
    else:
        out_specs.append(rows(2 * WIDTH_A))
        out_shape.append(jax.ShapeDtypeStruct((t, 2 * WIDTH_A), F32))
    return pl.pallas_call(
        functools.partial(_in_proj_kernel, with_vt=with_vt),
        grid=(t // IN_TM,),
        in_specs=in_specs, out_specs=out_specs, out_shape=out_shape, scratch_shapes=scratch,
        compiler_params=_params("arbitrary"),
    )(x, nw, w_in)


ATT_QB = 4 * CHUNK
ATT_KB = IN_TM
ATT_NKB = (ATT_QB + WINDOW) // ATT_KB


def _attn_prompt_kernel(q_ref, k_ref, vt_ref, bias_ref, o_ref, ot_ref):
    blk = pl.program_id(1)
    sb = jnp.maximum(blk - WINDOW // ATT_KB, 0)
    kstart = pl.multiple_of(sb * ATT_KB, ATT_KB)
    lane = lax.broadcasted_iota(jnp.int32, (ATT_QB, 128), 1)
    kinds = (jnp.where(blk == 0, 2, jnp.where(blk == 1, 1, 0)),
             jnp.where(blk == 0, 3, jnp.where(blk == 1, 2, 1)),
             jnp.where(blk <= 1, 3, 2))

    def scores(h):
        pair = slice(128 * (h // 2), 128 * (h // 2 + 1))
        qp = q_ref[0, :, pair]
        qz = jnp.where((lane >= DH_A) == bool(h % 2), qp, jnp.zeros_like(qp))
        return [_dot_nt(k_ref[0, pl.ds(kstart + ATT_KB * i, ATT_KB), pair], qz)
                + bias_ref[kinds[i], h] for i in range(ATT_NKB)]

    def finish(h, s):
        m = s[0].max(axis=0, keepdims=True)
        for si in s[1:]:
            m = jnp.maximum(m, si.max(axis=0, keepdims=True))
        l = None
        ot = None
        for i, si in enumerate(s):
            p = jnp.exp2(si - m)
            li = p.sum(axis=0, keepdims=True)
            oi = _dot(vt_ref[0, sb + i, DH_A * h:DH_A * (h + 1), :], p.astype(BF16))
            l = li if l is None else l + li
            ot = oi if ot is None else ot + oi
        ot_ref[DH_A * h:DH_A * (h + 1), :] = ot / l

    pending = scores(0)
    for h in range(H_A):
        upcoming = scores(h + 1) if h + 1 < H_A else None
        finish(h, pending)
        pending = upcoming
    o_ref[0] = ot_ref[...].T.astype(o_ref.dtype)


def _attn_prompt(qk, vt, bias_t):
    bsz, s, _ = qk.shape
    nkb = s // ATT_KB
    return pl.pallas_call(
        _attn_prompt_kernel,
        grid=(bsz, s // ATT_QB),
        in_specs=[pl.BlockSpec((1, ATT_QB, WIDTH_A), lambda b, c: (b, c, 0)),
                  pl.BlockSpec((1, s, WIDTH_A), lambda b, c: (b, 0, 1)),
                  pl.BlockSpec((1, nkb, WIDTH_A, ATT_KB), lambda b, c: (b, 0, 0, 0)),
                  pl.BlockSpec((4, H_A, ATT_KB, ATT_QB), lambda b, c: (0, 0, 0, 0))],
        out_specs=pl.BlockSpec((1, ATT_QB, WIDTH_A), lambda b, c: (b, c, 0)),
        out_shape=jax.ShapeDtypeStruct((bsz, s, WIDTH_A), BF16),
        scratch_shapes=[pltpu.VMEM((WIDTH_A, ATT_QB), F32)],
        compiler_params=_params("parallel", "arbitrary"),
    )(qk, qk, vt, bias_t)


def _attn_sample_kernel(q_ref, kv_ref, ck_ref, cv_ref, bias_ref, o_ref):
    lane = lax.broadcasted_iota(jnp.int32, (CHUNK, 128), 1)

    def scores(h):
        pair = slice(128 * (h // 2), 128 * (h // 2 + 1))
        qp = q_ref[0, :, pair]
        qz = jnp.where((lane >= DH_A) == bool(h % 2), qp, jnp.zeros_like(qp))
        return (_dot_nt(qz, ck_ref[0, 0, :, pair]) + bias_ref[h, :, 0:WINDOW],
                _dot_nt(qz, kv_ref[0, :, pair].astype(BF16)) + bias_ref[h, :, WINDOW:BAND])

    def finish(h, s):
        s1, s2 = s
        pair = slice(128 * (h // 2), 128 * (h // 2 + 1))
        vpair = slice(WIDTH_A + 128 * (h // 2), WIDTH_A + 128 * (h // 2 + 1))
        half = slice(DH_A * (h % 2), DH_A * (h % 2 + 1))
        m = jnp.maximum(s1.max(axis=-1, keepdims=True), s2.max(axis=-1, keepdims=True))
        p1 = jnp.exp2(s1 - m)
        p2 = jnp.exp2(s2 - m)
        l = p1.sum(axis=-1, keepdims=True) + p2.sum(axis=-1, keepdims=True)
        o = (_dot(p1.astype(BF16), cv_ref[0, 0, :, pair])
             + _dot(p2.astype(BF16), kv_ref[0, :, vpair].astype(BF16)))
        o_ref[0, :, DH_A * h:DH_A * (h + 1)] = (o[:, half] / l).astype(o_ref.dtype)

    pending = scores(0)
    for h in range(H_A):
        upcoming = scores(h + 1) if h + 1 < H_A else None
        finish(h, pending)
        pending = upcoming


def _attn_sample(qk, kv, cache_k, cache_v, layer, bias):
    bsz = qk.shape[0]
    cache_spec = pl.BlockSpec((1, 1, WINDOW, WIDTH_A), lambda b: (layer, b, 0, 0))
    return pl.pallas_call(
        _attn_sample_kernel,
        grid=(bsz,),
        in_specs=[pl.BlockSpec((1, CHUNK, WIDTH_A), lambda b: (b, 0, 0)),
                  pl.BlockSpec((1, CHUNK, 2 * WIDTH_A), lambda b: (b, 0, 0)),
                  cache_spec, cache_spec,
                  pl.BlockSpec((H_A, CHUNK, BAND), lambda b: (0, 0, 0))],
        out_specs=pl.BlockSpec((1, CHUNK, WIDTH_A), lambda b: (b, 0, 0)),
        out_shape=jax.ShapeDtypeStruct((bsz, CHUNK, WIDTH_A), BF16),
        compiler_params=_params("parallel"),
    )(qk, kv, cache_k, cache_v, bias)


def _scan_kernel(bqk_ref, bv_ref, bog_ref, cz_ref, cx_ref, misc_ref,
                 gup_ref, gb_ref, gnw_ref, cw_ref, cb_ref, dtb_ref, alog_ref, dsk_ref, snw_ref,
                 gla0_ref, conv0_ref, ssm0_ref,
                 gla_o_ref, ssm_o_ref, gla_s_ref, conv_s_ref, ssm_s_ref,
                 xpad_ref):
    c = pl.program_id(1)
    nseq = bqk_ref.shape[0]

    @pl.when(c == 0)
    def _():
        gla_s_ref[...] = gla0_ref[...]
        conv_s_ref[...] = conv0_ref[...]
        ssm_s_ref[...] = ssm0_ref[...]

    r64 = lax.broadcasted_iota(jnp.int32, (CHUNK, CHUNK), 0)
    c64 = lax.broadcasted_iota(jnp.int32, (CHUNK, CHUNK), 1)
    tri = c64 <= r64
    tri_bf = jnp.where(tri, 1.0, 0.0).astype(BF16)
    ones_bf = jnp.ones((CHUNK, CHUNK), BF16)
    row_x = lax.broadcasted_iota(jnp.int32, (CHUNK, WIDTH_C), 0)
    s_x = lax.broadcasted_iota(jnp.int32, (CHUNK, WIDTH_C), 1) & (CHUNK - 1)
    causal_x = s_x <= row_x
    upper_x = row_x <= s_x
    er = lax.broadcasted_iota(jnp.int32, (128, WIDTH_C), 0)
    el = lax.broadcasted_iota(jnp.int32, (128, WIDTH_C), 1)
    expand_bf = jnp.where(er == MISC_DT_LANE + (el >> 6), 1.0, 0.0).astype(BF16)
    br = lax.broadcasted_iota(jnp.int32, (WIDTH_C, WIDTH_C), 0)
    bl = lax.broadcasted_iota(jnp.int32, (WIDTH_C, WIDTH_C), 1)
    blk64_bf = jnp.where((br >> 6) == (bl >> 6), 1.0, 0.0).astype(BF16)
    blk128_bf = jnp.where((br >> 7) == (bl >> 7), 1.0, 0.0).astype(BF16)

    gup = gup_ref[...]
    gb = gb_ref[...]
    gnw = gnw_ref[...]
    cb = cb_ref[...]
    dtb = dtb_ref[...]
    neg_a = -jnp.exp(alog_ref[...])
    dsk = dsk_ref[...]
    snw = snw_ref[...]

    def load_seq(b):
        return dict(misc=misc_ref[b], qk=bqk_ref[b], v=bv_ref[b], bog=bog_ref[b], cz=cz_ref[b],
                    cx=cx_ref[b], gla=gla_s_ref[b], conv=conv_s_ref[b],
                    ssm=[ssm_s_ref[b, h] for h in range(H_C)])

    def store_seq(b, out):
        gla_o_ref[b] = out["gla_o"]
        ssm_o_ref[b] = out["ssm_o"]
        gla_s_ref[b] = out["gla"]
        conv_s_ref[b] = out["conv"]
        for h in range(H_C):
            ssm_s_ref[b, h] = out["ssm"][h]

    def compute_seq(inp, slot):
        out = {}
        misc = inp["misc"]

        z = _dot(misc.astype(BF16), gup) + gb
        g = (jnp.minimum(z, 0.0) - jnp.log1p(jnp.exp(-jnp.abs(z)))) * (1.0 / GATE_TAU)
        bcum = _dot01_left(tri_bf, g)
        yield
        q = inp["qk"][:, :H_B * DK_B]
        k = inp["qk"][:, H_B * DK_B:]
        blast = bcum[CHUNK - 1:CHUNK, :]
        qb = (q * (DK_B ** -0.5) * jnp.exp(bcum)).astype(BF16)
        kb = (k * jnp.exp(-bcum)).astype(BF16)
        kl = (k * jnp.exp(blast - bcum)).astype(BF16)
        vb = inp["v"].astype(BF16)
        st = inp["gla"]
        st_bf = st.astype(BF16)
        yield
        o_parts, s_parts = [], []
        for h in range(H_B):
            ks = slice(DK_B * h, DK_B * (h + 1))
            vs = slice(DV_B * h, DV_B * (h + 1))
            att = jnp.where(tri, _dot_nt(qb[:, ks], kb[:, ks]), 0.0).astype(BF16)
            o_parts.append(_dot(att, vb[:, vs]) + _dot_nt(qb[:, ks], st_bf[:, ks]))
            s_parts.append(_dot_tn(vb[:, vs], kl[:, ks]))
            yield
        o = jnp.concatenate(o_parts, axis=1)
        out["gla"] = jnp.exp(blast) * st + jnp.concatenate(s_parts, axis=1)
        msq = _dot01_right(o * o, blk64_bf) * (1.0 / DV_B)
        yield
        go = o * lax.rsqrt(msq + EPS) * gnw * _silu(inp["bog"])
        out["gla_o"] = go.astype(gla_o_ref.dtype)
        yield

        cx = inp["cx"]
        xpad_ref[slot, 0:8, :] = inp["conv"]
        xpad_ref[slot, 8:8 + CHUNK, :] = cx
        conv = cb + cx * cw_ref[CONV_W - 1:CONV_W, :]
        for i in range(CONV_W - 1):
            off = 8 - (CONV_W - 1) + i
            conv = conv + xpad_ref[slot, off:off + CHUNK, :] * cw_ref[i:i + 1, :]
        out["conv"] = cx[CHUNK - 8:, :]
        conv = _silu(conv)
        sx = conv[:, :WIDTH_C]
        sb_bf = conv[:, WIDTH_C:WIDTH_C + N_GROUPS_C * D_STATE].astype(BF16)
        sc_bf = conv[:, WIDTH_C + N_GROUPS_C * D_STATE:].astype(BF16)
        yield

        dtv = _softplus(misc + dtb)
        dt_x = _dot01_right(dtv, expand_bf)
        yield
        a_x = neg_a * dt_x
        cs_x = _dot01_left(tri_bf, a_x)
        cs_row = _dot01_left(ones_bf, jnp.where(upper_x, a_x, 0.0))
        yield
        decay = jnp.exp(jnp.where(causal_x, cs_x - cs_row, -jnp.inf))
        cb_parts = []
        for gidx in range(N_GROUPS_C):
            ns = slice(D_STATE * gidx, D_STATE * (gidx + 1))
            cbm = _dot_nt(sc_bf[:, ns], sb_bf[:, ns])
            cb_parts += [cbm] * (H_C // N_GROUPS_C)
            yield
        scores = (jnp.concatenate(cb_parts, axis=1) * decay).astype(BF16)
        xdt = (sx * dt_x).astype(BF16)
        ecs = jnp.exp(cs_x)
        cs_last = cs_x[CHUNK - 1:CHUNK, :]
        xw = (sx * (jnp.exp(cs_last - cs_x) * dt_x)).astype(BF16)
        e_last = jnp.exp(cs_last)
        yield
        y_parts = []
        out["ssm"] = []
        for h in range(H_C):
            ps = slice(P_C * h, P_C * (h + 1))
            gi = h // (H_C // N_GROUPS_C)
            ns = slice(D_STATE * gi, D_STATE * (gi + 1))
            hst = inp["ssm"][h]
            y_parts.append(_dot(scores[:, ps], xdt[:, ps])
                           + _dot_nt(sc_bf[:, ns], hst.astype(BF16)) * ecs[:, ps])
            out["ssm"].append(e_last[:, ps] * hst + _dot_tn(xw[:, ps], sb_bf[:, ns]))
            yield
        y = jnp.concatenate(y_parts, axis=1) + dsk * sx
        y = y * _silu(inp["cz"])
        ymsq = _dot01_right(y * y, blk128_bf) * (1.0 / (WIDTH_C // N_GROUPS_C))
        yield
        out["ssm_o"] = (y * lax.rsqrt(ymsq + EPS) * snw).astype(ssm_o_ref.dtype)
        return out

    def group_body(i, carry):
        seqs = [i * SEQ_UNROLL + u for u in range(SEQ_UNROLL)]
        inputs = [load_seq(b) for b in seqs]
        outputs = _round_robin([compute_seq(inp, slot) for slot, inp in enumerate(inputs)])
        for b, out in zip(seqs, outputs):
            store_seq(b, out)
        return carry

    lax.fori_loop(0, nseq // SEQ_UNROLL, group_body, 0)


def _scan(proj, lw, gla0, conv0, ssm0):
    bsz, l, _ = proj.shape
    nc = l // CHUNK
    sg = min(SEQ_GROUP, bsz)
    ng = bsz // sg

    def col(width, idx):
        return pl.BlockSpec((sg, CHUNK, width), lambda g, c: (g, c, idx))

    def const(shape):
        return pl.BlockSpec(shape, lambda g, c: (0,) * len(shape))

    def state(shape):
        return pl.BlockSpec((sg,) + shape, lambda g, c: (g,) + (0,) * len(shape))

    gla_shape = (DV_B, H_B * DK_B)
    conv_shape = (8, CONV_CH)
    ssm_shape = (H_C, P_C, D_STATE)
    return pl.pallas_call(
        _scan_kernel,
        grid=(ng, nc),
        in_specs=[col(256, 0), col(256, 1), col(256, 2), col(256, 3), col(512, 2), col(128, 12),
                  const((128, 128)), const((1, 128)), const((1, WIDTH_B)),
                  const((CONV_W, CONV_CH)), const((1, CONV_CH)),
                  const((1, 128)), const((1, WIDTH_C)), const((1, WIDTH_C)), const((1, WIDTH_C)),
                  state(gla_shape), state(conv_shape), state(ssm_shape)],
        out_specs=[pl.BlockSpec((sg, CHUNK, WIDTH_B), lambda g, c: (g, c, 0)),
                   pl.BlockSpec((sg, CHUNK, WIDTH_C), lambda g, c: (g, c, 0)),
                   state(gla_shape), state(conv_shape), state(ssm_shape)],
        out_shape=[jax.ShapeDtypeStruct((bsz, l, WIDTH_B), BF16),
                   jax.ShapeDtypeStruct((bsz, l, WIDTH_C), BF16),
                   jax.ShapeDtypeStruct((bsz,) + gla_shape, F32),
                   jax.ShapeDtypeStruct((bsz,) + conv_shape, F32),
                   jax.ShapeDtypeStruct((bsz,) + ssm_shape, F32)],
        scratch_shapes=[pltpu.VMEM((SEQ_UNROLL, 8 + CHUNK, CONV_CH), F32)],
        compiler_params=_params("parallel", "arbitrary"),
    )(proj, proj, proj, proj, proj, proj,
      lw["gup"], lw["gb"], lw["gnw"], lw["cw"], lw["cb"], lw["dtb"], lw["alog"], lw["dsk"], lw["snw"],
      gla0, conv0, ssm0)


OUT_TM = 256


def _route(logits):
    lane = lax.broadcasted_iota(jnp.int32, logits.shape, 1)
    lane_f = lane.astype(F32)
    neg = -jnp.inf
    big = 1000.0
    gmask = (lane >= ROUTER_GROUP_LANE) & (lane < ROUTER_GROUP_LANE + N_EXPERT_GROUPS)
    lg = jnp.where(gmask, logits, neg)
    gmax = lg.max(axis=-1, keepdims=True)
    g_idx = jnp.where(lg == gmax, lane_f, big).min(axis=-1, keepdims=True) - ROUTER_GROUP_LANE
    p_grp = 1.0 / jnp.where(gmask, jnp.exp(logits - gmax), 0.0).sum(axis=-1, keepdims=True)
    emask = (lane < N_EXPERTS) & ((lane >> 2).astype(F32) == g_idx)
    le = jnp.where(emask, logits, neg)
    v1 = le.max(axis=-1, keepdims=True)
    i1 = jnp.where(le == v1, lane_f, big).min(axis=-1, keepdims=True)
    le2 = jnp.where(lane_f == i1, neg, le)
    v2 = le2.max(axis=-1, keepdims=True)
    i2 = jnp.where(le2 == v2, lane_f, big).min(axis=-1, keepdims=True)
    e2 = jnp.exp(v2 - v1)
    den = 1.0 + e2
    return (jnp.where(lane_f == i1, (1.0 / den) * p_grp, 0.0)
            + jnp.where(lane_f == i2, (e2 / den) * p_grp, 0.0)
            + jnp.where(lane == ROUTER_GIDX_LANE, g_idx, 0.0))


def _out_proj_kernel(x_ref, att_ref, gla_ref, ssm_ref, w_out_ref, nw_ref, rwh_ref, rwl_ref, rb_ref,
                     x1_ref, hn_ref, gate_ref, wo_ref):
    @pl.when(pl.program_id(0) == 0)
    def _():
        for r in range(0, D_MODEL, PREP_ROWS):
            wo_ref[r:r + PREP_ROWS, :] = w_out_ref[0, r:r + PREP_ROWS, :].astype(BF16)

    mo = (_dot(att_ref[...], wo_ref[0:WIDTH_A, :])
          + _dot(gla_ref[...], wo_ref[WIDTH_A:WIDTH_A + WIDTH_B, :])
          + _dot(ssm_ref[...], wo_ref[WIDTH_A + WIDTH_B:, :]))
    x1 = x_ref[...] + mo
    x1_ref[...] = x1
    ms = jnp.mean(x1 * x1, axis=-1, keepdims=True)
    hn = x1 * lax.rsqrt(ms + EPS) * nw_ref[...]
    hn_ref[...] = hn.astype(hn_ref.dtype)

    h_hi = hn.astype(BF16)
    h_lo = (hn - h_hi.astype(F32)).astype(BF16)
    logits = (_dot(h_hi, rwh_ref[...]) + _dot(h_lo, rwh_ref[...]) + _dot(h_hi, rwl_ref[...])
              + rb_ref[...])
    rg = OUT_TM // 4
    for r in range(0, OUT_TM, rg):
        gate_ref[r:r + rg, :] = _route(logits[r:r + rg, :])


def _out_proj(x, att, gla, ssm, lw, layer):
    t = x.shape[0]

    def rows(width):
        return pl.BlockSpec((OUT_TM, width), lambda i: (i, 0))

    def const(shape):
        return pl.BlockSpec(shape, lambda i: (0, 0))

    return pl.pallas_call(
        _out_proj_kernel,
        grid=(t // OUT_TM,),
        in_specs=[rows(D_MODEL), rows(WIDTH_A), rows(WIDTH_B), rows(WIDTH_C),
                  pl.BlockSpec((1, D_MODEL, D_MODEL), lambda i: (layer, 0, 0)), const((1, D_MODEL)),
                  const((D_MODEL, 128)), const((D_MODEL, 128)), const((1, 128))],
        out_specs=[rows(D_MODEL), rows(D_MODEL), rows(128)],
        out_shape=[jax.ShapeDtypeStruct((t, D_MODEL), F32),
                   jax.ShapeDtypeStruct((t, D_MODEL), BF16),
                   jax.ShapeDtypeStruct((t, 128), F32)],
        scratch_shapes=[pltpu.VMEM((D_MODEL, D_MODEL), BF16)],
        compiler_params=_params("arbitrary"),
    )(x, att, gla, ssm, lw["wo"], lw["n2w"], lw["rw_hi"], lw["rw_lo"], lw["rb"])


RT_TS = 256
RT_ALIGN = 16
RT_ROWS = 384
RT_SIZES = (256, 128, 64, 32, 16)
PAY_COLS = D_MODEL + 256
MOE_TM = 512


def _moe_max_tiles(t):
    return (t + (t // RT_TS) * N_EXPERT_GROUPS * (RT_ALIGN - 1)) // MOE_TM + N_EXPERT_GROUPS


def _sort_matrix(gate, c_ref, tile):
    lane = lax.broadcasted_iota(jnp.int32, (RT_TS, 128), 1)
    gidx = jnp.where(lane == ROUTER_GIDX_LANE, gate, 0.0).sum(axis=1, keepdims=True)
    onehot = lane.astype(F32) == gidx
    r = lax.broadcasted_iota(jnp.int32, (RT_TS, RT_TS), 0)
    c = lax.broadcasted_iota(jnp.int32, (RT_TS, RT_TS), 1)
    earlier = jnp.where(c < r, 1.0, 0.0).astype(BF16)
    before = _dot(earlier, jnp.where(onehot, 1.0, 0.0).astype(BF16))
    start = jnp.zeros((RT_TS, 128), F32)
    for g in range(N_EXPERT_GROUPS):
        start = jnp.where(lane == g, c_ref[N_EXPERT_GROUPS * tile + g].astype(F32), start)
    dest = jnp.where(onehot, before + start, 0.0).sum(axis=1, keepdims=True)
    rows = lax.broadcasted_iota(jnp.int32, (RT_TS, RT_ROWS), 1).astype(F32)
    return jnp.where(rows == dest, 1.0, 0.0).astype(BF16)


def _piece_copies(length, local, glob, make_copy, act):
    for size in RT_SIZES:
        take = (length & size) != 0

        @pl.when(take)
        def _(local=local, glob=glob, size=size):
            act(make_copy(pl.multiple_of(local, RT_ALIGN), pl.multiple_of(glob, RT_ALIGN), size))

        step = jnp.where(take, size, 0)
        local = local + step
        glob = glob + step


def _run_copies(c_ref, lp_ref, off_ref, tile, make_copy, act):
    for g in range(N_EXPERT_GROUPS):
        k = N_EXPERT_GROUPS * tile + g
        _piece_copies(lp_ref[k], c_ref[k], off_ref[k], make_copy, act)


def _dispatch_kernel(c_ref, lp_ref, off_ref, tail_ref, nact_ref, hn_ref, gate_ref, xs_hbm,
                     y_ref, zero_ref, sem, zsem, *, first_spare, max_tiles):
    i = pl.program_id(0)
    nt = pl.num_programs(0)
    slot = i % 2
    gate = gate_ref[...]
    pt = _sort_matrix(gate, c_ref, i)
    g_hi = gate.astype(BF16)
    g_lo = (gate - g_hi.astype(F32)).astype(BF16)
    y_ref[slot, :, 0:D_MODEL] = _dot_tn(pt, hn_ref[...]).astype(BF16)
    y_ref[slot, :, D_MODEL:D_MODEL + 128] = _dot_tn(pt, g_hi).astype(BF16)
    y_ref[slot, :, D_MODEL + 128:] = _dot_tn(pt, g_lo).astype(BF16)

    def copy_from(s):
        return lambda local, glob, size: pltpu.make_async_copy(
            y_ref.at[s, pl.ds(local, size)], xs_hbm.at[pl.ds(glob, size)], sem.at[s])

    _run_copies(c_ref, lp_ref, off_ref, i, copy_from(slot), lambda cp: cp.start())

    @pl.when(i > 0)
    def _():
        _run_copies(c_ref, lp_ref, off_ref, i - 1, copy_from(1 - slot), lambda cp: cp.wait())

    @pl.when(i == nt - 1)
    def _():
        _run_copies(c_ref, lp_ref, off_ref, i, copy_from(slot), lambda cp: cp.wait())
        zero_ref[...] = jnp.zeros_like(zero_ref)

        def zero_copy(local, glob, size):
            return pltpu.make_async_copy(zero_ref.at[pl.ds(local, size)],
                                         xs_hbm.at[pl.ds(glob, size)], zsem)

        def zero_fill(act):
            for g in range(N_EXPERT_GROUPS):
                _piece_copies(tail_ref[N_EXPERT_GROUPS + g], 0, tail_ref[g], zero_copy, act)
            for j in range(first_spare, max_tiles):
                @pl.when(j >= nact_ref[0])
                def _(j=j):
                    act(zero_copy(0, j * MOE_TM, MOE_TM))

        zero_fill(lambda cp: cp.start())
        zero_fill(lambda cp: cp.wait())


def _moe_group_kernel(grp_ref, nact_ref, xs_ref, wg_ref, wu_ref, wd_ref, o_ref, wg_bf, wu_bf, wd_bf):
    j = pl.program_id(0)

    @pl.when(jnp.logical_or(j == 0, grp_ref[j] != grp_ref[jnp.maximum(j - 1, 0)]))
    def _():
        wg_bf[...] = wg_ref[0].astype(BF16)
        wu_bf[...] = wu_ref[0].astype(BF16)
        wd_bf[...] = wd_ref[0].astype(BF16)

    @pl.when(j >= nact_ref[0])
    def _():
        o_ref[...] = jnp.zeros_like(o_ref)

    @pl.when(j < nact_ref[0])
    def _():
        x = xs_ref[:, 0:D_MODEL]
        gate = (xs_ref[:, D_MODEL:D_MODEL + 128].astype(F32)
                + xs_ref[:, D_MODEL + 128:].astype(F32))
        lane = lax.broadcasted_iota(jnp.int32, gate.shape, 1)
        first = grp_ref[j] * EXPERTS_PER_GROUP
        acc = None
        for e in range(EXPERTS_PER_GROUP):
            ge = jnp.where(lane == first + e, gate, 0.0).sum(axis=-1, keepdims=True)
            hid = _silu(_dot(x, wg_bf[e])) * _dot(x, wu_bf[e]) * ge
            part = _dot(hid.astype(BF16), wd_bf[e])
            acc = part if acc is None else acc + part
        o_ref[...] = acc


def _combine_kernel(c_ref, lp_ref, off_ref, x_ref, gate_ref, fw_ref, ys_hbm, o_ref, z_ref, sem,
                    *, final_norm):
    i = pl.program_id(0)
    nt = pl.num_programs(0)
    slot = i % 2

    def copy_into(s):
        return lambda local, glob, size: pltpu.make_async_copy(
            ys_hbm.at[pl.ds(glob, size)], z_ref.at[s, pl.ds(local, size)], sem.at[s])

    @pl.when(i == 0)
    def _():
        z_ref[...] = jnp.zeros_like(z_ref)
        _run_copies(c_ref, lp_ref, off_ref, 0, copy_into(0), lambda cp: cp.start())

    @pl.when(i + 1 < nt)
    def _():
        _run_copies(c_ref, lp_ref, off_ref, i + 1, copy_into(1 - slot), lambda cp: cp.start())

    _run_copies(c_ref, lp_ref, off_ref, i, copy_into(slot), lambda cp: cp.wait())
    pt = _sort_matrix(gate_ref[...], c_ref, i)
    z = z_ref[slot]
    z_hi = z.astype(BF16)
    z_lo = (z - z_hi.astype(F32)).astype(BF16)
    y = x_ref[...] + _dot(pt, z_hi) + _dot(pt, z_lo)
    if final_norm:
        ms = jnp.mean(y * y, axis=-1, keepdims=True)
        y = y * lax.rsqrt(ms + EPS) * fw_ref[...]
    o_ref[...] = y


def _route_meta(gate, max_tiles):
    t = gate.shape[0]
    nt = t // RT_TS
    groups = jnp.arange(N_EXPERT_GROUPS, dtype=jnp.int32)
    g = gate[:, ROUTER_GIDX_LANE].astype(jnp.int32).reshape(nt, RT_TS, 1)
    count = (g == groups).astype(jnp.int32).sum(axis=1)
    lp = (count + RT_ALIGN - 1) // RT_ALIGN * RT_ALIGN
    total = lp.sum(axis=0)
    ntile = (total + MOE_TM - 1) // MOE_TM
    last = jnp.cumsum(ntile)
    first_row = (last - ntile) * MOE_TM
    c = jnp.cumsum(lp, axis=1) - lp
    off = first_row[None, :] + jnp.cumsum(lp, axis=0) - lp
    tail = jnp.concatenate([first_row + total, ntile * MOE_TM - total])
    j = jnp.arange(max_tiles, dtype=jnp.int32)[:, None]
    grp = jnp.minimum((j >= last[None, :]).astype(jnp.int32).sum(axis=1), N_EXPERT_GROUPS - 1)
    return c.reshape(-1), lp.reshape(-1), off.reshape(-1), tail, grp, last[-1:]


def _moe(x, hn, gate, lw, layer, fw, final_norm):
    t = x.shape[0]
    nt = t // RT_TS
    max_tiles = _moe_max_tiles(t)
    c, lp, off, tail, grp, nact = _route_meta(gate, max_tiles)

    def rows(width):
        return pl.BlockSpec((RT_TS, width), lambda i, *_: (i, 0))

    xs = pl.pallas_call(
        functools.partial(_dispatch_kernel, first_spare=t // MOE_TM, max_tiles=max_tiles),
        grid_spec=pltpu.PrefetchScalarGridSpec(
            num_scalar_prefetch=5, grid=(nt,),
            in_specs=[rows(D_MODEL), rows(128)],
            out_specs=pl.BlockSpec(memory_space=pl.ANY),
            scratch_shapes=[pltpu.VMEM((2, RT_ROWS, PAY_COLS), BF16),
                            pltpu.VMEM((MOE_TM, PAY_COLS), BF16),
                            pltpu.SemaphoreType.DMA((2,)), pltpu.SemaphoreType.DMA(())]),
        out_shape=jax.ShapeDtypeStruct((max_tiles * MOE_TM, PAY_COLS), BF16),
        compiler_params=_params("arbitrary"),
    )(c, lp, off, tail, nact, hn, gate)

    def experts(shape):
        return pl.BlockSpec((1, EXPERTS_PER_GROUP) + shape, lambda j, grp, nact: (layer, grp[j], 0, 0))

    def rounded(shape):
        return pltpu.VMEM((EXPERTS_PER_GROUP,) + shape, BF16)

    ys = pl.pallas_call(
        _moe_group_kernel,
        grid_spec=pltpu.PrefetchScalarGridSpec(
            num_scalar_prefetch=2, grid=(max_tiles,),
            in_specs=[pl.BlockSpec((MOE_TM, PAY_COLS), lambda j, grp, nact: (j, 0)),
                      experts((D_MODEL, D_EXPERT)), experts((D_MODEL, D_EXPERT)),
                      experts((D_EXPERT, D_MODEL))],
            out_specs=pl.BlockSpec((MOE_TM, D_MODEL), lambda j, grp, nact: (j, 0)),
            scratch_shapes=[rounded((D_MODEL, D_EXPERT)), rounded((D_MODEL, D_EXPERT)),
                            rounded((D_EXPERT, D_MODEL))]),
        out_shape=jax.ShapeDtypeStruct((max_tiles * MOE_TM, D_MODEL), F32),
        compiler_params=_params("arbitrary"),
    )(grp, nact, xs, lw["wg"], lw["wu"], lw["wd"])

    return pl.pallas_call(
        functools.partial(_combine_kernel, final_norm=final_norm),
        grid_spec=pltpu.PrefetchScalarGridSpec(
            num_scalar_prefetch=3, grid=(nt,),
            in_specs=[rows(D_MODEL), rows(128), pl.BlockSpec((1, D_MODEL), lambda i, *_: (0, 0)),
                      pl.BlockSpec(memory_space=pl.ANY)],
            out_specs=rows(D_MODEL),
            scratch_shapes=[pltpu.VMEM((2, RT_ROWS, D_MODEL), F32),
                            pltpu.SemaphoreType.DMA((2,))]),
        out_shape=jax.ShapeDtypeStruct((t, D_MODEL), F32),
        compiler_params=_params("arbitrary"),
    )(c, lp, off, x, gate, fw, ys)


def _lane_place(vec, start, width=128):
    return jnp.zeros((1, width), F32).at[0, start:start + vec.shape[0]].set(vec.astype(F32))


REL_PAD = 704


def _toeplitz(v, rows, cols):
    n = v.shape[-1]
    tiled = jnp.tile(v, (1,) * (v.ndim - 1) + (rows,))[..., :rows * (n - 1)]
    return tiled.reshape(v.shape[:-1] + (rows, n - 1))[..., :cols]


def _band_bias(table):
    t = table.astype(F32)
    ext = jnp.concatenate([jnp.repeat(t[:, :1], REL_PAD, axis=1), t,
                           jnp.repeat(t[:, -1:], REL_PAD, axis=1)], axis=1)
    extr = ext[:, ::-1]
    top = ext.shape[1] - 1 - (REL_CLIP + REL_PAD)

    def band(off, nq, nk):
        n = nq + nk
        v = jnp.concatenate([extr[:, top - off:top - off + nk + 1],
                             extr[:, top - off - (nq - 1):top - off]], axis=1)
        assert v.shape[1] == n
        return _toeplitz(v, nq, nk)

    sample = band(WINDOW, CHUNK, BAND)
    kc = (jnp.arange(ATT_KB) // CHUNK)[:, None]
    qc = (jnp.arange(ATT_QB) // CHUNK)[None, :]
    block = lambda off: jnp.swapaxes(band(off, ATT_QB, ATT_KB), 1, 2)
    prompt = jnp.stack([jnp.where((kc >= qc)[None], block(2 * ATT_KB), -1e30),
                        block(ATT_KB),
                        jnp.where((kc <= qc)[None], block(0), -1e30),
                        jnp.full((H_A, ATT_KB, ATT_QB), -1e30, F32)])
    return sample * LOG2_E, prompt * LOG2_E


def _layer_weights(i, norm1_w, w_in, rel_bias_table, gla_w_gate_up, gla_b_gate, gla_norm_w,
                   ssm_conv_w, ssm_conv_b, ssm_dt_bias, ssm_a_log, ssm_d, ssm_norm_w, w_out,
                   norm2_w, router_group_w, router_group_b, router_expert_w, router_expert_b,
                   exp_w_gate, exp_w_up, exp_w_down):
    bias_s, bias_p = _band_bias(rel_bias_table[i])
    rw = jnp.concatenate([router_expert_w[i], router_group_w[i],
                          jnp.zeros((D_MODEL, 128 - N_EXPERTS - N_EXPERT_GROUPS), F32)], axis=1)
    rw_hi = rw.astype(BF16)
    return dict(
        n1w=norm1_w[i][None, :],
        w_in=w_in,
        bias_s=bias_s,
        bias_p=bias_p,
        gup=jnp.zeros((128, 128), F32).at[:GATE_RANK, :].set(gla_w_gate_up[i]).astype(BF16),
        gb=gla_b_gate[i][None, :].astype(F32),
        gnw=jnp.tile(gla_norm_w[i], H_B)[None, :].astype(F32),
        cw=ssm_conv_w[i].astype(F32),
        cb=ssm_conv_b[i][None, :].astype(F32),
        dtb=_lane_place(ssm_dt_bias[i], MISC_DT_LANE),
        alog=jnp.repeat(ssm_a_log[i].astype(F32), P_C)[None, :],
        dsk=jnp.repeat(ssm_d[i].astype(F32), P_C)[None, :],
        snw=ssm_norm_w[i][None, :].astype(F32),
        wo=w_out,
        n2w=norm2_w[i][None, :],
        rw_hi=rw_hi,
        rw_lo=(rw - rw_hi.astype(F32)).astype(BF16),
        rb=jnp.concatenate([router_expert_b[i], router_group_b[i],
                            jnp.zeros((128 - N_EXPERTS - N_EXPERT_GROUPS,), F32)])[None, :],
        wg=exp_w_gate,
        wu=exp_w_up,
        wd=exp_w_down,
    )


def _stream_layer(x, bsz, lw, layer, cache_k, cache_v, gla0, conv0, ssm0, fw, final_norm):
    l = x.shape[0] // bsz
    if cache_k is None:
        qk, rest, k_keep, v_keep, vt = _in_proj(x, lw["n1w"], lw["w_in"], layer, True, l)
        att = _attn_prompt(qk.reshape(bsz, l, 2 * WIDTH_A),
                           vt.reshape(bsz, l // ATT_KB, WIDTH_A, ATT_KB), lw["bias_p"])
    else:
        qk, rest, k_keep, v_keep, kv = _in_proj(x, lw["n1w"], lw["w_in"], layer, False, l)
        att = _attn_sample(qk.reshape(bsz, l, 2 * WIDTH_A), kv.reshape(bsz, l, 2 * WIDTH_A),
                           cache_k, cache_v, layer, lw["bias_s"])
    gla_o, ssm_o, gla_s, conv_s, ssm_s = _scan(rest.reshape(bsz, l, REST_COLS), lw, gla0, conv0, ssm0)
    t = bsz * l
    x1, hn, gate = _out_proj(x, att.reshape(t, WIDTH_A), gla_o.reshape(t, WIDTH_B),
                             ssm_o.reshape(t, WIDTH_C), lw, layer)
    x2 = _moe(x1, hn, gate, lw, layer, fw, final_norm)
    keep = min(WINDOW, l)
    k_keep = k_keep.reshape(bsz, keep, H_A, DH_A)
    v_keep = v_keep.reshape(bsz, keep, H_A, DH_A)
    gla_state = gla_s.reshape(bsz, DV_B, H_B, DK_B).transpose(0, 2, 3, 1)
    conv_state = conv_s[:, 8 - (CONV_W - 1):, :]
    return x2, k_keep, v_keep, gla_state, conv_state, ssm_s


def kernel(x_prompt, x_sample, cache_k_a, cache_v_a, state_gla, state_conv, state_ssm, norm1_w, w_in, rel_bias_table, gla_w_gate_up, gla_b_gate, gla_norm_w, ssm_conv_w, ssm_conv_b, ssm_dt_bias, ssm_a_log, ssm_d, ssm_norm_w, w_out, norm2_w, router_group_w, router_group_b, router_expert_w, router_expert_b, exp_w_gate, exp_w_up, exp_w_down, final_norm_w):
    bp, sp, _ = x_prompt.shape
    bs, ss, _ = x_sample.shape
    xp = x_prompt.reshape(bp * sp, D_MODEL)
    xs = x_sample.reshape(bs * ss, D_MODEL)
    fw = final_norm_w[None, :].astype(F32)
    outs_p, outs_s = [], []
    cache_k = cache_k_a.astype(BF16).reshape(DEPTH, bs, -1, WIDTH_A)
    cache_v = cache_v_a.astype(BF16).reshape(DEPTH, bs, -1, WIDTH_A)
    for i in range(DEPTH):
        lw = _layer_weights(i, norm1_w, w_in, rel_bias_table, gla_w_gate_up, gla_b_gate, gla_norm_w,
                            ssm_conv_w, ssm_conv_b, ssm_dt_bias, ssm_a_log, ssm_d, ssm_norm_w, w_out,
                            norm2_w, router_group_w, router_group_b, router_expert_w, router_expert_b,
                            exp_w_gate, exp_w_up, exp_w_down)
        last = i == DEPTH - 1
        xp, *sp_out = _stream_layer(
            xp, bp, lw, i, None, None,
            jnp.zeros((bp, DV_B, H_B * DK_B), F32),
            jnp.zeros((bp, 8, CONV_CH), F32),
            jnp.zeros((bp, H_C, P_C, D_STATE), F32), fw, last)
        outs_p.append(sp_out)
        gla0 = state_gla[i].astype(F32).transpose(0, 3, 1, 2).reshape(bs, DV_B, H_B * DK_B)
        conv0 = jnp.pad(state_conv[i].astype(F32), ((0, 0), (8 - (CONV_W - 1), 0), (0, 0)))
        xs, *ss_out = _stream_layer(
            xs, bs, lw, i, cache_k, cache_v,
            gla0, conv0, state_ssm[i].astype(F32), fw, last)
        outs_s.append(ss_out)
    stack = lambda outs, j: jnp.stack([o[j] for o in outs])
    return (xp.reshape(bp, sp, D_MODEL), xs.reshape(bs, ss, D_MODEL),
            stack(outs_p, 0), stack(outs_p, 1), stack(outs_p, 2), stack(outs_p, 3), stack(outs_p, 4),
            stack(outs_s, 0), stack(outs_s, 1), stack(outs_s, 2), stack(outs_s, 3), stack(outs_s, 4))
```

```python
import functools

import jax
import jax.numpy as jnp
from jax import lax
from jax.experimental import pallas as pl
from jax.experimental.pallas import tpu as pltpu

F32 = jnp.float32
BF16 = jnp.bfloat16

D_MODEL = 1024
DEPTH = 2
EPS = 1e-6
CHUNK = 64
N_PAST_CHUNKS = 8
BAND = (N_PAST_CHUNKS + 1) * CHUNK
WINDOW = N_PAST_CHUNKS * CHUNK
H_A, DH_A, WIDTH_A = 8, 64, 512
REL_CLIP = 128
LOG2_E = 1.4426950408889634
H_B, DK_B, DV_B, WIDTH_B = 4, 32, 64, 256
GATE_RANK = 16
GATE_TAU = 16.0
H_C, P_C, WIDTH_C = 4, 64, 256
N_GROUPS_C = 2
D_STATE = 64
CONV_W = 4
CONV_CH = 512
N_EXPERT_GROUPS = 4
EXPERTS_PER_GROUP = 4
N_EXPERTS = 16
D_EXPERT = 256

PROJ_COLS = 3200
MISC_DT_LANE = GATE_RANK
ROUTER_GROUP_LANE = N_EXPERTS
ROUTER_GIDX_LANE = 20

SEQ_GROUP = 8
SEQ_UNROLL = 8
VMEM_LIMIT_BYTES = 56 * 1024 * 1024

NT_DIMS = (((1,), (1,)), ((), ()))
TN_DIMS = (((0,), (0,)), ((), ()))


def _params(*sem):
    return pltpu.CompilerParams(dimension_semantics=sem, vmem_limit_bytes=VMEM_LIMIT_BYTES)


def _dot(a, b):
    return jnp.dot(a, b, preferred_element_type=F32)


def _dot_nt(a, b):
    return lax.dot_general(a, b, NT_DIMS, preferred_element_type=F32)


def _dot_tn(a, b):
    return lax.dot_general(a, b, TN_DIMS, preferred_element_type=F32)


def _split2(a):
    hi = a.astype(BF16)
    return hi, (a - hi.astype(F32)).astype(BF16)


def _dot01_left(m01, a):
    hi, lo = _split2(a)
    return _dot(m01, hi) + _dot(m01, lo)


def _dot01_right(a, m01):
    hi, lo = _split2(a)
    return _dot(hi, m01) + _dot(lo, m01)


def _round_robin(gens):
    results = [None] * len(gens)
    live = list(range(len(gens)))
    while live:
        for idx in list(live):
            try:
                next(gens[idx])
            except StopIteration as stop:
                results[idx] = stop.value
                live.remove(idx)
    return results


def _silu(x):
    return x * jax.nn.sigmoid(x)


def _softplus(x):
    return jnp.maximum(x, 0.0) + jnp.log1p(jnp.exp(-jnp.abs(x)))


IN_TM = 256
QKV_COLS = 3 * WIDTH_A
REST_COLS = PROJ_COLS - QKV_COLS
REST_CHUNKS = ((0, 512), (512, 1024), (1024, REST_COLS))


IN_COLS = 3092
IN_TAIL = QKV_COLS + 512
PREP_ROWS = 128


def _prepare_w_in(w_in_ref, w_ref):
    scale = DH_A ** -0.5 * LOG2_E
    n_tail = REST_COLS - 512 - 128
    for r in range(0, D_MODEL, PREP_ROWS):
        rs = slice(r, r + PREP_ROWS)
        w_ref[rs, 0:WIDTH_A] = (w_in_ref[0, rs, 0:WIDTH_A] * scale).astype(BF16)
        w_ref[rs, WIDTH_A:IN_TAIL] = w_in_ref[0, rs, WIDTH_A:IN_TAIL].astype(BF16)
        tail = w_in_ref[0, rs, IN_TAIL:IN_COLS]
        w_ref[rs, IN_TAIL:IN_TAIL + n_tail] = tail[:, GATE_RANK:GATE_RANK + n_tail].astype(BF16)
        misc = jnp.concatenate([tail[:, 0:GATE_RANK], tail[:, GATE_RANK + n_tail:],
                                jnp.zeros((PREP_ROWS, 128 - GATE_RANK - H_C), F32)], axis=1)
        w_ref[rs, IN_TAIL + n_tail:] = misc.astype(BF16)


def _in_proj_kernel(x_ref, nw_ref, w_in_ref, *refs, with_vt):
    if with_vt:
        qk_ref, rest_ref, kk_ref, vk_ref, vt_ref, w_ref = refs
    else:
        qk_ref, rest_ref, kk_ref, vk_ref, kv_ref, w_ref = refs

    @pl.when(pl.program_id(0) == 0)
    def _():
        _prepare_w_in(w_in_ref, w_ref)

    x = x_ref[...]
    ms = jnp.mean(x * x, axis=-1, keepdims=True)
    xn = (x * lax.rsqrt(ms + EPS) * nw_ref[...]).astype(BF16)
    qk_ref[:, 0:WIDTH_A] = _dot(xn, w_ref[:, 0:WIDTH_A]).astype(BF16)
    k = _dot(xn, w_ref[:, WIDTH_A:2 * WIDTH_A])
    v = _dot(xn, w_ref[:, 2 * WIDTH_A:QKV_COLS])
    qk_ref[:, WIDTH_A:] = k.astype(BF16)
    if not with_vt:
        kv_ref[:, 0:WIDTH_A] = k
        kv_ref[:, WIDTH_A:] = v

    def keep_rows():
        kk = kk_ref.reshape(IN_TM * H_A, DH_A)
        vk = vk_ref.reshape(IN_TM * H_A, DH_A)
        for h in range(H_A):
            sl = slice(DH_A * h, DH_A * (h + 1))
            kk[pl.ds(h, IN_TM, stride=H_A), :] = k[:, sl]
            vk[pl.ds(h, IN_TM, stride=H_A), :] = v[:, sl]

    keep_rows()
    for lo, hi in REST_CHUNKS:
        rest_ref[:, lo:hi] = _dot(xn, w_ref[:, QKV_COLS + lo:QKV_COLS + hi])
    if with_vt:
        vt_ref[0] = v.T.astype(BF16)


def _in_proj(x, nw, w_in, layer, with_vt, seq_len):
    t = x.shape[0]
    tiles_per_seq = max(seq_len // IN_TM, 1)
    keep_tiles = max(min(WINDOW, seq_len) // IN_TM, 1)
    n_keep = t // IN_TM // tiles_per_seq * keep_tiles

    def rows(width):
        return pl.BlockSpec((IN_TM, width), lambda i: (i, 0))

    def kept(i):
        return (i // tiles_per_seq * keep_tiles
                + jnp.maximum(i % tiles_per_seq - (tiles_per_seq - keep_tiles), 0), 0, 0)

    keep_spec = pl.BlockSpec((IN_TM, H_A, DH_A), kept)
    keep_shape = jax.ShapeDtypeStruct((n_keep * IN_TM, H_A, DH_A), F32)
    in_specs = [rows(D_MODEL), pl.BlockSpec((1, D_MODEL), lambda i: (0, 0)),
                pl.BlockSpec((1, D_MODEL, IN_COLS), lambda i: (layer, 0, 0))]
    scratch = [pltpu.VMEM((D_MODEL, PROJ_COLS), BF16)]
    out_specs = [rows(2 * WIDTH_A), rows(REST_COLS), keep_spec, keep_spec]
    out_shape = [jax.ShapeDtypeStruct((t, 2 * WIDTH_A), BF16),
                 jax.ShapeDtypeStruct((t, REST_COLS), F32), keep_shape, keep_shape]
    if with_vt:
        out_specs.append(pl.BlockSpec((1, WIDTH_A, IN_TM), lambda i: (i, 0, 0)))
        out_shape.append(jax.ShapeDtypeStruct((t // IN_TM, WIDTH_A, IN_TM), BF16))
    else:
        out_specs.append(rows(2 * WIDTH_A))
        out_shape.append(jax.ShapeDtypeStruct((t, 2 * WIDTH_A), F32))
    return pl.pallas_call(
        functools.partial(_in_proj_kernel, with_vt=with_vt),
        grid=(t // IN_TM,),
        in_specs=in_specs, out_specs=out_specs, out_shape=out_shape, scratch_shapes=scratch,
        compiler_params=_params("arbitrary"),
    )(x, nw, w_in)


ATT_QB = 4 * CHUNK
ATT_KB = IN_TM
ATT_NKB = (ATT_QB + WINDOW) // ATT_KB


def _attn_prompt_kernel(q_ref, k_ref, vt_ref, bias_ref, o_ref, ot_ref):
    blk = pl.program_id(1)
    sb = jnp.maximum(blk - WINDOW // ATT_KB, 0)
    kstart = pl.multiple_of(sb * ATT_KB, ATT_KB)
    lane = lax.broadcasted_iota(jnp.int32, (ATT_QB, 128), 1)
    kinds = (jnp.where(blk == 0, 2, jnp.where(blk == 1, 1, 0)),
             jnp.where(blk == 0, 3, jnp.where(blk == 1, 2, 1)),
             jnp.where(blk <= 1, 3, 2))

    def scores(h):
        pair = slice(128 * (h // 2), 128 * (h // 2 + 1))
        qp = q_ref[0, :, pair]
        qz = jnp.where((lane >= DH_A) == bool(h % 2), qp, jnp.zeros_like(qp))
        return [_dot_nt(k_ref[0, pl.ds(kstart + ATT_KB * i, ATT_KB), pair], qz)
                + bias_ref[kinds[i], h] for i in range(ATT_NKB)]

    def finish(h, s):
        m = s[0].max(axis=0, keepdims=True)
        for si in s[1:]:
            m = jnp.maximum(m, si.max(axis=0, keepdims=True))
        l = None
        ot = None
        for i, si in enumerate(s):
            p = jnp.exp2(si - m)
            li = p.sum(axis=0, keepdims=True)
            oi = _dot(vt_ref[0, sb + i, DH_A * h:DH_A * (h + 1), :], p.astype(BF16))
            l = li if l is None else l + li
            ot = oi if ot is None else ot + oi
        ot_ref[DH_A * h:DH_A * (h + 1), :] = ot / l

    pending = scores(0)
    for h in range(H_A):
        upcoming = scores(h + 1) if h + 1 < H_A else None
        finish(h, pending)
        pending = upcoming
    o_ref[0] = ot_ref[...].T.astype(o_ref.dtype)


def _attn_prompt(qk, vt, bias_t):
    bsz, s, _ = qk.shape
    nkb = s // ATT_KB
    return pl.pallas_call(
        _attn_prompt_kernel,
        grid=(bsz, s // ATT_QB),
        in_specs=[pl.BlockSpec((1, ATT_QB, WIDTH_A), lambda b, c: (b, c, 0)),
                  pl.BlockSpec((1, s, WIDTH_A), lambda b, c: (b, 0, 1)),
                  pl.BlockSpec((1, nkb, WIDTH_A, ATT_KB), lambda b, c: (b, 0, 0, 0)),
                  pl.BlockSpec((4, H_A, ATT_KB, ATT_QB), lambda b, c: (0, 0, 0, 0))],
        out_specs=pl.BlockSpec((1, ATT_QB, WIDTH_A), lambda b, c: (b, c, 0)),
        out_shape=jax.ShapeDtypeStruct((bsz, s, WIDTH_A), BF16),
        scratch_shapes=[pltpu.VMEM((WIDTH_A, ATT_QB), F32)],
        compiler_params=_params("parallel", "arbitrary"),
    )(qk, qk, vt, bias_t)


def _attn_sample_kernel(q_ref, kv_ref, ck_ref, cv_ref, bias_ref, o_ref):
    lane = lax.broadcasted_iota(jnp.int32, (CHUNK, 128), 1)

    def scores(h):
        pair = slice(128 * (h // 2), 128 * (h // 2 + 1))
        qp = q_ref[0, :, pair]
        qz = jnp.where((lane >= DH_A) == bool(h % 2), qp, jnp.zeros_like(qp))
        return (_dot_nt(qz, ck_ref[0, 0, :, pair]) + bias_ref[h, :, 0:WINDOW],
                _dot_nt(qz, kv_ref[0, :, pair].astype(BF16)) + bias_ref[h, :, WINDOW:BAND])

    def finish(h, s):
        s1, s2 = s
        pair = slice(128 * (h // 2), 128 * (h // 2 + 1))
        vpair = slice(WIDTH_A + 128 * (h // 2), WIDTH_A + 128 * (h // 2 + 1))
        half = slice(DH_A * (h % 2), DH_A * (h % 2 + 1))
        m = jnp.maximum(s1.max(axis=-1, keepdims=True), s2.max(axis=-1, keepdims=True))
        p1 = jnp.exp2(s1 - m)
        p2 = jnp.exp2(s2 - m)
        l = p1.sum(axis=-1, keepdims=True) + p2.sum(axis=-1, keepdims=True)
        o = (_dot(p1.astype(BF16), cv_ref[0, 0, :, pair])
             + _dot(p2.astype(BF16), kv_ref[0, :, vpair].astype(BF16)))
        o_ref[0, :, DH_A * h:DH_A * (h + 1)] = (o[:, half] / l).astype(o_ref.dtype)

    pending = scores(0)
    for h in range(H_A):
        upcoming = scores(h + 1) if h + 1 < H_A else None
        finish(h, pending)
        pending = upcoming


def _attn_sample(qk, kv, cache_k, cache_v, layer, bias):
    bsz = qk.shape[0]
    cache_spec = pl.BlockSpec((1, 1, WINDOW, WIDTH_A), lambda b: (layer, b, 0, 0))
    return pl.pallas_call(
        _attn_sample_kernel,
        grid=(bsz,),
        in_specs=[pl.BlockSpec((1, CHUNK, WIDTH_A), lambda b: (b, 0, 0)),
                  pl.BlockSpec((1, CHUNK, 2 * WIDTH_A), lambda b: (b, 0, 0)),
                  cache_spec, cache_spec,
                  pl.BlockSpec((H_A, CHUNK, BAND), lambda b: (0, 0, 0))],
        out_specs=pl.BlockSpec((1, CHUNK, WIDTH_A), lambda b: (b, 0, 0)),
        out_shape=jax.ShapeDtypeStruct((bsz, CHUNK, WIDTH_A), BF16),
        compiler_params=_params("parallel"),
    )(qk, kv, cache_k, cache_v, bias)


def _scan_kernel(bqk_ref, bv_ref, bog_ref, cz_ref, cx_ref, misc_ref,
                 gup_ref, gb_ref, gnw_ref, cw_ref, cb_ref, dtb_ref, alog_ref, dsk_ref, snw_ref,
                 gla0_ref, conv0_ref, ssm0_ref,
                 gla_o_ref, ssm_o_ref, gla_s_ref, conv_s_ref, ssm_s_ref,
                 xpad_ref):
    c = pl.program_id(1)
    nseq = bqk_ref.shape[0]

    @pl.when(c == 0)
    def _():
        gla_s_ref[...] = gla0_ref[...]
        conv_s_ref[...] = conv0_ref[...]
        ssm_s_ref[...] = ssm0_ref[...]

    r64 = lax.broadcasted_iota(jnp.int32, (CHUNK, CHUNK), 0)
    c64 = lax.broadcasted_iota(jnp.int32, (CHUNK, CHUNK), 1)
    tri = c64 <= r64
    tri_bf = jnp.where(tri, 1.0, 0.0).astype(BF16)
    ones_bf = jnp.ones((CHUNK, CHUNK), BF16)
    row_x = lax.broadcasted_iota(jnp.int32, (CHUNK, WIDTH_C), 0)
    s_x = lax.broadcasted_iota(jnp.int32, (CHUNK, WIDTH_C), 1) & (CHUNK - 1)
    causal_x = s_x <= row_x
    upper_x = row_x <= s_x
    er = lax.broadcasted_iota(jnp.int32, (128, WIDTH_C), 0)
    el = lax.broadcasted_iota(jnp.int32, (128, WIDTH_C), 1)
    expand_bf = jnp.where(er == MISC_DT_LANE + (el >> 6), 1.0, 0.0).astype(BF16)
    br = lax.broadcasted_iota(jnp.int32, (WIDTH_C, WIDTH_C), 0)
    bl = lax.broadcasted_iota(jnp.int32, (WIDTH_C, WIDTH_C), 1)
    blk64_bf = jnp.where((br >> 6) == (bl >> 6), 1.0, 0.0).astype(BF16)
    blk128_bf = jnp.where((br >> 7) == (bl >> 7), 1.0, 0.0).astype(BF16)

    gup = gup_ref[...]
    gb = gb_ref[...]
    gnw = gnw_ref[...]
    cb = cb_ref[...]
    dtb = dtb_ref[...]
    neg_a = -jnp.exp(alog_ref[...])
    dsk = dsk_ref[...]
    snw = snw_ref[...]

    def load_seq(b):
        return dict(misc=misc_ref[b], qk=bqk_ref[b], v=bv_ref[b], bog=bog_ref[b], cz=cz_ref[b],
                    cx=cx_ref[b], gla=gla_s_ref[b], conv=conv_s_ref[b],
                    ssm=[ssm_s_ref[b, h] for h in range(H_C)])

    def store_seq(b, out):
        gla_o_ref[b] = out["gla_o"]
        ssm_o_ref[b] = out["ssm_o"]
        gla_s_ref[b] = out["gla"]
        conv_s_ref[b] = out["conv"]
        for h in range(H_C):
            ssm_s_ref[b, h] = out["ssm"][h]

    def compute_seq(inp, slot):
        out = {}
        misc = inp["misc"]

        z = _dot(misc.astype(BF16), gup) + gb
        g = (jnp.minimum(z, 0.0) - jnp.log1p(jnp.exp(-jnp.abs(z)))) * (1.0 / GATE_TAU)
        bcum = _dot01_left(tri_bf, g)
        yield
        q = inp["qk"][:, :H_B * DK_B]
        k = inp["qk"][:, H_B * DK_B:]
        blast = bcum[CHUNK - 1:CHUNK, :]
        qb = (q * (DK_B ** -0.5) * jnp.exp(bcum)).astype(BF16)
        kb = (k * jnp.exp(-bcum)).astype(BF16)
        kl = (k * jnp.exp(blast - bcum)).astype(BF16)
        vb = inp["v"].astype(BF16)
        st = inp["gla"]
        st_bf = st.astype(BF16)
        yield
        o_parts, s_parts = [], []
        for h in range(H_B):
            ks = slice(DK_B * h, DK_B * (h + 1))
            vs = slice(DV_B * h, DV_B * (h + 1))
            att = jnp.where(tri, _dot_nt(qb[:, ks], kb[:, ks]), 0.0).astype(BF16)
            o_parts.append(_dot(att, vb[:, vs]) + _dot_nt(qb[:, ks], st_bf[:, ks]))
            s_parts.append(_dot_tn(vb[:, vs], kl[:, ks]))
            yield
        o = jnp.concatenate(o_parts, axis=1)
        out["gla"] = jnp.exp(blast) * st + jnp.concatenate(s_parts, axis=1)
        msq = _dot01_right(o * o, blk64_bf) * (1.0 / DV_B)
        yield
        go = o * lax.rsqrt(msq + EPS) * gnw * _silu(inp["bog"])
        out["gla_o"] = go.astype(gla_o_ref.dtype)
        yield

        cx = inp["cx"]
        xpad_ref[slot, 0:8, :] = inp["conv"]
        xpad_ref[slot, 8:8 + CHUNK, :] = cx
        conv = cb + cx * cw_ref[CONV_W - 1:CONV_W, :]
        for i in range(CONV_W - 1):
            off = 8 - (CONV_W - 1) + i
            conv = conv + xpad_ref[slot, off:off + CHUNK, :] * cw_ref[i:i + 1, :]
        out["conv"] = cx[CHUNK - 8:, :]
        conv = _silu(conv)
        sx = conv[:, :WIDTH_C]
        sb_bf = conv[:, WIDTH_C:WIDTH_C + N_GROUPS_C * D_STATE].astype(BF16)
        sc_bf = conv[:, WIDTH_C + N_GROUPS_C * D_STATE:].astype(BF16)
        yield

        dtv = _softplus(misc + dtb)
        dt_x = _dot01_right(dtv, expand_bf)
        yield
        a_x = neg_a * dt_x
        cs_x = _dot01_left(tri_bf, a_x)
        cs_row = _dot01_left(ones_bf, jnp.where(upper_x, a_x, 0.0))
        yield
        decay = jnp.exp(jnp.where(causal_x, cs_x - cs_row, -jnp.inf))
        cb_parts = []
        for gidx in range(N_GROUPS_C):
            ns = slice(D_STATE * gidx, D_STATE * (gidx + 1))
            cbm = _dot_nt(sc_bf[:, ns], sb_bf[:, ns])
            cb_parts += [cbm] * (H_C // N_GROUPS_C)
            yield
        scores = (jnp.concatenate(cb_parts, axis=1) * decay).astype(BF16)
        xdt = (sx * dt_x).astype(BF16)
        ecs = jnp.exp(cs_x)
        cs_last = cs_x[CHUNK - 1:CHUNK, :]
        xw = (sx * (jnp.exp(cs_last - cs_x) * dt_x)).astype(BF16)
        e_last = jnp.exp(cs_last)
        yield
        y_parts = []
        out["ssm"] = []
        for h in range(H_C):
            ps = slice(P_C * h, P_C * (h + 1))
            gi = h // (H_C // N_GROUPS_C)
            ns = slice(D_STATE * gi, D_STATE * (gi + 1))
            hst = inp["ssm"][h]
            y_parts.append(_dot(scores[:, ps], xdt[:, ps])
                           + _dot_nt(sc_bf[:, ns], hst.astype(BF16)) * ecs[:, ps])
            out["ssm"].append(e_last[:, ps] * hst + _dot_tn(xw[:, ps], sb_bf[:, ns]))
            yield
        y = jnp.concatenate(y_parts, axis=1) + dsk * sx
        y = y * _silu(inp["cz"])
        ymsq = _dot01_right(y * y, blk128_bf) * (1.0 / (WIDTH_C // N_GROUPS_C))
        yield
        out["ssm_o"] = (y * lax.rsqrt(ymsq + EPS) * snw).astype(ssm_o_ref.dtype)
        return out

    def group_body(i, carry):
        seqs = [i * SEQ_UNROLL + u for u in range(SEQ_UNROLL)]
        inputs = [load_seq(b) for b in seqs]
        outputs = _round_robin([compute_seq(inp, slot) for slot, inp in enumerate(inputs)])
        for b, out in zip(seqs, outputs):
            store_seq(b, out)
        return carry

    lax.fori_loop(0, nseq // SEQ_UNROLL, group_body, 0)


def _scan(proj, lw, gla0, conv0, ssm0):
    bsz, l, _ = proj.shape
    nc = l // CHUNK
    sg = min(SEQ_GROUP, bsz)
    ng = bsz // sg

    def col(width, idx):
        return pl.BlockSpec((sg, CHUNK, width), lambda g, c: (g, c, idx))

    def const(shape):
        return pl.BlockSpec(shape, lambda g, c: (0,) * len(shape))

    def state(shape):
        return pl.BlockSpec((sg,) + shape, lambda g, c: (g,) + (0,) * len(shape))

    gla_shape = (DV_B, H_B * DK_B)
    conv_shape = (8, CONV_CH)
    ssm_shape = (H_C, P_C, D_STATE)
    return pl.pallas_call(
        _scan_kernel,
        grid=(ng, nc),
        in_specs=[col(256, 0), col(256, 1), col(256, 2), col(256, 3), col(512, 2), col(128, 12),
                  const((128, 128)), const((1, 128)), const((1, WIDTH_B)),
                  const((CONV_W, CONV_CH)), const((1, CONV_CH)),
                  const((1, 128)), const((1, WIDTH_C)), const((1, WIDTH_C)), const((1, WIDTH_C)),
                  state(gla_shape), state(conv_shape), state(ssm_shape)],
        out_specs=[pl.BlockSpec((sg, CHUNK, WIDTH_B), lambda g, c: (g, c, 0)),
                   pl.BlockSpec((sg, CHUNK, WIDTH_C), lambda g, c: (g, c, 0)),
                   state(gla_shape), state(conv_shape), state(ssm_shape)],
        out_shape=[jax.ShapeDtypeStruct((bsz, l, WIDTH_B), BF16),
                   jax.ShapeDtypeStruct((bsz, l, WIDTH_C), BF16),
                   jax.ShapeDtypeStruct((bsz,) + gla_shape, F32),
                   jax.ShapeDtypeStruct((bsz,) + conv_shape, F32),
                   jax.ShapeDtypeStruct((bsz,) + ssm_shape, F32)],
        scratch_shapes=[pltpu.VMEM((SEQ_UNROLL, 8 + CHUNK, CONV_CH), F32)],
        compiler_params=_params("parallel", "arbitrary"),
    )(proj, proj, proj, proj, proj, proj,
      lw["gup"], lw["gb"], lw["gnw"], lw["cw"], lw["cb"], lw["dtb"], lw["alog"], lw["dsk"], lw["snw"],
      gla0, conv0, ssm0)


OUT_TM = 512
OUT_SUB = 256


def _route(logits):
    lane = lax.broadcasted_iota(jnp.int32, logits.shape, 1)
    lane_f = lane.astype(F32)
    neg = -jnp.inf
    big = 1000.0
    gmask = (lane >= ROUTER_GROUP_LANE) & (lane < ROUTER_GROUP_LANE + N_EXPERT_GROUPS)
    lg = jnp.where(gmask, logits, neg)
    gmax = lg.max(axis=-1, keepdims=True)
    g_idx = jnp.where(lg == gmax, lane_f, big).min(axis=-1, keepdims=True) - ROUTER_GROUP_LANE
    p_grp = 1.0 / jnp.where(gmask, jnp.exp(logits - gmax), 0.0).sum(axis=-1, keepdims=True)
    emask = (lane < N_EXPERTS) & ((lane >> 2).astype(F32) == g_idx)
    le = jnp.where(emask, logits, neg)
    v1 = le.max(axis=-1, keepdims=True)
    i1 = jnp.where(le == v1, lane_f, big).min(axis=-1, keepdims=True)
    le2 = jnp.where(lane_f == i1, neg, le)
    v2 = le2.max(axis=-1, keepdims=True)
    i2 = jnp.where(le2 == v2, lane_f, big).min(axis=-1, keepdims=True)
    e2 = jnp.exp(v2 - v1)
    den = 1.0 + e2
    return (jnp.where(lane_f == i1, (1.0 / den) * p_grp, 0.0)
            + jnp.where(lane_f == i2, (e2 / den) * p_grp, 0.0)
            + jnp.where(lane == ROUTER_GIDX_LANE, g_idx, 0.0))


def _out_proj_kernel(x_ref, att_ref, gla_ref, ssm_ref, w_out_ref, nw_ref, rwh_ref, rwl_ref, rb_ref,
                     x1_ref, hn_ref, gate_ref, wo_ref):
    @pl.when(pl.program_id(0) == 0)
    def _():
        for r in range(0, D_MODEL, PREP_ROWS):
            wo_ref[r:r + PREP_ROWS, :] = w_out_ref[0, r:r + PREP_ROWS, :].astype(BF16)

    def sub_tile(r0):
        rows = slice(r0, r0 + OUT_SUB)
        mo = (_dot(att_ref[rows, :], wo_ref[0:WIDTH_A, :])
              + _dot(gla_ref[rows, :], wo_ref[WIDTH_A:WIDTH_A + WIDTH_B, :])
              + _dot(ssm_ref[rows, :], wo_ref[WIDTH_A + WIDTH_B:, :]))
        yield
        x1 = x_ref[rows, :] + mo
        x1_ref[rows, :] = x1
        ms = jnp.mean(x1 * x1, axis=-1, keepdims=True)
        hn = x1 * lax.rsqrt(ms + EPS) * nw_ref[...]
        hn_ref[rows, :] = hn.astype(hn_ref.dtype)
        h_hi = hn.astype(BF16)
        h_lo = (hn - h_hi.astype(F32)).astype(BF16)
        yield
        logits = (_dot(h_hi, rwh_ref[...]) + _dot(h_lo, rwh_ref[...]) + _dot(h_hi, rwl_ref[...])
                  + rb_ref[...])
        yield
        gate_ref[rows, :] = _route(logits)

    _round_robin([sub_tile(r0) for r0 in range(0, OUT_TM, OUT_SUB)])


def _out_proj(x, att, gla, ssm, lw, layer):
    t = x.shape[0]

    def rows(width):
        return pl.BlockSpec((OUT_TM, width), lambda i: (i, 0))

    def const(shape):
        return pl.BlockSpec(shape, lambda i: (0, 0))

    return pl.pallas_call(
        _out_proj_kernel,
        grid=(t // OUT_TM,),
        in_specs=[rows(D_MODEL), rows(WIDTH_A), rows(WIDTH_B), rows(WIDTH_C),
                  pl.BlockSpec((1, D_MODEL, D_MODEL), lambda i: (layer, 0, 0)), const((1, D_MODEL)),
                  const((D_MODEL, 128)), const((D_MODEL, 128)), const((1, 128))],
        out_specs=[rows(D_MODEL), rows(D_MODEL), rows(128)],
        out_shape=[jax.ShapeDtypeStruct((t, D_MODEL), F32),
                   jax.ShapeDtypeStruct((t, D_MODEL), BF16),
                   jax.ShapeDtypeStruct((t, 128), F32)],
        scratch_shapes=[pltpu.VMEM((D_MODEL, D_MODEL), BF16)],
        compiler_params=_params("arbitrary"),
    )(x, att, gla, ssm, lw["wo"], lw["n2w"], lw["rw_hi"], lw["rw_lo"], lw["rb"])


RT_TS = 256
RT_ALIGN = 16
RT_ROWS = 384
RT_SIZES = (256, 128, 64, 32, 16)
PAY_COLS = D_MODEL + 256
MOE_TM = 512


def _moe_max_tiles(t):
    return (t + (t // RT_TS) * N_EXPERT_GROUPS * (RT_ALIGN - 1)) // MOE_TM + N_EXPERT_GROUPS


def _sort_matrix(gate, c_ref, tile):
    lane = lax.broadcasted_iota(jnp.int32, (RT_TS, 128), 1)
    gidx = jnp.where(lane == ROUTER_GIDX_LANE, gate, 0.0).sum(axis=1, keepdims=True)
    onehot = lane.astype(F32) == gidx
    r = lax.broadcasted_iota(jnp.int32, (RT_TS, RT_TS), 0)
    c = lax.broadcasted_iota(jnp.int32, (RT_TS, RT_TS), 1)
    earlier = jnp.where(c < r, 1.0, 0.0).astype(BF16)
    before = _dot(earlier, jnp.where(onehot, 1.0, 0.0).astype(BF16))
    start = jnp.zeros((RT_TS, 128), F32)
    for g in range(N_EXPERT_GROUPS):
        start = jnp.where(lane == g, c_ref[N_EXPERT_GROUPS * tile + g].astype(F32), start)
    dest = jnp.where(onehot, before + start, 0.0).sum(axis=1, keepdims=True)
    rows = lax.broadcasted_iota(jnp.int32, (RT_TS, RT_ROWS), 1).astype(F32)
    return jnp.where(rows == dest, 1.0, 0.0).astype(BF16)


def _piece_copies(length, local, glob, make_copy, act):
    for size in RT_SIZES:
        take = (length & size) != 0

        @pl.when(take)
        def _(local=local, glob=glob, size=size):
            act(make_copy(pl.multiple_of(local, RT_ALIGN), pl.multiple_of(glob, RT_ALIGN), size))

        step = jnp.where(take, size, 0)
        local = local + step
        glob = glob + step


def _run_copies(c_ref, lp_ref, off_ref, tile, make_copy, act):
    for g in range(N_EXPERT_GROUPS):
        k = N_EXPERT_GROUPS * tile + g
        _piece_copies(lp_ref[k], c_ref[k], off_ref[k], make_copy, act)


def _dispatch_kernel(c_ref, lp_ref, off_ref, tail_ref, nact_ref, hn_ref, gate_ref, xs_hbm,
                     y_ref, zero_ref, sem, zsem, *, first_spare, max_tiles):
    i = pl.program_id(0)
    nt = pl.num_programs(0)
    slot = i % 2
    gate = gate_ref[...]
    pt = _sort_matrix(gate, c_ref, i)
    g_hi = gate.astype(BF16)
    g_lo = (gate - g_hi.astype(F32)).astype(BF16)
    y_ref[slot, :, 0:D_MODEL] = _dot_tn(pt, hn_ref[...]).astype(BF16)
    y_ref[slot, :, D_MODEL:D_MODEL + 128] = _dot_tn(pt, g_hi).astype(BF16)
    y_ref[slot, :, D_MODEL + 128:] = _dot_tn(pt, g_lo).astype(BF16)

    def copy_from(s):
        return lambda local, glob, size: pltpu.make_async_copy(
            y_ref.at[s, pl.ds(local, size)], xs_hbm.at[pl.ds(glob, size)], sem.at[s])

    _run_copies(c_ref, lp_ref, off_ref, i, copy_from(slot), lambda cp: cp.start())

    @pl.when(i > 0)
    def _():
        _run_copies(c_ref, lp_ref, off_ref, i - 1, copy_from(1 - slot), lambda cp: cp.wait())

    @pl.when(i == nt - 1)
    def _():
        _run_copies(c_ref, lp_ref, off_ref, i, copy_from(slot), lambda cp: cp.wait())
        zero_ref[...] = jnp.zeros_like(zero_ref)

        def zero_copy(local, glob, size):
            return pltpu.make_async_copy(zero_ref.at[pl.ds(local, size)],
                                         xs_hbm.at[pl.ds(glob, size)], zsem)

        def zero_fill(act):
            for g in range(N_EXPERT_GROUPS):
                _piece_copies(tail_ref[N_EXPERT_GROUPS + g], 0, tail_ref[g], zero_copy, act)
            for j in range(first_spare, max_tiles):
                @pl.when(j >= nact_ref[0])
                def _(j=j):
                    act(zero_copy(0, j * MOE_TM, MOE_TM))

        zero_fill(lambda cp: cp.start())
        zero_fill(lambda cp: cp.wait())


def _moe_group_kernel(grp_ref, nact_ref, xs_ref, wg_ref, wu_ref, wd_ref, o_ref, wg_bf, wu_bf, wd_bf):
    j = pl.program_id(0)

    @pl.when(jnp.logical_or(j == 0, grp_ref[j] != grp_ref[jnp.maximum(j - 1, 0)]))
    def _():
        wg_bf[...] = wg_ref[0].astype(BF16)
        wu_bf[...] = wu_ref[0].astype(BF16)
        wd_bf[...] = wd_ref[0].astype(BF16)

    @pl.when(j >= nact_ref[0])
    def _():
        o_ref[...] = jnp.zeros_like(o_ref)

    @pl.when(j < nact_ref[0])
    def _():
        x = xs_ref[:, 0:D_MODEL]
        gate = (xs_ref[:, D_MODEL:D_MODEL + 128].astype(F32)
                + xs_ref[:, D_MODEL + 128:].astype(F32))
        lane = lax.broadcasted_iota(jnp.int32, gate.shape, 1)
        first = grp_ref[j] * EXPERTS_PER_GROUP
        acc = None
        for e in range(EXPERTS_PER_GROUP):
            ge = jnp.where(lane == first + e, gate, 0.0).sum(axis=-1, keepdims=True)
            hid = _silu(_dot(x, wg_bf[e])) * _dot(x, wu_bf[e]) * ge
            part = _dot(hid.astype(BF16), wd_bf[e])
            acc = part if acc is None else acc + part
        o_ref[...] = acc


def _combine_kernel(c_ref, lp_ref, off_ref, x_ref, gate_ref, fw_ref, ys_hbm, o_ref, z_ref, sem,
                    *, final_norm):
    i = pl.program_id(0)
    nt = pl.num_programs(0)
    slot = i % 2

    def copy_into(s):
        return lambda local, glob, size: pltpu.make_async_copy(
            ys_hbm.at[pl.ds(glob, size)], z_ref.at[s, pl.ds(local, size)], sem.at[s])

    @pl.when(i == 0)
    def _():
        z_ref[...] = jnp.zeros_like(z_ref)
        _run_copies(c_ref, lp_ref, off_ref, 0, copy_into(0), lambda cp: cp.start())

    @pl.when(i + 1 < nt)
    def _():
        _run_copies(c_ref, lp_ref, off_ref, i + 1, copy_into(1 - slot), lambda cp: cp.start())

    _run_copies(c_ref, lp_ref, off_ref, i, copy_into(slot), lambda cp: cp.wait())
    pt = _sort_matrix(gate_ref[...], c_ref, i)
    z = z_ref[slot]
    z_hi = z.astype(BF16)
    z_lo = (z - z_hi.astype(F32)).astype(BF16)
    y = x_ref[...] + _dot(pt, z_hi) + _dot(pt, z_lo)
    if final_norm:
        ms = jnp.mean(y * y, axis=-1, keepdims=True)
        y = y * lax.rsqrt(ms + EPS) * fw_ref[...]
    o_ref[...] = y


def _route_meta(gate, max_tiles):
    t = gate.shape[0]
    nt = t // RT_TS
    groups = jnp.arange(N_EXPERT_GROUPS, dtype=jnp.int32)
    g = gate[:, ROUTER_GIDX_LANE].astype(jnp.int32).reshape(nt, RT_TS, 1)
    count = (g == groups).astype(jnp.int32).sum(axis=1)
    lp = (count + RT_ALIGN - 1) // RT_ALIGN * RT_ALIGN
    total = lp.sum(axis=0)
    ntile = (total + MOE_TM - 1) // MOE_TM
    last = jnp.cumsum(ntile)
    first_row = (last - ntile) * MOE_TM
    c = jnp.cumsum(lp, axis=1) - lp
    off = first_row[None, :] + jnp.cumsum(lp, axis=0) - lp
    tail = jnp.concatenate([first_row + total, ntile * MOE_TM - total])
    j = jnp.arange(max_tiles, dtype=jnp.int32)[:, None]
    grp = jnp.minimum((j >= last[None, :]).astype(jnp.int32).sum(axis=1), N_EXPERT_GROUPS - 1)
    return c.reshape(-1), lp.reshape(-1), off.reshape(-1), tail, grp, last[-1:]


def _moe(x, hn, gate, lw, layer, fw, final_norm):
    t = x.shape[0]
    nt = t // RT_TS
    max_tiles = _moe_max_tiles(t)
    c, lp, off, tail, grp, nact = _route_meta(gate, max_tiles)

    def rows(width):
        return pl.BlockSpec((RT_TS, width), lambda i, *_: (i, 0))

    xs = pl.pallas_call(
        functools.partial(_dispatch_kernel, first_spare=t // MOE_TM, max_tiles=max_tiles),
        grid_spec=pltpu.PrefetchScalarGridSpec(
            num_scalar_prefetch=5, grid=(nt,),
            in_specs=[rows(D_MODEL), rows(128)],
            out_specs=pl.BlockSpec(memory_space=pl.ANY),
            scratch_shapes=[pltpu.VMEM((2, RT_ROWS, PAY_COLS), BF16),
                            pltpu.VMEM((MOE_TM, PAY_COLS), BF16),
                            pltpu.SemaphoreType.DMA((2,)), pltpu.SemaphoreType.DMA(())]),
        out_shape=jax.ShapeDtypeStruct((max_tiles * MOE_TM, PAY_COLS), BF16),
        compiler_params=_params("arbitrary"),
    )(c, lp, off, tail, nact, hn, gate)

    def experts(shape):
        return pl.BlockSpec((1, EXPERTS_PER_GROUP) + shape, lambda j, grp, nact: (layer, grp[j], 0, 0))

    def rounded(shape):
        return pltpu.VMEM((EXPERTS_PER_GROUP,) + shape, BF16)

    ys = pl.pallas_call(
        _moe_group_kernel,
        grid_spec=pltpu.PrefetchScalarGridSpec(
            num_scalar_prefetch=2, grid=(max_tiles,),
            in_specs=[pl.BlockSpec((MOE_TM, PAY_COLS), lambda j, grp, nact: (j, 0)),
                      experts((D_MODEL, D_EXPERT)), experts((D_MODEL, D_EXPERT)),
                      experts((D_EXPERT, D_MODEL))],
            out_specs=pl.BlockSpec((MOE_TM, D_MODEL), lambda j, grp, nact: (j, 0)),
            scratch_shapes=[rounded((D_MODEL, D_EXPERT)), rounded((D_MODEL, D_EXPERT)),
                            rounded((D_EXPERT, D_MODEL))]),
        out_shape=jax.ShapeDtypeStruct((max_tiles * MOE_TM, D_MODEL), F32),
        compiler_params=_params("arbitrary"),
    )(grp, nact, xs, lw["wg"], lw["wu"], lw["wd"])

    return pl.pallas_call(
        functools.partial(_combine_kernel, final_norm=final_norm),
        grid_spec=pltpu.PrefetchScalarGridSpec(
            num_scalar_prefetch=3, grid=(nt,),
            in_specs=[rows(D_MODEL), rows(128), pl.BlockSpec((1, D_MODEL), lambda i, *_: (0, 0)),
                      pl.BlockSpec(memory_space=pl.ANY)],
            out_specs=rows(D_MODEL),
            scratch_shapes=[pltpu.VMEM((2, RT_ROWS, D_MODEL), F32),
                            pltpu.SemaphoreType.DMA((2,))]),
        out_shape=jax.ShapeDtypeStruct((t, D_MODEL), F32),
        compiler_params=_params("arbitrary"),
    )(c, lp, off, x, gate, fw, ys)


def _lane_place(vec, start, width=128):
    return jnp.zeros((1, width), F32).at[0, start:start + vec.shape[0]].set(vec.astype(F32))


REL_PAD = 704


def _toeplitz(v, rows, cols):
    n = v.shape[-1]
    tiled = jnp.tile(v, (1,) * (v.ndim - 1) + (rows,))[..., :rows * (n - 1)]
    return tiled.reshape(v.shape[:-1] + (rows, n - 1))[..., :cols]


def _band_bias(table):
    t = table.astype(F32)
    ext = jnp.concatenate([jnp.repeat(t[:, :1], REL_PAD, axis=1), t,
                           jnp.repeat(t[:, -1:], REL_PAD, axis=1)], axis=1)
    extr = ext[:, ::-1]
    top = ext.shape[1] - 1 - (REL_CLIP + REL_PAD)

    def band(off, nq, nk):
        n = nq + nk
        v = jnp.concatenate([extr[:, top - off:top - off + nk + 1],
                             extr[:, top - off - (nq - 1):top - off]], axis=1)
        assert v.shape[1] == n
        return _toeplitz(v, nq, nk)

    sample = band(WINDOW, CHUNK, BAND)
    kc = (jnp.arange(ATT_KB) // CHUNK)[:, None]
    qc = (jnp.arange(ATT_QB) // CHUNK)[None, :]
    block = lambda off: jnp.swapaxes(band(off, ATT_QB, ATT_KB), 1, 2)
    prompt = jnp.stack([jnp.where((kc >= qc)[None], block(2 * ATT_KB), -1e30),
                        block(ATT_KB),
                        jnp.where((kc <= qc)[None], block(0), -1e30),
                        jnp.full((H_A, ATT_KB, ATT_QB), -1e30, F32)])
    return sample * LOG2_E, prompt * LOG2_E


def _layer_weights(i, norm1_w, w_in, rel_bias_table, gla_w_gate_up, gla_b_gate, gla_norm_w,
                   ssm_conv_w, ssm_conv_b, ssm_dt_bias, ssm_a_log, ssm_d, ssm_norm_w, w_out,
                   norm2_w, router_group_w, router_group_b, router_expert_w, router_expert_b,
                   exp_w_gate, exp_w_up, exp_w_down):
    bias_s, bias_p = _band_bias(rel_bias_table[i])
    rw = jnp.concatenate([router_expert_w[i], router_group_w[i],
                          jnp.zeros((D_MODEL, 128 - N_EXPERTS - N_EXPERT_GROUPS), F32)], axis=1)
    rw_hi = rw.astype(BF16)
    return dict(
        n1w=norm1_w[i][None, :],
        w_in=w_in,
        bias_s=bias_s,
        bias_p=bias_p,
        gup=jnp.zeros((128, 128), F32).at[:GATE_RANK, :].set(gla_w_gate_up[i]).astype(BF16),
        gb=gla_b_gate[i][None, :].astype(F32),
        gnw=jnp.tile(gla_norm_w[i], H_B)[None, :].astype(F32),
        cw=ssm_conv_w[i].astype(F32),
        cb=ssm_conv_b[i][None, :].astype(F32),
        dtb=_lane_place(ssm_dt_bias[i], MISC_DT_LANE),
        alog=jnp.repeat(ssm_a_log[i].astype(F32), P_C)[None, :],
        dsk=jnp.repeat(ssm_d[i].astype(F32), P_C)[None, :],
        snw=ssm_norm_w[i][None, :].astype(F32),
        wo=w_out,
        n2w=norm2_w[i][None, :],
        rw_hi=rw_hi,
        rw_lo=(rw - rw_hi.astype(F32)).astype(BF16),
        rb=jnp.concatenate([router_expert_b[i], router_group_b[i],
                            jnp.zeros((128 - N_EXPERTS - N_EXPERT_GROUPS,), F32)])[None, :],
        wg=exp_w_gate,
        wu=exp_w_up,
        wd=exp_w_down,
    )


def _stream_layer(x, bsz, lw, layer, cache_k, cache_v, gla0, conv0, ssm0, fw, final_norm):
    l = x.shape[0] // bsz
    if cache_k is None:
        qk, rest, k_keep, v_keep, vt = _in_proj(x, lw["n1w"], lw["w_in"], layer, True, l)
        att = _attn_prompt(qk.reshape(bsz, l, 2 * WIDTH_A),
                           vt.reshape(bsz, l // ATT_KB, WIDTH_A, ATT_KB), lw["bias_p"])
    else:
        qk, rest, k_keep, v_keep, kv = _in_proj(x, lw["n1w"], lw["w_in"], layer, False, l)
        att = _attn_sample(qk.reshape(bsz, l, 2 * WIDTH_A), kv.reshape(bsz, l, 2 * WIDTH_A),
                           cache_k, cache_v, layer, lw["bias_s"])
    gla_o, ssm_o, gla_s, conv_s, ssm_s = _scan(rest.reshape(bsz, l, REST_COLS), lw, gla0, conv0, ssm0)
    t = bsz * l
    x1, hn, gate = _out_proj(x, att.reshape(t, WIDTH_A), gla_o.reshape(t, WIDTH_B),
                             ssm_o.reshape(t, WIDTH_C), lw, layer)
    x2 = _moe(x1, hn, gate, lw, layer, fw, final_norm)
    keep = min(WINDOW, l)
    k_keep = k_keep.reshape(bsz, keep, H_A, DH_A)
    v_keep = v_keep.reshape(bsz, keep, H_A, DH_A)
    gla_state = gla_s.reshape(bsz, DV_B, H_B, DK_B).transpose(0, 2, 3, 1)
    conv_state = conv_s[:, 8 - (CONV_W - 1):, :]
    return x2, k_keep, v_keep, gla_state, conv_state, ssm_s


def kernel(x_prompt, x_sample, cache_k_a, cache_v_a, state_gla, state_conv, state_ssm, norm1_w, w_in, rel_bias_table, gla_w_gate_up, gla_b_gate, gla_norm_w, ssm_conv_w, ssm_conv_b, ssm_dt_bias, ssm_a_log, ssm_d, ssm_norm_w, w_out, norm2_w, router_group_w, router_group_b, router_expert_w, router_expert_b, exp_w_gate, exp_w_up, exp_w_down, final_norm_w):
    bp, sp, _ = x_prompt.shape
    bs, ss, _ = x_sample.shape
    xp = x_prompt.reshape(bp * sp, D_MODEL)
    xs = x_sample.reshape(bs * ss, D_MODEL)
    fw = final_norm_w[None, :].astype(F32)
    outs_p, outs_s = [], []
    cache_k = cache_k_a.astype(BF16).reshape(DEPTH, bs, -1, WIDTH_A)
    cache_v = cache_v_a.astype(BF16).reshape(DEPTH, bs, -1, WIDTH_A)
    for i in range(DEPTH):
        lw = _layer_weights(i, norm1_w, w_in, rel_bias_table, gla_w_gate_up, gla_b_gate, gla_norm_w,
                            ssm_conv_w, ssm_conv_b, ssm_dt_bias, ssm_a_log, ssm_d, ssm_norm_w, w_out,
                            norm2_w, router_group_w, router_group_b, router_expert_w, router_expert_b,
                            exp_w_gate, exp_w_up, exp_w_down)
        last = i == DEPTH - 1
        xp, *sp_out = _stream_layer(
            xp, bp, lw, i, None, None,
            jnp.zeros((bp, DV_B, H_B * DK_B), F32),
            jnp.zeros((bp, 8, CONV_CH), F32),
            jnp.zeros((bp, H_C, P_C, D_STATE), F32), fw, last)
        outs_p.append(sp_out)
        gla0 = state_gla[i].astype(F32).transpose(0, 3, 1, 2).reshape(bs, DV_B, H_B * DK_B)
        conv0 = jnp.pad(state_conv[i].astype(F32), ((0, 0), (8 - (CONV_W - 1), 0), (0, 0)))
        xs, *ss_out = _stream_layer(
            xs, bs, lw, i, cache_k, cache_v,
            gla0, conv0, state_ssm[i].astype(F32), fw, last)
        outs_s.append(ss_out)
    stack = lambda outs, j: jnp.stack([o[j] for o in outs])
    return (xp.reshape(bp, sp, D_MODEL), xs.reshape(bs, ss, D_MODEL),
            stack(outs_p, 0), stack(outs_p, 1), stack(outs_p, 2), stack(outs_p, 3), stack(outs_p, 4),
            stack(outs_s, 0), stack(outs_s, 1), stack(outs_s, 2), stack(outs_s, 3), stack(outs_s, 4))
```

```python
import functools

import jax
import jax.numpy as jnp
from jax import lax
from jax.experimental import pallas as pl
from jax.experimental.pallas import tpu as pltpu

F32 = jnp.float32
BF16 = jnp.bfloat16

D_MODEL = 1024
DEPTH = 2
EPS = 1e-6
CHUNK = 64
N_PAST_CHUNKS = 8
BAND = (N_PAST_CHUNKS + 1) * CHUNK
WINDOW = N_PAST_CHUNKS * CHUNK
H_A, DH_A, WIDTH_A = 8, 64, 512
REL_CLIP = 128
LOG2_E = 1.4426950408889634
H_B, DK_B, DV_B, WIDTH_B = 4, 32, 64, 256
GATE_RANK = 16
GATE_TAU = 16.0
H_C, P_C, WIDTH_C = 4, 64, 256
N_GROUPS_C = 2
D_STATE = 64
CONV_W = 4
CONV_CH = 512
N_EXPERT_GROUPS = 4
EXPERTS_PER_GROUP = 4
N_EXPERTS = 16
D_EXPERT = 256

PROJ_COLS = 3200
MISC_DT_LANE = GATE_RANK
ROUTER_GROUP_LANE = N_EXPERTS
ROUTER_GIDX_LANE = 20

SEQ_GROUP = 8
SEQ_UNROLL = 8
VMEM_LIMIT_BYTES = 56 * 1024 * 1024

NT_DIMS = (((1,), (1,)), ((), ()))
TN_DIMS = (((0,), (0,)), ((), ()))


def _params(*sem):
    return pltpu.CompilerParams(dimension_semantics=sem, vmem_limit_bytes=VMEM_LIMIT_BYTES)


def _dot(a, b):
    return jnp.dot(a, b, preferred_element_type=F32)


def _dot_nt(a, b):
    return lax.dot_general(a, b, NT_DIMS, preferred_element_type=F32)


def _dot_tn(a, b):
    return lax.dot_general(a, b, TN_DIMS, preferred_element_type=F32)


def _split2(a):
    hi = a.astype(BF16)
    return hi, (a - hi.astype(F32)).astype(BF16)


def _dot01_left(m01, a):
    hi, lo = _split2(a)
    return _dot(m01, hi) + _dot(m01, lo)


def _dot01_right(a, m01):
    hi, lo = _split2(a)
    return _dot(hi, m01) + _dot(lo, m01)


def _round_robin(gens):
    results = [None] * len(gens)
    live = list(range(len(gens)))
    while live:
        for idx in list(live):
            try:
                next(gens[idx])
            except StopIteration as stop:
                results[idx] = stop.value
                live.remove(idx)
    return results


def _silu(x):
    return x * jax.nn.sigmoid(x)


def _softplus(x):
    return jnp.maximum(x, 0.0) + jnp.log1p(jnp.exp(-jnp.abs(x)))


IN_TM = 256
QKV_COLS = 3 * WIDTH_A
REST_COLS = PROJ_COLS - QKV_COLS
REST_CHUNKS = ((0, 512), (512, 1024), (1024, REST_COLS))


IN_COLS = 3092
IN_TAIL = QKV_COLS + 512
PREP_ROWS = 128


def _prepare_w_in(w_in_ref, w_ref):
    scale = DH_A ** -0.5 * LOG2_E
    n_tail = REST_COLS - 512 - 128
    for r in range(0, D_MODEL, PREP_ROWS):
        rs = slice(r, r + PREP_ROWS)
        w_ref[rs, 0:WIDTH_A] = (w_in_ref[0, rs, 0:WIDTH_A] * scale).astype(BF16)
        w_ref[rs, WIDTH_A:IN_TAIL] = w_in_ref[0, rs, WIDTH_A:IN_TAIL].astype(BF16)
        tail = w_in_ref[0, rs, IN_TAIL:IN_COLS]
        w_ref[rs, IN_TAIL:IN_TAIL + n_tail] = tail[:, GATE_RANK:GATE_RANK + n_tail].astype(BF16)
        misc = jnp.concatenate([tail[:, 0:GATE_RANK], tail[:, GATE_RANK + n_tail:],
                                jnp.zeros((PREP_ROWS, 128 - GATE_RANK - H_C), F32)], axis=1)
        w_ref[rs, IN_TAIL + n_tail:] = misc.astype(BF16)


def _in_proj_kernel(x_ref, nw_ref, w_in_ref, *refs, with_vt):
    if with_vt:
        qk_ref, rest_ref, kk_ref, vk_ref, vt_ref, w_ref = refs
    else:
        qk_ref, rest_ref, kk_ref, vk_ref, kv_ref, w_ref = refs

    @pl.when(pl.program_id(0) == 0)
    def _():
        _prepare_w_in(w_in_ref, w_ref)

    x = x_ref[...]
    ms = jnp.mean(x * x, axis=-1, keepdims=True)
    xn = (x * lax.rsqrt(ms + EPS) * nw_ref[...]).astype(BF16)
    qk_ref[:, 0:WIDTH_A] = _dot(xn, w_ref[:, 0:WIDTH_A]).astype(BF16)
    k = _dot(xn, w_ref[:, WIDTH_A:2 * WIDTH_A])
    v = _dot(xn, w_ref[:, 2 * WIDTH_A:QKV_COLS])
    qk_ref[:, WIDTH_A:] = k.astype(BF16)
    if not with_vt:
        kv_ref[:, 0:WIDTH_A] = k
        kv_ref[:, WIDTH_A:] = v

    def keep_rows():
        kk = kk_ref.reshape(IN_TM * H_A, DH_A)
        vk = vk_ref.reshape(IN_TM * H_A, DH_A)
        for h in range(H_A):
            sl = slice(DH_A * h, DH_A * (h + 1))
            kk[pl.ds(h, IN_TM, stride=H_A), :] = k[:, sl]
            vk[pl.ds(h, IN_TM, stride=H_A), :] = v[:, sl]

    keep_rows()
    for lo, hi in REST_CHUNKS:
        rest_ref[:, lo:hi] = _dot(xn, w_ref[:, QKV_COLS + lo:QKV_COLS + hi])
    if with_vt:
        vt_ref[0] = v.T.astype(BF16)


def _in_proj(x, nw, w_in, layer, with_vt, seq_len):
    t = x.shape[0]
    tiles_per_seq = max(seq_len // IN_TM, 1)
    keep_tiles = max(min(WINDOW, seq_len) // IN_TM, 1)
    n_keep = t // IN_TM // tiles_per_seq * keep_tiles

    def rows(width):
        return pl.BlockSpec((IN_TM, width), lambda i: (i, 0))

    def kept(i):
        return (i // tiles_per_seq * keep_tiles
                + jnp.maximum(i % tiles_per_seq - (tiles_per_seq - keep_tiles), 0), 0, 0)

    keep_spec = pl.BlockSpec((IN_TM, H_A, DH_A), kept)
    keep_shape = jax.ShapeDtypeStruct((n_keep * IN_TM, H_A, DH_A), F32)
    in_specs = [rows(D_MODEL), pl.BlockSpec((1, D_MODEL), lambda i: (0, 0)),
                pl.BlockSpec((1, D_MODEL, IN_COLS), lambda i: (layer, 0, 0))]
    scratch = [pltpu.VMEM((D_MODEL, PROJ_COLS), BF16)]
    out_specs = [rows(2 * WIDTH_A), rows(REST_COLS), keep_spec, keep_spec]
    out_shape = [jax.ShapeDtypeStruct((t, 2 * WIDTH_A), BF16),
                 jax.ShapeDtypeStruct((t, REST_COLS), F32), keep_shape, keep_shape]
    if with_vt:
        out_specs.append(pl.BlockSpec((1, WIDTH_A, IN_TM), lambda i: (i, 0, 0)))
        out_shape.append(jax.ShapeDtypeStruct((t // IN_TM, WIDTH_A, IN_TM), BF16))
    else:
        out_specs.append(rows(2 * WIDTH_A))
        out_shape.append(jax.ShapeDtypeStruct((t, 2 * WIDTH_A), F32))
    return pl.pallas_call(
        functools.partial(_in_proj_kernel, with_vt=with_vt),
        grid=(t // IN_TM,),
        in_specs=in_specs, out_specs=out_specs, out_shape=out_shape, scratch_shapes=scratch,
        compiler_params=_params("arbitrary"),
    )(x, nw, w_in)


ATT_QB = 4 * CHUNK
ATT_KB = IN_TM
ATT_NKB = (ATT_QB + WINDOW) // ATT_KB


def _attn_prompt_kernel(q_ref, k_ref, vt_ref, bias_ref, o_ref, ot_ref):
    blk = pl.program_id(1)
    sb = jnp.maximum(blk - WINDOW // ATT_KB, 0)
    kstart = pl.multiple_of(sb * ATT_KB, ATT_KB)
    lane = lax.broadcasted_iota(jnp.int32, (ATT_QB, 128), 1)

    def attend(kinds):
        def scores(h):
            pair = slice(128 * (h // 2), 128 * (h // 2 + 1))
            qp = q_ref[0, :, pair]
            qz = jnp.where((lane >= DH_A) == bool(h % 2), qp, jnp.zeros_like(qp))
            return [_dot_nt(k_ref[0, pl.ds(kstart + ATT_KB * i, ATT_KB), pair], qz)
                    + bias_ref[kind, h] for i, kind in enumerate(kinds)]

        def finish(h, s):
            m = s[0].max(axis=0, keepdims=True)
            for si in s[1:]:
                m = jnp.maximum(m, si.max(axis=0, keepdims=True))
            l = None
            ot = None
            for i, si in enumerate(s):
                p = jnp.exp2(si - m)
                li = p.sum(axis=0, keepdims=True)
                oi = _dot(vt_ref[0, sb + i, DH_A * h:DH_A * (h + 1), :], p.astype(BF16))
                l = li if l is None else l + li
                ot = oi if ot is None else ot + oi
            ot_ref[DH_A * h:DH_A * (h + 1), :] = ot / l

        pending = scores(0)
        for h in range(H_A):
            upcoming = scores(h + 1) if h + 1 < H_A else None
            finish(h, pending)
            pending = upcoming
        o_ref[0] = ot_ref[...].T.astype(o_ref.dtype)

    pl.when(blk == 0)(lambda: attend((2,)))
    pl.when(blk == 1)(lambda: attend((1, 2)))
    pl.when(blk >= 2)(lambda: attend((0, 1, 2)))


def _attn_prompt(qk, vt, bias_t):
    bsz, s, _ = qk.shape
    nkb = s // ATT_KB
    return pl.pallas_call(
        _attn_prompt_kernel,
        grid=(bsz, s // ATT_QB),
        in_specs=[pl.BlockSpec((1, ATT_QB, WIDTH_A), lambda b, c: (b, c, 0)),
                  pl.BlockSpec((1, s, WIDTH_A), lambda b, c: (b, 0, 1)),
                  pl.BlockSpec((1, nkb, WIDTH_A, ATT_KB), lambda b, c: (b, 0, 0, 0)),
                  pl.BlockSpec((3, H_A, ATT_KB, ATT_QB), lambda b, c: (0, 0, 0, 0))],
        out_specs=pl.BlockSpec((1, ATT_QB, WIDTH_A), lambda b, c: (b, c, 0)),
        out_shape=jax.ShapeDtypeStruct((bsz, s, WIDTH_A), BF16),
        scratch_shapes=[pltpu.VMEM((WIDTH_A, ATT_QB), F32)],
        compiler_params=_params("parallel", "arbitrary"),
    )(qk, qk, vt, bias_t)


def _attn_sample_kernel(q_ref, kv_ref, ck_ref, cv_ref, bias_ref, o_ref):
    lane = lax.broadcasted_iota(jnp.int32, (CHUNK, 128), 1)

    def scores(h):
        pair = slice(128 * (h // 2), 128 * (h // 2 + 1))
        qp = q_ref[0, :, pair]
        qz = jnp.where((lane >= DH_A) == bool(h % 2), qp, jnp.zeros_like(qp))
        return (_dot_nt(qz, ck_ref[0, 0, :, pair]) + bias_ref[h, :, 0:WINDOW],
                _dot_nt(qz, kv_ref[0, :, pair].astype(BF16)) + bias_ref[h, :, WINDOW:BAND])

    def finish(h, s):
        s1, s2 = s
        pair = slice(128 * (h // 2), 128 * (h // 2 + 1))
        vpair = slice(WIDTH_A + 128 * (h // 2), WIDTH_A + 128 * (h // 2 + 1))
        half = slice(DH_A * (h % 2), DH_A * (h % 2 + 1))
        m = jnp.maximum(s1.max(axis=-1, keepdims=True), s2.max(axis=-1, keepdims=True))
        p1 = jnp.exp2(s1 - m)
        p2 = jnp.exp2(s2 - m)
        l = p1.sum(axis=-1, keepdims=True) + p2.sum(axis=-1, keepdims=True)
        o = (_dot(p1.astype(BF16), cv_ref[0, 0, :, pair])
             + _dot(p2.astype(BF16), kv_ref[0, :, vpair].astype(BF16)))
        o_ref[0, :, DH_A * h:DH_A * (h + 1)] = (o[:, half] / l).astype(o_ref.dtype)

    pending = scores(0)
    for h in range(H_A):
        upcoming = scores(h + 1) if h + 1 < H_A else None
        finish(h, pending)
        pending = upcoming


def _attn_sample(qk, kv, cache_k, cache_v, layer, bias):
    bsz = qk.shape[0]
    cache_spec = pl.BlockSpec((1, 1, WINDOW, WIDTH_A), lambda b: (layer, b, 0, 0))
    return pl.pallas_call(
        _attn_sample_kernel,
        grid=(bsz,),
        in_specs=[pl.BlockSpec((1, CHUNK, WIDTH_A), lambda b: (b, 0, 0)),
                  pl.BlockSpec((1, CHUNK, 2 * WIDTH_A), lambda b: (b, 0, 0)),
                  cache_spec, cache_spec,
                  pl.BlockSpec((H_A, CHUNK, BAND), lambda b: (0, 0, 0))],
        out_specs=pl.BlockSpec((1, CHUNK, WIDTH_A), lambda b: (b, 0, 0)),
        out_shape=jax.ShapeDtypeStruct((bsz, CHUNK, WIDTH_A), BF16),
        compiler_params=_params("parallel"),
    )(qk, kv, cache_k, cache_v, bias)


def _scan_kernel(bqk_ref, bv_ref, bog_ref, cz_ref, cx_ref, misc_ref,
                 gup_ref, gb_ref, gnw_ref, cw_ref, cb_ref, dtb_ref, alog_ref, dsk_ref, snw_ref,
                 gla0_ref, conv0_ref, ssm0_ref,
                 gla_o_ref, ssm_o_ref, gla_s_ref, conv_s_ref, ssm_s_ref,
                 xpad_ref):
    c = pl.program_id(1)
    nseq = bqk_ref.shape[0]

    @pl.when(c == 0)
    def _():
        gla_s_ref[...] = gla0_ref[...]
        conv_s_ref[...] = conv0_ref[...]
        ssm_s_ref[...] = ssm0_ref[...]

    r64 = lax.broadcasted_iota(jnp.int32, (CHUNK, CHUNK), 0)
    c64 = lax.broadcasted_iota(jnp.int32, (CHUNK, CHUNK), 1)
    tri = c64 <= r64
    tri_bf = jnp.where(tri, 1.0, 0.0).astype(BF16)
    ones_bf = jnp.ones((CHUNK, CHUNK), BF16)
    row_x = lax.broadcasted_iota(jnp.int32, (CHUNK, WIDTH_C), 0)
    s_x = lax.broadcasted_iota(jnp.int32, (CHUNK, WIDTH_C), 1) & (CHUNK - 1)
    causal_x = s_x <= row_x
    upper_x = row_x <= s_x
    er = lax.broadcasted_iota(jnp.int32, (128, WIDTH_C), 0)
    el = lax.broadcasted_iota(jnp.int32, (128, WIDTH_C), 1)
    expand_bf = jnp.where(er == MISC_DT_LANE + (el >> 6), 1.0, 0.0).astype(BF16)
    br = lax.broadcasted_iota(jnp.int32, (WIDTH_C, WIDTH_C), 0)
    bl = lax.broadcasted_iota(jnp.int32, (WIDTH_C, WIDTH_C), 1)
    blk64_bf = jnp.where((br >> 6) == (bl >> 6), 1.0, 0.0).astype(BF16)
    blk128_bf = jnp.where((br >> 7) == (bl >> 7), 1.0, 0.0).astype(BF16)

    gup = gup_ref[...]
    gb = gb_ref[...]
    gnw = gnw_ref[...]
    cb = cb_ref[...]
    dtb = dtb_ref[...]
    neg_a = -jnp.exp(alog_ref[...])
    dsk = dsk_ref[...]
    snw = snw_ref[...]

    def load_seq(b):
        return dict(misc=misc_ref[b], qk=bqk_ref[b], v=bv_ref[b], bog=bog_ref[b], cz=cz_ref[b],
                    cx=cx_ref[b], gla=gla_s_ref[b], conv=conv_s_ref[b],
                    ssm=[ssm_s_ref[b, h] for h in range(H_C)])

    def store_seq(b, out):
        gla_o_ref[b] = out["gla_o"]
        ssm_o_ref[b] = out["ssm_o"]
        gla_s_ref[b] = out["gla"]
        conv_s_ref[b] = out["conv"]
        for h in range(H_C):
            ssm_s_ref[b, h] = out["ssm"][h]

    def compute_seq(inp, slot):
        out = {}
        misc = inp["misc"]

        z = _dot(misc.astype(BF16), gup) + gb
        g = (jnp.minimum(z, 0.0) - jnp.log1p(jnp.exp(-jnp.abs(z)))) * (1.0 / GATE_TAU)
        bcum = _dot01_left(tri_bf, g)
        yield
        q = inp["qk"][:, :H_B * DK_B]
        k = inp["qk"][:, H_B * DK_B:]
        blast = bcum[CHUNK - 1:CHUNK, :]
        qb = (q * (DK_B ** -0.5) * jnp.exp(bcum)).astype(BF16)
        kb = (k * jnp.exp(-bcum)).astype(BF16)
        kl = (k * jnp.exp(blast - bcum)).astype(BF16)
        vb = inp["v"].astype(BF16)
        st = inp["gla"]
        st_bf = st.astype(BF16)
        yield
        o_parts, s_parts = [], []
        for h in range(H_B):
            ks = slice(DK_B * h, DK_B * (h + 1))
            vs = slice(DV_B * h, DV_B * (h + 1))
            att = jnp.where(tri, _dot_nt(qb[:, ks], kb[:, ks]), 0.0).astype(BF16)
            o_parts.append(_dot(att, vb[:, vs]) + _dot_nt(qb[:, ks], st_bf[:, ks]))
            s_parts.append(_dot_tn(vb[:, vs], kl[:, ks]))
            yield
        o = jnp.concatenate(o_parts, axis=1)
        out["gla"] = jnp.exp(blast) * st + jnp.concatenate(s_parts, axis=1)
        msq = _dot01_right(o * o, blk64_bf) * (1.0 / DV_B)
        yield
        go = o * lax.rsqrt(msq + EPS) * gnw * _silu(inp["bog"])
        out["gla_o"] = go.astype(gla_o_ref.dtype)
        yield

        cx = inp["cx"]
        xpad_ref[slot, 0:8, :] = inp["conv"]
        xpad_ref[slot, 8:8 + CHUNK, :] = cx
        conv = cb + cx * cw_ref[CONV_W - 1:CONV_W, :]
        for i in range(CONV_W - 1):
            off = 8 - (CONV_W - 1) + i
            conv = conv + xpad_ref[slot, off:off + CHUNK, :] * cw_ref[i:i + 1, :]
        out["conv"] = cx[CHUNK - 8:, :]
        conv = _silu(conv)
        sx = conv[:, :WIDTH_C]
        sb_bf = conv[:, WIDTH_C:WIDTH_C + N_GROUPS_C * D_STATE].astype(BF16)
        sc_bf = conv[:, WIDTH_C + N_GROUPS_C * D_STATE:].astype(BF16)
        yield

        dtv = _softplus(misc + dtb)
        dt_x = _dot01_right(dtv, expand_bf)
        yield
        a_x = neg_a * dt_x
        cs_x = _dot01_left(tri_bf, a_x)
        cs_row = _dot01_left(ones_bf, jnp.where(upper_x, a_x, 0.0))
        yield
        decay = jnp.exp(jnp.where(causal_x, cs_x - cs_row, -jnp.inf))
        cb_parts = []
        for gidx in range(N_GROUPS_C):
            ns = slice(D_STATE * gidx, D_STATE * (gidx + 1))
            cbm = _dot_nt(sc_bf[:, ns], sb_bf[:, ns])
            cb_parts += [cbm] * (H_C // N_GROUPS_C)
            yield
        scores = (jnp.concatenate(cb_parts, axis=1) * decay).astype(BF16)
        xdt = (sx * dt_x).astype(BF16)
        ecs = jnp.exp(cs_x)
        cs_last = cs_x[CHUNK - 1:CHUNK, :]
        xw = (sx * (jnp.exp(cs_last - cs_x) * dt_x)).astype(BF16)
        e_last = jnp.exp(cs_last)
        yield
        y_parts = []
        out["ssm"] = []
        for h in range(H_C):
            ps = slice(P_C * h, P_C * (h + 1))
            gi = h // (H_C // N_GROUPS_C)
            ns = slice(D_STATE * gi, D_STATE * (gi + 1))
            hst = inp["ssm"][h]
            y_parts.append(_dot(scores[:, ps], xdt[:, ps])
                           + _dot_nt(sc_bf[:, ns], hst.astype(BF16)) * ecs[:, ps])
            out["ssm"].append(e_last[:, ps] * hst + _dot_tn(xw[:, ps], sb_bf[:, ns]))
            yield
        y = jnp.concatenate(y_parts, axis=1) + dsk * sx
        y = y * _silu(inp["cz"])
        ymsq = _dot01_right(y * y, blk128_bf) * (1.0 / (WIDTH_C // N_GROUPS_C))
        yield
        out["ssm_o"] = (y * lax.rsqrt(ymsq + EPS) * snw).astype(ssm_o_ref.dtype)
        return out

    def group_body(i, carry):
        seqs = [i * SEQ_UNROLL + u for u in range(SEQ_UNROLL)]
        inputs = [load_seq(b) for b in seqs]
        outputs = _round_robin([compute_seq(inp, slot) for slot, inp in enumerate(inputs)])
        for b, out in zip(seqs, outputs):
            store_seq(b, out)
        return carry

    lax.fori_loop(0, nseq // SEQ_UNROLL, group_body, 0)


def _scan(proj, lw, gla0, conv0, ssm0):
    bsz, l, _ = proj.shape
    nc = l // CHUNK
    sg = min(SEQ_GROUP, bsz)
    ng = bsz // sg

    def col(width, idx):
        return pl.BlockSpec((sg, CHUNK, width), lambda g, c: (g, c, idx))

    def const(shape):
        return pl.BlockSpec(shape, lambda g, c: (0,) * len(shape))

    def state(shape):
        return pl.BlockSpec((sg,) + shape, lambda g, c: (g,) + (0,) * len(shape))

    gla_shape = (DV_B, H_B * DK_B)
    conv_shape = (8, CONV_CH)
    ssm_shape = (H_C, P_C, D_STATE)
    return pl.pallas_call(
        _scan_kernel,
        grid=(ng, nc),
        in_specs=[col(256, 0), col(256, 1), col(256, 2), col(256, 3), col(512, 2), col(128, 12),
                  const((128, 128)), const((1, 128)), const((1, WIDTH_B)),
                  const((CONV_W, CONV_CH)), const((1, CONV_CH)),
                  const((1, 128)), const((1, WIDTH_C)), const((1, WIDTH_C)), const((1, WIDTH_C)),
                  state(gla_shape), state(conv_shape), state(ssm_shape)],
        out_specs=[pl.BlockSpec((sg, CHUNK, WIDTH_B), lambda g, c: (g, c, 0)),
                   pl.BlockSpec((sg, CHUNK, WIDTH_C), lambda g, c: (g, c, 0)),
                   state(gla_shape), state(conv_shape), state(ssm_shape)],
        out_shape=[jax.ShapeDtypeStruct((bsz, l, WIDTH_B), BF16),
                   jax.ShapeDtypeStruct((bsz, l, WIDTH_C), BF16),
                   jax.ShapeDtypeStruct((bsz,) + gla_shape, F32),
                   jax.ShapeDtypeStruct((bsz,) + conv_shape, F32),
                   jax.ShapeDtypeStruct((bsz,) + ssm_shape, F32)],
        scratch_shapes=[pltpu.VMEM((SEQ_UNROLL, 8 + CHUNK, CONV_CH), F32)],
        compiler_params=_params("parallel", "arbitrary"),
    )(proj, proj, proj, proj, proj, proj,
      lw["gup"], lw["gb"], lw["gnw"], lw["cw"], lw["cb"], lw["dtb"], lw["alog"], lw["dsk"], lw["snw"],
      gla0, conv0, ssm0)


OUT_TM = 512
OUT_SUB = 256


def _route(logits):
    lane = lax.broadcasted_iota(jnp.int32, logits.shape, 1)
    lane_f = lane.astype(F32)
    neg = -jnp.inf
    big = 1000.0
    gmask = (lane >= ROUTER_GROUP_LANE) & (lane < ROUTER_GROUP_LANE + N_EXPERT_GROUPS)
    lg = jnp.where(gmask, logits, neg)
    gmax = lg.max(axis=-1, keepdims=True)
    g_idx = jnp.where(lg == gmax, lane_f, big).min(axis=-1, keepdims=True) - ROUTER_GROUP_LANE
    p_grp = 1.0 / jnp.where(gmask, jnp.exp(logits - gmax), 0.0).sum(axis=-1, keepdims=True)
    emask = (lane < N_EXPERTS) & ((lane >> 2).astype(F32) == g_idx)
    le = jnp.where(emask, logits, neg)
    v1 = le.max(axis=-1, keepdims=True)
    i1 = jnp.where(le == v1, lane_f, big).min(axis=-1, keepdims=True)
    le2 = jnp.where(lane_f == i1, neg, le)
    v2 = le2.max(axis=-1, keepdims=True)
    i2 = jnp.where(le2 == v2, lane_f, big).min(axis=-1, keepdims=True)
    e2 = jnp.exp(v2 - v1)
    den = 1.0 + e2
    return (jnp.where(lane_f == i1, (1.0 / den) * p_grp, 0.0)
            + jnp.where(lane_f == i2, (e2 / den) * p_grp, 0.0)
            + jnp.where(lane == ROUTER_GIDX_LANE, g_idx, 0.0))


def _out_proj_kernel(x_ref, att_ref, gla_ref, ssm_ref, w_out_ref, nw_ref, rwh_ref, rwl_ref, rb_ref,
                     x1_ref, hn_ref, gate_ref, wo_ref):
    @pl.when(pl.program_id(0) == 0)
    def _():
        for r in range(0, D_MODEL, PREP_ROWS):
            wo_ref[r:r + PREP_ROWS, :] = w_out_ref[0, r:r + PREP_ROWS, :].astype(BF16)

    def sub_tile(r0):
        rows = slice(r0, r0 + OUT_SUB)
        mo = (_dot(att_ref[rows, :], wo_ref[0:WIDTH_A, :])
              + _dot(gla_ref[rows, :], wo_ref[WIDTH_A:WIDTH_A + WIDTH_B, :])
              + _dot(ssm_ref[rows, :], wo_ref[WIDTH_A + WIDTH_B:, :]))
        yield
        x1 = x_ref[rows, :] + mo
        x1_ref[rows, :] = x1
        ms = jnp.mean(x1 * x1, axis=-1, keepdims=True)
        hn = x1 * lax.rsqrt(ms + EPS) * nw_ref[...]
        hn_ref[rows, :] = hn.astype(hn_ref.dtype)
        h_hi = hn.astype(BF16)
        h_lo = (hn - h_hi.astype(F32)).astype(BF16)
        yield
        logits = (_dot(h_hi, rwh_ref[...]) + _dot(h_lo, rwh_ref[...]) + _dot(h_hi, rwl_ref[...])
                  + rb_ref[...])
        yield
        gate_ref[rows, :] = _route(logits)

    _round_robin([sub_tile(r0) for r0 in range(0, OUT_TM, OUT_SUB)])


def _out_proj(x, att, gla, ssm, lw, layer):
    t = x.shape[0]

    def rows(width):
        return pl.BlockSpec((OUT_TM, width), lambda i: (i, 0))

    def const(shape):
        return pl.BlockSpec(shape, lambda i: (0, 0))

    return pl.pallas_call(
        _out_proj_kernel,
        grid=(t // OUT_TM,),
        in_specs=[rows(D_MODEL), rows(WIDTH_A), rows(WIDTH_B), rows(WIDTH_C),
                  pl.BlockSpec((1, D_MODEL, D_MODEL), lambda i: (layer, 0, 0)), const((1, D_MODEL)),
                  const((D_MODEL, 128)), const((D_MODEL, 128)), const((1, 128))],
        out_specs=[rows(D_MODEL), rows(D_MODEL), rows(128)],
        out_shape=[jax.ShapeDtypeStruct((t, D_MODEL), F32),
                   jax.ShapeDtypeStruct((t, D_MODEL), BF16),
                   jax.ShapeDtypeStruct((t, 128), F32)],
        scratch_shapes=[pltpu.VMEM((D_MODEL, D_MODEL), BF16)],
        compiler_params=_params("arbitrary"),
    )(x, att, gla, ssm, lw["wo"], lw["n2w"], lw["rw_hi"], lw["rw_lo"], lw["rb"])


RT_TS = 256
RT_ALIGN = 16
RT_ROWS = 384
RT_SIZES = (256, 128, 64, 32, 16)
PAY_COLS = D_MODEL + 256
MOE_TM = 512


def _moe_max_tiles(t):
    return (t + (t // RT_TS) * N_EXPERT_GROUPS * (RT_ALIGN - 1)) // MOE_TM + N_EXPERT_GROUPS


def _sort_matrix(gate, c_ref, tile):
    lane = lax.broadcasted_iota(jnp.int32, (RT_TS, 128), 1)
    gidx = jnp.where(lane == ROUTER_GIDX_LANE, gate, 0.0).sum(axis=1, keepdims=True)
    onehot = lane.astype(F32) == gidx
    r = lax.broadcasted_iota(jnp.int32, (RT_TS, RT_TS), 0)
    c = lax.broadcasted_iota(jnp.int32, (RT_TS, RT_TS), 1)
    earlier = jnp.where(c < r, 1.0, 0.0).astype(BF16)
    before = _dot(earlier, jnp.where(onehot, 1.0, 0.0).astype(BF16))
    start = jnp.zeros((RT_TS, 128), F32)
    for g in range(N_EXPERT_GROUPS):
        start = jnp.where(lane == g, c_ref[N_EXPERT_GROUPS * tile + g].astype(F32), start)
    dest = jnp.where(onehot, before + start, 0.0).sum(axis=1, keepdims=True)
    rows = lax.broadcasted_iota(jnp.int32, (RT_TS, RT_ROWS), 1).astype(F32)
    return jnp.where(rows == dest, 1.0, 0.0).astype(BF16)


def _piece_copies(length, local, glob, make_copy, act):
    for size in RT_SIZES:
        take = (length & size) != 0

        @pl.when(take)
        def _(local=local, glob=glob, size=size):
            act(make_copy(pl.multiple_of(local, RT_ALIGN), pl.multiple_of(glob, RT_ALIGN), size))

        step = jnp.where(take, size, 0)
        local = local + step
        glob = glob + step


def _run_copies(c_ref, lp_ref, off_ref, tile, make_copy, act):
    for g in range(N_EXPERT_GROUPS):
        k = N_EXPERT_GROUPS * tile + g
        _piece_copies(lp_ref[k], c_ref[k], off_ref[k], make_copy, act)


def _dispatch_kernel(c_ref, lp_ref, off_ref, tail_ref, nact_ref, hn_ref, gate_ref, xs_hbm,
                     y_ref, zero_ref, sem, zsem, *, first_spare, max_tiles):
    i = pl.program_id(0)
    nt = pl.num_programs(0)
    slot = i % 2
    gate = gate_ref[...]
    pt = _sort_matrix(gate, c_ref, i)
    g_hi = gate.astype(BF16)
    g_lo = (gate - g_hi.astype(F32)).astype(BF16)
    y_ref[slot, :, 0:D_MODEL] = _dot_tn(pt, hn_ref[...]).astype(BF16)
    y_ref[slot, :, D_MODEL:D_MODEL + 128] = _dot_tn(pt, g_hi).astype(BF16)
    y_ref[slot, :, D_MODEL + 128:] = _dot_tn(pt, g_lo).astype(BF16)

    def copy_from(s):
        return lambda local, glob, size: pltpu.make_async_copy(
            y_ref.at[s, pl.ds(local, size)], xs_hbm.at[pl.ds(glob, size)], sem.at[s])

    _run_copies(c_ref, lp_ref, off_ref, i, copy_from(slot), lambda cp: cp.start())

    @pl.when(i > 0)
    def _():
        _run_copies(c_ref, lp_ref, off_ref, i - 1, copy_from(1 - slot), lambda cp: cp.wait())

    @pl.when(i == nt - 1)
    def _():
        _run_copies(c_ref, lp_ref, off_ref, i, copy_from(slot), lambda cp: cp.wait())
        zero_ref[...] = jnp.zeros_like(zero_ref)

        def zero_copy(local, glob, size):
            return pltpu.make_async_copy(zero_ref.at[pl.ds(local, size)],
                                         xs_hbm.at[pl.ds(glob, size)], zsem)

        def zero_fill(act):
            for g in range(N_EXPERT_GROUPS):
                _piece_copies(tail_ref[N_EXPERT_GROUPS + g], 0, tail_ref[g], zero_copy, act)
            for j in range(first_spare, max_tiles):
                @pl.when(j >= nact_ref[0])
                def _(j=j):
                    act(zero_copy(0, j * MOE_TM, MOE_TM))

        zero_fill(lambda cp: cp.start())
        zero_fill(lambda cp: cp.wait())


def _moe_group_kernel(grp_ref, nact_ref, xs_ref, wg_ref, wu_ref, wd_ref, o_ref, wg_bf, wu_bf, wd_bf):
    j = pl.program_id(0)

    @pl.when(jnp.logical_or(j == 0, grp_ref[j] != grp_ref[jnp.maximum(j - 1, 0)]))
    def _():
        wg_bf[...] = wg_ref[0].astype(BF16)
        wu_bf[...] = wu_ref[0].astype(BF16)
        wd_bf[...] = wd_ref[0].astype(BF16)

    @pl.when(j >= nact_ref[0])
    def _():
        o_ref[...] = jnp.zeros_like(o_ref)

    @pl.when(j < nact_ref[0])
    def _():
        x = xs_ref[:, 0:D_MODEL]
        gate = (xs_ref[:, D_MODEL:D_MODEL + 128].astype(F32)
                + xs_ref[:, D_MODEL + 128:].astype(F32))
        lane = lax.broadcasted_iota(jnp.int32, gate.shape, 1)
        first = grp_ref[j] * EXPERTS_PER_GROUP
        acc = None
        for e in range(EXPERTS_PER_GROUP):
            ge = jnp.where(lane == first + e, gate, 0.0).sum(axis=-1, keepdims=True)
            hid = _silu(_dot(x, wg_bf[e])) * _dot(x, wu_bf[e]) * ge
            part = _dot(hid.astype(BF16), wd_bf[e])
            acc = part if acc is None else acc + part
        o_ref[...] = acc


def _combine_kernel(c_ref, lp_ref, off_ref, x_ref, gate_ref, fw_ref, ys_hbm, o_ref, z_ref, sem,
                    *, final_norm):
    i = pl.program_id(0)
    nt = pl.num_programs(0)
    slot = i % 2

    def copy_into(s):
        return lambda local, glob, size: pltpu.make_async_copy(
            ys_hbm.at[pl.ds(glob, size)], z_ref.at[s, pl.ds(local, size)], sem.at[s])

    @pl.when(i == 0)
    def _():
        z_ref[...] = jnp.zeros_like(z_ref)
        _run_copies(c_ref, lp_ref, off_ref, 0, copy_into(0), lambda cp: cp.start())

    @pl.when(i + 1 < nt)
    def _():
        _run_copies(c_ref, lp_ref, off_ref, i + 1, copy_into(1 - slot), lambda cp: cp.start())

    _run_copies(c_ref, lp_ref, off_ref, i, copy_into(slot), lambda cp: cp.wait())
    pt = _sort_matrix(gate_ref[...], c_ref, i)
    z = z_ref[slot]
    z_hi = z.astype(BF16)
    z_lo = (z - z_hi.astype(F32)).astype(BF16)
    y = x_ref[...] + _dot(pt, z_hi) + _dot(pt, z_lo)
    if final_norm:
        ms = jnp.mean(y * y, axis=-1, keepdims=True)
        y = y * lax.rsqrt(ms + EPS) * fw_ref[...]
    o_ref[...] = y


def _route_meta(gate, max_tiles):
    t = gate.shape[0]
    nt = t // RT_TS
    groups = jnp.arange(N_EXPERT_GROUPS, dtype=jnp.int32)
    g = gate[:, ROUTER_GIDX_LANE].astype(jnp.int32).reshape(nt, RT_TS, 1)
    count = (g == groups).astype(jnp.int32).sum(axis=1)
    lp = (count + RT_ALIGN - 1) // RT_ALIGN * RT_ALIGN
    total = lp.sum(axis=0)
    ntile = (total + MOE_TM - 1) // MOE_TM
    last = jnp.cumsum(ntile)
    first_row = (last - ntile) * MOE_TM
    c = jnp.cumsum(lp, axis=1) - lp
    off = first_row[None, :] + jnp.cumsum(lp, axis=0) - lp
    tail = jnp.concatenate([first_row + total, ntile * MOE_TM - total])
    j = jnp.arange(max_tiles, dtype=jnp.int32)[:, None]
    grp = jnp.minimum((j >= last[None, :]).astype(jnp.int32).sum(axis=1), N_EXPERT_GROUPS - 1)
    return c.reshape(-1), lp.reshape(-1), off.reshape(-1), tail, grp, last[-1:]


def _moe(x, hn, gate, lw, layer, fw, final_norm):
    t = x.shape[0]
    nt = t // RT_TS
    max_tiles = _moe_max_tiles(t)
    c, lp, off, tail, grp, nact = _route_meta(gate, max_tiles)

    def rows(width):
        return pl.BlockSpec((RT_TS, width), lambda i, *_: (i, 0))

    xs = pl.pallas_call(
        functools.partial(_dispatch_kernel, first_spare=t // MOE_TM, max_tiles=max_tiles),
        grid_spec=pltpu.PrefetchScalarGridSpec(
            num_scalar_prefetch=5, grid=(nt,),
            in_specs=[rows(D_MODEL), rows(128)],
            out_specs=pl.BlockSpec(memory_space=pl.ANY),
            scratch_shapes=[pltpu.VMEM((2, RT_ROWS, PAY_COLS), BF16),
                            pltpu.VMEM((MOE_TM, PAY_COLS), BF16),
                            pltpu.SemaphoreType.DMA((2,)), pltpu.SemaphoreType.DMA(())]),
        out_shape=jax.ShapeDtypeStruct((max_tiles * MOE_TM, PAY_COLS), BF16),
        compiler_params=_params("arbitrary"),
    )(c, lp, off, tail, nact, hn, gate)

    def experts(shape):
        return pl.BlockSpec((1, EXPERTS_PER_GROUP) + shape, lambda j, grp, nact: (layer, grp[j], 0, 0))

    def rounded(shape):
        return pltpu.VMEM((EXPERTS_PER_GROUP,) + shape, BF16)

    ys = pl.pallas_call(
        _moe_group_kernel,
        grid_spec=pltpu.PrefetchScalarGridSpec(
            num_scalar_prefetch=2, grid=(max_tiles,),
            in_specs=[pl.BlockSpec((MOE_TM, PAY_COLS), lambda j, grp, nact: (j, 0)),
                      experts((D_MODEL, D_EXPERT)), experts((D_MODEL, D_EXPERT)),
                      experts((D_EXPERT, D_MODEL))],
            out_specs=pl.BlockSpec((MOE_TM, D_MODEL), lambda j, grp, nact: (j, 0)),
            scratch_shapes=[rounded((D_MODEL, D_EXPERT)), rounded((D_MODEL, D_EXPERT)),
                            rounded((D_EXPERT, D_MODEL))]),
        out_shape=jax.ShapeDtypeStruct((max_tiles * MOE_TM, D_MODEL), F32),
        compiler_params=_params("arbitrary"),
    )(grp, nact, xs, lw["wg"], lw["wu"], lw["wd"])

    return pl.pallas_call(
        functools.partial(_combine_kernel, final_norm=final_norm),
        grid_spec=pltpu.PrefetchScalarGridSpec(
            num_scalar_prefetch=3, grid=(nt,),
            in_specs=[rows(D_MODEL), rows(128), pl.BlockSpec((1, D_MODEL), lambda i, *_: (0, 0)),
                      pl.BlockSpec(memory_space=pl.ANY)],
            out_specs=rows(D_MODEL),
            scratch_shapes=[pltpu.VMEM((2, RT_ROWS, D_MODEL), F32),
                            pltpu.SemaphoreType.DMA((2,))]),
        out_shape=jax.ShapeDtypeStruct((t, D_MODEL), F32),
        compiler_params=_params("arbitrary"),
    )(c, lp, off, x, gate, fw, ys)


def _lane_place(vec, start, width=128):
    return jnp.zeros((1, width), F32).at[0, start:start + vec.shape[0]].set(vec.astype(F32))


REL_PAD = 704


def _toeplitz(v, rows, cols):
    n = v.shape[-1]
    tiled = jnp.tile(v, (1,) * (v.ndim - 1) + (rows,))[..., :rows * (n - 1)]
    return tiled.reshape(v.shape[:-1] + (rows, n - 1))[..., :cols]


def _band_bias(table):
    t = table.astype(F32)
    ext = jnp.concatenate([jnp.repeat(t[:, :1], REL_PAD, axis=1), t,
                           jnp.repeat(t[:, -1:], REL_PAD, axis=1)], axis=1)
    extr = ext[:, ::-1]
    top = ext.shape[1] - 1 - (REL_CLIP + REL_PAD)

    def band(off, nq, nk):
        n = nq + nk
        v = jnp.concatenate([extr[:, top - off:top - off + nk + 1],
                             extr[:, top - off - (nq - 1):top - off]], axis=1)
        assert v.shape[1] == n
        return _toeplitz(v, nq, nk)

    sample = band(WINDOW, CHUNK, BAND)
    kc = (jnp.arange(ATT_KB) // CHUNK)[:, None]
    qc = (jnp.arange(ATT_QB) // CHUNK)[None, :]
    block = lambda off: jnp.swapaxes(band(off, ATT_QB, ATT_KB), 1, 2)
    prompt = jnp.stack([jnp.where((kc >= qc)[None], block(2 * ATT_KB), -1e30),
                        block(ATT_KB),
                        jnp.where((kc <= qc)[None], block(0), -1e30)])
    return sample * LOG2_E, prompt * LOG2_E


def _layer_weights(i, norm1_w, w_in, rel_bias_table, gla_w_gate_up, gla_b_gate, gla_norm_w,
                   ssm_conv_w, ssm_conv_b, ssm_dt_bias, ssm_a_log, ssm_d, ssm_norm_w, w_out,
                   norm2_w, router_group_w, router_group_b, router_expert_w, router_expert_b,
                   exp_w_gate, exp_w_up, exp_w_down):
    bias_s, bias_p = _band_bias(rel_bias_table[i])
    rw = jnp.concatenate([router_expert_w[i], router_group_w[i],
                          jnp.zeros((D_MODEL, 128 - N_EXPERTS - N_EXPERT_GROUPS), F32)], axis=1)
    rw_hi = rw.astype(BF16)
    return dict(
        n1w=norm1_w[i][None, :],
        w_in=w_in,
        bias_s=bias_s,
        bias_p=bias_p,
        gup=jnp.zeros((128, 128), F32).at[:GATE_RANK, :].set(gla_w_gate_up[i]).astype(BF16),
        gb=gla_b_gate[i][None, :].astype(F32),
        gnw=jnp.tile(gla_norm_w[i], H_B)[None, :].astype(F32),
        cw=ssm_conv_w[i].astype(F32),
        cb=ssm_conv_b[i][None, :].astype(F32),
        dtb=_lane_place(ssm_dt_bias[i], MISC_DT_LANE),
        alog=jnp.repeat(ssm_a_log[i].astype(F32), P_C)[None, :],
        dsk=jnp.repeat(ssm_d[i].astype(F32), P_C)[None, :],
        snw=ssm_norm_w[i][None, :].astype(F32),
        wo=w_out,
        n2w=norm2_w[i][None, :],
        rw_hi=rw_hi,
        rw_lo=(rw - rw_hi.astype(F32)).astype(BF16),
        rb=jnp.concatenate([router_expert_b[i], router_group_b[i],
                            jnp.zeros((128 - N_EXPERTS - N_EXPERT_GROUPS,), F32)])[None, :],
        wg=exp_w_gate,
        wu=exp_w_up,
        wd=exp_w_down,
    )


def _stream_layer(x, bsz, lw, layer, cache_k, cache_v, gla0, conv0, ssm0, fw, final_norm):
    l = x.shape[0] // bsz
    if cache_k is None:
        qk, rest, k_keep, v_keep, vt = _in_proj(x, lw["n1w"], lw["w_in"], layer, True, l)
        att = _attn_prompt(qk.reshape(bsz, l, 2 * WIDTH_A),
                           vt.reshape(bsz, l // ATT_KB, WIDTH_A, ATT_KB), lw["bias_p"])
    else:
        qk, rest, k_keep, v_keep, kv = _in_proj(x, lw["n1w"], lw["w_in"], layer, False, l)
        att = _attn_sample(qk.reshape(bsz, l, 2 * WIDTH_A), kv.reshape(bsz, l, 2 * WIDTH_A),
                           cache_k, cache_v, layer, lw["bias_s"])
    gla_o, ssm_o, gla_s, conv_s, ssm_s = _scan(rest.reshape(bsz, l, REST_COLS), lw, gla0, conv0, ssm0)
    t = bsz * l
    x1, hn, gate = _out_proj(x, att.reshape(t, WIDTH_A), gla_o.reshape(t, WIDTH_B),
                             ssm_o.reshape(t, WIDTH_C), lw, layer)
    x2 = _moe(x1, hn, gate, lw, layer, fw, final_norm)
    keep = min(WINDOW, l)
    k_keep = k_keep.reshape(bsz, keep, H_A, DH_A)
    v_keep = v_keep.reshape(bsz, keep, H_A, DH_A)
    gla_state = gla_s.reshape(bsz, DV_B, H_B, DK_B).transpose(0, 2, 3, 1)
    conv_state = conv_s[:, 8 - (CONV_W - 1):, :]
    return x2, k_keep, v_keep, gla_state, conv_state, ssm_s


def kernel(x_prompt, x_sample, cache_k_a, cache_v_a, state_gla, state_conv, state_ssm, norm1_w, w_in, rel_bias_table, gla_w_gate_up, gla_b_gate, gla_norm_w, ssm_conv_w, ssm_conv_b, ssm_dt_bias, ssm_a_log, ssm_d, ssm_norm_w, w_out, norm2_w, router_group_w, router_group_b, router_expert_w, router_expert_b, exp_w_gate, exp_w_up, exp_w_down, final_norm_w):
    bp, sp, _ = x_prompt.shape
    bs, ss, _ = x_sample.shape
    xp = x_prompt.reshape(bp * sp, D_MODEL)
    xs = x_sample.reshape(bs * ss, D_MODEL)
    fw = final_norm_w[None, :].astype(F32)
    outs_p, outs_s = [], []
    cache_k = cache_k_a.astype(BF16).reshape(DEPTH, bs, -1, WIDTH_A)
    cache_v = cache_v_a.astype(BF16).reshape(DEPTH, bs, -1, WIDTH_A)
    for i in range(DEPTH):
        lw = _layer_weights(i, norm1_w, w_in, rel_bias_table, gla_w_gate_up, gla_b_gate, gla_norm_w,
                            ssm_conv_w, ssm_conv_b, ssm_dt_bias, ssm_a_log, ssm_d, ssm_norm_w, w_out,
                            norm2_w, router_group_w, router_group_b, router_expert_w, router_expert_b,
                            exp_w_gate, exp_w_up, exp_w_down)
        last = i == DEPTH - 1
        xp, *sp_out = _stream_layer(
            xp, bp, lw, i, None, None,
            jnp.zeros((bp, DV_B, H_B * DK_B), F32),
            jnp.zeros((bp, 8, CONV_CH), F32),
            jnp.zeros((bp, H_C, P_C, D_STATE), F32), fw, last)
        outs_p.append(sp_out)
        gla0 = state_gla[i].astype(F32).transpose(0, 3, 1, 2).reshape(bs, DV_B, H_B * DK_B)
        conv0 = jnp.pad(state_conv[i].astype(F32), ((0, 0), (8 - (CONV_W - 1), 0), (0, 0)))
        xs, *ss_out = _stream_layer(
            xs, bs, lw, i, cache_k, cache_v,
            gla0, conv0, state_ssm[i].astype(F32), fw, last)
        outs_s.append(ss_out)
    stack = lambda outs, j: jnp.stack([o[j] for o in outs])
    return (xp.reshape(bp, sp, D_MODEL), xs.reshape(bs, ss, D_MODEL),
            stack(outs_p, 0), stack(outs_p, 1), stack(outs_p, 2), stack(outs_p, 3), stack(outs_p, 4),
            stack(outs_s, 0), stack(outs_s, 1), stack(outs_s, 2), stack(outs_s, 3), stack(outs_s, 4))
```

```python
import functools

import jax
import jax.numpy as jnp
from jax import lax
from jax.experimental import pallas as pl
from jax.experimental.pallas import tpu as pltpu

F32 = jnp.float32
BF16 = jnp.bfloat16

D_MODEL = 1024
DEPTH = 2
EPS = 1e-6
CHUNK = 64
N_PAST_CHUNKS = 8
BAND = (N_PAST_CHUNKS + 1) * CHUNK
WINDOW = N_PAST_CHUNKS * CHUNK
H_A, DH_A, WIDTH_A = 8, 64, 512
REL_CLIP = 128
LOG2_E = 1.4426950408889634
H_B, DK_B, DV_B, WIDTH_B = 4, 32, 64, 256
GATE_RANK = 16
GATE_TAU = 16.0
H_C, P_C, WIDTH_C = 4, 64, 256
N_GROUPS_C = 2
D_STATE = 64
CONV_W = 4
CONV_CH = 512
N_EXPERT_GROUPS = 4
EXPERTS_PER_GROUP = 4
N_EXPERTS = 16
D_EXPERT = 256

PROJ_COLS = 3200
MISC_DT_LANE = GATE_RANK
ROUTER_GROUP_LANE = N_EXPERTS
ROUTER_GIDX_LANE = 20
ROUTER_DEST_LANE = 21

SEQ_GROUP = 8
SEQ_UNROLL = 8
VMEM_LIMIT_BYTES = 56 * 1024 * 1024

NT_DIMS = (((1,), (1,)), ((), ()))
TN_DIMS = (((0,), (0,)), ((), ()))


def _params(*sem):
    return pltpu.CompilerParams(dimension_semantics=sem, vmem_limit_bytes=VMEM_LIMIT_BYTES)


def _dot(a, b):
    return jnp.dot(a, b, preferred_element_type=F32)


def _dot_nt(a, b):
    return lax.dot_general(a, b, NT_DIMS, preferred_element_type=F32)


def _dot_tn(a, b):
    return lax.dot_general(a, b, TN_DIMS, preferred_element_type=F32)


def _split2(a):
    hi = a.astype(BF16)
    return hi, (a - hi.astype(F32)).astype(BF16)


def _dot01_left(m01, a):
    hi, lo = _split2(a)
    return _dot(m01, hi) + _dot(m01, lo)


def _dot01_right(a, m01):
    hi, lo = _split2(a)
    return _dot(hi, m01) + _dot(lo, m01)


def _round_robin(gens):
    results = [None] * len(gens)
    live = list(range(len(gens)))
    while live:
        for idx in list(live):
            try:
                next(gens[idx])
            except StopIteration as stop:
                results[idx] = stop.value
                live.remove(idx)
    return results


def _silu(x):
    return x * jax.nn.sigmoid(x)


def _softplus(x):
    return jnp.maximum(x, 0.0) + jnp.log1p(jnp.exp(-jnp.abs(x)))


IN_TM = 256
QKV_COLS = 3 * WIDTH_A
REST_COLS = PROJ_COLS - QKV_COLS
REST_CHUNKS = ((0, 512), (512, 1024), (1024, REST_COLS))


IN_COLS = 3092
IN_TAIL = QKV_COLS + 512
PREP_ROWS = 128


def _prepare_w_in(w_in_ref, w_ref):
    scale = DH_A ** -0.5 * LOG2_E
    n_tail = REST_COLS - 512 - 128
    for r in range(0, D_MODEL, PREP_ROWS):
        rs = slice(r, r + PREP_ROWS)
        w_ref[rs, 0:WIDTH_A] = (w_in_ref[0, rs, 0:WIDTH_A] * scale).astype(BF16)
        w_ref[rs, WIDTH_A:IN_TAIL] = w_in_ref[0, rs, WIDTH_A:IN_TAIL].astype(BF16)
        tail = w_in_ref[0, rs, IN_TAIL:IN_COLS]
        w_ref[rs, IN_TAIL:IN_TAIL + n_tail] = tail[:, GATE_RANK:GATE_RANK + n_tail].astype(BF16)
        misc = jnp.concatenate([tail[:, 0:GATE_RANK], tail[:, GATE_RANK + n_tail:],
                                jnp.zeros((PREP_ROWS, 128 - GATE_RANK - H_C), F32)], axis=1)
        w_ref[rs, IN_TAIL + n_tail:] = misc.astype(BF16)


def _in_proj_kernel(x_ref, nw_ref, w_in_ref, *refs, with_vt):
    if with_vt:
        qk_ref, rest_ref, kk_ref, vk_ref, vt_ref, w_ref = refs
    else:
        qk_ref, rest_ref, kk_ref, vk_ref, kv_ref, w_ref = refs

    @pl.when(pl.program_id(0) == 0)
    def _():
        _prepare_w_in(w_in_ref, w_ref)

    x = x_ref[...]
    ms = jnp.mean(x * x, axis=-1, keepdims=True)
    xn = (x * lax.rsqrt(ms + EPS) * nw_ref[...]).astype(BF16)
    qk_ref[:, 0:WIDTH_A] = _dot(xn, w_ref[:, 0:WIDTH_A]).astype(BF16)
    k = _dot(xn, w_ref[:, WIDTH_A:2 * WIDTH_A])
    v = _dot(xn, w_ref[:, 2 * WIDTH_A:QKV_COLS])
    qk_ref[:, WIDTH_A:] = k.astype(BF16)
    if not with_vt:
        kv_ref[:, 0:WIDTH_A] = k
        kv_ref[:, WIDTH_A:] = v

    def keep_rows():
        kk = kk_ref.reshape(IN_TM * H_A, DH_A)
        vk = vk_ref.reshape(IN_TM * H_A, DH_A)
        for h in range(H_A):
            sl = slice(DH_A * h, DH_A * (h + 1))
            kk[pl.ds(h, IN_TM, stride=H_A), :] = k[:, sl]
            vk[pl.ds(h, IN_TM, stride=H_A), :] = v[:, sl]

    keep_rows()
    for lo, hi in REST_CHUNKS:
        rest_ref[:, lo:hi] = _dot(xn, w_ref[:, QKV_COLS + lo:QKV_COLS + hi])
    if with_vt:
        vt_ref[0] = v.T.astype(BF16)


def _in_proj(x, nw, w_in, layer, with_vt, seq_len):
    t = x.shape[0]
    tiles_per_seq = max(seq_len // IN_TM, 1)
    keep_tiles = max(min(WINDOW, seq_len) // IN_TM, 1)
    n_keep = t // IN_TM // tiles_per_seq * keep_tiles

    def rows(width):
        return pl.BlockSpec((IN_TM, width), lambda i: (i, 0))

    def kept(i):
        return (i // tiles_per_seq * keep_tiles
                + jnp.maximum(i % tiles_per_seq - (tiles_per_seq - keep_tiles), 0), 0, 0)

    keep_spec = pl.BlockSpec((IN_TM, H_A, DH_A), kept)
    keep_shape = jax.ShapeDtypeStruct((n_keep * IN_TM, H_A, DH_A), F32)
    in_specs = [rows(D_MODEL), pl.BlockSpec((1, D_MODEL), lambda i: (0, 0)),
                pl.BlockSpec((1, D_MODEL, IN_COLS), lambda i: (layer, 0, 0))]
    scratch = [pltpu.VMEM((D_MODEL, PROJ_COLS), BF16)]
    out_specs = [rows(2 * WIDTH_A), rows(REST_COLS), keep_spec, keep_spec]
    out_shape = [jax.ShapeDtypeStruct((t, 2 * WIDTH_A), BF16),
                 jax.ShapeDtypeStruct((t, REST_COLS), F32), keep_shape, keep_shape]
    if with_vt:
        out_specs.append(pl.BlockSpec((1, WIDTH_A, IN_TM), lambda i: (i, 0, 0)))
        out_shape.append(jax.ShapeDtypeStruct((t // IN_TM, WIDTH_A, IN_TM), BF16))
    else:
        out_specs.append(rows(2 * WIDTH_A))
        out_shape.append(jax.ShapeDtypeStruct((t, 2 * WIDTH_A), F32))
    return pl.pallas_call(
        functools.partial(_in_proj_kernel, with_vt=with_vt),
        grid=(t // IN_TM,),
        in_specs=in_specs, out_specs=out_specs, out_shape=out_shape, scratch_shapes=scratch,
        compiler_params=_params("arbitrary"),
    )(x, nw, w_in)


ATT_QB = 4 * CHUNK
ATT_KB = IN_TM
ATT_NKB = (ATT_QB + WINDOW) // ATT_KB


def _attn_prompt_kernel(q_ref, k_ref, vt_ref, bias_ref, o_ref, ot_ref):
    blk = pl.program_id(1)
    sb = jnp.maximum(blk - WINDOW // ATT_KB, 0)
    kstart = pl.multiple_of(sb * ATT_KB, ATT_KB)
    lane = lax.broadcasted_iota(jnp.int32, (ATT_QB, 128), 1)

    def attend(kinds):
        def scores(h):
            pair = slice(128 * (h // 2), 128 * (h // 2 + 1))
            qp = q_ref[0, :, pair]
            qz = jnp.where((lane >= DH_A) == bool(h % 2), qp, jnp.zeros_like(qp))
            return [_dot_nt(k_ref[0, pl.ds(kstart + ATT_KB * i, ATT_KB), pair], qz)
                    + bias_ref[kind, h] for i, kind in enumerate(kinds)]

        def finish(h, s):
            m = s[0].max(axis=0, keepdims=True)
            for si in s[1:]:
                m = jnp.maximum(m, si.max(axis=0, keepdims=True))
            l = None
            ot = None
            for i, si in enumerate(s):
                p = jnp.exp2(si - m)
                li = p.sum(axis=0, keepdims=True)
                oi = _dot(vt_ref[0, sb + i, DH_A * h:DH_A * (h + 1), :], p.astype(BF16))
                l = li if l is None else l + li
                ot = oi if ot is None else ot + oi
            ot_ref[DH_A * h:DH_A * (h + 1), :] = ot / l

        pending = scores(0)
        for h in range(H_A):
            upcoming = scores(h + 1) if h + 1 < H_A else None
            finish(h, pending)
            pending = upcoming
        o_ref[0] = ot_ref[...].T.astype(o_ref.dtype)

    pl.when(blk == 0)(lambda: attend((2,)))
    pl.when(blk == 1)(lambda: attend((1, 2)))
    pl.when(blk >= 2)(lambda: attend((0, 1, 2)))


def _attn_prompt(qk, vt, bias_t):
    bsz, s, _ = qk.shape
    nkb = s // ATT_KB
    return pl.pallas_call(
        _attn_prompt_kernel,
        grid=(bsz, s // ATT_QB),
        in_specs=[pl.BlockSpec((1, ATT_QB, WIDTH_A), lambda b, c: (b, c, 0)),
                  pl.BlockSpec((1, s, WIDTH_A), lambda b, c: (b, 0, 1)),
                  pl.BlockSpec((1, nkb, WIDTH_A, ATT_KB), lambda b, c: (b, 0, 0, 0)),
                  pl.BlockSpec((3, H_A, ATT_KB, ATT_QB), lambda b, c: (0, 0, 0, 0))],
        out_specs=pl.BlockSpec((1, ATT_QB, WIDTH_A), lambda b, c: (b, c, 0)),
        out_shape=jax.ShapeDtypeStruct((bsz, s, WIDTH_A), BF16),
        scratch_shapes=[pltpu.VMEM((WIDTH_A, ATT_QB), F32)],
        compiler_params=_params("parallel", "arbitrary"),
    )(qk, qk, vt, bias_t)


def _attn_sample_kernel(q_ref, kv_ref, ck_ref, cv_ref, bias_ref, o_ref):
    lane = lax.broadcasted_iota(jnp.int32, (CHUNK, 128), 1)

    def scores(h):
        pair = slice(128 * (h // 2), 128 * (h // 2 + 1))
        qp = q_ref[0, :, pair]
        qz = jnp.where((lane >= DH_A) == bool(h % 2), qp, jnp.zeros_like(qp))
        return (_dot_nt(qz, ck_ref[0, 0, :, pair]) + bias_ref[h, :, 0:WINDOW],
                _dot_nt(qz, kv_ref[0, :, pair].astype(BF16)) + bias_ref[h, :, WINDOW:BAND])

    def finish(h, s):
        s1, s2 = s
        pair = slice(128 * (h // 2), 128 * (h // 2 + 1))
        vpair = slice(WIDTH_A + 128 * (h // 2), WIDTH_A + 128 * (h // 2 + 1))
        half = slice(DH_A * (h % 2), DH_A * (h % 2 + 1))
        m = jnp.maximum(s1.max(axis=-1, keepdims=True), s2.max(axis=-1, keepdims=True))
        p1 = jnp.exp2(s1 - m)
        p2 = jnp.exp2(s2 - m)
        l = p1.sum(axis=-1, keepdims=True) + p2.sum(axis=-1, keepdims=True)
        o = (_dot(p1.astype(BF16), cv_ref[0, 0, :, pair])
             + _dot(p2.astype(BF16), kv_ref[0, :, vpair].astype(BF16)))
        o_ref[0, :, DH_A * h:DH_A * (h + 1)] = (o[:, half] / l).astype(o_ref.dtype)

    pending = scores(0)
    for h in range(H_A):
        upcoming = scores(h + 1) if h + 1 < H_A else None
        finish(h, pending)
        pending = upcoming


def _attn_sample(qk, kv, cache_k, cache_v, layer, bias):
    bsz = qk.shape[0]
    cache_spec = pl.BlockSpec((1, 1, WINDOW, WIDTH_A), lambda b: (layer, b, 0, 0))
    return pl.pallas_call(
        _attn_sample_kernel,
        grid=(bsz,),
        in_specs=[pl.BlockSpec((1, CHUNK, WIDTH_A), lambda b: (b, 0, 0)),
                  pl.BlockSpec((1, CHUNK, 2 * WIDTH_A), lambda b: (b, 0, 0)),
                  cache_spec, cache_spec,
                  pl.BlockSpec((H_A, CHUNK, BAND), lambda b: (0, 0, 0))],
        out_specs=pl.BlockSpec((1, CHUNK, WIDTH_A), lambda b: (b, 0, 0)),
        out_shape=jax.ShapeDtypeStruct((bsz, CHUNK, WIDTH_A), BF16),
        compiler_params=_params("parallel"),
    )(qk, kv, cache_k, cache_v, bias)


def _scan_kernel(bqk_ref, bv_ref, bog_ref, cz_ref, cx_ref, misc_ref,
                 gup_ref, gb_ref, gnw_ref, cw_ref, cb_ref, dtb_ref, alog_ref, dsk_ref, snw_ref,
                 gla0_ref, conv0_ref, ssm0_ref,
                 gla_o_ref, ssm_o_ref, gla_s_ref, conv_s_ref, ssm_s_ref,
                 xpad_ref):
    c = pl.program_id(1)
    nseq = bqk_ref.shape[0]

    @pl.when(c == 0)
    def _():
        gla_s_ref[...] = gla0_ref[...]
        conv_s_ref[...] = conv0_ref[...]
        ssm_s_ref[...] = ssm0_ref[...]

    r64 = lax.broadcasted_iota(jnp.int32, (CHUNK, CHUNK), 0)
    c64 = lax.broadcasted_iota(jnp.int32, (CHUNK, CHUNK), 1)
    tri = c64 <= r64
    tri_bf = jnp.where(tri, 1.0, 0.0).astype(BF16)
    ones_bf = jnp.ones((CHUNK, CHUNK), BF16)
    row_x = lax.broadcasted_iota(jnp.int32, (CHUNK, WIDTH_C), 0)
    s_x = lax.broadcasted_iota(jnp.int32, (CHUNK, WIDTH_C), 1) & (CHUNK - 1)
    causal_x = s_x <= row_x
    upper_x = row_x <= s_x
    er = lax.broadcasted_iota(jnp.int32, (128, WIDTH_C), 0)
    el = lax.broadcasted_iota(jnp.int32, (128, WIDTH_C), 1)
    expand_bf = jnp.where(er == MISC_DT_LANE + (el >> 6), 1.0, 0.0).astype(BF16)
    br = lax.broadcasted_iota(jnp.int32, (WIDTH_C, WIDTH_C), 0)
    bl = lax.broadcasted_iota(jnp.int32, (WIDTH_C, WIDTH_C), 1)
    blk64_bf = jnp.where((br >> 6) == (bl >> 6), 1.0, 0.0).astype(BF16)
    blk128_bf = jnp.where((br >> 7) == (bl >> 7), 1.0, 0.0).astype(BF16)

    gup = gup_ref[...]
    gb = gb_ref[...]
    gnw = gnw_ref[...]
    cb = cb_ref[...]
    dtb = dtb_ref[...]
    neg_a = -jnp.exp(alog_ref[...])
    dsk = dsk_ref[...]
    snw = snw_ref[...]

    def load_seq(b):
        return dict(misc=misc_ref[b], qk=bqk_ref[b], v=bv_ref[b], bog=bog_ref[b], cz=cz_ref[b],
                    cx=cx_ref[b], gla=gla_s_ref[b], conv=conv_s_ref[b],
                    ssm=[ssm_s_ref[b, h] for h in range(H_C)])

    def store_seq(b, out):
        gla_o_ref[b] = out["gla_o"]
        ssm_o_ref[b] = out["ssm_o"]
        gla_s_ref[b] = out["gla"]
        conv_s_ref[b] = out["conv"]
        for h in range(H_C):
            ssm_s_ref[b, h] = out["ssm"][h]

    def compute_seq(inp, slot):
        out = {}
        misc = inp["misc"]

        z = _dot(misc.astype(BF16), gup) + gb
        g = (jnp.minimum(z, 0.0) - jnp.log1p(jnp.exp(-jnp.abs(z)))) * (1.0 / GATE_TAU)
        bcum = _dot01_left(tri_bf, g)
        yield
        q = inp["qk"][:, :H_B * DK_B]
        k = inp["qk"][:, H_B * DK_B:]
        blast = bcum[CHUNK - 1:CHUNK, :]
        qb = (q * (DK_B ** -0.5) * jnp.exp(bcum)).astype(BF16)
        kb = (k * jnp.exp(-bcum)).astype(BF16)
        kl = (k * jnp.exp(blast - bcum)).astype(BF16)
        vb = inp["v"].astype(BF16)
        st = inp["gla"]
        st_bf = st.astype(BF16)
        yield
        o_parts, s_parts = [], []
        for h in range(H_B):
            ks = slice(DK_B * h, DK_B * (h + 1))
            vs = slice(DV_B * h, DV_B * (h + 1))
            att = jnp.where(tri, _dot_nt(qb[:, ks], kb[:, ks]), 0.0).astype(BF16)
            o_parts.append(_dot(att, vb[:, vs]) + _dot_nt(qb[:, ks], st_bf[:, ks]))
            s_parts.append(_dot_tn(vb[:, vs], kl[:, ks]))
            yield
        o = jnp.concatenate(o_parts, axis=1)
        out["gla"] = jnp.exp(blast) * st + jnp.concatenate(s_parts, axis=1)
        msq = _dot01_right(o * o, blk64_bf) * (1.0 / DV_B)
        yield
        go = o * lax.rsqrt(msq + EPS) * gnw * _silu(inp["bog"])
        out["gla_o"] = go.astype(gla_o_ref.dtype)
        yield

        cx = inp["cx"]
        xpad_ref[slot, 0:8, :] = inp["conv"]
        xpad_ref[slot, 8:8 + CHUNK, :] = cx
        conv = cb + cx * cw_ref[CONV_W - 1:CONV_W, :]
        for i in range(CONV_W - 1):
            off = 8 - (CONV_W - 1) + i
            conv = conv + xpad_ref[slot, off:off + CHUNK, :] * cw_ref[i:i + 1, :]
        out["conv"] = cx[CHUNK - 8:, :]
        conv = _silu(conv)
        sx = conv[:, :WIDTH_C]
        sb_bf = conv[:, WIDTH_C:WIDTH_C + N_GROUPS_C * D_STATE].astype(BF16)
        sc_bf = conv[:, WIDTH_C + N_GROUPS_C * D_STATE:].astype(BF16)
        yield

        dtv = _softplus(misc + dtb)
        dt_x = _dot01_right(dtv, expand_bf)
        yield
        a_x = neg_a * dt_x
        cs_x = _dot01_left(tri_bf, a_x)
        cs_row = _dot01_left(ones_bf, jnp.where(upper_x, a_x, 0.0))
        yield
        decay = jnp.exp(jnp.where(causal_x, cs_x - cs_row, -jnp.inf))
        cb_parts = []
        for gidx in range(N_GROUPS_C):
            ns = slice(D_STATE * gidx, D_STATE * (gidx + 1))
            cbm = _dot_nt(sc_bf[:, ns], sb_bf[:, ns])
            cb_parts += [cbm] * (H_C // N_GROUPS_C)
            yield
        scores = (jnp.concatenate(cb_parts, axis=1) * decay).astype(BF16)
        xdt = (sx * dt_x).astype(BF16)
        ecs = jnp.exp(cs_x)
        cs_last = cs_x[CHUNK - 1:CHUNK, :]
        xw = (sx * (jnp.exp(cs_last - cs_x) * dt_x)).astype(BF16)
        e_last = jnp.exp(cs_last)
        yield
        y_parts = []
        out["ssm"] = []
        for h in range(H_C):
            ps = slice(P_C * h, P_C * (h + 1))
            gi = h // (H_C // N_GROUPS_C)
            ns = slice(D_STATE * gi, D_STATE * (gi + 1))
            hst = inp["ssm"][h]
            y_parts.append(_dot(scores[:, ps], xdt[:, ps])
                           + _dot_nt(sc_bf[:, ns], hst.astype(BF16)) * ecs[:, ps])
            out["ssm"].append(e_last[:, ps] * hst + _dot_tn(xw[:, ps], sb_bf[:, ns]))
            yield
        y = jnp.concatenate(y_parts, axis=1) + dsk * sx
        y = y * _silu(inp["cz"])
        ymsq = _dot01_right(y * y, blk128_bf) * (1.0 / (WIDTH_C // N_GROUPS_C))
        yield
        out["ssm_o"] = (y * lax.rsqrt(ymsq + EPS) * snw).astype(ssm_o_ref.dtype)
        return out

    def group_body(i, carry):
        seqs = [i * SEQ_UNROLL + u for u in range(SEQ_UNROLL)]
        inputs = [load_seq(b) for b in seqs]
        outputs = _round_robin([compute_seq(inp, slot) for slot, inp in enumerate(inputs)])
        for b, out in zip(seqs, outputs):
            store_seq(b, out)
        return carry

    lax.fori_loop(0, nseq // SEQ_UNROLL, group_body, 0)


def _scan(proj, lw, gla0, conv0, ssm0):
    bsz, l, _ = proj.shape
    nc = l // CHUNK
    sg = min(SEQ_GROUP, bsz)
    ng = bsz // sg

    def col(width, idx):
        return pl.BlockSpec((sg, CHUNK, width), lambda g, c: (g, c, idx))

    def const(shape):
        return pl.BlockSpec(shape, lambda g, c: (0,) * len(shape))

    def state(shape):
        return pl.BlockSpec((sg,) + shape, lambda g, c: (g,) + (0,) * len(shape))

    gla_shape = (DV_B, H_B * DK_B)
    conv_shape = (8, CONV_CH)
    ssm_shape = (H_C, P_C, D_STATE)
    return pl.pallas_call(
        _scan_kernel,
        grid=(ng, nc),
        in_specs=[col(256, 0), col(256, 1), col(256, 2), col(256, 3), col(512, 2), col(128, 12),
                  const((128, 128)), const((1, 128)), const((1, WIDTH_B)),
                  const((CONV_W, CONV_CH)), const((1, CONV_CH)),
                  const((1, 128)), const((1, WIDTH_C)), const((1, WIDTH_C)), const((1, WIDTH_C)),
                  state(gla_shape), state(conv_shape), state(ssm_shape)],
        out_specs=[pl.BlockSpec((sg, CHUNK, WIDTH_B), lambda g, c: (g, c, 0)),
                   pl.BlockSpec((sg, CHUNK, WIDTH_C), lambda g, c: (g, c, 0)),
                   state(gla_shape), state(conv_shape), state(ssm_shape)],
        out_shape=[jax.ShapeDtypeStruct((bsz, l, WIDTH_B), BF16),
                   jax.ShapeDtypeStruct((bsz, l, WIDTH_C), BF16),
                   jax.ShapeDtypeStruct((bsz,) + gla_shape, F32),
                   jax.ShapeDtypeStruct((bsz,) + conv_shape, F32),
                   jax.ShapeDtypeStruct((bsz,) + ssm_shape, F32)],
        scratch_shapes=[pltpu.VMEM((SEQ_UNROLL, 8 + CHUNK, CONV_CH), F32)],
        compiler_params=_params("parallel", "arbitrary"),
    )(proj, proj, proj, proj, proj, proj,
      lw["gup"], lw["gb"], lw["gnw"], lw["cw"], lw["cb"], lw["dtb"], lw["alog"], lw["dsk"], lw["snw"],
      gla0, conv0, ssm0)


OUT_TM = 512
OUT_SUB = 256


def _route(logits):
    lane = lax.broadcasted_iota(jnp.int32, logits.shape, 1)
    lane_f = lane.astype(F32)
    neg = -jnp.inf
    big = 1000.0
    gmask = (lane >= ROUTER_GROUP_LANE) & (lane < ROUTER_GROUP_LANE + N_EXPERT_GROUPS)
    lg = jnp.where(gmask, logits, neg)
    gmax = lg.max(axis=-1, keepdims=True)
    g_idx = jnp.where(lg == gmax, lane_f, big).min(axis=-1, keepdims=True) - ROUTER_GROUP_LANE
    p_grp = 1.0 / jnp.where(gmask, jnp.exp(logits - gmax), 0.0).sum(axis=-1, keepdims=True)
    emask = (lane < N_EXPERTS) & ((lane >> 2).astype(F32) == g_idx)
    le = jnp.where(emask, logits, neg)
    v1 = le.max(axis=-1, keepdims=True)
    i1 = jnp.where(le == v1, lane_f, big).min(axis=-1, keepdims=True)
    le2 = jnp.where(lane_f == i1, neg, le)
    v2 = le2.max(axis=-1, keepdims=True)
    i2 = jnp.where(le2 == v2, lane_f, big).min(axis=-1, keepdims=True)
    e2 = jnp.exp(v2 - v1)
    den = 1.0 + e2
    n = logits.shape[0]
    onehot = lane_f == g_idx
    member = jnp.where(onehot, 1.0, 0.0)
    r = lax.broadcasted_iota(jnp.int32, (n, n), 0)
    c = lax.broadcasted_iota(jnp.int32, (n, n), 1)
    before = _dot(jnp.where(c < r, 1.0, 0.0).astype(BF16), member.astype(BF16))
    padded = jnp.floor((member.sum(axis=0, keepdims=True) + (RT_ALIGN - 1)) * (1.0 / RT_ALIGN)) * RT_ALIGN
    lane1 = lane[0:1, :]
    start = jnp.zeros((1, 128), F32)
    for g in range(N_EXPERT_GROUPS - 1):
        run = jnp.where(lane1 == g, padded, 0.0).sum(axis=1, keepdims=True)
        start = start + jnp.where(lane1 > g, run, 0.0)
    dest = jnp.where(onehot, before + start, 0.0).sum(axis=1, keepdims=True)
    return (jnp.where(lane_f == i1, (1.0 / den) * p_grp, 0.0)
            + jnp.where(lane_f == i2, (e2 / den) * p_grp, 0.0)
            + jnp.where(lane == ROUTER_GIDX_LANE, g_idx, 0.0)
            + jnp.where(lane == ROUTER_DEST_LANE, dest, 0.0))


def _out_proj_kernel(x_ref, att_ref, gla_ref, ssm_ref, w_out_ref, nw_ref, rwh_ref, rwl_ref, rb_ref,
                     x1_ref, hn_ref, gate_ref, wo_ref):
    @pl.when(pl.program_id(0) == 0)
    def _():
        for r in range(0, D_MODEL, PREP_ROWS):
            wo_ref[r:r + PREP_ROWS, :] = w_out_ref[0, r:r + PREP_ROWS, :].astype(BF16)

    def sub_tile(r0):
        rows = slice(r0, r0 + OUT_SUB)
        mo = (_dot(att_ref[rows, :], wo_ref[0:WIDTH_A, :])
              + _dot(gla_ref[rows, :], wo_ref[WIDTH_A:WIDTH_A + WIDTH_B, :])
              + _dot(ssm_ref[rows, :], wo_ref[WIDTH_A + WIDTH_B:, :]))
        yield
        x1 = x_ref[rows, :] + mo
        x1_ref[rows, :] = x1
        ms = jnp.mean(x1 * x1, axis=-1, keepdims=True)
        hn = x1 * lax.rsqrt(ms + EPS) * nw_ref[...]
        hn_ref[rows, :] = hn.astype(hn_ref.dtype)
        h_hi = hn.astype(BF16)
        h_lo = (hn - h_hi.astype(F32)).astype(BF16)
        yield
        logits = (_dot(h_hi, rwh_ref[...]) + _dot(h_lo, rwh_ref[...]) + _dot(h_hi, rwl_ref[...])
                  + rb_ref[...])
        yield
        gate_ref[rows, :] = _route(logits)

    _round_robin([sub_tile(r0) for r0 in range(0, OUT_TM, OUT_SUB)])


def _out_proj(x, att, gla, ssm, lw, layer):
    t = x.shape[0]

    def rows(width):
        return pl.BlockSpec((OUT_TM, width), lambda i: (i, 0))

    def const(shape):
        return pl.BlockSpec(shape, lambda i: (0, 0))

    return pl.pallas_call(
        _out_proj_kernel,
        grid=(t // OUT_TM,),
        in_specs=[rows(D_MODEL), rows(WIDTH_A), rows(WIDTH_B), rows(WIDTH_C),
                  pl.BlockSpec((1, D_MODEL, D_MODEL), lambda i: (layer, 0, 0)), const((1, D_MODEL)),
                  const((D_MODEL, 128)), const((D_MODEL, 128)), const((1, 128))],
        out_specs=[rows(D_MODEL), rows(D_MODEL), rows(128)],
        out_shape=[jax.ShapeDtypeStruct((t, D_MODEL), F32),
                   jax.ShapeDtypeStruct((t, D_MODEL), BF16),
                   jax.ShapeDtypeStruct((t, 128), F32)],
        scratch_shapes=[pltpu.VMEM((D_MODEL, D_MODEL), BF16)],
        compiler_params=_params("arbitrary"),
    )(x, att, gla, ssm, lw["wo"], lw["n2w"], lw["rw_hi"], lw["rw_lo"], lw["rb"])


RT_TS = OUT_SUB
RT_ALIGN = 16
RT_ROWS = 384
RT_SIZES = (256, 128, 64, 32, 16)
PAY_COLS = D_MODEL + 256
MOE_TM = 512


def _moe_max_tiles(t):
    return (t + (t // RT_TS) * N_EXPERT_GROUPS * (RT_ALIGN - 1)) // MOE_TM + N_EXPERT_GROUPS


def _sort_matrix(gate):
    lane = lax.broadcasted_iota(jnp.int32, (RT_TS, 128), 1)
    dest = jnp.where(lane == ROUTER_DEST_LANE, gate, 0.0).sum(axis=1, keepdims=True)
    rows = lax.broadcasted_iota(jnp.int32, (RT_TS, RT_ROWS), 1).astype(F32)
    return jnp.where(rows == dest, 1.0, 0.0).astype(BF16)


def _piece_copies(length, local, glob, make_copy, act):
    for size in RT_SIZES:
        take = (length & size) != 0

        @pl.when(take)
        def _(local=local, glob=glob, size=size):
            act(make_copy(pl.multiple_of(local, RT_ALIGN), pl.multiple_of(glob, RT_ALIGN), size))

        step = jnp.where(take, size, 0)
        local = local + step
        glob = glob + step


def _run_copies(c_ref, lp_ref, off_ref, tile, make_copy, act):
    for g in range(N_EXPERT_GROUPS):
        k = N_EXPERT_GROUPS * tile + g
        _piece_copies(lp_ref[k], c_ref[k], off_ref[k], make_copy, act)


def _dispatch_kernel(c_ref, lp_ref, off_ref, tail_ref, nact_ref, hn_ref, gate_ref, xs_hbm,
                     y_ref, zero_ref, sem, zsem, *, first_spare, max_tiles):
    i = pl.program_id(0)
    nt = pl.num_programs(0)
    slot = i % 2
    gate = gate_ref[...]
    pt = _sort_matrix(gate)
    g_hi = gate.astype(BF16)
    g_lo = (gate - g_hi.astype(F32)).astype(BF16)
    y_ref[slot, :, 0:D_MODEL] = _dot_tn(pt, hn_ref[...]).astype(BF16)
    y_ref[slot, :, D_MODEL:D_MODEL + 128] = _dot_tn(pt, g_hi).astype(BF16)
    y_ref[slot, :, D_MODEL + 128:] = _dot_tn(pt, g_lo).astype(BF16)

    def copy_from(s):
        return lambda local, glob, size: pltpu.make_async_copy(
            y_ref.at[s, pl.ds(local, size)], xs_hbm.at[pl.ds(glob, size)], sem.at[s])

    _run_copies(c_ref, lp_ref, off_ref, i, copy_from(slot), lambda cp: cp.start())

    @pl.when(i > 0)
    def _():
        _run_copies(c_ref, lp_ref, off_ref, i - 1, copy_from(1 - slot), lambda cp: cp.wait())

    @pl.when(i == nt - 1)
    def _():
        _run_copies(c_ref, lp_ref, off_ref, i, copy_from(slot), lambda cp: cp.wait())
        zero_ref[...] = jnp.zeros_like(zero_ref)

        def zero_copy(local, glob, size):
            return pltpu.make_async_copy(zero_ref.at[pl.ds(local, size)],
                                         xs_hbm.at[pl.ds(glob, size)], zsem)

        def zero_fill(act):
            for g in range(N_EXPERT_GROUPS):
                _piece_copies(tail_ref[N_EXPERT_GROUPS + g], 0, tail_ref[g], zero_copy, act)
            for j in range(first_spare, max_tiles):
                @pl.when(j >= nact_ref[0])
                def _(j=j):
                    act(zero_copy(0, j * MOE_TM, MOE_TM))

        zero_fill(lambda cp: cp.start())
        zero_fill(lambda cp: cp.wait())


def _moe_group_kernel(grp_ref, nact_ref, xs_ref, wg_ref, wu_ref, wd_ref, o_ref, wg_bf, wu_bf, wd_bf):
    j = pl.program_id(0)

    @pl.when(jnp.logical_or(j == 0, grp_ref[j] != grp_ref[jnp.maximum(j - 1, 0)]))
    def _():
        wg_bf[...] = wg_ref[0].astype(BF16)
        wu_bf[...] = wu_ref[0].astype(BF16)
        wd_bf[...] = wd_ref[0].astype(BF16)

    @pl.when(j >= nact_ref[0])
    def _():
        o_ref[...] = jnp.zeros_like(o_ref)

    @pl.when(j < nact_ref[0])
    def _():
        x = xs_ref[:, 0:D_MODEL]
        gate = (xs_ref[:, D_MODEL:D_MODEL + 128].astype(F32)
                + xs_ref[:, D_MODEL + 128:].astype(F32))
        lane = lax.broadcasted_iota(jnp.int32, gate.shape, 1)
        first = grp_ref[j] * EXPERTS_PER_GROUP
        acc = None
        for e in range(EXPERTS_PER_GROUP):
            ge = jnp.where(lane == first + e, gate, 0.0).sum(axis=-1, keepdims=True)
            hid = _silu(_dot(x, wg_bf[e])) * _dot(x, wu_bf[e]) * ge
            part = _dot(hid.astype(BF16), wd_bf[e])
            acc = part if acc is None else acc + part
        o_ref[...] = acc


def _combine_kernel(c_ref, lp_ref, off_ref, x_ref, gate_ref, fw_ref, ys_hbm, o_ref, z_ref, sem,
                    *, final_norm):
    i = pl.program_id(0)
    nt = pl.num_programs(0)
    slot = i % 2

    def copy_into(s):
        return lambda local, glob, size: pltpu.make_async_copy(
            ys_hbm.at[pl.ds(glob, size)], z_ref.at[s, pl.ds(local, size)], sem.at[s])

    @pl.when(i == 0)
    def _():
        z_ref[...] = jnp.zeros_like(z_ref)
        _run_copies(c_ref, lp_ref, off_ref, 0, copy_into(0), lambda cp: cp.start())

    @pl.when(i + 1 < nt)
    def _():
        _run_copies(c_ref, lp_ref, off_ref, i + 1, copy_into(1 - slot), lambda cp: cp.start())

    _run_copies(c_ref, lp_ref, off_ref, i, copy_into(slot), lambda cp: cp.wait())
    pt = _sort_matrix(gate_ref[...])
    z = z_ref[slot]
    z_hi = z.astype(BF16)
    z_lo = (z - z_hi.astype(F32)).astype(BF16)
    y = x_ref[...] + _dot(pt, z_hi) + _dot(pt, z_lo)
    if final_norm:
        ms = jnp.mean(y * y, axis=-1, keepdims=True)
        y = y * lax.rsqrt(ms + EPS) * fw_ref[...]
    o_ref[...] = y


def _route_meta(gate, max_tiles):
    t = gate.shape[0]
    nt = t // RT_TS
    groups = jnp.arange(N_EXPERT_GROUPS, dtype=jnp.int32)
    g = gate[:, ROUTER_GIDX_LANE].astype(jnp.int32).reshape(nt, RT_TS, 1)
    count = (g == groups).astype(jnp.int32).sum(axis=1)
    lp = (count + RT_ALIGN - 1) // RT_ALIGN * RT_ALIGN
    total = lp.sum(axis=0)
    ntile = (total + MOE_TM - 1) // MOE_TM
    last = jnp.cumsum(ntile)
    first_row = (last - ntile) * MOE_TM
    c = jnp.cumsum(lp, axis=1) - lp
    off = first_row[None, :] + jnp.cumsum(lp, axis=0) - lp
    tail = jnp.concatenate([first_row + total, ntile * MOE_TM - total])
    j = jnp.arange(max_tiles, dtype=jnp.int32)[:, None]
    grp = jnp.minimum((j >= last[None, :]).astype(jnp.int32).sum(axis=1), N_EXPERT_GROUPS - 1)
    return c.reshape(-1), lp.reshape(-1), off.reshape(-1), tail, grp, last[-1:]


def _moe(x, hn, gate, lw, layer, fw, final_norm):
    t = x.shape[0]
    nt = t // RT_TS
    max_tiles = _moe_max_tiles(t)
    c, lp, off, tail, grp, nact = _route_meta(gate, max_tiles)

    def rows(width):
        return pl.BlockSpec((RT_TS, width), lambda i, *_: (i, 0))

    xs = pl.pallas_call(
        functools.partial(_dispatch_kernel, first_spare=t // MOE_TM, max_tiles=max_tiles),
        grid_spec=pltpu.PrefetchScalarGridSpec(
            num_scalar_prefetch=5, grid=(nt,),
            in_specs=[rows(D_MODEL), rows(128)],
            out_specs=pl.BlockSpec(memory_space=pl.ANY),
            scratch_shapes=[pltpu.VMEM((2, RT_ROWS, PAY_COLS), BF16),
                            pltpu.VMEM((MOE_TM, PAY_COLS), BF16),
                            pltpu.SemaphoreType.DMA((2,)), pltpu.SemaphoreType.DMA(())]),
        out_shape=jax.ShapeDtypeStruct((max_tiles * MOE_TM, PAY_COLS), BF16),
        compiler_params=_params("arbitrary"),
    )(c, lp, off, tail, nact, hn, gate)

    def experts(shape):
        return pl.BlockSpec((1, EXPERTS_PER_GROUP) + shape, lambda j, grp, nact: (layer, grp[j], 0, 0))

    def rounded(shape):
        return pltpu.VMEM((EXPERTS_PER_GROUP,) + shape, BF16)

    ys = pl.pallas_call(
        _moe_group_kernel,
        grid_spec=pltpu.PrefetchScalarGridSpec(
            num_scalar_prefetch=2, grid=(max_tiles,),
            in_specs=[pl.BlockSpec((MOE_TM, PAY_COLS), lambda j, grp, nact: (j, 0)),
                      experts((D_MODEL, D_EXPERT)), experts((D_MODEL, D_EXPERT)),
                      experts((D_EXPERT, D_MODEL))],
            out_specs=pl.BlockSpec((MOE_TM, D_MODEL), lambda j, grp, nact: (j, 0)),
            scratch_shapes=[rounded((D_MODEL, D_EXPERT)), rounded((D_MODEL, D_EXPERT)),
                            rounded((D_EXPERT, D_MODEL))]),
        out_shape=jax.ShapeDtypeStruct((max_tiles * MOE_TM, D_MODEL), F32),
        compiler_params=_params("arbitrary"),
    )(grp, nact, xs, lw["wg"], lw["wu"], lw["wd"])

    return pl.pallas_call(
        functools.partial(_combine_kernel, final_norm=final_norm),
        grid_spec=pltpu.PrefetchScalarGridSpec(
            num_scalar_prefetch=3, grid=(nt,),
            in_specs=[rows(D_MODEL), rows(128), pl.BlockSpec((1, D_MODEL), lambda i, *_: (0, 0)),
                      pl.BlockSpec(memory_space=pl.ANY)],
            out_specs=rows(D_MODEL),
            scratch_shapes=[pltpu.VMEM((2, RT_ROWS, D_MODEL), F32),
                            pltpu.SemaphoreType.DMA((2,))]),
        out_shape=jax.ShapeDtypeStruct((t, D_MODEL), F32),
        compiler_params=_params("arbitrary"),
    )(c, lp, off, x, gate, fw, ys)


def _lane_place(vec, start, width=128):
    return jnp.zeros((1, width), F32).at[0, start:start + vec.shape[0]].set(vec.astype(F32))


REL_PAD = 704


def _toeplitz(v, rows, cols):
    n = v.shape[-1]
    tiled = jnp.tile(v, (1,) * (v.ndim - 1) + (rows,))[..., :rows * (n - 1)]
    return tiled.reshape(v.shape[:-1] + (rows, n - 1))[..., :cols]


def _band_bias(table):
    t = table.astype(F32)
    ext = jnp.concatenate([jnp.repeat(t[:, :1], REL_PAD, axis=1), t,
                           jnp.repeat(t[:, -1:], REL_PAD, axis=1)], axis=1)
    extr = ext[:, ::-1]
    top = ext.shape[1] - 1 - (REL_CLIP + REL_PAD)

    def band(off, nq, nk):
        n = nq + nk
        v = jnp.concatenate([extr[:, top - off:top - off + nk + 1],
                             extr[:, top - off - (nq - 1):top - off]], axis=1)
        assert v.shape[1] == n
        return _toeplitz(v, nq, nk)

    sample = band(WINDOW, CHUNK, BAND)
    kc = (jnp.arange(ATT_KB) // CHUNK)[:, None]
    qc = (jnp.arange(ATT_QB) // CHUNK)[None, :]
    block = lambda off: jnp.swapaxes(band(off, ATT_QB, ATT_KB), 1, 2)
    prompt = jnp.stack([jnp.where((kc >= qc)[None], block(2 * ATT_KB), -1e30),
                        block(ATT_KB),
                        jnp.where((kc <= qc)[None], block(0), -1e30)])
    return sample * LOG2_E, prompt * LOG2_E


def _layer_weights(i, norm1_w, w_in, rel_bias_table, gla_w_gate_up, gla_b_gate, gla_norm_w,
                   ssm_conv_w, ssm_conv_b, ssm_dt_bias, ssm_a_log, ssm_d, ssm_norm_w, w_out,
                   norm2_w, router_group_w, router_group_b, router_expert_w, router_expert_b,
                   exp_w_gate, exp_w_up, exp_w_down):
    bias_s, bias_p = _band_bias(rel_bias_table[i])
    rw = jnp.concatenate([router_expert_w[i], router_group_w[i],
                          jnp.zeros((D_MODEL, 128 - N_EXPERTS - N_EXPERT_GROUPS), F32)], axis=1)
    rw_hi = rw.astype(BF16)
    return dict(
        n1w=norm1_w[i][None, :],
        w_in=w_in,
        bias_s=bias_s,
        bias_p=bias_p,
        gup=jnp.zeros((128, 128), F32).at[:GATE_RANK, :].set(gla_w_gate_up[i]).astype(BF16),
        gb=gla_b_gate[i][None, :].astype(F32),
        gnw=jnp.tile(gla_norm_w[i], H_B)[None, :].astype(F32),
        cw=ssm_conv_w[i].astype(F32),
        cb=ssm_conv_b[i][None, :].astype(F32),
        dtb=_lane_place(ssm_dt_bias[i], MISC_DT_LANE),
        alog=jnp.repeat(ssm_a_log[i].astype(F32), P_C)[None, :],
        dsk=jnp.repeat(ssm_d[i].astype(F32), P_C)[None, :],
        snw=ssm_norm_w[i][None, :].astype(F32),
        wo=w_out,
        n2w=norm2_w[i][None, :],
        rw_hi=rw_hi,
        rw_lo=(rw - rw_hi.astype(F32)).astype(BF16),
        rb=jnp.concatenate([router_expert_b[i], router_group_b[i],
                            jnp.zeros((128 - N_EXPERTS - N_EXPERT_GROUPS,), F32)])[None, :],
        wg=exp_w_gate,
        wu=exp_w_up,
        wd=exp_w_down,
    )


def _stream_layer(x, bsz, lw, layer, cache_k, cache_v, gla0, conv0, ssm0, fw, final_norm):
    l = x.shape[0] // bsz
    if cache_k is None:
        qk, rest, k_keep, v_keep, vt = _in_proj(x, lw["n1w"], lw["w_in"], layer, True, l)
        att = _attn_prompt(qk.reshape(bsz, l, 2 * WIDTH_A),
                           vt.reshape(bsz, l // ATT_KB, WIDTH_A, ATT_KB), lw["bias_p"])
    else:
        qk, rest, k_keep, v_keep, kv = _in_proj(x, lw["n1w"], lw["w_in"], layer, False, l)
        att = _attn_sample(qk.reshape(bsz, l, 2 * WIDTH_A), kv.reshape(bsz, l, 2 * WIDTH_A),
                           cache_k, cache_v, layer, lw["bias_s"])
    gla_o, ssm_o, gla_s, conv_s, ssm_s = _scan(rest.reshape(bsz, l, REST_COLS), lw, gla0, conv0, ssm0)
    t = bsz * l
    x1, hn, gate = _out_proj(x, att.reshape(t, WIDTH_A), gla_o.reshape(t, WIDTH_B),
                             ssm_o.reshape(t, WIDTH_C), lw, layer)
    x2 = _moe(x1, hn, gate, lw, layer, fw, final_norm)
    keep = min(WINDOW, l)
    k_keep = k_keep.reshape(bsz, keep, H_A, DH_A)
    v_keep = v_keep.reshape(bsz, keep, H_A, DH_A)
    gla_state = gla_s.reshape(bsz, DV_B, H_B, DK_B).transpose(0, 2, 3, 1)
    conv_state = conv_s[:, 8 - (CONV_W - 1):, :]
    return x2, k_keep, v_keep, gla_state, conv_state, ssm_s


def kernel(x_prompt, x_sample, cache_k_a, cache_v_a, state_gla, state_conv, state_ssm, norm1_w, w_in, rel_bias_table, gla_w_gate_up, gla_b_gate, gla_norm_w, ssm_conv_w, ssm_conv_b, ssm_dt_bias, ssm_a_log, ssm_d, ssm_norm_w, w_out, norm2_w, router_group_w, router_group_b, router_expert_w, router_expert_b, exp_w_gate, exp_w_up, exp_w_down, final_norm_w):
    bp, sp, _ = x_prompt.shape
    bs, ss, _ = x_sample.shape
    xp = x_prompt.reshape(bp * sp, D_MODEL)
    xs = x_sample.reshape(bs * ss, D_MODEL)
    fw = final_norm_w[None, :].astype(F32)
    outs_p, outs_s = [], []
    cache_k = cache_k_a.astype(BF16).reshape(DEPTH, bs, -1, WIDTH_A)
    cache_v = cache_v_a.astype(BF16).reshape(DEPTH, bs, -1, WIDTH_A)
    for i in range(DEPTH):
        lw = _layer_weights(i, norm1_w, w_in, rel_bias_table, gla_w_gate_up, gla_b_gate, gla_norm_w,
                            ssm_conv_w, ssm_conv_b, ssm_dt_bias, ssm_a_log, ssm_d, ssm_norm_w, w_out,
                            norm2_w, router_group_w, router_group_b, router_expert_w, router_expert_b,
                            exp_w_gate, exp_w_up, exp_w_down)
        last = i == DEPTH - 1
        xp, *sp_out = _stream_layer(
            xp, bp, lw, i, None, None,
            jnp.zeros((bp, DV_B, H_B * DK_B), F32),
            jnp.zeros((bp, 8, CONV_CH), F32),
            jnp.zeros((bp, H_C, P_C, D_STATE), F32), fw, last)
        outs_p.append(sp_out)
        gla0 = state_gla[i].astype(F32).transpose(0, 3, 1, 2).reshape(bs, DV_B, H_B * DK_B)
        conv0 = jnp.pad(state_conv[i].astype(F32), ((0, 0), (8 - (CONV_W - 1), 0), (0, 0)))
        xs, *ss_out = _stream_layer(
            xs, bs, lw, i, cache_k, cache_v,
            gla0, conv0, state_ssm[i].astype(F32), fw, last)
        outs_s.append(ss_out)
    stack = lambda outs, j: jnp.stack([o[j] for o in outs])
    return (xp.reshape(bp, sp, D_MODEL), xs.reshape(bs, ss, D_MODEL),
            stack(outs_p, 0), stack(outs_p, 1), stack(outs_p, 2), stack(outs_p, 3), stack(outs_p, 4),
            stack(outs_s, 0), stack(outs_s, 1), stack(outs_s, 2), stack(outs_s, 3), stack(outs_s, 4))
```

```python
import functools

import jax
import jax.numpy as jnp
from jax import lax
from jax.experimental import pallas as pl
from jax.experimental.pallas import tpu as pltpu

F32 = jnp.float32
BF16 = jnp.bfloat16

D_MODEL = 1024
DEPTH = 2
EPS = 1e-6
CHUNK = 64
N_PAST_CHUNKS = 8
BAND = (N_PAST_CHUNKS + 1) * CHUNK
WINDOW = N_PAST_CHUNKS * CHUNK
H_A, DH_A, WIDTH_A = 8, 64, 512
REL_CLIP = 128
LOG2_E = 1.4426950408889634
H_B, DK_B, DV_B, WIDTH_B = 4, 32, 64, 256
GATE_RANK = 16
GATE_TAU = 16.0
H_C, P_C, WIDTH_C = 4, 64, 256
N_GROUPS_C = 2
D_STATE = 64
CONV_W = 4
CONV_CH = 512
N_EXPERT_GROUPS = 4
EXPERTS_PER_GROUP = 4
N_EXPERTS = 16
D_EXPERT = 256

PROJ_COLS = 3200
MISC_DT_LANE = GATE_RANK
ROUTER_GROUP_LANE = N_EXPERTS
ROUTER_GIDX_LANE = 20
ROUTER_DEST_LANE = 21

SEQ_GROUP = 8
SEQ_UNROLL = 8
VMEM_LIMIT_BYTES = 56 * 1024 * 1024

NT_DIMS = (((1,), (1,)), ((), ()))
TN_DIMS = (((0,), (0,)), ((), ()))


def _params(*sem):
    return pltpu.CompilerParams(dimension_semantics=sem, vmem_limit_bytes=VMEM_LIMIT_BYTES)


def _dot(a, b):
    return jnp.dot(a, b, preferred_element_type=F32)


def _dot_nt(a, b):
    return lax.dot_general(a, b, NT_DIMS, preferred_element_type=F32)


def _dot_tn(a, b):
    return lax.dot_general(a, b, TN_DIMS, preferred_element_type=F32)


def _split2(a):
    hi = a.astype(BF16)
    return hi, (a - hi.astype(F32)).astype(BF16)


def _dot01_left(m01, a):
    hi, lo = _split2(a)
    return _dot(m01, hi) + _dot(m01, lo)


def _dot01_right(a, m01):
    hi, lo = _split2(a)
    return _dot(hi, m01) + _dot(lo, m01)


def _round_robin(gens):
    results = [None] * len(gens)
    live = list(range(len(gens)))
    while live:
        for idx in list(live):
            try:
                next(gens[idx])
            except StopIteration as stop:
                results[idx] = stop.value
                live.remove(idx)
    return results


def _silu(x):
    return x * jax.nn.sigmoid(x)


def _softplus(x):
    return jnp.maximum(x, 0.0) + jnp.log1p(jnp.exp(-jnp.abs(x)))


IN_TM = 256
QKV_COLS = 3 * WIDTH_A
REST_COLS = PROJ_COLS - QKV_COLS
REST_CHUNKS = ((0, 512), (512, 1024), (1024, REST_COLS))


IN_COLS = 3092
IN_TAIL = QKV_COLS + 512
PREP_ROWS = 128


def _prepare_w_in(w_in_ref, w_ref):
    scale = DH_A ** -0.5 * LOG2_E
    n_tail = REST_COLS - 512 - 128
    for r in range(0, D_MODEL, PREP_ROWS):
        rs = slice(r, r + PREP_ROWS)
        w_ref[rs, 0:WIDTH_A] = (w_in_ref[0, rs, 0:WIDTH_A] * scale).astype(BF16)
        w_ref[rs, WIDTH_A:IN_TAIL] = w_in_ref[0, rs, WIDTH_A:IN_TAIL].astype(BF16)
        tail = w_in_ref[0, rs, IN_TAIL:IN_COLS]
        w_ref[rs, IN_TAIL:IN_TAIL + n_tail] = tail[:, GATE_RANK:GATE_RANK + n_tail].astype(BF16)
        misc = jnp.concatenate([tail[:, 0:GATE_RANK], tail[:, GATE_RANK + n_tail:],
                                jnp.zeros((PREP_ROWS, 128 - GATE_RANK - H_C), F32)], axis=1)
        w_ref[rs, IN_TAIL + n_tail:] = misc.astype(BF16)


def _in_proj_kernel(x_ref, nw_ref, w_in_ref, *refs, with_vt):
    if with_vt:
        qk_ref, rest_ref, kk_ref, vk_ref, vt_ref, w_ref = refs
    else:
        qk_ref, rest_ref, kk_ref, vk_ref, kv_ref, w_ref = refs

    @pl.when(pl.program_id(0) == 0)
    def _():
        _prepare_w_in(w_in_ref, w_ref)

    x = x_ref[...]
    ms = jnp.mean(x * x, axis=-1, keepdims=True)
    xn = (x * lax.rsqrt(ms + EPS) * nw_ref[...]).astype(BF16)
    qk_ref[:, 0:WIDTH_A] = _dot(xn, w_ref[:, 0:WIDTH_A]).astype(BF16)
    k = _dot(xn, w_ref[:, WIDTH_A:2 * WIDTH_A])
    v = _dot(xn, w_ref[:, 2 * WIDTH_A:QKV_COLS])
    qk_ref[:, WIDTH_A:] = k.astype(BF16)
    if not with_vt:
        kv_ref[:, 0:WIDTH_A] = k
        kv_ref[:, WIDTH_A:] = v

    def keep_rows():
        kk = kk_ref.reshape(IN_TM * H_A, DH_A)
        vk = vk_ref.reshape(IN_TM * H_A, DH_A)
        for h in range(H_A):
            sl = slice(DH_A * h, DH_A * (h + 1))
            kk[pl.ds(h, IN_TM, stride=H_A), :] = k[:, sl]
            vk[pl.ds(h, IN_TM, stride=H_A), :] = v[:, sl]

    keep_rows()
    for lo, hi in REST_CHUNKS:
        rest_ref[:, lo:hi] = _dot(xn, w_ref[:, QKV_COLS + lo:QKV_COLS + hi])
    if with_vt:
        vt_ref[0] = v.T.astype(BF16)


def _in_proj(x, nw, w_in, layer, with_vt, seq_len):
    t = x.shape[0]
    tiles_per_seq = max(seq_len // IN_TM, 1)
    keep_tiles = max(min(WINDOW, seq_len) // IN_TM, 1)
    n_keep = t // IN_TM // tiles_per_seq * keep_tiles

    def rows(width):
        return pl.BlockSpec((IN_TM, width), lambda i: (i, 0))

    def kept(i):
        return (i // tiles_per_seq * keep_tiles
                + jnp.maximum(i % tiles_per_seq - (tiles_per_seq - keep_tiles), 0), 0, 0)

    keep_spec = pl.BlockSpec((IN_TM, H_A, DH_A), kept)
    keep_shape = jax.ShapeDtypeStruct((n_keep * IN_TM, H_A, DH_A), F32)
    in_specs = [rows(D_MODEL), pl.BlockSpec((1, D_MODEL), lambda i: (0, 0)),
                pl.BlockSpec((1, D_MODEL, PROJ_COLS), lambda i: (layer, 0, 0))]
    scratch = [pltpu.VMEM((D_MODEL, PROJ_COLS), BF16)]
    out_specs = [rows(2 * WIDTH_A), rows(REST_COLS), keep_spec, keep_spec]
    out_shape = [jax.ShapeDtypeStruct((t, 2 * WIDTH_A), BF16),
                 jax.ShapeDtypeStruct((t, REST_COLS), F32), keep_shape, keep_shape]
    if with_vt:
        out_specs.append(pl.BlockSpec((1, WIDTH_A, IN_TM), lambda i: (i, 0, 0)))
        out_shape.append(jax.ShapeDtypeStruct((t // IN_TM, WIDTH_A, IN_TM), BF16))
    else:
        out_specs.append(rows(2 * WIDTH_A))
        out_shape.append(jax.ShapeDtypeStruct((t, 2 * WIDTH_A), F32))
    return pl.pallas_call(
        functools.partial(_in_proj_kernel, with_vt=with_vt),
        grid=(t // IN_TM,),
        in_specs=in_specs, out_specs=out_specs, out_shape=out_shape, scratch_shapes=scratch,
        compiler_params=_params("arbitrary"),
    )(x, nw, w_in)


ATT_QB = 4 * CHUNK
ATT_KB = IN_TM
ATT_NKB = (ATT_QB + WINDOW) // ATT_KB


def _attn_prompt_kernel(q_ref, k_ref, vt_ref, bias_ref, o_ref, ot_ref):
    blk = pl.program_id(1)
    sb = jnp.maximum(blk - WINDOW // ATT_KB, 0)
    kstart = pl.multiple_of(sb * ATT_KB, ATT_KB)
    lane = lax.broadcasted_iota(jnp.int32, (ATT_QB, 128), 1)

    def attend(kinds):
        def scores(h):
            pair = slice(128 * (h // 2), 128 * (h // 2 + 1))
            qp = q_ref[0, :, pair]
            qz = jnp.where((lane >= DH_A) == bool(h % 2), qp, jnp.zeros_like(qp))
            return [_dot_nt(k_ref[0, pl.ds(kstart + ATT_KB * i, ATT_KB), pair], qz)
                    + bias_ref[kind, h] for i, kind in enumerate(kinds)]

        def finish(h, s):
            m = s[0].max(axis=0, keepdims=True)
            for si in s[1:]:
                m = jnp.maximum(m, si.max(axis=0, keepdims=True))
            l = None
            ot = None
            for i, si in enumerate(s):
                p = jnp.exp2(si - m)
                li = p.sum(axis=0, keepdims=True)
                oi = _dot(vt_ref[0, sb + i, DH_A * h:DH_A * (h + 1), :], p.astype(BF16))
                l = li if l is None else l + li
                ot = oi if ot is None else ot + oi
            ot_ref[DH_A * h:DH_A * (h + 1), :] = ot / l

        pending = scores(0)
        for h in range(H_A):
            upcoming = scores(h + 1) if h + 1 < H_A else None
            finish(h, pending)
            pending = upcoming
        o_ref[0] = ot_ref[...].T.astype(o_ref.dtype)

    pl.when(blk == 0)(lambda: attend((2,)))
    pl.when(blk == 1)(lambda: attend((1, 2)))
    pl.when(blk >= 2)(lambda: attend((0, 1, 2)))


def _attn_prompt(qk, vt, bias_t):
    bsz, s, _ = qk.shape
    nkb = s // ATT_KB
    return pl.pallas_call(
        _attn_prompt_kernel,
        grid=(bsz, s // ATT_QB),
        in_specs=[pl.BlockSpec((1, ATT_QB, WIDTH_A), lambda b, c: (b, c, 0)),
                  pl.BlockSpec((1, s, WIDTH_A), lambda b, c: (b, 0, 1)),
                  pl.BlockSpec((1, nkb, WIDTH_A, ATT_KB), lambda b, c: (b, 0, 0, 0)),
                  pl.BlockSpec((3, H_A, ATT_KB, ATT_QB), lambda b, c: (0, 0, 0, 0))],
        out_specs=pl.BlockSpec((1, ATT_QB, WIDTH_A), lambda b, c: (b, c, 0)),
        out_shape=jax.ShapeDtypeStruct((bsz, s, WIDTH_A), BF16),
        scratch_shapes=[pltpu.VMEM((WIDTH_A, ATT_QB), F32)],
        compiler_params=_params("parallel", "arbitrary"),
    )(qk, qk, vt, bias_t)


def _attn_sample_kernel(q_ref, kv_ref, ck_ref, cv_ref, bias_ref, o_ref):
    lane = lax.broadcasted_iota(jnp.int32, (CHUNK, 128), 1)

    def scores(h):
        pair = slice(128 * (h // 2), 128 * (h // 2 + 1))
        qp = q_ref[0, :, pair]
        qz = jnp.where((lane >= DH_A) == bool(h % 2), qp, jnp.zeros_like(qp))
        return (_dot_nt(qz, ck_ref[0, 0, :, pair]) + bias_ref[h, :, 0:WINDOW],
                _dot_nt(qz, kv_ref[0, :, pair].astype(BF16)) + bias_ref[h, :, WINDOW:BAND])

    def finish(h, s):
        s1, s2 = s
        pair = slice(128 * (h // 2), 128 * (h // 2 + 1))
        vpair = slice(WIDTH_A + 128 * (h // 2), WIDTH_A + 128 * (h // 2 + 1))
        half = slice(DH_A * (h % 2), DH_A * (h % 2 + 1))
        m = jnp.maximum(s1.max(axis=-1, keepdims=True), s2.max(axis=-1, keepdims=True))
        p1 = jnp.exp2(s1 - m)
        p2 = jnp.exp2(s2 - m)
        l = p1.sum(axis=-1, keepdims=True) + p2.sum(axis=-1, keepdims=True)
        o = (_dot(p1.astype(BF16), cv_ref[0, 0, :, pair])
             + _dot(p2.astype(BF16), kv_ref[0, :, vpair].astype(BF16)))
        o_ref[0, :, DH_A * h:DH_A * (h + 1)] = (o[:, half] / l).astype(o_ref.dtype)

    pending = scores(0)
    for h in range(H_A):
        upcoming = scores(h + 1) if h + 1 < H_A else None
        finish(h, pending)
        pending = upcoming


def _attn_sample(qk, kv, cache_k, cache_v, layer, bias):
    bsz = qk.shape[0]
    cache_spec = pl.BlockSpec((1, 1, WINDOW, WIDTH_A), lambda b: (layer, b, 0, 0))
    return pl.pallas_call(
        _attn_sample_kernel,
        grid=(bsz,),
        in_specs=[pl.BlockSpec((1, CHUNK, WIDTH_A), lambda b: (b, 0, 0)),
                  pl.BlockSpec((1, CHUNK, 2 * WIDTH_A), lambda b: (b, 0, 0)),
                  cache_spec, cache_spec,
                  pl.BlockSpec((H_A, CHUNK, BAND), lambda b: (0, 0, 0))],
        out_specs=pl.BlockSpec((1, CHUNK, WIDTH_A), lambda b: (b, 0, 0)),
        out_shape=jax.ShapeDtypeStruct((bsz, CHUNK, WIDTH_A), BF16),
        compiler_params=_params("parallel"),
    )(qk, kv, cache_k, cache_v, bias)


def _scan_kernel(bqk_ref, bv_ref, bog_ref, cz_ref, cx_ref, misc_ref,
                 gup_ref, gb_ref, gnw_ref, cw_ref, cb_ref, dtb_ref, alog_ref, dsk_ref, snw_ref,
                 gla0_ref, conv0_ref, ssm0_ref,
                 gla_o_ref, ssm_o_ref, gla_s_ref, conv_s_ref, ssm_s_ref,
                 xpad_ref):
    c = pl.program_id(1)
    nseq = bqk_ref.shape[0]

    @pl.when(c == 0)
    def _():
        gla_s_ref[...] = gla0_ref[...]
        conv_s_ref[...] = conv0_ref[...]
        ssm_s_ref[...] = ssm0_ref[...]

    r64 = lax.broadcasted_iota(jnp.int32, (CHUNK, CHUNK), 0)
    c64 = lax.broadcasted_iota(jnp.int32, (CHUNK, CHUNK), 1)
    tri = c64 <= r64
    tri_bf = jnp.where(tri, 1.0, 0.0).astype(BF16)
    ones_bf = jnp.ones((CHUNK, CHUNK), BF16)
    row_x = lax.broadcasted_iota(jnp.int32, (CHUNK, WIDTH_C), 0)
    s_x = lax.broadcasted_iota(jnp.int32, (CHUNK, WIDTH_C), 1) & (CHUNK - 1)
    causal_x = s_x <= row_x
    upper_x = row_x <= s_x
    er = lax.broadcasted_iota(jnp.int32, (128, WIDTH_C), 0)
    el = lax.broadcasted_iota(jnp.int32, (128, WIDTH_C), 1)
    expand_bf = jnp.where(er == MISC_DT_LANE + (el >> 6), 1.0, 0.0).astype(BF16)
    br = lax.broadcasted_iota(jnp.int32, (WIDTH_C, WIDTH_C), 0)
    bl = lax.broadcasted_iota(jnp.int32, (WIDTH_C, WIDTH_C), 1)
    blk64_bf = jnp.where((br >> 6) == (bl >> 6), 1.0, 0.0).astype(BF16)
    blk128_bf = jnp.where((br >> 7) == (bl >> 7), 1.0, 0.0).astype(BF16)

    gup = gup_ref[...]
    gb = gb_ref[...]
    gnw = gnw_ref[...]
    cb = cb_ref[...]
    dtb = dtb_ref[...]
    neg_a = -jnp.exp(alog_ref[...])
    dsk = dsk_ref[...]
    snw = snw_ref[...]

    def load_seq(b):
        return dict(misc=misc_ref[b], qk=bqk_ref[b], v=bv_ref[b], bog=bog_ref[b], cz=cz_ref[b],
                    cx=cx_ref[b], gla=gla_s_ref[b], conv=conv_s_ref[b],
                    ssm=[ssm_s_ref[b, h] for h in range(H_C)])

    def store_seq(b, out):
        gla_o_ref[b] = out["gla_o"]
        ssm_o_ref[b] = out["ssm_o"]
        gla_s_ref[b] = out["gla"]
        conv_s_ref[b] = out["conv"]
        for h in range(H_C):
            ssm_s_ref[b, h] = out["ssm"][h]

    def compute_seq(inp, slot):
        out = {}
        misc = inp["misc"]

        z = _dot(misc.astype(BF16), gup) + gb
        g = (jnp.minimum(z, 0.0) - jnp.log1p(jnp.exp(-jnp.abs(z)))) * (1.0 / GATE_TAU)
        bcum = _dot01_left(tri_bf, g)
        yield
        q = inp["qk"][:, :H_B * DK_B]
        k = inp["qk"][:, H_B * DK_B:]
        blast = bcum[CHUNK - 1:CHUNK, :]
        qb = (q * (DK_B ** -0.5) * jnp.exp(bcum)).astype(BF16)
        kb = (k * jnp.exp(-bcum)).astype(BF16)
        kl = (k * jnp.exp(blast - bcum)).astype(BF16)
        vb = inp["v"].astype(BF16)
        st = inp["gla"]
        st_bf = st.astype(BF16)
        yield
        o_parts, s_parts = [], []
        for h in range(H_B):
            ks = slice(DK_B * h, DK_B * (h + 1))
            vs = slice(DV_B * h, DV_B * (h + 1))
            att = jnp.where(tri, _dot_nt(qb[:, ks], kb[:, ks]), 0.0).astype(BF16)
            o_parts.append(_dot(att, vb[:, vs]) + _dot_nt(qb[:, ks], st_bf[:, ks]))
            s_parts.append(_dot_tn(vb[:, vs], kl[:, ks]))
            yield
        o = jnp.concatenate(o_parts, axis=1)
        out["gla"] = jnp.exp(blast) * st + jnp.concatenate(s_parts, axis=1)
        msq = _dot01_right(o * o, blk64_bf) * (1.0 / DV_B)
        yield
        go = o * lax.rsqrt(msq + EPS) * gnw * _silu(inp["bog"])
        out["gla_o"] = go.astype(gla_o_ref.dtype)
        yield

        cx = inp["cx"]
        xpad_ref[slot, 0:8, :] = inp["conv"]
        xpad_ref[slot, 8:8 + CHUNK, :] = cx
        conv = cb + cx * cw_ref[CONV_W - 1:CONV_W, :]
        for i in range(CONV_W - 1):
            off = 8 - (CONV_W - 1) + i
            conv = conv + xpad_ref[slot, off:off + CHUNK, :] * cw_ref[i:i + 1, :]
        out["conv"] = cx[CHUNK - 8:, :]
        conv = _silu(conv)
        sx = conv[:, :WIDTH_C]
        sb_bf = conv[:, WIDTH_C:WIDTH_C + N_GROUPS_C * D_STATE].astype(BF16)
        sc_bf = conv[:, WIDTH_C + N_GROUPS_C * D_STATE:].astype(BF16)
        yield

        dtv = _softplus(misc + dtb)
        dt_x = _dot01_right(dtv, expand_bf)
        yield
        a_x = neg_a * dt_x
        cs_x = _dot01_left(tri_bf, a_x)
        cs_row = _dot01_left(ones_bf, jnp.where(upper_x, a_x, 0.0))
        yield
        decay = jnp.exp(jnp.where(causal_x, cs_x - cs_row, -jnp.inf))
        cb_parts = []
        for gidx in range(N_GROUPS_C):
            ns = slice(D_STATE * gidx, D_STATE * (gidx + 1))
            cbm = _dot_nt(sc_bf[:, ns], sb_bf[:, ns])
            cb_parts += [cbm] * (H_C // N_GROUPS_C)
            yield
        scores = (jnp.concatenate(cb_parts, axis=1) * decay).astype(BF16)
        xdt = (sx * dt_x).astype(BF16)
        ecs = jnp.exp(cs_x)
        cs_last = cs_x[CHUNK - 1:CHUNK, :]
        xw = (sx * (jnp.exp(cs_last - cs_x) * dt_x)).astype(BF16)
        e_last = jnp.exp(cs_last)
        yield
        y_parts = []
        out["ssm"] = []
        for h in range(H_C):
            ps = slice(P_C * h, P_C * (h + 1))
            gi = h // (H_C // N_GROUPS_C)
            ns = slice(D_STATE * gi, D_STATE * (gi + 1))
            hst = inp["ssm"][h]
            y_parts.append(_dot(scores[:, ps], xdt[:, ps])
                           + _dot_nt(sc_bf[:, ns], hst.astype(BF16)) * ecs[:, ps])
            out["ssm"].append(e_last[:, ps] * hst + _dot_tn(xw[:, ps], sb_bf[:, ns]))
            yield
        y = jnp.concatenate(y_parts, axis=1) + dsk * sx
        y = y * _silu(inp["cz"])
        ymsq = _dot01_right(y * y, blk128_bf) * (1.0 / (WIDTH_C // N_GROUPS_C))
        yield
        out["ssm_o"] = (y * lax.rsqrt(ymsq + EPS) * snw).astype(ssm_o_ref.dtype)
        return out

    def group_body(i, carry):
        seqs = [i * SEQ_UNROLL + u for u in range(SEQ_UNROLL)]
        inputs = [load_seq(b) for b in seqs]
        outputs = _round_robin([compute_seq(inp, slot) for slot, inp in enumerate(inputs)])
        for b, out in zip(seqs, outputs):
            store_seq(b, out)
        return carry

    lax.fori_loop(0, nseq // SEQ_UNROLL, group_body, 0)


def _scan(proj, lw, gla0, conv0, ssm0):
    bsz, l, _ = proj.shape
    nc = l // CHUNK
    sg = min(SEQ_GROUP, bsz)
    ng = bsz // sg

    def col(width, idx):
        return pl.BlockSpec((sg, CHUNK, width), lambda g, c: (g, c, idx))

    def const(shape):
        return pl.BlockSpec(shape, lambda g, c: (0,) * len(shape))

    def state(shape):
        return pl.BlockSpec((sg,) + shape, lambda g, c: (g,) + (0,) * len(shape))

    gla_shape = (DV_B, H_B * DK_B)
    conv_shape = (8, CONV_CH)
    ssm_shape = (H_C, P_C, D_STATE)
    return pl.pallas_call(
        _scan_kernel,
        grid=(ng, nc),
        in_specs=[col(256, 0), col(256, 1), col(256, 2), col(256, 3), col(512, 2), col(128, 12),
                  const((128, 128)), const((1, 128)), const((1, WIDTH_B)),
                  const((CONV_W, CONV_CH)), const((1, CONV_CH)),
                  const((1, 128)), const((1, WIDTH_C)), const((1, WIDTH_C)), const((1, WIDTH_C)),
                  state(gla_shape), state(conv_shape), state(ssm_shape)],
        out_specs=[pl.BlockSpec((sg, CHUNK, WIDTH_B), lambda g, c: (g, c, 0)),
                   pl.BlockSpec((sg, CHUNK, WIDTH_C), lambda g, c: (g, c, 0)),
                   state(gla_shape), state(conv_shape), state(ssm_shape)],
        out_shape=[jax.ShapeDtypeStruct((bsz, l, WIDTH_B), BF16),
                   jax.ShapeDtypeStruct((bsz, l, WIDTH_C), BF16),
                   jax.ShapeDtypeStruct((bsz,) + gla_shape, F32),
                   jax.ShapeDtypeStruct((bsz,) + conv_shape, F32),
                   jax.ShapeDtypeStruct((bsz,) + ssm_shape, F32)],
        scratch_shapes=[pltpu.VMEM((SEQ_UNROLL, 8 + CHUNK, CONV_CH), F32)],
        compiler_params=_params("parallel", "arbitrary"),
    )(proj, proj, proj, proj, proj, proj,
      lw["gup"], lw["gb"], lw["gnw"], lw["cw"], lw["cb"], lw["dtb"], lw["alog"], lw["dsk"], lw["snw"],
      gla0, conv0, ssm0)


OUT_TM = 512
OUT_SUB = 256


def _route(logits):
    lane = lax.broadcasted_iota(jnp.int32, logits.shape, 1)
    lane_f = lane.astype(F32)
    neg = -jnp.inf
    big = 1000.0
    gmask = (lane >= ROUTER_GROUP_LANE) & (lane < ROUTER_GROUP_LANE + N_EXPERT_GROUPS)
    lg = jnp.where(gmask, logits, neg)
    gmax = lg.max(axis=-1, keepdims=True)
    g_idx = jnp.where(lg == gmax, lane_f, big).min(axis=-1, keepdims=True) - ROUTER_GROUP_LANE
    p_grp = 1.0 / jnp.where(gmask, jnp.exp(logits - gmax), 0.0).sum(axis=-1, keepdims=True)
    emask = (lane < N_EXPERTS) & ((lane >> 2).astype(F32) == g_idx)
    le = jnp.where(emask, logits, neg)
    v1 = le.max(axis=-1, keepdims=True)
    i1 = jnp.where(le == v1, lane_f, big).min(axis=-1, keepdims=True)
    le2 = jnp.where(lane_f == i1, neg, le)
    v2 = le2.max(axis=-1, keepdims=True)
    i2 = jnp.where(le2 == v2, lane_f, big).min(axis=-1, keepdims=True)
    e2 = jnp.exp(v2 - v1)
    den = 1.0 + e2
    n = logits.shape[0]
    onehot = lane_f == g_idx
    member = jnp.where(onehot, 1.0, 0.0)
    r = lax.broadcasted_iota(jnp.int32, (n, n), 0)
    c = lax.broadcasted_iota(jnp.int32, (n, n), 1)
    before = _dot(jnp.where(c < r, 1.0, 0.0).astype(BF16), member.astype(BF16))
    padded = jnp.floor((member.sum(axis=0, keepdims=True) + (RT_ALIGN - 1)) * (1.0 / RT_ALIGN)) * RT_ALIGN
    lane1 = lane[0:1, :]
    start = jnp.zeros((1, 128), F32)
    for g in range(N_EXPERT_GROUPS - 1):
        run = jnp.where(lane1 == g, padded, 0.0).sum(axis=1, keepdims=True)
        start = start + jnp.where(lane1 > g, run, 0.0)
    dest = jnp.where(onehot, before + start, 0.0).sum(axis=1, keepdims=True)
    return (jnp.where(lane_f == i1, (1.0 / den) * p_grp, 0.0)
            + jnp.where(lane_f == i2, (e2 / den) * p_grp, 0.0)
            + jnp.where(lane == ROUTER_GIDX_LANE, g_idx, 0.0)
            + jnp.where(lane == ROUTER_DEST_LANE, dest, 0.0))


def _out_proj_kernel(x_ref, att_ref, gla_ref, ssm_ref, w_out_ref, nw_ref, rwh_ref, rwl_ref, rb_ref,
                     x1_ref, hn_ref, gate_ref, wo_ref):
    @pl.when(pl.program_id(0) == 0)
    def _():
        for r in range(0, D_MODEL, PREP_ROWS):
            wo_ref[r:r + PREP_ROWS, :] = w_out_ref[0, r:r + PREP_ROWS, :].astype(BF16)

    def sub_tile(r0):
        rows = slice(r0, r0 + OUT_SUB)
        mo = (_dot(att_ref[rows, :], wo_ref[0:WIDTH_A, :])
              + _dot(gla_ref[rows, :], wo_ref[WIDTH_A:WIDTH_A + WIDTH_B, :])
              + _dot(ssm_ref[rows, :], wo_ref[WIDTH_A + WIDTH_B:, :]))
        yield
        x1 = x_ref[rows, :] + mo
        x1_ref[rows, :] = x1
        ms = jnp.mean(x1 * x1, axis=-1, keepdims=True)
        hn = x1 * lax.rsqrt(ms + EPS) * nw_ref[...]
        hn_ref[rows, :] = hn.astype(hn_ref.dtype)
        h_hi = hn.astype(BF16)
        h_lo = (hn - h_hi.astype(F32)).astype(BF16)
        yield
        logits = (_dot(h_hi, rwh_ref[...]) + _dot(h_lo, rwh_ref[...]) + _dot(h_hi, rwl_ref[...])
                  + rb_ref[...])
        yield
        gate_ref[rows, :] = _route(logits)

    _round_robin([sub_tile(r0) for r0 in range(0, OUT_TM, OUT_SUB)])


def _out_proj(x, att, gla, ssm, lw, layer):
    t = x.shape[0]

    def rows(width):
        return pl.BlockSpec((OUT_TM, width), lambda i: (i, 0))

    def const(shape):
        return pl.BlockSpec(shape, lambda i: (0, 0))

    return pl.pallas_call(
        _out_proj_kernel,
        grid=(t // OUT_TM,),
        in_specs=[rows(D_MODEL), rows(WIDTH_A), rows(WIDTH_B), rows(WIDTH_C),
                  pl.BlockSpec((1, D_MODEL, D_MODEL), lambda i: (layer, 0, 0)), const((1, D_MODEL)),
                  const((D_MODEL, 128)), const((D_MODEL, 128)), const((1, 128))],
        out_specs=[rows(D_MODEL), rows(D_MODEL), rows(128)],
        out_shape=[jax.ShapeDtypeStruct((t, D_MODEL), F32),
                   jax.ShapeDtypeStruct((t, D_MODEL), BF16),
                   jax.ShapeDtypeStruct((t, 128), F32)],
        scratch_shapes=[pltpu.VMEM((D_MODEL, D_MODEL), BF16)],
        compiler_params=_params("arbitrary"),
    )(x, att, gla, ssm, lw["wo"], lw["n2w"], lw["rw_hi"], lw["rw_lo"], lw["rb"])


RT_TS = OUT_SUB
RT_ALIGN = 16
RT_ROWS = 384
RT_SIZES = (256, 128, 64, 32, 16)
PAY_COLS = D_MODEL + 256
MOE_TM = 512


def _moe_max_tiles(t):
    return (t + (t // RT_TS) * N_EXPERT_GROUPS * (RT_ALIGN - 1)) // MOE_TM + N_EXPERT_GROUPS


def _sort_matrix(gate):
    lane = lax.broadcasted_iota(jnp.int32, (RT_TS, 128), 1)
    dest = jnp.where(lane == ROUTER_DEST_LANE, gate, 0.0).sum(axis=1, keepdims=True)
    rows = lax.broadcasted_iota(jnp.int32, (RT_TS, RT_ROWS), 1).astype(F32)
    return jnp.where(rows == dest, 1.0, 0.0).astype(BF16)


def _piece_copies(length, local, glob, make_copy, act):
    for size in RT_SIZES:
        take = (length & size) != 0

        @pl.when(take)
        def _(local=local, glob=glob, size=size):
            act(make_copy(pl.multiple_of(local, RT_ALIGN), pl.multiple_of(glob, RT_ALIGN), size))

        step = jnp.where(take, size, 0)
        local = local + step
        glob = glob + step


def _run_copies(c_ref, lp_ref, off_ref, tile, make_copy, act):
    for g in range(N_EXPERT_GROUPS):
        k = N_EXPERT_GROUPS * tile + g
        _piece_copies(lp_ref[k], c_ref[k], off_ref[k], make_copy, act)


def _dispatch_kernel(c_ref, lp_ref, off_ref, tail_ref, nact_ref, hn_ref, gate_ref, xs_hbm,
                     y_ref, zero_ref, sem, zsem, *, first_spare, max_tiles):
    i = pl.program_id(0)
    nt = pl.num_programs(0)
    slot = i % 2
    gate = gate_ref[...]
    pt = _sort_matrix(gate)
    g_hi = gate.astype(BF16)
    g_lo = (gate - g_hi.astype(F32)).astype(BF16)
    y_ref[slot, :, 0:D_MODEL] = _dot_tn(pt, hn_ref[...]).astype(BF16)
    y_ref[slot, :, D_MODEL:D_MODEL + 128] = _dot_tn(pt, g_hi).astype(BF16)
    y_ref[slot, :, D_MODEL + 128:] = _dot_tn(pt, g_lo).astype(BF16)

    def copy_from(s):
        return lambda local, glob, size: pltpu.make_async_copy(
            y_ref.at[s, pl.ds(local, size)], xs_hbm.at[pl.ds(glob, size)], sem.at[s])

    _run_copies(c_ref, lp_ref, off_ref, i, copy_from(slot), lambda cp: cp.start())

    @pl.when(i > 0)
    def _():
        _run_copies(c_ref, lp_ref, off_ref, i - 1, copy_from(1 - slot), lambda cp: cp.wait())

    @pl.when(i == nt - 1)
    def _():
        _run_copies(c_ref, lp_ref, off_ref, i, copy_from(slot), lambda cp: cp.wait())
        zero_ref[...] = jnp.zeros_like(zero_ref)

        def zero_copy(local, glob, size):
            return pltpu.make_async_copy(zero_ref.at[pl.ds(local, size)],
                                         xs_hbm.at[pl.ds(glob, size)], zsem)

        def zero_fill(act):
            for g in range(N_EXPERT_GROUPS):
                _piece_copies(tail_ref[N_EXPERT_GROUPS + g], 0, tail_ref[g], zero_copy, act)
            for j in range(first_spare, max_tiles):
                @pl.when(j >= nact_ref[0])
                def _(j=j):
                    act(zero_copy(0, j * MOE_TM, MOE_TM))

        zero_fill(lambda cp: cp.start())
        zero_fill(lambda cp: cp.wait())


def _moe_group_kernel(grp_ref, nact_ref, xs_ref, wg_ref, wu_ref, wd_ref, o_ref, wg_bf, wu_bf, wd_bf):
    j = pl.program_id(0)

    @pl.when(jnp.logical_or(j == 0, grp_ref[j] != grp_ref[jnp.maximum(j - 1, 0)]))
    def _():
        wg_bf[...] = wg_ref[0].astype(BF16)
        wu_bf[...] = wu_ref[0].astype(BF16)
        wd_bf[...] = wd_ref[0].astype(BF16)

    @pl.when(j >= nact_ref[0])
    def _():
        o_ref[...] = jnp.zeros_like(o_ref)

    @pl.when(j < nact_ref[0])
    def _():
        x = xs_ref[:, 0:D_MODEL]
        gate = (xs_ref[:, D_MODEL:D_MODEL + 128].astype(F32)
                + xs_ref[:, D_MODEL + 128:].astype(F32))
        lane = lax.broadcasted_iota(jnp.int32, gate.shape, 1)
        first = grp_ref[j] * EXPERTS_PER_GROUP
        acc = None
        for e in range(EXPERTS_PER_GROUP):
            ge = jnp.where(lane == first + e, gate, 0.0).sum(axis=-1, keepdims=True)
            hid = _silu(_dot(x, wg_bf[e])) * _dot(x, wu_bf[e]) * ge
            part = _dot(hid.astype(BF16), wd_bf[e])
            acc = part if acc is None else acc + part
        o_ref[...] = acc


CB_TILES = 2


def _combine_kernel(c_ref, lp_ref, off_ref, x_ref, gate_ref, fw_ref, ys_hbm, o_ref, z_ref, sem,
                    *, final_norm):
    s = pl.program_id(0)
    ns = pl.num_programs(0)
    base = (s % 2) * CB_TILES

    def copy_into(slot):
        return lambda local, glob, size: pltpu.make_async_copy(
            ys_hbm.at[pl.ds(glob, size)], z_ref.at[slot, pl.ds(local, size)], sem.at[slot])

    def copies(step, slot0, act):
        for u in range(CB_TILES):
            _run_copies(c_ref, lp_ref, off_ref, step * CB_TILES + u, copy_into(slot0 + u), act)

    @pl.when(s == 0)
    def _():
        z_ref[...] = jnp.zeros_like(z_ref)
        copies(0, 0, lambda cp: cp.start())

    @pl.when(s + 1 < ns)
    def _():
        copies(s + 1, CB_TILES - base, lambda cp: cp.start())

    copies(s, base, lambda cp: cp.wait())

    def tile(u):
        rows = slice(u * RT_TS, (u + 1) * RT_TS)
        pt = _sort_matrix(gate_ref[rows, :])
        yield
        z = z_ref[base + u]
        z_hi = z.astype(BF16)
        z_lo = (z - z_hi.astype(F32)).astype(BF16)
        y = x_ref[rows, :] + _dot(pt, z_hi) + _dot(pt, z_lo)
        yield
        if final_norm:
            ms = jnp.mean(y * y, axis=-1, keepdims=True)
            y = y * lax.rsqrt(ms + EPS) * fw_ref[...]
        o_ref[rows, :] = y

    _round_robin([tile(u) for u in range(CB_TILES)])


def _route_meta(gate, max_tiles):
    t = gate.shape[0]
    nt = t // RT_TS
    groups = jnp.arange(N_EXPERT_GROUPS, dtype=jnp.int32)
    g = gate[:, ROUTER_GIDX_LANE].astype(jnp.int32).reshape(nt, RT_TS, 1)
    count = (g == groups).astype(jnp.int32).sum(axis=1)
    lp = (count + RT_ALIGN - 1) // RT_ALIGN * RT_ALIGN
    total = lp.sum(axis=0)
    ntile = (total + MOE_TM - 1) // MOE_TM
    last = jnp.cumsum(ntile)
    first_row = (last - ntile) * MOE_TM
    c = jnp.cumsum(lp, axis=1) - lp
    off = first_row[None, :] + jnp.cumsum(lp, axis=0) - lp
    tail = jnp.concatenate([first_row + total, ntile * MOE_TM - total])
    j = jnp.arange(max_tiles, dtype=jnp.int32)[:, None]
    grp = jnp.minimum((j >= last[None, :]).astype(jnp.int32).sum(axis=1), N_EXPERT_GROUPS - 1)
    return c.reshape(-1), lp.reshape(-1), off.reshape(-1), tail, grp, last[-1:]


def _moe(x, hn, gate, lw, layer, fw, final_norm):
    t = x.shape[0]
    nt = t // RT_TS
    max_tiles = _moe_max_tiles(t)
    c, lp, off, tail, grp, nact = _route_meta(gate, max_tiles)

    def rows(width):
        return pl.BlockSpec((RT_TS, width), lambda i, *_: (i, 0))

    xs = pl.pallas_call(
        functools.partial(_dispatch_kernel, first_spare=t // MOE_TM, max_tiles=max_tiles),
        grid_spec=pltpu.PrefetchScalarGridSpec(
            num_scalar_prefetch=5, grid=(nt,),
            in_specs=[rows(D_MODEL), rows(128)],
            out_specs=pl.BlockSpec(memory_space=pl.ANY),
            scratch_shapes=[pltpu.VMEM((2, RT_ROWS, PAY_COLS), BF16),
                            pltpu.VMEM((MOE_TM, PAY_COLS), BF16),
                            pltpu.SemaphoreType.DMA((2,)), pltpu.SemaphoreType.DMA(())]),
        out_shape=jax.ShapeDtypeStruct((max_tiles * MOE_TM, PAY_COLS), BF16),
        compiler_params=_params("arbitrary"),
    )(c, lp, off, tail, nact, hn, gate)

    def experts(shape):
        return pl.BlockSpec((1, EXPERTS_PER_GROUP) + shape, lambda j, grp, nact: (layer, grp[j], 0, 0))

    def rounded(shape):
        return pltpu.VMEM((EXPERTS_PER_GROUP,) + shape, BF16)

    ys = pl.pallas_call(
        _moe_group_kernel,
        grid_spec=pltpu.PrefetchScalarGridSpec(
            num_scalar_prefetch=2, grid=(max_tiles,),
            in_specs=[pl.BlockSpec((MOE_TM, PAY_COLS), lambda j, grp, nact: (j, 0)),
                      experts((D_MODEL, D_EXPERT)), experts((D_MODEL, D_EXPERT)),
                      experts((D_EXPERT, D_MODEL))],
            out_specs=pl.BlockSpec((MOE_TM, D_MODEL), lambda j, grp, nact: (j, 0)),
            scratch_shapes=[rounded((D_MODEL, D_EXPERT)), rounded((D_MODEL, D_EXPERT)),
                            rounded((D_EXPERT, D_MODEL))]),
        out_shape=jax.ShapeDtypeStruct((max_tiles * MOE_TM, D_MODEL), F32),
        compiler_params=_params("arbitrary"),
    )(grp, nact, xs, lw["wg"], lw["wu"], lw["wd"])

    def wide(width):
        return pl.BlockSpec((CB_TILES * RT_TS, width), lambda i, *_: (i, 0))

    return pl.pallas_call(
        functools.partial(_combine_kernel, final_norm=final_norm),
        grid_spec=pltpu.PrefetchScalarGridSpec(
            num_scalar_prefetch=3, grid=(nt // CB_TILES,),
            in_specs=[wide(D_MODEL), wide(128), pl.BlockSpec((1, D_MODEL), lambda i, *_: (0, 0)),
                      pl.BlockSpec(memory_space=pl.ANY)],
            out_specs=wide(D_MODEL),
            scratch_shapes=[pltpu.VMEM((2 * CB_TILES, RT_ROWS, D_MODEL), F32),
                            pltpu.SemaphoreType.DMA((2 * CB_TILES,))]),
        out_shape=jax.ShapeDtypeStruct((t, D_MODEL), F32),
        compiler_params=_params("arbitrary"),
    )(c, lp, off, x, gate, fw, ys)


def _lane_place(vec, start, width=128):
    return jnp.zeros((1, width), F32).at[0, start:start + vec.shape[0]].set(vec.astype(F32))


REL_PAD = 704


def _toeplitz(v, rows, cols):
    n = v.shape[-1]
    tiled = jnp.tile(v, (1,) * (v.ndim - 1) + (rows,))[..., :rows * (n - 1)]
    return tiled.reshape(v.shape[:-1] + (rows, n - 1))[..., :cols]


def _band_bias(table):
    t = table.astype(F32)
    ext = jnp.concatenate([jnp.repeat(t[:, :1], REL_PAD, axis=1), t,
                           jnp.repeat(t[:, -1:], REL_PAD, axis=1)], axis=1)
    extr = ext[:, ::-1]
    top = ext.shape[1] - 1 - (REL_CLIP + REL_PAD)

    def band(off, nq, nk):
        n = nq + nk
        v = jnp.concatenate([extr[:, top - off:top - off + nk + 1],
                             extr[:, top - off - (nq - 1):top - off]], axis=1)
        assert v.shape[1] == n
        return _toeplitz(v, nq, nk)

    sample = band(WINDOW, CHUNK, BAND)
    kc = (jnp.arange(ATT_KB) // CHUNK)[:, None]
    qc = (jnp.arange(ATT_QB) // CHUNK)[None, :]
    block = lambda off: jnp.swapaxes(band(off, ATT_QB, ATT_KB), 1, 2)
    prompt = jnp.stack([jnp.where((kc >= qc)[None], block(2 * ATT_KB), -1e30),
                        block(ATT_KB),
                        jnp.where((kc <= qc)[None], block(0), -1e30)])
    return sample * LOG2_E, prompt * LOG2_E


def _layer_weights(i, norm1_w, w_in, rel_bias_table, gla_w_gate_up, gla_b_gate, gla_norm_w,
                   ssm_conv_w, ssm_conv_b, ssm_dt_bias, ssm_a_log, ssm_d, ssm_norm_w, w_out,
                   norm2_w, router_group_w, router_group_b, router_expert_w, router_expert_b,
                   exp_w_gate, exp_w_up, exp_w_down):
    bias_s, bias_p = _band_bias(rel_bias_table[i])
    rw = jnp.concatenate([router_expert_w[i], router_group_w[i],
                          jnp.zeros((D_MODEL, 128 - N_EXPERTS - N_EXPERT_GROUPS), F32)], axis=1)
    rw_hi = rw.astype(BF16)
    return dict(
        n1w=norm1_w[i][None, :],
        w_in=w_in,
        bias_s=bias_s,
        bias_p=bias_p,
        gup=jnp.zeros((128, 128), F32).at[:GATE_RANK, :].set(gla_w_gate_up[i]).astype(BF16),
        gb=gla_b_gate[i][None, :].astype(F32),
        gnw=jnp.tile(gla_norm_w[i], H_B)[None, :].astype(F32),
        cw=ssm_conv_w[i].astype(F32),
        cb=ssm_conv_b[i][None, :].astype(F32),
        dtb=_lane_place(ssm_dt_bias[i], MISC_DT_LANE),
        alog=jnp.repeat(ssm_a_log[i].astype(F32), P_C)[None, :],
        dsk=jnp.repeat(ssm_d[i].astype(F32), P_C)[None, :],
        snw=ssm_norm_w[i][None, :].astype(F32),
        wo=w_out,
        n2w=norm2_w[i][None, :],
        rw_hi=rw_hi,
        rw_lo=(rw - rw_hi.astype(F32)).astype(BF16),
        rb=jnp.concatenate([router_expert_b[i], router_group_b[i],
                            jnp.zeros((128 - N_EXPERTS - N_EXPERT_GROUPS,), F32)])[None, :],
        wg=exp_w_gate,
        wu=exp_w_up,
        wd=exp_w_down,
    )


def _stream_layer(x, bsz, lw, layer, cache_k, cache_v, gla0, conv0, ssm0, fw, final_norm):
    l = x.shape[0] // bsz
    if cache_k is None:
        qk, rest, k_keep, v_keep, vt = _in_proj(x, lw["n1w"], lw["w_in"], layer, True, l)
        att = _attn_prompt(qk.reshape(bsz, l, 2 * WIDTH_A),
                           vt.reshape(bsz, l // ATT_KB, WIDTH_A, ATT_KB), lw["bias_p"])
    else:
        qk, rest, k_keep, v_keep, kv = _in_proj(x, lw["n1w"], lw["w_in"], layer, False, l)
        att = _attn_sample(qk.reshape(bsz, l, 2 * WIDTH_A), kv.reshape(bsz, l, 2 * WIDTH_A),
                           cache_k, cache_v, layer, lw["bias_s"])
    gla_o, ssm_o, gla_s, conv_s, ssm_s = _scan(rest.reshape(bsz, l, REST_COLS), lw, gla0, conv0, ssm0)
    t = bsz * l
    x1, hn, gate = _out_proj(x, att.reshape(t, WIDTH_A), gla_o.reshape(t, WIDTH_B),
                             ssm_o.reshape(t, WIDTH_C), lw, layer)
    x2 = _moe(x1, hn, gate, lw, layer, fw, final_norm)
    keep = min(WINDOW, l)
    k_keep = k_keep.reshape(bsz, keep, H_A, DH_A)
    v_keep = v_keep.reshape(bsz, keep, H_A, DH_A)
    gla_state = gla_s.reshape(bsz, DV_B, H_B, DK_B).transpose(0, 2, 3, 1)
    conv_state = conv_s[:, 8 - (CONV_W - 1):, :]
    return x2, k_keep, v_keep, gla_state, conv_state, ssm_s


def kernel(x_prompt, x_sample, cache_k_a, cache_v_a, state_gla, state_conv, state_ssm, norm1_w, w_in, rel_bias_table, gla_w_gate_up, gla_b_gate, gla_norm_w, ssm_conv_w, ssm_conv_b, ssm_dt_bias, ssm_a_log, ssm_d, ssm_norm_w, w_out, norm2_w, router_group_w, router_group_b, router_expert_w, router_expert_b, exp_w_gate, exp_w_up, exp_w_down, final_norm_w):
    bp, sp, _ = x_prompt.shape
    bs, ss, _ = x_sample.shape
    xp = x_prompt.reshape(bp * sp, D_MODEL)
    xs = x_sample.reshape(bs * ss, D_MODEL)
    fw = final_norm_w[None, :].astype(F32)
    outs_p, outs_s = [], []
    cache_k = cache_k_a.astype(BF16).reshape(DEPTH, bs, -1, WIDTH_A)
    cache_v = cache_v_a.astype(BF16).reshape(DEPTH, bs, -1, WIDTH_A)
    w_in = jnp.pad(w_in, ((0, 0), (0, 0), (0, PROJ_COLS - IN_COLS)))
    for i in range(DEPTH):
        lw = _layer_weights(i, norm1_w, w_in, rel_bias_table, gla_w_gate_up, gla_b_gate, gla_norm_w,
                            ssm_conv_w, ssm_conv_b, ssm_dt_bias, ssm_a_log, ssm_d, ssm_norm_w, w_out,
                            norm2_w, router_group_w, router_group_b, router_expert_w, router_expert_b,
                            exp_w_gate, exp_w_up, exp_w_down)
        last = i == DEPTH - 1
        xp, *sp_out = _stream_layer(
            xp, bp, lw, i, None, None,
            jnp.zeros((bp, DV_B, H_B * DK_B), F32),
            jnp.zeros((bp, 8, CONV_CH), F32),
            jnp.zeros((bp, H_C, P_C, D_STATE), F32), fw, last)
        outs_p.append(sp_out)
        gla0 = state_gla[i].astype(F32).transpose(0, 3, 1, 2).reshape(bs, DV_B, H_B * DK_B)
        conv0 = jnp.pad(state_conv[i].astype(F32), ((0, 0), (8 - (CONV_W - 1), 0), (0, 0)))
        xs, *ss_out = _stream_layer(
            xs, bs, lw, i, cache_k, cache_v,
            gla0, conv0, state_ssm[i].astype(F32), fw, last)
        outs_s.append(ss_out)
    stack = lambda outs, j: jnp.stack([o[j] for o in outs])
    return (xp.reshape(bp, sp, D_MODEL), xs.reshape(bs, ss, D_MODEL),
            stack(outs_p, 0), stack(outs_p, 1), stack(outs_p, 2), stack(outs_p, 3), stack(outs_p, 4),
            stack(outs_s, 0), stack(outs_s, 1), stack(outs_s, 2), stack(outs_s, 3), stack(outs_s, 4))
```

```python
import functools

import jax
import jax.numpy as jnp
from jax import lax
from jax.experimental import pallas as pl
from jax.experimental.pallas import tpu as pltpu

F32 = jnp.float32
BF16 = jnp.bfloat16

D_MODEL = 1024
DEPTH = 2
EPS = 1e-6
CHUNK = 64
N_PAST_CHUNKS = 8
BAND = (N_PAST_CHUNKS + 1) * CHUNK
WINDOW = N_PAST_CHUNKS * CHUNK
H_A, DH_A, WIDTH_A = 8, 64, 512
REL_CLIP = 128
LOG2_E = 1.4426950408889634
H_B, DK_B, DV_B, WIDTH_B = 4, 32, 64, 256
GATE_RANK = 16
GATE_TAU = 16.0
H_C, P_C, WIDTH_C = 4, 64, 256
N_GROUPS_C = 2
D_STATE = 64
CONV_W = 4
CONV_CH = 512
N_EXPERT_GROUPS = 4
EXPERTS_PER_GROUP = 4
N_EXPERTS = 16
D_EXPERT = 256

PROJ_COLS = 3200
MISC_DT_LANE = GATE_RANK
ROUTER_GROUP_LANE = N_EXPERTS
ROUTER_GIDX_LANE = 20
ROUTER_DEST_LANE = 21

SEQ_GROUP = 8
SEQ_UNROLL = 8
VMEM_LIMIT_BYTES = 56 * 1024 * 1024

NT_DIMS = (((1,), (1,)), ((), ()))
TN_DIMS = (((0,), (0,)), ((), ()))


def _params(*sem):
    return pltpu.CompilerParams(dimension_semantics=sem, vmem_limit_bytes=VMEM_LIMIT_BYTES)


def _dot(a, b):
    return jnp.dot(a, b, preferred_element_type=F32)


def _dot_nt(a, b):
    return lax.dot_general(a, b, NT_DIMS, preferred_element_type=F32)


def _dot_tn(a, b):
    return lax.dot_general(a, b, TN_DIMS, preferred_element_type=F32)


def _split2(a):
    hi = a.astype(BF16)
    return hi, (a - hi.astype(F32)).astype(BF16)


def _dot01_left(m01, a):
    hi, lo = _split2(a)
    return _dot(m01, hi) + _dot(m01, lo)


def _dot01_right(a, m01):
    hi, lo = _split2(a)
    return _dot(hi, m01) + _dot(lo, m01)


def _round_robin(gens):
    results = [None] * len(gens)
    live = list(range(len(gens)))
    while live:
        for idx in list(live):
            try:
                next(gens[idx])
            except StopIteration as stop:
                results[idx] = stop.value
                live.remove(idx)
    return results


def _silu(x):
    return x * jax.nn.sigmoid(x)


def _softplus(x):
    return jnp.maximum(x, 0.0) + jnp.log1p(jnp.exp(-jnp.abs(x)))


IN_TM = 256
QKV_COLS = 3 * WIDTH_A
REST_COLS = PROJ_COLS - QKV_COLS
REST_CHUNKS = ((0, 512), (512, 1024), (1024, REST_COLS))


IN_COLS = 3092
IN_TAIL = QKV_COLS + 512
PREP_ROWS = 128


def _prepare_w_in(w_in_ref, w_ref):
    scale = DH_A ** -0.5 * LOG2_E
    n_tail = REST_COLS - 512 - 128
    for r in range(0, D_MODEL, PREP_ROWS):
        rs = slice(r, r + PREP_ROWS)
        w_ref[rs, 0:WIDTH_A] = (w_in_ref[0, rs, 0:WIDTH_A] * scale).astype(BF16)
        w_ref[rs, WIDTH_A:IN_TAIL] = w_in_ref[0, rs, WIDTH_A:IN_TAIL].astype(BF16)
        tail = w_in_ref[0, rs, IN_TAIL:IN_COLS]
        w_ref[rs, IN_TAIL:IN_TAIL + n_tail] = tail[:, GATE_RANK:GATE_RANK + n_tail].astype(BF16)
        misc = jnp.concatenate([tail[:, 0:GATE_RANK], tail[:, GATE_RANK + n_tail:],
                                jnp.zeros((PREP_ROWS, 128 - GATE_RANK - H_C), F32)], axis=1)
        w_ref[rs, IN_TAIL + n_tail:] = misc.astype(BF16)


def _in_proj_kernel(x_ref, nw_ref, w_in_ref, *refs, with_vt):
    if with_vt:
        qk_ref, rest_ref, kk_ref, vk_ref, vt_ref, w_ref = refs
    else:
        qk_ref, rest_ref, kk_ref, vk_ref, kv_ref, w_ref = refs

    @pl.when(pl.program_id(0) == 0)
    def _():
        _prepare_w_in(w_in_ref, w_ref)

    x = x_ref[...]
    ms = jnp.mean(x * x, axis=-1, keepdims=True)
    xn = (x * lax.rsqrt(ms + EPS) * nw_ref[...]).astype(BF16)
    qk_ref[:, 0:WIDTH_A] = _dot(xn, w_ref[:, 0:WIDTH_A]).astype(BF16)
    k = _dot(xn, w_ref[:, WIDTH_A:2 * WIDTH_A])
    v = _dot(xn, w_ref[:, 2 * WIDTH_A:QKV_COLS])
    qk_ref[:, WIDTH_A:] = k.astype(BF16)
    if not with_vt:
        kv_ref[:, 0:WIDTH_A] = k
        kv_ref[:, WIDTH_A:] = v

    def keep_rows():
        kk = kk_ref.reshape(IN_TM * H_A, DH_A)
        vk = vk_ref.reshape(IN_TM * H_A, DH_A)
        for h in range(H_A):
            sl = slice(DH_A * h, DH_A * (h + 1))
            kk[pl.ds(h, IN_TM, stride=H_A), :] = k[:, sl]
            vk[pl.ds(h, IN_TM, stride=H_A), :] = v[:, sl]

    keep_rows()
    for lo, hi in REST_CHUNKS:
        rest_ref[:, lo:hi] = _dot(xn, w_ref[:, QKV_COLS + lo:QKV_COLS + hi])
    if with_vt:
        vt_ref[0] = v.T.astype(BF16)


def _in_proj(x, nw, w_in, layer, with_vt, seq_len):
    t = x.shape[0]
    tiles_per_seq = max(seq_len // IN_TM, 1)
    keep_tiles = max(min(WINDOW, seq_len) // IN_TM, 1)
    n_keep = t // IN_TM // tiles_per_seq * keep_tiles

    def rows(width):
        return pl.BlockSpec((IN_TM, width), lambda i: (i, 0))

    def kept(i):
        return (i // tiles_per_seq * keep_tiles
                + jnp.maximum(i % tiles_per_seq - (tiles_per_seq - keep_tiles), 0), 0, 0)

    keep_spec = pl.BlockSpec((IN_TM, H_A, DH_A), kept)
    keep_shape = jax.ShapeDtypeStruct((n_keep * IN_TM, H_A, DH_A), F32)
    in_specs = [rows(D_MODEL), pl.BlockSpec((1, D_MODEL), lambda i: (0, 0)),
                pl.BlockSpec((1, D_MODEL, IN_COLS), lambda i: (layer, 0, 0))]
    scratch = [pltpu.VMEM((D_MODEL, PROJ_COLS), BF16)]
    out_specs = [rows(2 * WIDTH_A), rows(REST_COLS), keep_spec, keep_spec]
    out_shape = [jax.ShapeDtypeStruct((t, 2 * WIDTH_A), BF16),
                 jax.ShapeDtypeStruct((t, REST_COLS), F32), keep_shape, keep_shape]
    if with_vt:
        out_specs.append(pl.BlockSpec((1, WIDTH_A, IN_TM), lambda i: (i, 0, 0)))
        out_shape.append(jax.ShapeDtypeStruct((t // IN_TM, WIDTH_A, IN_TM), BF16))
    else:
        out_specs.append(rows(2 * WIDTH_A))
        out_shape.append(jax.ShapeDtypeStruct((t, 2 * WIDTH_A), F32))
    return pl.pallas_call(
        functools.partial(_in_proj_kernel, with_vt=with_vt),
        grid=(t // IN_TM,),
        in_specs=in_specs, out_specs=out_specs, out_shape=out_shape, scratch_shapes=scratch,
        compiler_params=_params("arbitrary"),
    )(x, nw, w_in)


ATT_QB = 4 * CHUNK
ATT_KB = IN_TM
ATT_NKB = (ATT_QB + WINDOW) // ATT_KB


def _attn_prompt_kernel(q_ref, k_ref, vt_ref, bias_ref, o_ref, ot_ref):
    blk = pl.program_id(1)
    sb = jnp.maximum(blk - WINDOW // ATT_KB, 0)
    kstart = pl.multiple_of(sb * ATT_KB, ATT_KB)
    lane = lax.broadcasted_iota(jnp.int32, (ATT_QB, 128), 1)

    def attend(kinds):
        def scores(h):
            pair = slice(128 * (h // 2), 128 * (h // 2 + 1))
            qp = q_ref[0, :, pair]
            qz = jnp.where((lane >= DH_A) == bool(h % 2), qp, jnp.zeros_like(qp))
            return [_dot_nt(k_ref[0, pl.ds(kstart + ATT_KB * i, ATT_KB), pair], qz)
                    + bias_ref[kind, h] for i, kind in enumerate(kinds)]

        def finish(h, s):
            m = s[0].max(axis=0, keepdims=True)
            for si in s[1:]:
                m = jnp.maximum(m, si.max(axis=0, keepdims=True))
            l = None
            ot = None
            for i, si in enumerate(s):
                p = jnp.exp2(si - m)
                li = p.sum(axis=0, keepdims=True)
                oi = _dot(vt_ref[0, sb + i, DH_A * h:DH_A * (h + 1), :], p.astype(BF16))
                l = li if l is None else l + li
                ot = oi if ot is None else ot + oi
            ot_ref[DH_A * h:DH_A * (h + 1), :] = ot / l

        pending = scores(0)
        for h in range(H_A):
            upcoming = scores(h + 1) if h + 1 < H_A else None
            finish(h, pending)
            pending = upcoming
        o_ref[0] = ot_ref[...].T.astype(o_ref.dtype)

    pl.when(blk == 0)(lambda: attend((2,)))
    pl.when(blk == 1)(lambda: attend((1, 2)))
    pl.when(blk >= 2)(lambda: attend((0, 1, 2)))


def _attn_prompt(qk, vt, bias_t):
    bsz, s, _ = qk.shape
    nkb = s // ATT_KB
    return pl.pallas_call(
        _attn_prompt_kernel,
        grid=(bsz, s // ATT_QB),
        in_specs=[pl.BlockSpec((1, ATT_QB, WIDTH_A), lambda b, c: (b, c, 0)),
                  pl.BlockSpec((1, s, WIDTH_A), lambda b, c: (b, 0, 1)),
                  pl.BlockSpec((1, nkb, WIDTH_A, ATT_KB), lambda b, c: (b, 0, 0, 0)),
                  pl.BlockSpec((3, H_A, ATT_KB, ATT_QB), lambda b, c: (0, 0, 0, 0))],
        out_specs=pl.BlockSpec((1, ATT_QB, WIDTH_A), lambda b, c: (b, c, 0)),
        out_shape=jax.ShapeDtypeStruct((bsz, s, WIDTH_A), BF16),
        scratch_shapes=[pltpu.VMEM((WIDTH_A, ATT_QB), F32)],
        compiler_params=_params("parallel", "arbitrary"),
    )(qk, qk, vt, bias_t)


def _attn_sample_kernel(q_ref, kv_ref, ck_ref, cv_ref, bias_ref, o_ref):
    lane = lax.broadcasted_iota(jnp.int32, (CHUNK, 128), 1)

    def scores(h):
        pair = slice(128 * (h // 2), 128 * (h // 2 + 1))
        qp = q_ref[0, :, pair]
        qz = jnp.where((lane >= DH_A) == bool(h % 2), qp, jnp.zeros_like(qp))
        return (_dot_nt(qz, ck_ref[0, 0, :, pair]) + bias_ref[h, :, 0:WINDOW],
                _dot_nt(qz, kv_ref[0, :, pair].astype(BF16)) + bias_ref[h, :, WINDOW:BAND])

    def finish(h, s):
        s1, s2 = s
        pair = slice(128 * (h // 2), 128 * (h // 2 + 1))
        vpair = slice(WIDTH_A + 128 * (h // 2), WIDTH_A + 128 * (h // 2 + 1))
        half = slice(DH_A * (h % 2), DH_A * (h % 2 + 1))
        m = jnp.maximum(s1.max(axis=-1, keepdims=True), s2.max(axis=-1, keepdims=True))
        p1 = jnp.exp2(s1 - m)
        p2 = jnp.exp2(s2 - m)
        l = p1.sum(axis=-1, keepdims=True) + p2.sum(axis=-1, keepdims=True)
        o = (_dot(p1.astype(BF16), cv_ref[0, 0, :, pair])
             + _dot(p2.astype(BF16), kv_ref[0, :, vpair].astype(BF16)))
        o_ref[0, :, DH_A * h:DH_A * (h + 1)] = (o[:, half] / l).astype(o_ref.dtype)

    pending = scores(0)
    for h in range(H_A):
        upcoming = scores(h + 1) if h + 1 < H_A else None
        finish(h, pending)
        pending = upcoming


def _attn_sample(qk, kv, cache_k, cache_v, layer, bias):
    bsz = qk.shape[0]
    cache_spec = pl.BlockSpec((1, 1, WINDOW, WIDTH_A), lambda b: (layer, b, 0, 0))
    return pl.pallas_call(
        _attn_sample_kernel,
        grid=(bsz,),
        in_specs=[pl.BlockSpec((1, CHUNK, WIDTH_A), lambda b: (b, 0, 0)),
                  pl.BlockSpec((1, CHUNK, 2 * WIDTH_A), lambda b: (b, 0, 0)),
                  cache_spec, cache_spec,
                  pl.BlockSpec((H_A, CHUNK, BAND), lambda b: (0, 0, 0))],
        out_specs=pl.BlockSpec((1, CHUNK, WIDTH_A), lambda b: (b, 0, 0)),
        out_shape=jax.ShapeDtypeStruct((bsz, CHUNK, WIDTH_A), BF16),
        compiler_params=_params("parallel"),
    )(qk, kv, cache_k, cache_v, bias)


def _scan_kernel(bqk_ref, bv_ref, bog_ref, cz_ref, cx_ref, misc_ref,
                 gup_ref, gb_ref, gnw_ref, cw_ref, cb_ref, dtb_ref, alog_ref, dsk_ref, snw_ref,
                 gla0_ref, conv0_ref, ssm0_ref,
                 gla_o_ref, ssm_o_ref, gla_s_ref, conv_s_ref, ssm_s_ref,
                 xpad_ref):
    c = pl.program_id(1)
    nseq = bqk_ref.shape[0]

    @pl.when(c == 0)
    def _():
        gla_s_ref[...] = gla0_ref[...]
        conv_s_ref[...] = conv0_ref[...]
        ssm_s_ref[...] = ssm0_ref[...]

    r64 = lax.broadcasted_iota(jnp.int32, (CHUNK, CHUNK), 0)
    c64 = lax.broadcasted_iota(jnp.int32, (CHUNK, CHUNK), 1)
    tri = c64 <= r64
    tri_bf = jnp.where(tri, 1.0, 0.0).astype(BF16)
    ones_bf = jnp.ones((CHUNK, CHUNK), BF16)
    row_x = lax.broadcasted_iota(jnp.int32, (CHUNK, WIDTH_C), 0)
    s_x = lax.broadcasted_iota(jnp.int32, (CHUNK, WIDTH_C), 1) & (CHUNK - 1)
    causal_x = s_x <= row_x
    upper_x = row_x <= s_x
    er = lax.broadcasted_iota(jnp.int32, (128, WIDTH_C), 0)
    el = lax.broadcasted_iota(jnp.int32, (128, WIDTH_C), 1)
    expand_bf = jnp.where(er == MISC_DT_LANE + (el >> 6), 1.0, 0.0).astype(BF16)
    br = lax.broadcasted_iota(jnp.int32, (WIDTH_C, WIDTH_C), 0)
    bl = lax.broadcasted_iota(jnp.int32, (WIDTH_C, WIDTH_C), 1)
    blk64_bf = jnp.where((br >> 6) == (bl >> 6), 1.0, 0.0).astype(BF16)
    blk128_bf = jnp.where((br >> 7) == (bl >> 7), 1.0, 0.0).astype(BF16)

    gup = gup_ref[...]
    gb = gb_ref[...]
    gnw = gnw_ref[...]
    cb = cb_ref[...]
    dtb = dtb_ref[...]
    neg_a = -jnp.exp(alog_ref[...])
    dsk = dsk_ref[...]
    snw = snw_ref[...]

    def load_seq(b):
        return dict(misc=misc_ref[b], qk=bqk_ref[b], v=bv_ref[b], bog=bog_ref[b], cz=cz_ref[b],
                    cx=cx_ref[b], gla=gla_s_ref[b], conv=conv_s_ref[b],
                    ssm=[ssm_s_ref[b, h] for h in range(H_C)])

    def store_seq(b, out):
        gla_o_ref[b] = out["gla_o"]
        ssm_o_ref[b] = out["ssm_o"]
        gla_s_ref[b] = out["gla"]
        conv_s_ref[b] = out["conv"]
        for h in range(H_C):
            ssm_s_ref[b, h] = out["ssm"][h]

    def compute_seq(inp, slot):
        out = {}
        misc = inp["misc"]

        z = _dot(misc.astype(BF16), gup) + gb
        g = (jnp.minimum(z, 0.0) - jnp.log1p(jnp.exp(-jnp.abs(z)))) * (1.0 / GATE_TAU)
        bcum = _dot01_left(tri_bf, g)
        yield
        q = inp["qk"][:, :H_B * DK_B]
        k = inp["qk"][:, H_B * DK_B:]
        blast = bcum[CHUNK - 1:CHUNK, :]
        qb = (q * (DK_B ** -0.5) * jnp.exp(bcum)).astype(BF16)
        kb = (k * jnp.exp(-bcum)).astype(BF16)
        kl = (k * jnp.exp(blast - bcum)).astype(BF16)
        vb = inp["v"].astype(BF16)
        st = inp["gla"]
        st_bf = st.astype(BF16)
        yield
        o_parts, s_parts = [], []
        for h in range(H_B):
            ks = slice(DK_B * h, DK_B * (h + 1))
            vs = slice(DV_B * h, DV_B * (h + 1))
            att = jnp.where(tri, _dot_nt(qb[:, ks], kb[:, ks]), 0.0).astype(BF16)
            o_parts.append(_dot(att, vb[:, vs]) + _dot_nt(qb[:, ks], st_bf[:, ks]))
            s_parts.append(_dot_tn(vb[:, vs], kl[:, ks]))
            yield
        o = jnp.concatenate(o_parts, axis=1)
        out["gla"] = jnp.exp(blast) * st + jnp.concatenate(s_parts, axis=1)
        msq = _dot01_right(o * o, blk64_bf) * (1.0 / DV_B)
        yield
        go = o * lax.rsqrt(msq + EPS) * gnw * _silu(inp["bog"])
        out["gla_o"] = go.astype(gla_o_ref.dtype)
        yield

        cx = inp["cx"]
        xpad_ref[slot, 0:8, :] = inp["conv"]
        xpad_ref[slot, 8:8 + CHUNK, :] = cx
        conv = cb + cx * cw_ref[CONV_W - 1:CONV_W, :]
        for i in range(CONV_W - 1):
            off = 8 - (CONV_W - 1) + i
            conv = conv + xpad_ref[slot, off:off + CHUNK, :] * cw_ref[i:i + 1, :]
        out["conv"] = cx[CHUNK - 8:, :]
        conv = _silu(conv)
        sx = conv[:, :WIDTH_C]
        sb_bf = conv[:, WIDTH_C:WIDTH_C + N_GROUPS_C * D_STATE].astype(BF16)
        sc_bf = conv[:, WIDTH_C + N_GROUPS_C * D_STATE:].astype(BF16)
        yield

        dtv = _softplus(misc + dtb)
        dt_x = _dot01_right(dtv, expand_bf)
        yield
        a_x = neg_a * dt_x
        cs_x = _dot01_left(tri_bf, a_x)
        cs_row = _dot01_left(ones_bf, jnp.where(upper_x, a_x, 0.0))
        yield
        decay = jnp.exp(jnp.where(causal_x, cs_x - cs_row, -jnp.inf))
        cb_parts = []
        for gidx in range(N_GROUPS_C):
            ns = slice(D_STATE * gidx, D_STATE * (gidx + 1))
            cbm = _dot_nt(sc_bf[:, ns], sb_bf[:, ns])
            cb_parts += [cbm] * (H_C // N_GROUPS_C)
            yield
        scores = (jnp.concatenate(cb_parts, axis=1) * decay).astype(BF16)
        xdt = (sx * dt_x).astype(BF16)
        ecs = jnp.exp(cs_x)
        cs_last = cs_x[CHUNK - 1:CHUNK, :]
        xw = (sx * (jnp.exp(cs_last - cs_x) * dt_x)).astype(BF16)
        e_last = jnp.exp(cs_last)
        yield
        y_parts = []
        out["ssm"] = []
        for h in range(H_C):
            ps = slice(P_C * h, P_C * (h + 1))
            gi = h // (H_C // N_GROUPS_C)
            ns = slice(D_STATE * gi, D_STATE * (gi + 1))
            hst = inp["ssm"][h]
            y_parts.append(_dot(scores[:, ps], xdt[:, ps])
                           + _dot_nt(sc_bf[:, ns], hst.astype(BF16)) * ecs[:, ps])
            out["ssm"].append(e_last[:, ps] * hst + _dot_tn(xw[:, ps], sb_bf[:, ns]))
            yield
        y = jnp.concatenate(y_parts, axis=1) + dsk * sx
        y = y * _silu(inp["cz"])
        ymsq = _dot01_right(y * y, blk128_bf) * (1.0 / (WIDTH_C // N_GROUPS_C))
        yield
        out["ssm_o"] = (y * lax.rsqrt(ymsq + EPS) * snw).astype(ssm_o_ref.dtype)
        return out

    def group_body(i, carry):
        seqs = [i * SEQ_UNROLL + u for u in range(SEQ_UNROLL)]
        inputs = [load_seq(b) for b in seqs]
        outputs = _round_robin([compute_seq(inp, slot) for slot, inp in enumerate(inputs)])
        for b, out in zip(seqs, outputs):
            store_seq(b, out)
        return carry

    lax.fori_loop(0, nseq // SEQ_UNROLL, group_body, 0)


def _scan(proj, lw, gla0, conv0, ssm0):
    bsz, l, _ = proj.shape
    nc = l // CHUNK
    sg = min(SEQ_GROUP, bsz)
    ng = bsz // sg

    def col(width, idx):
        return pl.BlockSpec((sg, CHUNK, width), lambda g, c: (g, c, idx))

    def const(shape):
        return pl.BlockSpec(shape, lambda g, c: (0,) * len(shape))

    def state(shape):
        return pl.BlockSpec((sg,) + shape, lambda g, c: (g,) + (0,) * len(shape))

    gla_shape = (DV_B, H_B * DK_B)
    conv_shape = (8, CONV_CH)
    ssm_shape = (H_C, P_C, D_STATE)
    return pl.pallas_call(
        _scan_kernel,
        grid=(ng, nc),
        in_specs=[col(256, 0), col(256, 1), col(256, 2), col(256, 3), col(512, 2), col(128, 12),
                  const((128, 128)), const((1, 128)), const((1, WIDTH_B)),
                  const((CONV_W, CONV_CH)), const((1, CONV_CH)),
                  const((1, 128)), const((1, WIDTH_C)), const((1, WIDTH_C)), const((1, WIDTH_C)),
                  state(gla_shape), state(conv_shape), state(ssm_shape)],
        out_specs=[pl.BlockSpec((sg, CHUNK, WIDTH_B), lambda g, c: (g, c, 0)),
                   pl.BlockSpec((sg, CHUNK, WIDTH_C), lambda g, c: (g, c, 0)),
                   state(gla_shape), state(conv_shape), state(ssm_shape)],
        out_shape=[jax.ShapeDtypeStruct((bsz, l, WIDTH_B), BF16),
                   jax.ShapeDtypeStruct((bsz, l, WIDTH_C), BF16),
                   jax.ShapeDtypeStruct((bsz,) + gla_shape, F32),
                   jax.ShapeDtypeStruct((bsz,) + conv_shape, F32),
                   jax.ShapeDtypeStruct((bsz,) + ssm_shape, F32)],
        scratch_shapes=[pltpu.VMEM((SEQ_UNROLL, 8 + CHUNK, CONV_CH), F32)],
        compiler_params=_params("parallel", "arbitrary"),
    )(proj, proj, proj, proj, proj, proj,
      lw["gup"], lw["gb"], lw["gnw"], lw["cw"], lw["cb"], lw["dtb"], lw["alog"], lw["dsk"], lw["snw"],
      gla0, conv0, ssm0)


OUT_TM = 512
OUT_SUB = 256


def _route(logits):
    lane = lax.broadcasted_iota(jnp.int32, logits.shape, 1)
    lane_f = lane.astype(F32)
    neg = -jnp.inf
    big = 1000.0
    gmask = (lane >= ROUTER_GROUP_LANE) & (lane < ROUTER_GROUP_LANE + N_EXPERT_GROUPS)
    lg = jnp.where(gmask, logits, neg)
    gmax = lg.max(axis=-1, keepdims=True)
    g_idx = jnp.where(lg == gmax, lane_f, big).min(axis=-1, keepdims=True) - ROUTER_GROUP_LANE
    p_grp = 1.0 / jnp.where(gmask, jnp.exp(logits - gmax), 0.0).sum(axis=-1, keepdims=True)
    emask = (lane < N_EXPERTS) & ((lane >> 2).astype(F32) == g_idx)
    le = jnp.where(emask, logits, neg)
    v1 = le.max(axis=-1, keepdims=True)
    i1 = jnp.where(le == v1, lane_f, big).min(axis=-1, keepdims=True)
    le2 = jnp.where(lane_f == i1, neg, le)
    v2 = le2.max(axis=-1, keepdims=True)
    i2 = jnp.where(le2 == v2, lane_f, big).min(axis=-1, keepdims=True)
    e2 = jnp.exp(v2 - v1)
    den = 1.0 + e2
    n = logits.shape[0]
    onehot = lane_f == g_idx
    member = jnp.where(onehot, 1.0, 0.0)
    r = lax.broadcasted_iota(jnp.int32, (n, n), 0)
    c = lax.broadcasted_iota(jnp.int32, (n, n), 1)
    before = _dot(jnp.where(c < r, 1.0, 0.0).astype(BF16), member.astype(BF16))
    padded = jnp.floor((member.sum(axis=0, keepdims=True) + (RT_ALIGN - 1)) * (1.0 / RT_ALIGN)) * RT_ALIGN
    lane1 = lane[0:1, :]
    start = jnp.zeros((1, 128), F32)
    for g in range(N_EXPERT_GROUPS - 1):
        run = jnp.where(lane1 == g, padded, 0.0).sum(axis=1, keepdims=True)
        start = start + jnp.where(lane1 > g, run, 0.0)
    dest = jnp.where(onehot, before + start, 0.0).sum(axis=1, keepdims=True)
    return (jnp.where(lane_f == i1, (1.0 / den) * p_grp, 0.0)
            + jnp.where(lane_f == i2, (e2 / den) * p_grp, 0.0)
            + jnp.where(lane == ROUTER_GIDX_LANE, g_idx, 0.0)
            + jnp.where(lane == ROUTER_DEST_LANE, dest, 0.0))


def _out_proj_kernel(x_ref, att_ref, gla_ref, ssm_ref, w_out_ref, nw_ref, rwh_ref, rwl_ref, rb_ref,
                     x1_ref, hn_ref, gate_ref, wo_ref):
    @pl.when(pl.program_id(0) == 0)
    def _():
        for r in range(0, D_MODEL, PREP_ROWS):
            wo_ref[r:r + PREP_ROWS, :] = w_out_ref[0, r:r + PREP_ROWS, :].astype(BF16)

    def sub_tile(r0):
        rows = slice(r0, r0 + OUT_SUB)
        mo = (_dot(att_ref[rows, :], wo_ref[0:WIDTH_A, :])
              + _dot(gla_ref[rows, :], wo_ref[WIDTH_A:WIDTH_A + WIDTH_B, :])
              + _dot(ssm_ref[rows, :], wo_ref[WIDTH_A + WIDTH_B:, :]))
        yield
        x1 = x_ref[rows, :] + mo
        x1_ref[rows, :] = x1
        ms = jnp.mean(x1 * x1, axis=-1, keepdims=True)
        hn = x1 * lax.rsqrt(ms + EPS) * nw_ref[...]
        hn_ref[rows, :] = hn.astype(hn_ref.dtype)
        h_hi = hn.astype(BF16)
        h_lo = (hn - h_hi.astype(F32)).astype(BF16)
        yield
        logits = (_dot(h_hi, rwh_ref[...]) + _dot(h_lo, rwh_ref[...]) + _dot(h_hi, rwl_ref[...])
                  + rb_ref[...])
        yield
        gate_ref[rows, :] = _route(logits)

    _round_robin([sub_tile(r0) for r0 in range(0, OUT_TM, OUT_SUB)])


def _out_proj(x, att, gla, ssm, lw, layer):
    t = x.shape[0]

    def rows(width):
        return pl.BlockSpec((OUT_TM, width), lambda i: (i, 0))

    def const(shape):
        return pl.BlockSpec(shape, lambda i: (0, 0))

    return pl.pallas_call(
        _out_proj_kernel,
        grid=(t // OUT_TM,),
        in_specs=[rows(D_MODEL), rows(WIDTH_A), rows(WIDTH_B), rows(WIDTH_C),
                  pl.BlockSpec((1, D_MODEL, D_MODEL), lambda i: (layer, 0, 0)), const((1, D_MODEL)),
                  const((D_MODEL, 128)), const((D_MODEL, 128)), const((1, 128))],
        out_specs=[rows(D_MODEL), rows(D_MODEL), rows(128)],
        out_shape=[jax.ShapeDtypeStruct((t, D_MODEL), F32),
                   jax.ShapeDtypeStruct((t, D_MODEL), BF16),
                   jax.ShapeDtypeStruct((t, 128), F32)],
        scratch_shapes=[pltpu.VMEM((D_MODEL, D_MODEL), BF16)],
        compiler_params=_params("arbitrary"),
    )(x, att, gla, ssm, lw["wo"], lw["n2w"], lw["rw_hi"], lw["rw_lo"], lw["rb"])


RT_TS = OUT_SUB
CB_TILES = 2
RT_ALIGN = 16
RT_ROWS = 384
RT_SIZES = (256, 128, 64, 32, 16)
PAY_COLS = D_MODEL + 256
MOE_TM = 512


def _moe_max_tiles(t):
    return (t + (t // RT_TS) * N_EXPERT_GROUPS * (RT_ALIGN - 1)) // MOE_TM + N_EXPERT_GROUPS


def _sort_matrix(gate):
    lane = lax.broadcasted_iota(jnp.int32, (RT_TS, 128), 1)
    dest = jnp.where(lane == ROUTER_DEST_LANE, gate, 0.0).sum(axis=1, keepdims=True)
    rows = lax.broadcasted_iota(jnp.int32, (RT_TS, RT_ROWS), 1).astype(F32)
    return jnp.where(rows == dest, 1.0, 0.0).astype(BF16)


def _piece_copies(length, local, glob, make_copy, act):
    for size in RT_SIZES:
        take = (length & size) != 0

        @pl.when(take)
        def _(local=local, glob=glob, size=size):
            act(make_copy(pl.multiple_of(local, RT_ALIGN), pl.multiple_of(glob, RT_ALIGN), size))

        step = jnp.where(take, size, 0)
        local = local + step
        glob = glob + step


def _run_copies(c_ref, lp_ref, off_ref, tile, make_copy, act):
    for g in range(N_EXPERT_GROUPS):
        k = N_EXPERT_GROUPS * tile + g
        _piece_copies(lp_ref[k], c_ref[k], off_ref[k], make_copy, act)


def _dispatch_kernel(c_ref, lp_ref, off_ref, tail_ref, nact_ref, hn_ref, gate_ref, xs_hbm,
                     y_ref, zero_ref, sem, zsem, *, first_spare, max_tiles):
    s = pl.program_id(0)
    ns = pl.num_programs(0)
    base = (s % 2) * CB_TILES

    def tile(u):
        rows = slice(u * RT_TS, (u + 1) * RT_TS)
        gate = gate_ref[rows, :]
        pt = _sort_matrix(gate)
        g_hi = gate.astype(BF16)
        g_lo = (gate - g_hi.astype(F32)).astype(BF16)
        yield
        y_ref[base + u, :, 0:D_MODEL] = _dot_tn(pt, hn_ref[rows, :]).astype(BF16)
        y_ref[base + u, :, D_MODEL:D_MODEL + 128] = _dot_tn(pt, g_hi).astype(BF16)
        y_ref[base + u, :, D_MODEL + 128:] = _dot_tn(pt, g_lo).astype(BF16)

    _round_robin([tile(u) for u in range(CB_TILES)])

    def copy_from(slot):
        return lambda local, glob, size: pltpu.make_async_copy(
            y_ref.at[slot, pl.ds(local, size)], xs_hbm.at[pl.ds(glob, size)], sem.at[slot])

    def copies(step, slot0, act):
        for u in range(CB_TILES):
            _run_copies(c_ref, lp_ref, off_ref, step * CB_TILES + u, copy_from(slot0 + u), act)

    copies(s, base, lambda cp: cp.start())

    @pl.when(s > 0)
    def _():
        copies(s - 1, CB_TILES - base, lambda cp: cp.wait())

    @pl.when(s == ns - 1)
    def _():
        copies(s, base, lambda cp: cp.wait())
        zero_ref[...] = jnp.zeros_like(zero_ref)

        def zero_copy(local, glob, size):
            return pltpu.make_async_copy(zero_ref.at[pl.ds(local, size)],
                                         xs_hbm.at[pl.ds(glob, size)], zsem)

        def zero_fill(act):
            for g in range(N_EXPERT_GROUPS):
                _piece_copies(tail_ref[N_EXPERT_GROUPS + g], 0, tail_ref[g], zero_copy, act)
            for j in range(first_spare, max_tiles):
                @pl.when(j >= nact_ref[0])
                def _(j=j):
                    act(zero_copy(0, j * MOE_TM, MOE_TM))

        zero_fill(lambda cp: cp.start())
        zero_fill(lambda cp: cp.wait())


def _moe_group_kernel(grp_ref, nact_ref, xs_ref, wg_ref, wu_ref, wd_ref, o_ref, wg_bf, wu_bf, wd_bf):
    j = pl.program_id(0)

    @pl.when(jnp.logical_or(j == 0, grp_ref[j] != grp_ref[jnp.maximum(j - 1, 0)]))
    def _():
        wg_bf[...] = wg_ref[0].astype(BF16)
        wu_bf[...] = wu_ref[0].astype(BF16)
        wd_bf[...] = wd_ref[0].astype(BF16)

    @pl.when(j >= nact_ref[0])
    def _():
        o_ref[...] = jnp.zeros_like(o_ref)

    @pl.when(j < nact_ref[0])
    def _():
        x = xs_ref[:, 0:D_MODEL]
        gate = (xs_ref[:, D_MODEL:D_MODEL + 128].astype(F32)
                + xs_ref[:, D_MODEL + 128:].astype(F32))
        lane = lax.broadcasted_iota(jnp.int32, gate.shape, 1)
        first = grp_ref[j] * EXPERTS_PER_GROUP
        acc = None
        for e in range(EXPERTS_PER_GROUP):
            ge = jnp.where(lane == first + e, gate, 0.0).sum(axis=-1, keepdims=True)
            hid = _silu(_dot(x, wg_bf[e])) * _dot(x, wu_bf[e]) * ge
            part = _dot(hid.astype(BF16), wd_bf[e])
            acc = part if acc is None else acc + part
        o_ref[...] = acc


def _combine_kernel(c_ref, lp_ref, off_ref, x_ref, gate_ref, fw_ref, ys_hbm, o_ref, z_ref, sem,
                    *, final_norm):
    s = pl.program_id(0)
    ns = pl.num_programs(0)
    base = (s % 2) * CB_TILES

    def copy_into(slot):
        return lambda local, glob, size: pltpu.make_async_copy(
            ys_hbm.at[pl.ds(glob, size)], z_ref.at[slot, pl.ds(local, size)], sem.at[slot])

    def copies(step, slot0, act):
        for u in range(CB_TILES):
            _run_copies(c_ref, lp_ref, off_ref, step * CB_TILES + u, copy_into(slot0 + u), act)

    @pl.when(s == 0)
    def _():
        z_ref[...] = jnp.zeros_like(z_ref)
        copies(0, 0, lambda cp: cp.start())

    @pl.when(s + 1 < ns)
    def _():
        copies(s + 1, CB_TILES - base, lambda cp: cp.start())

    copies(s, base, lambda cp: cp.wait())

    def tile(u):
        rows = slice(u * RT_TS, (u + 1) * RT_TS)
        pt = _sort_matrix(gate_ref[rows, :])
        yield
        z = z_ref[base + u]
        z_hi = z.astype(BF16)
        z_lo = (z - z_hi.astype(F32)).astype(BF16)
        y = x_ref[rows, :] + _dot(pt, z_hi) + _dot(pt, z_lo)
        yield
        if final_norm:
            ms = jnp.mean(y * y, axis=-1, keepdims=True)
            y = y * lax.rsqrt(ms + EPS) * fw_ref[...]
        o_ref[rows, :] = y

    _round_robin([tile(u) for u in range(CB_TILES)])


def _route_meta(gate, max_tiles):
    t = gate.shape[0]
    nt = t // RT_TS
    groups = jnp.arange(N_EXPERT_GROUPS, dtype=jnp.int32)
    g = gate[:, ROUTER_GIDX_LANE].astype(jnp.int32).reshape(nt, RT_TS, 1)
    count = (g == groups).astype(jnp.int32).sum(axis=1)
    lp = (count + RT_ALIGN - 1) // RT_ALIGN * RT_ALIGN
    total = lp.sum(axis=0)
    ntile = (total + MOE_TM - 1) // MOE_TM
    last = jnp.cumsum(ntile)
    first_row = (last - ntile) * MOE_TM
    c = jnp.cumsum(lp, axis=1) - lp
    off = first_row[None, :] + jnp.cumsum(lp, axis=0) - lp
    tail = jnp.concatenate([first_row + total, ntile * MOE_TM - total])
    j = jnp.arange(max_tiles, dtype=jnp.int32)[:, None]
    grp = jnp.minimum((j >= last[None, :]).astype(jnp.int32).sum(axis=1), N_EXPERT_GROUPS - 1)
    return c.reshape(-1), lp.reshape(-1), off.reshape(-1), tail, grp, last[-1:]


def _moe(x, hn, gate, lw, layer, fw, final_norm):
    t = x.shape[0]
    nt = t // RT_TS
    max_tiles = _moe_max_tiles(t)
    c, lp, off, tail, grp, nact = _route_meta(gate, max_tiles)

    def wide(width):
        return pl.BlockSpec((CB_TILES * RT_TS, width), lambda i, *_: (i, 0))

    xs = pl.pallas_call(
        functools.partial(_dispatch_kernel, first_spare=t // MOE_TM, max_tiles=max_tiles),
        grid_spec=pltpu.PrefetchScalarGridSpec(
            num_scalar_prefetch=5, grid=(nt // CB_TILES,),
            in_specs=[wide(D_MODEL), wide(128)],
            out_specs=pl.BlockSpec(memory_space=pl.ANY),
            scratch_shapes=[pltpu.VMEM((2 * CB_TILES, RT_ROWS, PAY_COLS), BF16),
                            pltpu.VMEM((MOE_TM, PAY_COLS), BF16),
                            pltpu.SemaphoreType.DMA((2 * CB_TILES,)), pltpu.SemaphoreType.DMA(())]),
        out_shape=jax.ShapeDtypeStruct((max_tiles * MOE_TM, PAY_COLS), BF16),
        compiler_params=_params("arbitrary"),
    )(c, lp, off, tail, nact, hn, gate)

    def experts(shape):
        return pl.BlockSpec((1, EXPERTS_PER_GROUP) + shape, lambda j, grp, nact: (layer, grp[j], 0, 0))

    def rounded(shape):
        return pltpu.VMEM((EXPERTS_PER_GROUP,) + shape, BF16)

    ys = pl.pallas_call(
        _moe_group_kernel,
        grid_spec=pltpu.PrefetchScalarGridSpec(
            num_scalar_prefetch=2, grid=(max_tiles,),
            in_specs=[pl.BlockSpec((MOE_TM, PAY_COLS), lambda j, grp, nact: (j, 0)),
                      experts((D_MODEL, D_EXPERT)), experts((D_MODEL, D_EXPERT)),
                      experts((D_EXPERT, D_MODEL))],
            out_specs=pl.BlockSpec((MOE_TM, D_MODEL), lambda j, grp, nact: (j, 0)),
            scratch_shapes=[rounded((D_MODEL, D_EXPERT)), rounded((D_MODEL, D_EXPERT)),
                            rounded((D_EXPERT, D_MODEL))]),
        out_shape=jax.ShapeDtypeStruct((max_tiles * MOE_TM, D_MODEL), F32),
        compiler_params=_params("arbitrary"),
    )(grp, nact, xs, lw["wg"], lw["wu"], lw["wd"])

    return pl.pallas_call(
        functools.partial(_combine_kernel, final_norm=final_norm),
        grid_spec=pltpu.PrefetchScalarGridSpec(
            num_scalar_prefetch=3, grid=(nt // CB_TILES,),
            in_specs=[wide(D_MODEL), wide(128), pl.BlockSpec((1, D_MODEL), lambda i, *_: (0, 0)),
                      pl.BlockSpec(memory_space=pl.ANY)],
            out_specs=wide(D_MODEL),
            scratch_shapes=[pltpu.VMEM((2 * CB_TILES, RT_ROWS, D_MODEL), F32),
                            pltpu.SemaphoreType.DMA((2 * CB_TILES,))]),
        out_shape=jax.ShapeDtypeStruct((t, D_MODEL), F32),
        compiler_params=_params("arbitrary"),
    )(c, lp, off, x, gate, fw, ys)


def _lane_place(vec, start, width=128):
    return jnp.zeros((1, width), F32).at[0, start:start + vec.shape[0]].set(vec.astype(F32))


REL_PAD = 704


def _toeplitz(v, rows, cols):
    n = v.shape[-1]
    tiled = jnp.tile(v, (1,) * (v.ndim - 1) + (rows,))[..., :rows * (n - 1)]
    return tiled.reshape(v.shape[:-1] + (rows, n - 1))[..., :cols]


def _band_bias(table):
    t = table.astype(F32)
    ext = jnp.concatenate([jnp.repeat(t[:, :1], REL_PAD, axis=1), t,
                           jnp.repeat(t[:, -1:], REL_PAD, axis=1)], axis=1)
    extr = ext[:, ::-1]
    top = ext.shape[1] - 1 - (REL_CLIP + REL_PAD)

    def band(off, nq, nk):
        n = nq + nk
        v = jnp.concatenate([extr[:, top - off:top - off + nk + 1],
                             extr[:, top - off - (nq - 1):top - off]], axis=1)
        assert v.shape[1] == n
        return _toeplitz(v, nq, nk)

    sample = band(WINDOW, CHUNK, BAND)
    kc = (jnp.arange(ATT_KB) // CHUNK)[:, None]
    qc = (jnp.arange(ATT_QB) // CHUNK)[None, :]
    block = lambda off: jnp.swapaxes(band(off, ATT_QB, ATT_KB), 1, 2)
    prompt = jnp.stack([jnp.where((kc >= qc)[None], block(2 * ATT_KB), -1e30),
                        block(ATT_KB),
                        jnp.where((kc <= qc)[None], block(0), -1e30)])
    return sample * LOG2_E, prompt * LOG2_E


def _layer_weights(i, norm1_w, w_in, rel_bias_table, gla_w_gate_up, gla_b_gate, gla_norm_w,
                   ssm_conv_w, ssm_conv_b, ssm_dt_bias, ssm_a_log, ssm_d, ssm_norm_w, w_out,
                   norm2_w, router_group_w, router_group_b, router_expert_w, router_expert_b,
                   exp_w_gate, exp_w_up, exp_w_down):
    bias_s, bias_p = _band_bias(rel_bias_table[i])
    rw = jnp.concatenate([router_expert_w[i], router_group_w[i],
                          jnp.zeros((D_MODEL, 128 - N_EXPERTS - N_EXPERT_GROUPS), F32)], axis=1)
    rw_hi = rw.astype(BF16)
    return dict(
        n1w=norm1_w[i][None, :],
        w_in=w_in,
        bias_s=bias_s,
        bias_p=bias_p,
        gup=jnp.zeros((128, 128), F32).at[:GATE_RANK, :].set(gla_w_gate_up[i]).astype(BF16),
        gb=gla_b_gate[i][None, :].astype(F32),
        gnw=jnp.tile(gla_norm_w[i], H_B)[None, :].astype(F32),
        cw=ssm_conv_w[i].astype(F32),
        cb=ssm_conv_b[i][None, :].astype(F32),
        dtb=_lane_place(ssm_dt_bias[i], MISC_DT_LANE),
        alog=jnp.repeat(ssm_a_log[i].astype(F32), P_C)[None, :],
        dsk=jnp.repeat(ssm_d[i].astype(F32), P_C)[None, :],
        snw=ssm_norm_w[i][None, :].astype(F32),
        wo=w_out,
        n2w=norm2_w[i][None, :],
        rw_hi=rw_hi,
        rw_lo=(rw - rw_hi.astype(F32)).astype(BF16),
        rb=jnp.concatenate([router_expert_b[i], router_group_b[i],
                            jnp.zeros((128 - N_EXPERTS - N_EXPERT_GROUPS,), F32)])[None, :],
        wg=exp_w_gate,
        wu=exp_w_up,
        wd=exp_w_down,
    )


def _stream_layer(x, bsz, lw, layer, cache_k, cache_v, gla0, conv0, ssm0, fw, final_norm):
    l = x.shape[0] // bsz
    if cache_k is None:
        qk, rest, k_keep, v_keep, vt = _in_proj(x, lw["n1w"], lw["w_in"], layer, True, l)
        att = _attn_prompt(qk.reshape(bsz, l, 2 * WIDTH_A),
                           vt.reshape(bsz, l // ATT_KB, WIDTH_A, ATT_KB), lw["bias_p"])
    else:
        qk, rest, k_keep, v_keep, kv = _in_proj(x, lw["n1w"], lw["w_in"], layer, False, l)
        att = _attn_sample(qk.reshape(bsz, l, 2 * WIDTH_A), kv.reshape(bsz, l, 2 * WIDTH_A),
                           cache_k, cache_v, layer, lw["bias_s"])
    gla_o, ssm_o, gla_s, conv_s, ssm_s = _scan(rest.reshape(bsz, l, REST_COLS), lw, gla0, conv0, ssm0)
    t = bsz * l
    x1, hn, gate = _out_proj(x, att.reshape(t, WIDTH_A), gla_o.reshape(t, WIDTH_B),
                             ssm_o.reshape(t, WIDTH_C), lw, layer)
    x2 = _moe(x1, hn, gate, lw, layer, fw, final_norm)
    keep = min(WINDOW, l)
    k_keep = k_keep.reshape(bsz, keep, H_A, DH_A)
    v_keep = v_keep.reshape(bsz, keep, H_A, DH_A)
    gla_state = gla_s.reshape(bsz, DV_B, H_B, DK_B).transpose(0, 2, 3, 1)
    conv_state = conv_s[:, 8 - (CONV_W - 1):, :]
    return x2, k_keep, v_keep, gla_state, conv_state, ssm_s


def kernel(x_prompt, x_sample, cache_k_a, cache_v_a, state_gla, state_conv, state_ssm, norm1_w, w_in, rel_bias_table, gla_w_gate_up, gla_b_gate, gla_norm_w, ssm_conv_w, ssm_conv_b, ssm_dt_bias, ssm_a_log, ssm_d, ssm_norm_w, w_out, norm2_w, router_group_w, router_group_b, router_expert_w, router_expert_b, exp_w_gate, exp_w_up, exp_w_down, final_norm_w):
    bp, sp, _ = x_prompt.shape
    bs, ss, _ = x_sample.shape
    xp = x_prompt.reshape(bp * sp, D_MODEL)
    xs = x_sample.reshape(bs * ss, D_MODEL)
    fw = final_norm_w[None, :].astype(F32)
    outs_p, outs_s = [], []
    cache_k = cache_k_a.astype(BF16).reshape(DEPTH, bs, -1, WIDTH_A)
    cache_v = cache_v_a.astype(BF16).reshape(DEPTH, bs, -1, WIDTH_A)
    for i in range(DEPTH):
        lw = _layer_weights(i, norm1_w, w_in, rel_bias_table, gla_w_gate_up, gla_b_gate, gla_norm_w,
                            ssm_conv_w, ssm_conv_b, ssm_dt_bias, ssm_a_log, ssm_d, ssm_norm_w, w_out,
                            norm2_w, router_group_w, router_group_b, router_expert_w, router_expert_b,
                            exp_w_gate, exp_w_up, exp_w_down)
        last = i == DEPTH - 1
        xp, *sp_out = _stream_layer(
            xp, bp, lw, i, None, None,
            jnp.zeros((bp, DV_B, H_B * DK_B), F32),
            jnp.zeros((bp, 8, CONV_CH), F32),
            jnp.zeros((bp, H_C, P_C, D_STATE), F32), fw, last)
        outs_p.append(sp_out)
        gla0 = state_gla[i].astype(F32).transpose(0, 3, 1, 2).reshape(bs, DV_B, H_B * DK_B)
        conv0 = jnp.pad(state_conv[i].astype(F32), ((0, 0), (8 - (CONV_W - 1), 0), (0, 0)))
        xs, *ss_out = _stream_layer(
            xs, bs, lw, i, cache_k, cache_v,
            gla0, conv0, state_ssm[i].astype(F32), fw, last)
        outs_s.append(ss_out)
    stack = lambda outs, j: jnp.stack([o[j] for o in outs])
    return (xp.reshape(bp, sp, D_MODEL), xs.reshape(bs, ss, D_MODEL),
            stack(outs_p, 0), stack(outs_p, 1), stack(outs_p, 2), stack(outs_p, 3), stack(outs_p, 4),
            stack(outs_s, 0), stack(outs_s, 1), stack(outs_s, 2), stack(outs_s, 3), stack(outs_s, 4))
```

```python
import functools

import jax
import jax.numpy as jnp
from jax import lax
from jax.experimental import pallas as pl
from jax.experimental.pallas import tpu as pltpu

F32 = jnp.float32
BF16 = jnp.bfloat16

D_MODEL = 1024
DEPTH = 2
EPS = 1e-6
CHUNK = 64
N_PAST_CHUNKS = 8
BAND = (N_PAST_CHUNKS + 1) * CHUNK
WINDOW = N_PAST_CHUNKS * CHUNK
H_A, DH_A, WIDTH_A = 8, 64, 512
REL_CLIP = 128
LOG2_E = 1.4426950408889634
H_B, DK_B, DV_B, WIDTH_B = 4, 32, 64, 256
GATE_RANK = 16
GATE_TAU = 16.0
H_C, P_C, WIDTH_C = 4, 64, 256
N_GROUPS_C = 2
D_STATE = 64
CONV_W = 4
CONV_CH = 512
N_EXPERT_GROUPS = 4
EXPERTS_PER_GROUP = 4
N_EXPERTS = 16
D_EXPERT = 256

PROJ_COLS = 3200
MISC_DT_LANE = GATE_RANK
ROUTER_GROUP_LANE = N_EXPERTS
ROUTER_GIDX_LANE = 20
ROUTER_DEST_LANE = 21

SEQ_GROUP = 8
SEQ_UNROLL = 8
VMEM_LIMIT_BYTES = 56 * 1024 * 1024

NT_DIMS = (((1,), (1,)), ((), ()))
TN_DIMS = (((0,), (0,)), ((), ()))


def _params(*sem):
    return pltpu.CompilerParams(dimension_semantics=sem, vmem_limit_bytes=VMEM_LIMIT_BYTES)


def _dot(a, b):
    return jnp.dot(a, b, preferred_element_type=F32)


def _dot_nt(a, b):
    return lax.dot_general(a, b, NT_DIMS, preferred_element_type=F32)


def _dot_tn(a, b):
    return lax.dot_general(a, b, TN_DIMS, preferred_element_type=F32)


def _split2(a):
    hi = a.astype(BF16)
    return hi, (a - hi.astype(F32)).astype(BF16)


def _dot01_left(m01, a):
    hi, lo = _split2(a)
    return _dot(m01, hi) + _dot(m01, lo)


def _dot01_right(a, m01):
    hi, lo = _split2(a)
    return _dot(hi, m01) + _dot(lo, m01)


def _round_robin(gens):
    results = [None] * len(gens)
    live = list(range(len(gens)))
    while live:
        for idx in list(live):
            try:
                next(gens[idx])
            except StopIteration as stop:
                results[idx] = stop.value
                live.remove(idx)
    return results


def _silu(x):
    return x * jax.nn.sigmoid(x)


def _softplus(x):
    return jnp.maximum(x, 0.0) + jnp.log1p(jnp.exp(-jnp.abs(x)))


IN_TM = 256
QKV_COLS = 3 * WIDTH_A
REST_COLS = PROJ_COLS - QKV_COLS
REST_CHUNKS = ((0, 512), (512, 1024), (1024, REST_COLS))


IN_COLS = 3092
IN_TAIL = QKV_COLS + 512
PREP_ROWS = 128


def _prepare_w_in(w_in_ref, w_ref):
    scale = DH_A ** -0.5 * LOG2_E
    n_tail = REST_COLS - 512 - 128
    for r in range(0, D_MODEL, PREP_ROWS):
        rs = slice(r, r + PREP_ROWS)
        w_ref[rs, 0:WIDTH_A] = (w_in_ref[0, rs, 0:WIDTH_A] * scale).astype(BF16)
        w_ref[rs, WIDTH_A:IN_TAIL] = w_in_ref[0, rs, WIDTH_A:IN_TAIL].astype(BF16)
        tail = w_in_ref[0, rs, IN_TAIL:IN_COLS]
        w_ref[rs, IN_TAIL:IN_TAIL + n_tail] = tail[:, GATE_RANK:GATE_RANK + n_tail].astype(BF16)
        misc = jnp.concatenate([tail[:, 0:GATE_RANK], tail[:, GATE_RANK + n_tail:],
                                jnp.zeros((PREP_ROWS, 128 - GATE_RANK - H_C), F32)], axis=1)
        w_ref[rs, IN_TAIL + n_tail:] = misc.astype(BF16)


def _in_proj_kernel(x_ref, nw_ref, w_in_ref, *refs, with_vt):
    if with_vt:
        qk_ref, rest_ref, kk_ref, vk_ref, vt_ref, w_ref = refs
    else:
        qk_ref, rest_ref, kk_ref, vk_ref, kv_ref, w_ref = refs

    @pl.when(pl.program_id(0) == 0)
    def _():
        _prepare_w_in(w_in_ref, w_ref)

    x = x_ref[...]
    ms = jnp.mean(x * x, axis=-1, keepdims=True)
    xn = (x * lax.rsqrt(ms + EPS) * nw_ref[...]).astype(BF16)
    qk_ref[:, 0:WIDTH_A] = _dot(xn, w_ref[:, 0:WIDTH_A]).astype(BF16)
    k = _dot(xn, w_ref[:, WIDTH_A:2 * WIDTH_A])
    v = _dot(xn, w_ref[:, 2 * WIDTH_A:QKV_COLS])
    qk_ref[:, WIDTH_A:] = k.astype(BF16)
    if not with_vt:
        kv_ref[:, 0:WIDTH_A] = k
        kv_ref[:, WIDTH_A:] = v

    def keep_rows():
        kk = kk_ref.reshape(IN_TM * H_A, DH_A)
        vk = vk_ref.reshape(IN_TM * H_A, DH_A)
        for h in range(H_A):
            sl = slice(DH_A * h, DH_A * (h + 1))
            kk[pl.ds(h, IN_TM, stride=H_A), :] = k[:, sl]
            vk[pl.ds(h, IN_TM, stride=H_A), :] = v[:, sl]

    keep_rows()
    for lo, hi in REST_CHUNKS:
        rest_ref[:, lo:hi] = _dot(xn, w_ref[:, QKV_COLS + lo:QKV_COLS + hi])
    if with_vt:
        vt_ref[0] = v.T.astype(BF16)


def _in_proj(x, nw, w_in, layer, with_vt, seq_len):
    t = x.shape[0]
    tiles_per_seq = max(seq_len // IN_TM, 1)
    keep_tiles = max(min(WINDOW, seq_len) // IN_TM, 1)
    n_keep = t // IN_TM // tiles_per_seq * keep_tiles

    def rows(width):
        return pl.BlockSpec((IN_TM, width), lambda i: (i, 0))

    def kept(i):
        return (i // tiles_per_seq * keep_tiles
                + jnp.maximum(i % tiles_per_seq - (tiles_per_seq - keep_tiles), 0), 0, 0)

    keep_spec = pl.BlockSpec((IN_TM, H_A, DH_A), kept)
    keep_shape = jax.ShapeDtypeStruct((n_keep * IN_TM, H_A, DH_A), F32)
    in_specs = [rows(D_MODEL), pl.BlockSpec((1, D_MODEL), lambda i: (0, 0)),
                pl.BlockSpec((1, D_MODEL, IN_COLS), lambda i: (layer, 0, 0))]
    scratch = [pltpu.VMEM((D_MODEL, PROJ_COLS), BF16)]
    out_specs = [rows(2 * WIDTH_A), rows(REST_COLS), keep_spec, keep_spec]
    out_shape = [jax.ShapeDtypeStruct((t, 2 * WIDTH_A), BF16),
                 jax.ShapeDtypeStruct((t, REST_COLS), F32), keep_shape, keep_shape]
    if with_vt:
        out_specs.append(pl.BlockSpec((1, WIDTH_A, IN_TM), lambda i: (i, 0, 0)))
        out_shape.append(jax.ShapeDtypeStruct((t // IN_TM, WIDTH_A, IN_TM), BF16))
    else:
        out_specs.append(rows(2 * WIDTH_A))
        out_shape.append(jax.ShapeDtypeStruct((t, 2 * WIDTH_A), F32))
    return pl.pallas_call(
        functools.partial(_in_proj_kernel, with_vt=with_vt),
        grid=(t // IN_TM,),
        in_specs=in_specs, out_specs=out_specs, out_shape=out_shape, scratch_shapes=scratch,
        compiler_params=_params("arbitrary"),
    )(x, nw, w_in)


ATT_QB = 4 * CHUNK
ATT_KB = IN_TM
ATT_NKB = (ATT_QB + WINDOW) // ATT_KB


def _attn_prompt_kernel(q_ref, k_ref, vt_ref, bias_ref, o_ref, ot_ref):
    blk = pl.program_id(1)
    sb = jnp.maximum(blk - WINDOW // ATT_KB, 0)
    kstart = pl.multiple_of(sb * ATT_KB, ATT_KB)
    lane = lax.broadcasted_iota(jnp.int32, (ATT_QB, 128), 1)

    def attend(kinds):
        def scores(h):
            pair = slice(128 * (h // 2), 128 * (h // 2 + 1))
            qp = q_ref[0, :, pair]
            qz = jnp.where((lane >= DH_A) == bool(h % 2), qp, jnp.zeros_like(qp))
            return [_dot_nt(k_ref[0, pl.ds(kstart + ATT_KB * i, ATT_KB), pair], qz)
                    + bias_ref[kind, h] for i, kind in enumerate(kinds)]

        def finish(h, s):
            m = s[0].max(axis=0, keepdims=True)
            for si in s[1:]:
                m = jnp.maximum(m, si.max(axis=0, keepdims=True))
            l = None
            ot = None
            for i, si in enumerate(s):
                p = jnp.exp2(si - m)
                li = p.sum(axis=0, keepdims=True)
                oi = _dot(vt_ref[0, sb + i, DH_A * h:DH_A * (h + 1), :], p.astype(BF16))
                l = li if l is None else l + li
                ot = oi if ot is None else ot + oi
            ot_ref[DH_A * h:DH_A * (h + 1), :] = ot / l

        pending = scores(0)
        for h in range(H_A):
            upcoming = scores(h + 1) if h + 1 < H_A else None
            finish(h, pending)
            pending = upcoming
        o_ref[0] = ot_ref[...].T.astype(o_ref.dtype)

    pl.when(blk == 0)(lambda: attend((2,)))
    pl.when(blk == 1)(lambda: attend((1, 2)))
    pl.when(blk >= 2)(lambda: attend((0, 1, 2)))


def _attn_prompt(qk, vt, bias_t):
    bsz, s, _ = qk.shape
    nkb = s // ATT_KB
    return pl.pallas_call(
        _attn_prompt_kernel,
        grid=(bsz, s // ATT_QB),
        in_specs=[pl.BlockSpec((1, ATT_QB, WIDTH_A), lambda b, c: (b, c, 0)),
                  pl.BlockSpec((1, s, WIDTH_A), lambda b, c: (b, 0, 1)),
                  pl.BlockSpec((1, nkb, WIDTH_A, ATT_KB), lambda b, c: (b, 0, 0, 0)),
                  pl.BlockSpec((3, H_A, ATT_KB, ATT_QB), lambda b, c: (0, 0, 0, 0))],
        out_specs=pl.BlockSpec((1, ATT_QB, WIDTH_A), lambda b, c: (b, c, 0)),
        out_shape=jax.ShapeDtypeStruct((bsz, s, WIDTH_A), BF16),
        scratch_shapes=[pltpu.VMEM((WIDTH_A, ATT_QB), F32)],
        compiler_params=_params("parallel", "arbitrary"),
    )(qk, qk, vt, bias_t)


def _attn_sample_kernel(q_ref, kv_ref, ck_ref, cv_ref, bias_ref, o_ref):
    lane = lax.broadcasted_iota(jnp.int32, (CHUNK, 128), 1)

    def scores(h):
        pair = slice(128 * (h // 2), 128 * (h // 2 + 1))
        qp = q_ref[0, :, pair]
        qz = jnp.where((lane >= DH_A) == bool(h % 2), qp, jnp.zeros_like(qp))
        return (_dot_nt(qz, ck_ref[0, 0, :, pair]) + bias_ref[h, :, 0:WINDOW],
                _dot_nt(qz, kv_ref[0, :, pair].astype(BF16)) + bias_ref[h, :, WINDOW:BAND])

    def finish(h, s):
        s1, s2 = s
        pair = slice(128 * (h // 2), 128 * (h // 2 + 1))
        vpair = slice(WIDTH_A + 128 * (h // 2), WIDTH_A + 128 * (h // 2 + 1))
        half = slice(DH_A * (h % 2), DH_A * (h % 2 + 1))
        m = jnp.maximum(s1.max(axis=-1, keepdims=True), s2.max(axis=-1, keepdims=True))
        p1 = jnp.exp2(s1 - m)
        p2 = jnp.exp2(s2 - m)
        l = p1.sum(axis=-1, keepdims=True) + p2.sum(axis=-1, keepdims=True)
        o = (_dot(p1.astype(BF16), cv_ref[0, 0, :, pair])
             + _dot(p2.astype(BF16), kv_ref[0, :, vpair].astype(BF16)))
        o_ref[0, :, DH_A * h:DH_A * (h + 1)] = (o[:, half] / l).astype(o_ref.dtype)

    pending = scores(0)
    for h in range(H_A):
        upcoming = scores(h + 1) if h + 1 < H_A else None
        finish(h, pending)
        pending = upcoming


def _attn_sample(qk, kv, cache_k, cache_v, layer, bias):
    bsz = qk.shape[0]
    cache_spec = pl.BlockSpec((1, 1, WINDOW, WIDTH_A), lambda b: (layer, b, 0, 0))
    return pl.pallas_call(
        _attn_sample_kernel,
        grid=(bsz,),
        in_specs=[pl.BlockSpec((1, CHUNK, WIDTH_A), lambda b: (b, 0, 0)),
                  pl.BlockSpec((1, CHUNK, 2 * WIDTH_A), lambda b: (b, 0, 0)),
                  cache_spec, cache_spec,
                  pl.BlockSpec((H_A, CHUNK, BAND), lambda b: (0, 0, 0))],
        out_specs=pl.BlockSpec((1, CHUNK, WIDTH_A), lambda b: (b, 0, 0)),
        out_shape=jax.ShapeDtypeStruct((bsz, CHUNK, WIDTH_A), BF16),
        compiler_params=_params("parallel"),
    )(qk, kv, cache_k, cache_v, bias)


def _scan_kernel(bqk_ref, bv_ref, bog_ref, cz_ref, cx_ref, misc_ref,
                 gup_ref, gb_ref, gnw_ref, cw_ref, cb_ref, dtb_ref, alog_ref, dsk_ref, snw_ref,
                 gla0_ref, conv0_ref, ssm0_ref,
                 gla_o_ref, ssm_o_ref, gla_s_ref, conv_s_ref, ssm_s_ref,
                 xpad_ref):
    c = pl.program_id(1)
    nseq = bqk_ref.shape[0]

    @pl.when(c == 0)
    def _():
        gla_s_ref[...] = gla0_ref[...]
        conv_s_ref[...] = conv0_ref[...]
        ssm_s_ref[...] = ssm0_ref[...]

    r64 = lax.broadcasted_iota(jnp.int32, (CHUNK, CHUNK), 0)
    c64 = lax.broadcasted_iota(jnp.int32, (CHUNK, CHUNK), 1)
    tri = c64 <= r64
    tri_bf = jnp.where(tri, 1.0, 0.0).astype(BF16)
    ones_bf = jnp.ones((CHUNK, CHUNK), BF16)
    row_x = lax.broadcasted_iota(jnp.int32, (CHUNK, WIDTH_C), 0)
    s_x = lax.broadcasted_iota(jnp.int32, (CHUNK, WIDTH_C), 1) & (CHUNK - 1)
    causal_x = s_x <= row_x
    upper_x = row_x <= s_x
    er = lax.broadcasted_iota(jnp.int32, (128, WIDTH_C), 0)
    el = lax.broadcasted_iota(jnp.int32, (128, WIDTH_C), 1)
    expand_bf = jnp.where(er == MISC_DT_LANE + (el >> 6), 1.0, 0.0).astype(BF16)
    br = lax.broadcasted_iota(jnp.int32, (WIDTH_C, WIDTH_C), 0)
    bl = lax.broadcasted_iota(jnp.int32, (WIDTH_C, WIDTH_C), 1)
    blk64_bf = jnp.where((br >> 6) == (bl >> 6), 1.0, 0.0).astype(BF16)
    blk128_bf = jnp.where((br >> 7) == (bl >> 7), 1.0, 0.0).astype(BF16)

    gup = gup_ref[...]
    gb = gb_ref[...]
    gnw = gnw_ref[...]
    cb = cb_ref[...]
    dtb = dtb_ref[...]
    neg_a = -jnp.exp(alog_ref[...])
    dsk = dsk_ref[...]
    snw = snw_ref[...]

    def load_seq(b):
        return dict(misc=misc_ref[b], qk=bqk_ref[b], v=bv_ref[b], bog=bog_ref[b], cz=cz_ref[b],
                    cx=cx_ref[b], gla=gla_s_ref[b], conv=conv_s_ref[b],
                    ssm=[ssm_s_ref[b, h] for h in range(H_C)])

    def store_seq(b, out):
        gla_o_ref[b] = out["gla_o"]
        ssm_o_ref[b] = out["ssm_o"]
        gla_s_ref[b] = out["gla"]
        conv_s_ref[b] = out["conv"]
        for h in range(H_C):
            ssm_s_ref[b, h] = out["ssm"][h]

    def compute_seq(inp, slot):
        out = {}
        misc = inp["misc"]

        z = _dot(misc.astype(BF16), gup) + gb
        g = (jnp.minimum(z, 0.0) - jnp.log1p(jnp.exp(-jnp.abs(z)))) * (1.0 / GATE_TAU)
        bcum = _dot01_left(tri_bf, g)
        yield
        q = inp["qk"][:, :H_B * DK_B]
        k = inp["qk"][:, H_B * DK_B:]
        blast = bcum[CHUNK - 1:CHUNK, :]
        qb = (q * (DK_B ** -0.5) * jnp.exp(bcum)).astype(BF16)
        kb = (k * jnp.exp(-bcum)).astype(BF16)
        kl = (k * jnp.exp(blast - bcum)).astype(BF16)
        vb = inp["v"].astype(BF16)
        st = inp["gla"]
        st_bf = st.astype(BF16)
        yield
        o_parts, s_parts = [], []
        for h in range(H_B):
            ks = slice(DK_B * h, DK_B * (h + 1))
            vs = slice(DV_B * h, DV_B * (h + 1))
            att = jnp.where(tri, _dot_nt(qb[:, ks], kb[:, ks]), 0.0).astype(BF16)
            o_parts.append(_dot(att, vb[:, vs]) + _dot_nt(qb[:, ks], st_bf[:, ks]))
            s_parts.append(_dot_tn(vb[:, vs], kl[:, ks]))
            yield
        o = jnp.concatenate(o_parts, axis=1)
        out["gla"] = jnp.exp(blast) * st + jnp.concatenate(s_parts, axis=1)
        msq = _dot01_right(o * o, blk64_bf) * (1.0 / DV_B)
        yield
        go = o * lax.rsqrt(msq + EPS) * gnw * _silu(inp["bog"])
        out["gla_o"] = go.astype(gla_o_ref.dtype)
        yield

        cx = inp["cx"]
        xpad_ref[slot, 0:8, :] = inp["conv"]
        xpad_ref[slot, 8:8 + CHUNK, :] = cx
        conv = cb + cx * cw_ref[CONV_W - 1:CONV_W, :]
        for i in range(CONV_W - 1):
            off = 8 - (CONV_W - 1) + i
            conv = conv + xpad_ref[slot, off:off + CHUNK, :] * cw_ref[i:i + 1, :]
        out["conv"] = cx[CHUNK - 8:, :]
        conv = _silu(conv)
        sx = conv[:, :WIDTH_C]
        sb_bf = conv[:, WIDTH_C:WIDTH_C + N_GROUPS_C * D_STATE].astype(BF16)
        sc_bf = conv[:, WIDTH_C + N_GROUPS_C * D_STATE:].astype(BF16)
        yield

        dtv = _softplus(misc + dtb)
        dt_x = _dot01_right(dtv, expand_bf)
        yield
        a_x = neg_a * dt_x
        cs_x = _dot01_left(tri_bf, a_x)
        cs_row = _dot01_left(ones_bf, jnp.where(upper_x, a_x, 0.0))
        yield
        decay = jnp.exp(jnp.where(causal_x, cs_x - cs_row, -jnp.inf))
        cb_parts = []
        for gidx in range(N_GROUPS_C):
            ns = slice(D_STATE * gidx, D_STATE * (gidx + 1))
            cbm = _dot_nt(sc_bf[:, ns], sb_bf[:, ns])
            cb_parts += [cbm] * (H_C // N_GROUPS_C)
            yield
        scores = (jnp.concatenate(cb_parts, axis=1) * decay).astype(BF16)
        xdt = (sx * dt_x).astype(BF16)
        ecs = jnp.exp(cs_x)
        cs_last = cs_x[CHUNK - 1:CHUNK, :]
        xw = (sx * (jnp.exp(cs_last - cs_x) * dt_x)).astype(BF16)
        e_last = jnp.exp(cs_last)
        yield
        y_parts = []
        out["ssm"] = []
        for h in range(H_C):
            ps = slice(P_C * h, P_C * (h + 1))
            gi = h // (H_C // N_GROUPS_C)
            ns = slice(D_STATE * gi, D_STATE * (gi + 1))
            hst = inp["ssm"][h]
            y_parts.append(_dot(scores[:, ps], xdt[:, ps])
                           + _dot_nt(sc_bf[:, ns], hst.astype(BF16)) * ecs[:, ps])
            out["ssm"].append(e_last[:, ps] * hst + _dot_tn(xw[:, ps], sb_bf[:, ns]))
            yield
        y = jnp.concatenate(y_parts, axis=1) + dsk * sx
        y = y * _silu(inp["cz"])
        ymsq = _dot01_right(y * y, blk128_bf) * (1.0 / (WIDTH_C // N_GROUPS_C))
        yield
        out["ssm_o"] = (y * lax.rsqrt(ymsq + EPS) * snw).astype(ssm_o_ref.dtype)
        return out

    def group_body(i, carry):
        seqs = [i * SEQ_UNROLL + u for u in range(SEQ_UNROLL)]
        inputs = [load_seq(b) for b in seqs]
        outputs = _round_robin([compute_seq(inp, slot) for slot, inp in enumerate(inputs)])
        for b, out in zip(seqs, outputs):
            store_seq(b, out)
        return carry

    lax.fori_loop(0, nseq // SEQ_UNROLL, group_body, 0)


def _scan(proj, lw, gla0, conv0, ssm0):
    bsz, l, _ = proj.shape
    nc = l // CHUNK
    sg = min(SEQ_GROUP, bsz)
    ng = bsz // sg

    def col(width, idx):
        return pl.BlockSpec((sg, CHUNK, width), lambda g, c: (g, c, idx))

    def const(shape):
        return pl.BlockSpec(shape, lambda g, c: (0,) * len(shape))

    def state(shape):
        return pl.BlockSpec((sg,) + shape, lambda g, c: (g,) + (0,) * len(shape))

    gla_shape = (DV_B, H_B * DK_B)
    conv_shape = (8, CONV_CH)
    ssm_shape = (H_C, P_C, D_STATE)
    return pl.pallas_call(
        _scan_kernel,
        grid=(ng, nc),
        in_specs=[col(256, 0), col(256, 1), col(256, 2), col(256, 3), col(512, 2), col(128, 12),
                  const((128, 128)), const((1, 128)), const((1, WIDTH_B)),
                  const((CONV_W, CONV_CH)), const((1, CONV_CH)),
                  const((1, 128)), const((1, WIDTH_C)), const((1, WIDTH_C)), const((1, WIDTH_C)),
                  state(gla_shape), state(conv_shape), state(ssm_shape)],
        out_specs=[pl.BlockSpec((sg, CHUNK, WIDTH_B), lambda g, c: (g, c, 0)),
                   pl.BlockSpec((sg, CHUNK, WIDTH_C), lambda g, c: (g, c, 0)),
                   state(gla_shape), state(conv_shape), state(ssm_shape)],
        out_shape=[jax.ShapeDtypeStruct((bsz, l, WIDTH_B), BF16),
                   jax.ShapeDtypeStruct((bsz, l, WIDTH_C), BF16),
                   jax.ShapeDtypeStruct((bsz,) + gla_shape, F32),
                   jax.ShapeDtypeStruct((bsz,) + conv_shape, F32),
                   jax.ShapeDtypeStruct((bsz,) + ssm_shape, F32)],
        scratch_shapes=[pltpu.VMEM((SEQ_UNROLL, 8 + CHUNK, CONV_CH), F32)],
        compiler_params=_params("parallel", "arbitrary"),
    )(proj, proj, proj, proj, proj, proj,
      lw["gup"], lw["gb"], lw["gnw"], lw["cw"], lw["cb"], lw["dtb"], lw["alog"], lw["dsk"], lw["snw"],
      gla0, conv0, ssm0)


OUT_TM = 512
OUT_SUB = 256


def _route(logits):
    lane = lax.broadcasted_iota(jnp.int32, logits.shape, 1)
    lane_f = lane.astype(F32)
    neg = -jnp.inf
    big = 1000.0
    gmask = (lane >= ROUTER_GROUP_LANE) & (lane < ROUTER_GROUP_LANE + N_EXPERT_GROUPS)
    lg = jnp.where(gmask, logits, neg)
    gmax = lg.max(axis=-1, keepdims=True)
    g_idx = jnp.where(lg == gmax, lane_f, big).min(axis=-1, keepdims=True) - ROUTER_GROUP_LANE
    p_grp = 1.0 / jnp.where(gmask, jnp.exp(logits - gmax), 0.0).sum(axis=-1, keepdims=True)
    emask = (lane < N_EXPERTS) & ((lane >> 2).astype(F32) == g_idx)
    le = jnp.where(emask, logits, neg)
    v1 = le.max(axis=-1, keepdims=True)
    i1 = jnp.where(le == v1, lane_f, big).min(axis=-1, keepdims=True)
    le2 = jnp.where(lane_f == i1, neg, le)
    v2 = le2.max(axis=-1, keepdims=True)
    i2 = jnp.where(le2 == v2, lane_f, big).min(axis=-1, keepdims=True)
    e2 = jnp.exp(v2 - v1)
    den = 1.0 + e2
    n = logits.shape[0]
    onehot = lane_f == g_idx
    member = jnp.where(onehot, 1.0, 0.0)
    r = lax.broadcasted_iota(jnp.int32, (n, n), 0)
    c = lax.broadcasted_iota(jnp.int32, (n, n), 1)
    before = _dot(jnp.where(c < r, 1.0, 0.0).astype(BF16), member.astype(BF16))
    padded = jnp.floor((member.sum(axis=0, keepdims=True) + (RT_ALIGN - 1)) * (1.0 / RT_ALIGN)) * RT_ALIGN
    lane1 = lane[0:1, :]
    start = jnp.zeros((1, 128), F32)
    for g in range(N_EXPERT_GROUPS - 1):
        run = jnp.where(lane1 == g, padded, 0.0).sum(axis=1, keepdims=True)
        start = start + jnp.where(lane1 > g, run, 0.0)
    dest = jnp.where(onehot, before + start, 0.0).sum(axis=1, keepdims=True)
    return (jnp.where(lane_f == i1, (1.0 / den) * p_grp, 0.0)
            + jnp.where(lane_f == i2, (e2 / den) * p_grp, 0.0)
            + jnp.where(lane == ROUTER_GIDX_LANE, g_idx, 0.0)
            + jnp.where(lane == ROUTER_DEST_LANE, dest, 0.0))


def _out_proj_kernel(x_ref, att_ref, gla_ref, ssm_ref, w_out_ref, nw_ref, rwh_ref, rwl_ref, rb_ref,
                     x1_ref, hn_ref, gate_ref, wo_ref):
    @pl.when(pl.program_id(0) == 0)
    def _():
        for r in range(0, D_MODEL, PREP_ROWS):
            wo_ref[r:r + PREP_ROWS, :] = w_out_ref[0, r:r + PREP_ROWS, :].astype(BF16)

    def sub_tile(r0):
        rows = slice(r0, r0 + OUT_SUB)
        mo = (_dot(att_ref[rows, :], wo_ref[0:WIDTH_A, :])
              + _dot(gla_ref[rows, :], wo_ref[WIDTH_A:WIDTH_A + WIDTH_B, :])
              + _dot(ssm_ref[rows, :], wo_ref[WIDTH_A + WIDTH_B:, :]))
        yield
        x1 = x_ref[rows, :] + mo
        x1_ref[rows, :] = x1
        ms = jnp.mean(x1 * x1, axis=-1, keepdims=True)
        hn = x1 * lax.rsqrt(ms + EPS) * nw_ref[...]
        hn_ref[rows, :] = hn.astype(hn_ref.dtype)
        h_hi = hn.astype(BF16)
        h_lo = (hn - h_hi.astype(F32)).astype(BF16)
        yield
        logits = (_dot(h_hi, rwh_ref[...]) + _dot(h_lo, rwh_ref[...]) + _dot(h_hi, rwl_ref[...])
                  + rb_ref[...])
        yield
        gate_ref[rows, :] = _route(logits)

    _round_robin([sub_tile(r0) for r0 in range(0, OUT_TM, OUT_SUB)])


def _out_proj(x, att, gla, ssm, lw, layer):
    t = x.shape[0]

    def rows(width):
        return pl.BlockSpec((OUT_TM, width), lambda i: (i, 0))

    def const(shape):
        return pl.BlockSpec(shape, lambda i: (0, 0))

    return pl.pallas_call(
        _out_proj_kernel,
        grid=(t // OUT_TM,),
        in_specs=[rows(D_MODEL), rows(WIDTH_A), rows(WIDTH_B), rows(WIDTH_C),
                  pl.BlockSpec((1, D_MODEL, D_MODEL), lambda i: (layer, 0, 0)), const((1, D_MODEL)),
                  const((D_MODEL, 128)), const((D_MODEL, 128)), const((1, 128))],
        out_specs=[rows(D_MODEL), rows(D_MODEL), rows(128)],
        out_shape=[jax.ShapeDtypeStruct((t, D_MODEL), F32),
                   jax.ShapeDtypeStruct((t, D_MODEL), BF16),
                   jax.ShapeDtypeStruct((t, 128), F32)],
        scratch_shapes=[pltpu.VMEM((D_MODEL, D_MODEL), BF16)],
        compiler_params=_params("arbitrary"),
    )(x, att, gla, ssm, lw["wo"], lw["n2w"], lw["rw_hi"], lw["rw_lo"], lw["rb"])


RT_TS = OUT_SUB
CB_TILES = 4
RT_ALIGN = 16
RT_ROWS = 384
RT_SIZES = (256, 128, 64, 32, 16)
PAY_COLS = D_MODEL + 256
MOE_TM = 512


def _moe_max_tiles(t):
    return (t + (t // RT_TS) * N_EXPERT_GROUPS * (RT_ALIGN - 1)) // MOE_TM + N_EXPERT_GROUPS


def _sort_matrix(gate):
    lane = lax.broadcasted_iota(jnp.int32, (RT_TS, 128), 1)
    dest = jnp.where(lane == ROUTER_DEST_LANE, gate, 0.0).sum(axis=1, keepdims=True)
    rows = lax.broadcasted_iota(jnp.int32, (RT_TS, RT_ROWS), 1).astype(F32)
    return jnp.where(rows == dest, 1.0, 0.0).astype(BF16)


def _piece_copies(length, local, glob, make_copy, act):
    for size in RT_SIZES:
        take = (length & size) != 0

        @pl.when(take)
        def _(local=local, glob=glob, size=size):
            act(make_copy(pl.multiple_of(local, RT_ALIGN), pl.multiple_of(glob, RT_ALIGN), size))

        step = jnp.where(take, size, 0)
        local = local + step
        glob = glob + step


def _run_copies(c_ref, lp_ref, off_ref, tile, make_copy, act):
    for g in range(N_EXPERT_GROUPS):
        k = N_EXPERT_GROUPS * tile + g
        _piece_copies(lp_ref[k], c_ref[k], off_ref[k], make_copy, act)


def _dispatch_kernel(c_ref, lp_ref, off_ref, tail_ref, nact_ref, hn_ref, gate_ref, xs_hbm,
                     y_ref, zero_ref, sem, zsem, *, first_spare, max_tiles):
    s = pl.program_id(0)
    ns = pl.num_programs(0)
    base = (s % 2) * CB_TILES

    def tile(u):
        rows = slice(u * RT_TS, (u + 1) * RT_TS)
        gate = gate_ref[rows, :]
        pt = _sort_matrix(gate)
        g_hi = gate.astype(BF16)
        g_lo = (gate - g_hi.astype(F32)).astype(BF16)
        yield
        y_ref[base + u, :, 0:D_MODEL] = _dot_tn(pt, hn_ref[rows, :]).astype(BF16)
        y_ref[base + u, :, D_MODEL:D_MODEL + 128] = _dot_tn(pt, g_hi).astype(BF16)
        y_ref[base + u, :, D_MODEL + 128:] = _dot_tn(pt, g_lo).astype(BF16)

    _round_robin([tile(u) for u in range(CB_TILES)])

    def copy_from(slot):
        return lambda local, glob, size: pltpu.make_async_copy(
            y_ref.at[slot, pl.ds(local, size)], xs_hbm.at[pl.ds(glob, size)], sem.at[slot])

    def copies(step, slot0, act):
        for u in range(CB_TILES):
            _run_copies(c_ref, lp_ref, off_ref, step * CB_TILES + u, copy_from(slot0 + u), act)

    copies(s, base, lambda cp: cp.start())

    @pl.when(s > 0)
    def _():
        copies(s - 1, CB_TILES - base, lambda cp: cp.wait())

    @pl.when(s == ns - 1)
    def _():
        copies(s, base, lambda cp: cp.wait())
        zero_ref[...] = jnp.zeros_like(zero_ref)

        def zero_copy(local, glob, size):
            return pltpu.make_async_copy(zero_ref.at[pl.ds(local, size)],
                                         xs_hbm.at[pl.ds(glob, size)], zsem)

        def zero_fill(act):
            for g in range(N_EXPERT_GROUPS):
                _piece_copies(tail_ref[N_EXPERT_GROUPS + g], 0, tail_ref[g], zero_copy, act)
            for j in range(first_spare, max_tiles):
                @pl.when(j >= nact_ref[0])
                def _(j=j):
                    act(zero_copy(0, j * MOE_TM, MOE_TM))

        zero_fill(lambda cp: cp.start())
        zero_fill(lambda cp: cp.wait())


def _moe_group_kernel(grp_ref, nact_ref, xs_ref, wg_ref, wu_ref, wd_ref, o_ref, wg_bf, wu_bf, wd_bf):
    j = pl.program_id(0)

    @pl.when(jnp.logical_or(j == 0, grp_ref[j] != grp_ref[jnp.maximum(j - 1, 0)]))
    def _():
        wg_bf[...] = wg_ref[0].astype(BF16)
        wu_bf[...] = wu_ref[0].astype(BF16)
        wd_bf[...] = wd_ref[0].astype(BF16)

    @pl.when(j >= nact_ref[0])
    def _():
        o_ref[...] = jnp.zeros_like(o_ref)

    @pl.when(j < nact_ref[0])
    def _():
        x = xs_ref[:, 0:D_MODEL]
        gate = (xs_ref[:, D_MODEL:D_MODEL + 128].astype(F32)
                + xs_ref[:, D_MODEL + 128:].astype(F32))
        lane = lax.broadcasted_iota(jnp.int32, gate.shape, 1)
        first = grp_ref[j] * EXPERTS_PER_GROUP
        acc = None
        for e in range(EXPERTS_PER_GROUP):
            ge = jnp.where(lane == first + e, gate, 0.0).sum(axis=-1, keepdims=True)
            hid = _silu(_dot(x, wg_bf[e])) * _dot(x, wu_bf[e]) * ge
            part = _dot(hid.astype(BF16), wd_bf[e])
            acc = part if acc is None else acc + part
        o_ref[...] = acc


def _combine_kernel(c_ref, lp_ref, off_ref, x_ref, gate_ref, fw_ref, ys_hbm, o_ref, z_ref, sem,
                    *, final_norm):
    s = pl.program_id(0)
    ns = pl.num_programs(0)
    base = (s % 2) * CB_TILES

    def copy_into(slot):
        return lambda local, glob, size: pltpu.make_async_copy(
            ys_hbm.at[pl.ds(glob, size)], z_ref.at[slot, pl.ds(local, size)], sem.at[slot])

    def copies(step, slot0, act):
        for u in range(CB_TILES):
            _run_copies(c_ref, lp_ref, off_ref, step * CB_TILES + u, copy_into(slot0 + u), act)

    @pl.when(s == 0)
    def _():
        z_ref[...] = jnp.zeros_like(z_ref)
        copies(0, 0, lambda cp: cp.start())

    @pl.when(s + 1 < ns)
    def _():
        copies(s + 1, CB_TILES - base, lambda cp: cp.start())

    copies(s, base, lambda cp: cp.wait())

    def tile(u):
        rows = slice(u * RT_TS, (u + 1) * RT_TS)
        pt = _sort_matrix(gate_ref[rows, :])
        yield
        z = z_ref[base + u]
        z_hi = z.astype(BF16)
        z_lo = (z - z_hi.astype(F32)).astype(BF16)
        y = x_ref[rows, :] + _dot(pt, z_hi) + _dot(pt, z_lo)
        yield
        if final_norm:
            ms = jnp.mean(y * y, axis=-1, keepdims=True)
            y = y * lax.rsqrt(ms + EPS) * fw_ref[...]
        o_ref[rows, :] = y

    _round_robin([tile(u) for u in range(CB_TILES)])


def _route_meta(gate, max_tiles):
    t = gate.shape[0]
    nt = t // RT_TS
    groups = jnp.arange(N_EXPERT_GROUPS, dtype=jnp.int32)
    g = gate[:, ROUTER_GIDX_LANE].astype(jnp.int32).reshape(nt, RT_TS, 1)
    count = (g == groups).astype(jnp.int32).sum(axis=1)
    lp = (count + RT_ALIGN - 1) // RT_ALIGN * RT_ALIGN
    total = lp.sum(axis=0)
    ntile = (total + MOE_TM - 1) // MOE_TM
    last = jnp.cumsum(ntile)
    first_row = (last - ntile) * MOE_TM
    c = jnp.cumsum(lp, axis=1) - lp
    off = first_row[None, :] + jnp.cumsum(lp, axis=0) - lp
    tail = jnp.concatenate([first_row + total, ntile * MOE_TM - total])
    j = jnp.arange(max_tiles, dtype=jnp.int32)[:, None]
    grp = jnp.minimum((j >= last[None, :]).astype(jnp.int32).sum(axis=1), N_EXPERT_GROUPS - 1)
    return c.reshape(-1), lp.reshape(-1), off.reshape(-1), tail, grp, last[-1:]


def _moe(x, hn, gate, lw, layer, fw, final_norm):
    t = x.shape[0]
    nt = t // RT_TS
    max_tiles = _moe_max_tiles(t)
    c, lp, off, tail, grp, nact = _route_meta(gate, max_tiles)

    def wide(width):
        return pl.BlockSpec((CB_TILES * RT_TS, width), lambda i, *_: (i, 0))

    xs = pl.pallas_call(
        functools.partial(_dispatch_kernel, first_spare=t // MOE_TM, max_tiles=max_tiles),
        grid_spec=pltpu.PrefetchScalarGridSpec(
            num_scalar_prefetch=5, grid=(nt // CB_TILES,),
            in_specs=[wide(D_MODEL), wide(128)],
            out_specs=pl.BlockSpec(memory_space=pl.ANY),
            scratch_shapes=[pltpu.VMEM((2 * CB_TILES, RT_ROWS, PAY_COLS), BF16),
                            pltpu.VMEM((MOE_TM, PAY_COLS), BF16),
                            pltpu.SemaphoreType.DMA((2 * CB_TILES,)), pltpu.SemaphoreType.DMA(())]),
        out_shape=jax.ShapeDtypeStruct((max_tiles * MOE_TM, PAY_COLS), BF16),
        compiler_params=_params("arbitrary"),
    )(c, lp, off, tail, nact, hn, gate)

    def experts(shape):
        return pl.BlockSpec((1, EXPERTS_PER_GROUP) + shape, lambda j, grp, nact: (layer, grp[j], 0, 0))

    def rounded(shape):
        return pltpu.VMEM((EXPERTS_PER_GROUP,) + shape, BF16)

    ys = pl.pallas_call(
        _moe_group_kernel,
        grid_spec=pltpu.PrefetchScalarGridSpec(
            num_scalar_prefetch=2, grid=(max_tiles,),
            in_specs=[pl.BlockSpec((MOE_TM, PAY_COLS), lambda j, grp, nact: (j, 0)),
                      experts((D_MODEL, D_EXPERT)), experts((D_MODEL, D_EXPERT)),
                      experts((D_EXPERT, D_MODEL))],
            out_specs=pl.BlockSpec((MOE_TM, D_MODEL), lambda j, grp, nact: (j, 0)),
            scratch_shapes=[rounded((D_MODEL, D_EXPERT)), rounded((D_MODEL, D_EXPERT)),
                            rounded((D_EXPERT, D_MODEL))]),
        out_shape=jax.ShapeDtypeStruct((max_tiles * MOE_TM, D_MODEL), F32),
        compiler_params=_params("arbitrary"),
    )(grp, nact, xs, lw["wg"], lw["wu"], lw["wd"])

    return pl.pallas_call(
        functools.partial(_combine_kernel, final_norm=final_norm),
        grid_spec=pltpu.PrefetchScalarGridSpec(
            num_scalar_prefetch=3, grid=(nt // CB_TILES,),
            in_specs=[wide(D_MODEL), wide(128), pl.BlockSpec((1, D_MODEL), lambda i, *_: (0, 0)),
                      pl.BlockSpec(memory_space=pl.ANY)],
            out_specs=wide(D_MODEL),
            scratch_shapes=[pltpu.VMEM((2 * CB_TILES, RT_ROWS, D_MODEL), F32),
                            pltpu.SemaphoreType.DMA((2 * CB_TILES,))]),
        out_shape=jax.ShapeDtypeStruct((t, D_MODEL), F32),
        compiler_params=_params("arbitrary"),
    )(c, lp, off, x, gate, fw, ys)


def _lane_place(vec, start, width=128):
    return jnp.zeros((1, width), F32).at[0, start:start + vec.shape[0]].set(vec.astype(F32))


REL_PAD = 704


def _toeplitz(v, rows, cols):
    n = v.shape[-1]
    tiled = jnp.tile(v, (1,) * (v.ndim - 1) + (rows,))[..., :rows * (n - 1)]
    return tiled.reshape(v.shape[:-1] + (rows, n - 1))[..., :cols]


def _band_bias(table):
    t = table.astype(F32)
    ext = jnp.concatenate([jnp.repeat(t[:, :1], REL_PAD, axis=1), t,
                           jnp.repeat(t[:, -1:], REL_PAD, axis=1)], axis=1)
    extr = ext[:, ::-1]
    top = ext.shape[1] - 1 - (REL_CLIP + REL_PAD)

    def band(off, nq, nk):
        n = nq + nk
        v = jnp.concatenate([extr[:, top - off:top - off + nk + 1],
                             extr[:, top - off - (nq - 1):top - off]], axis=1)
        assert v.shape[1] == n
        return _toeplitz(v, nq, nk)

    sample = band(WINDOW, CHUNK, BAND)
    kc = (jnp.arange(ATT_KB) // CHUNK)[:, None]
    qc = (jnp.arange(ATT_QB) // CHUNK)[None, :]
    block = lambda off: jnp.swapaxes(band(off, ATT_QB, ATT_KB), 1, 2)
    prompt = jnp.stack([jnp.where((kc >= qc)[None], block(2 * ATT_KB), -1e30),
                        block(ATT_KB),
                        jnp.where((kc <= qc)[None], block(0), -1e30)])
    return sample * LOG2_E, prompt * LOG2_E


def _layer_weights(i, norm1_w, w_in, rel_bias_table, gla_w_gate_up, gla_b_gate, gla_norm_w,
                   ssm_conv_w, ssm_conv_b, ssm_dt_bias, ssm_a_log, ssm_d, ssm_norm_w, w_out,
                   norm2_w, router_group_w, router_group_b, router_expert_w, router_expert_b,
                   exp_w_gate, exp_w_up, exp_w_down):
    bias_s, bias_p = _band_bias(rel_bias_table[i])
    rw = jnp.concatenate([router_expert_w[i], router_group_w[i],
                          jnp.zeros((D_MODEL, 128 - N_EXPERTS - N_EXPERT_GROUPS), F32)], axis=1)
    rw_hi = rw.astype(BF16)
    return dict(
        n1w=norm1_w[i][None, :],
        w_in=w_in,
        bias_s=bias_s,
        bias_p=bias_p,
        gup=jnp.zeros((128, 128), F32).at[:GATE_RANK, :].set(gla_w_gate_up[i]).astype(BF16),
        gb=gla_b_gate[i][None, :].astype(F32),
        gnw=jnp.tile(gla_norm_w[i], H_B)[None, :].astype(F32),
        cw=ssm_conv_w[i].astype(F32),
        cb=ssm_conv_b[i][None, :].astype(F32),
        dtb=_lane_place(ssm_dt_bias[i], MISC_DT_LANE),
        alog=jnp.repeat(ssm_a_log[i].astype(F32), P_C)[None, :],
        dsk=jnp.repeat(ssm_d[i].astype(F32), P_C)[None, :],
        snw=ssm_norm_w[i][None, :].astype(F32),
        wo=w_out,
        n2w=norm2_w[i][None, :],
        rw_hi=rw_hi,
        rw_lo=(rw - rw_hi.astype(F32)).astype(BF16),
        rb=jnp.concatenate([router_expert_b[i], router_group_b[i],
                            jnp.zeros((128 - N_EXPERTS - N_EXPERT_GROUPS,), F32)])[None, :],
        wg=exp_w_gate,
        wu=exp_w_up,
        wd=exp_w_down,
    )


def _stream_layer(x, bsz, lw, layer, cache_k, cache_v, gla0, conv0, ssm0, fw, final_norm):
    l = x.shape[0] // bsz
    if cache_k is None:
        qk, rest, k_keep, v_keep, vt = _in_proj(x, lw["n1w"], lw["w_in"], layer, True, l)
        att = _attn_prompt(qk.reshape(bsz, l, 2 * WIDTH_A),
                           vt.reshape(bsz, l // ATT_KB, WIDTH_A, ATT_KB), lw["bias_p"])
    else:
        qk, rest, k_keep, v_keep, kv = _in_proj(x, lw["n1w"], lw["w_in"], layer, False, l)
        att = _attn_sample(qk.reshape(bsz, l, 2 * WIDTH_A), kv.reshape(bsz, l, 2 * WIDTH_A),
                           cache_k, cache_v, layer, lw["bias_s"])
    gla_o, ssm_o, gla_s, conv_s, ssm_s = _scan(rest.reshape(bsz, l, REST_COLS), lw, gla0, conv0, ssm0)
    t = bsz * l
    x1, hn, gate = _out_proj(x, att.reshape(t, WIDTH_A), gla_o.reshape(t, WIDTH_B),
                             ssm_o.reshape(t, WIDTH_C), lw, layer)
    x2 = _moe(x1, hn, gate, lw, layer, fw, final_norm)
    keep = min(WINDOW, l)
    k_keep = k_keep.reshape(bsz, keep, H_A, DH_A)
    v_keep = v_keep.reshape(bsz, keep, H_A, DH_A)
    gla_state = gla_s.reshape(bsz, DV_B, H_B, DK_B).transpose(0, 2, 3, 1)
    conv_state = conv_s[:, 8 - (CONV_W - 1):, :]
    return x2, k_keep, v_keep, gla_state, conv_state, ssm_s


def kernel(x_prompt, x_sample, cache_k_a, cache_v_a, state_gla, state_conv, state_ssm, norm1_w, w_in, rel_bias_table, gla_w_gate_up, gla_b_gate, gla_norm_w, ssm_conv_w, ssm_conv_b, ssm_dt_bias, ssm_a_log, ssm_d, ssm_norm_w, w_out, norm2_w, router_group_w, router_group_b, router_expert_w, router_expert_b, exp_w_gate, exp_w_up, exp_w_down, final_norm_w):
    bp, sp, _ = x_prompt.shape
    bs, ss, _ = x_sample.shape
    xp = x_prompt.reshape(bp * sp, D_MODEL)
    xs = x_sample.reshape(bs * ss, D_MODEL)
    fw = final_norm_w[None, :].astype(F32)
    outs_p, outs_s = [], []
    cache_k = cache_k_a.astype(BF16).reshape(DEPTH, bs, -1, WIDTH_A)
    cache_v = cache_v_a.astype(BF16).reshape(DEPTH, bs, -1, WIDTH_A)
    for i in range(DEPTH):
        lw = _layer_weights(i, norm1_w, w_in, rel_bias_table, gla_w_gate_up, gla_b_gate, gla_norm_w,
                            ssm_conv_w, ssm_conv_b, ssm_dt_bias, ssm_a_log, ssm_d, ssm_norm_w, w_out,
                            norm2_w, router_group_w, router_group_b, router_expert_w, router_expert_b,
                            exp_w_gate, exp_w_up, exp_w_down)
        last = i == DEPTH - 1
        xp, *sp_out = _stream_layer(
            xp, bp, lw, i, None, None,
            jnp.zeros((bp, DV_B, H_B * DK_B), F32),
            jnp.zeros((bp, 8, CONV_CH), F32),
            jnp.zeros((bp, H_C, P_C, D_STATE), F32), fw, last)
        outs_p.append(sp_out)
        gla0 = state_gla[i].astype(F32).transpose(0, 3, 1, 2).reshape(bs, DV_B, H_B * DK_B)
        conv0 = jnp.pad(state_conv[i].astype(F32), ((0, 0), (8 - (CONV_W - 1), 0), (0, 0)))
        xs, *ss_out = _stream_layer(
            xs, bs, lw, i, cache_k, cache_v,
            gla0, conv0, state_ssm[i].astype(F32), fw, last)
        outs_s.append(ss_out)
    stack = lambda outs, j: jnp.stack([o[j] for o in outs])
    return (xp.reshape(bp, sp, D_MODEL), xs.reshape(bs, ss, D_MODEL),
            stack(outs_p, 0), stack(outs_p, 1), stack(outs_p, 2), stack(outs_p, 3), stack(outs_p, 4),
            stack(outs_s, 0), stack(outs_s, 1), stack(outs_s, 2), stack(outs_s, 3), stack(outs_s, 4))
```
